```python
import math
import jax
import jax.numpy as jnp
from jax import lax
import numpy as np

D_MODEL = 2048
BATCH = 2
SEQ = 8192
DEPTH = 2

F32 = jnp.float32
HEAD_DIM = 64
GROUP_WIDTH = D_MODEL // 4
D_MIX = 4 * GROUP_WIDTH
CONV_WIDTH = 3
RWKV_HEADS = GROUP_WIDTH // HEAD_DIM
RWKV_DECAY_RANK = 96
RWKV_A_RANK = 96
RWKV_GATE_RANK = 128
RWKV_GN_EPS = 64e-5
ATT_HEADS = GROUP_WIDTH // HEAD_DIM
ATT_KV_HEADS = max(1, ATT_HEADS // 4)
WINDOW = 128
ATT_BLOCK = 128
N_BUCKETS = 32
NEG_INF = -1e30
S5_CH = 16
S5_GROUPS = GROUP_WIDTH // S5_CH
S5_STATE = 64
N_EXPERT_GROUPS = 4
EXPERTS_PER_GROUP = 8
N_EXPERTS = N_EXPERT_GROUPS * EXPERTS_PER_GROUP
TOP_K = 2
D_EXPERT = D_MODEL // 4
MOE_BLOCK = 128
ALPHA = (2 * DEPTH) ** 0.25
BETA = (8 * DEPTH) ** -0.25
LN_EPS = 1e-5
RW_OFF = 3 * GROUP_WIDTH
RW_COLS = 3 * GROUP_WIDTH + RWKV_DECAY_RANK + RWKV_A_RANK + RWKV_GATE_RANK
ATT_OFF = RW_OFF + RW_COLS
ATT_Q = ATT_HEADS * HEAD_DIM
ATT_KV = ATT_KV_HEADS * HEAD_DIM
S5_OFF = ATT_OFF + ATT_Q + 2 * ATT_KV
N_IN = S5_OFF + GROUP_WIDTH

kernel_name = 'hybrid_parallel_heads_hmoe_deepnorm'


def layer_norm(x, gain=None, bias=None, eps=LN_EPS):
    x32 = x.astype(F32)
    mean = jnp.mean(x32, axis=-1, keepdims=True)
    var = jnp.mean(jnp.square(x32 - mean), axis=-1, keepdims=True)
    y = (x32 - mean) * lax.rsqrt(var + eps)
    if gain is not None:
        y = y * gain.astype(F32) + bias.astype(F32)
    return y


def ada_input(x, shift, scale):
    return (layer_norm(x) * (1.0 + scale[:, None, :]) + shift[:, None, :]).astype(x.dtype)


def short_conv_mixer(b_gate, c_gate, h, conv_w):
    z = c_gate * h
    z = lax.conv_general_dilated(z, conv_w[:, None, :].astype(z.dtype), window_strides=(1,),
                                 padding=[(CONV_WIDTH - 1, 0)],
                                 dimension_numbers=('NWC', 'WIO', 'NWC'),
                                 feature_group_count=z.shape[-1])
    return b_gate * z


def wkv7_scan(r, decay, k, v, a_vec, b_vec):
    Bsz, _, H, N = r.shape

    def step(state, inp):
        r_t, w_t, k_t, v_t, a_t, b_t = inp
        sa = jnp.einsum('bhvk,bhk->bhv', state, a_t)
        state = (state * w_t[:, :, None, :] + v_t[..., :, None] * k_t[..., None, :]
                 + sa[..., :, None] * b_t[..., None, :])
        return state, jnp.einsum('bhvk,bhk->bhv', state, r_t)

    xs = tuple(jnp.swapaxes(t, 0, 1) for t in (r, decay, k, v, a_vec, b_vec))
    _, ys = lax.scan(step, jnp.zeros((Bsz, H, N, N), F32), xs)
    return jnp.swapaxes(ys, 0, 1)


def rwkv7_mixer(cols, mu, w0, w2, a0, a2, g2, k_k, k_a, r_k, gn_g, gn_b):
    Bsz, S, _ = cols.shape
    G = GROUP_WIDTH
    p = cols.astype(F32)
    p_prev = jnp.pad(p, ((0, 0), (1, 0), (0, 0)))[:, :-1]
    p = p + (p_prev - p) * mu.astype(F32)
    r, k, v = p[..., :G], p[..., G:2 * G], p[..., 2 * G:3 * G]
    o = 3 * G
    w_lo = p[..., o:o + RWKV_DECAY_RANK]
    o += RWKV_DECAY_RANK
    a_lo = p[..., o:o + RWKV_A_RANK]
    o += RWKV_A_RANK
    g_lo = p[..., o:o + RWKV_GATE_RANK]
    w = -jax.nn.softplus(-(w0 + jnp.tanh(w_lo) @ w2)) - 0.5
    decay = jnp.exp(-jnp.exp(w))
    a = jax.nn.sigmoid(a0 + a_lo @ a2)
    g = jax.nn.sigmoid(g_lo) @ g2

    def heads(t):
        return t.reshape(Bsz, S, RWKV_HEADS, HEAD_DIM)

    kk = heads(k * k_k)
    kk = kk / jnp.maximum(jnp.sqrt(jnp.sum(kk * kk, axis=-1, keepdims=True)), 1e-12)
    k = k * (1.0 + (a - 1.0) * k_a)
    rh, kh, vh = heads(r), heads(k), heads(v)
    y = wkv7_scan(rh, heads(decay), kh, vh, -kk, kk * heads(a))
    mean = jnp.mean(y, axis=-1, keepdims=True)
    var = jnp.mean(jnp.square(y - mean), axis=-1, keepdims=True)
    y = ((y - mean) * lax.rsqrt(var + RWKV_GN_EPS)).reshape(Bsz, S, G) * gn_g + gn_b
    bonus = jnp.sum(rh * kh * r_k, axis=-1, keepdims=True) * vh
    return (y + bonus.reshape(Bsz, S, G)) * g


def t5_bucket(rel):
    n = jnp.maximum(rel, 0)
    max_exact = N_BUCKETS // 2
    n_f = jnp.maximum(n, 1).astype(F32)
    large = max_exact + (jnp.log(n_f / max_exact) / math.log(WINDOW / max_exact)
                         * (N_BUCKETS - max_exact)).astype(jnp.int32)
    return jnp.where(n < max_exact, n, jnp.minimum(large, N_BUCKETS - 1))


def swa_sink_attention(q, k, v, sinks, rel_bias):
    Bsz, S, _ = q.shape
    nb = S // ATT_BLOCK
    rep = ATT_HEADS // ATT_KV_HEADS
    qb = q.astype(F32).reshape(Bsz, nb, ATT_BLOCK, ATT_KV_HEADS, rep, HEAD_DIM)

    def with_prev(t):
        t = t.astype(F32).reshape(Bsz, nb, ATT_BLOCK, ATT_KV_HEADS, HEAD_DIM)
        prev = jnp.concatenate([jnp.zeros_like(t[:, :1]), t[:, :-1]], axis=1)
        return jnp.concatenate([prev, t], axis=2)

    kw, vw = with_prev(k), with_prev(v)
    scores = jnp.einsum('bnqgrd,bnkgd->bngrqk', qb, kw) * (HEAD_DIM ** -0.5)
    qi = jnp.arange(ATT_BLOCK)[:, None]
    kj = jnp.arange(2 * ATT_BLOCK)[None, :]
    rel = qi + ATT_BLOCK - kj
    bias = rel_bias.astype(F32)[t5_bucket(rel)]
    bias = jnp.transpose(bias, (2, 0, 1)).reshape(ATT_KV_HEADS, rep, ATT_BLOCK, 2 * ATT_BLOCK)
    key_pos = jnp.arange(nb)[:, None] * ATT_BLOCK + kj - ATT_BLOCK
    valid = ((rel >= 0) & (rel < WINDOW))[None] & (key_pos >= 0)[:, None, :]
    scores = jnp.where(valid[None, :, None, None], scores + bias, NEG_INF)
    sink = jnp.broadcast_to(sinks.astype(F32).reshape(1, 1, ATT_KV_HEADS, rep, 1, 1),
                            scores.shape[:-1] + (1,))
    probs = jax.nn.softmax(jnp.concatenate([scores, sink], axis=-1), axis=-1)[..., :-1]
    out = jnp.einsum('bngrqk,bnkgd->bnqgrd', probs, vw)
    return out.reshape(Bsz, S, ATT_Q)


def s5_mixer(u, lam_re, lam_im, log_dt, b_re, b_im, c_re, c_im, d_skip, glu_w, glu_b):
    Bsz, S, _ = u.shape
    u32 = u.astype(F32).reshape(Bsz, S, S5_GROUPS, S5_CH)
    lr, li = lam_re.astype(F32), lam_im.astype(F32)
    delta = jnp.exp(log_dt.astype(F32))[:, None]
    mag = jnp.exp(lr * delta)
    ab_re, ab_im = mag * jnp.cos(li * delta), mag * jnp.sin(li * delta)
    den = lr * lr + li * li
    z_re = ((ab_re - 1.0) * lr + ab_im * li) / den
    z_im = (ab_im * lr - (ab_re - 1.0) * li) / den
    br, bi = b_re.astype(F32), b_im.astype(F32)
    bb_re = z_re[..., None] * br - z_im[..., None] * bi
    bb_im = z_re[..., None] * bi + z_im[..., None] * br
    bu_re = jnp.einsum('gpc,bsgc->bsgp', bb_re, u32)
    bu_im = jnp.einsum('gpc,bsgc->bsgp', bb_im, u32)
    a_re = jnp.broadcast_to(ab_re, (1, S) + ab_re.shape)
    a_im = jnp.broadcast_to(ab_im, (1, S) + ab_im.shape)

    def combine(e1, e2):
        a1r, a1i, b1r, b1i = e1
        a2r, a2i, b2r, b2i = e2
        return (a2r * a1r - a2i * a1i, a2r * a1i + a2i * a1r,
                a2r * b1r - a2i * b1i + b2r, a2r * b1i + a2i * b1r + b2i)

    _, _, xr, xi = lax.associative_scan(combine, (a_re, a_im, bu_re, bu_im), axis=1)
    y = (jnp.einsum('gcp,bsgp->bsgc', c_re.astype(F32), xr)
         - jnp.einsum('gcp,bsgp->bsgc', c_im.astype(F32), xi)
         + d_skip.astype(F32) * u32)
    y = jax.nn.gelu(y.reshape(Bsz, S, GROUP_WIDTH))
    return y * jax.nn.sigmoid(y @ glu_w.astype(F32) + glu_b.astype(F32))


def hierarchical_moe(h, wg, bg, we, be, w1, w3, w2):
    Bsz, S, D = h.shape
    T = Bsz * S
    ht = h.reshape(T, D)
    h32 = ht.astype(F32)
    g_probs = jax.nn.softmax(h32 @ wg.astype(F32) + bg.astype(F32), axis=-1)
    g_val, g_idx = lax.top_k(g_probs, 1)
    e_logits = (h32 @ we.astype(F32) + be.astype(F32)).reshape(T, N_EXPERT_GROUPS, EXPERTS_PER_GROUP)
    e_logits = jnp.take_along_axis(e_logits, g_idx[:, :, None], axis=1)[:, 0]
    e_val, e_idx = lax.top_k(e_logits, TOP_K)
    weights = jax.nn.softmax(e_val, axis=-1) * g_val
    expert_id = g_idx * EXPERTS_PER_GROUP + e_idx
    n_assign = T * TOP_K
    n_rows = ((n_assign + N_EXPERTS * (MOE_BLOCK - 1) + MOE_BLOCK - 1) // MOE_BLOCK) * MOE_BLOCK
    n_blocks = n_rows // MOE_BLOCK
    flat_e = expert_id.reshape(n_assign)
    flat_tok = jnp.repeat(jnp.arange(T, dtype=jnp.int32), TOP_K)
    order = jnp.argsort(flat_e)
    se, stok, sw = flat_e[order], flat_tok[order], weights.reshape(n_assign)[order]
    counts = jnp.bincount(flat_e, length=N_EXPERTS)
    padded = ((counts + MOE_BLOCK - 1) // MOE_BLOCK) * MOE_BLOCK
    pad_end = jnp.cumsum(padded)
    pad_start = pad_end - padded
    raw_start = jnp.cumsum(counts) - counts
    dest = pad_start[se] + (jnp.arange(n_assign) - raw_start[se])
    row_tok = jnp.zeros((n_rows,), jnp.int32).at[dest].set(stok)
    block_expert = jnp.minimum(
        jnp.searchsorted(pad_end, jnp.arange(n_blocks) * MOE_BLOCK, side='right'), N_EXPERTS - 1)
    xb = ht[row_tok].reshape(n_blocks, MOE_BLOCK, D)

    def expert_block(args):
        xblk, e = args
        return (jax.nn.silu(xblk @ w1[e]) * (xblk @ w3[e])) @ w2[e]

    yr = lax.map(expert_block, (xb, block_expert)).reshape(n_rows, D)
    ya = yr[dest].astype(F32) * sw[:, None]
    return jax.ops.segment_sum(ya, stok, num_segments=T).reshape(Bsz, S, D)


def setup_inputs(seed: int = 0) -> dict:
    key = jax.random.key(seed)
    keys = jax.random.split(key, 48)
    counter = [0]

    def nxt():
        k = keys[counter[0]]
        counter[0] += 1
        return k

    def nrm(shape, scale):
        return jax.random.normal(nxt(), shape, F32) * scale

    def unif(shape, lo, hi):
        return jax.random.uniform(nxt(), shape, F32, lo, hi)

    L, D, G = DEPTH, D_MODEL, GROUP_WIDTH
    col_scale = np.ones((N_IN,), np.float32)
    col_scale[RW_OFF + 2 * G:RW_OFF + 3 * G] = BETA
    col_scale[ATT_OFF + ATT_Q + ATT_KV:ATT_OFF + ATT_Q + 2 * ATT_KV] = BETA
    return {
        'x': nrm((BATCH, SEQ, D), 1.0),
        'c': nrm((BATCH, D), 1.0),
        'w_ada': nrm((L, D, 6 * D), 0.2 * D ** -0.5),
        'b_ada': nrm((L, 6 * D), 0.02),
        'ln_g': 1.0 + nrm((L, 2, D), 0.02),
        'ln_b': nrm((L, 2, D), 0.02),
        'w_in': nrm((L, D, N_IN), D ** -0.5) * jnp.asarray(col_scale),
        'w_out': nrm((L, D_MIX, D), BETA * D_MIX ** -0.5),
        'conv_w': nrm((L, CONV_WIDTH, G), CONV_WIDTH ** -0.5),
        'rwkv_mu': unif((L, RW_COLS), 0.0, 1.0),
        'rwkv_w0': jnp.linspace(-6.0, -1.0, G, dtype=F32)[None] + nrm((L, G), 0.1),
        'rwkv_w2': nrm((L, RWKV_DECAY_RANK, G), 0.1 * RWKV_DECAY_RANK ** -0.5),
        'rwkv_a0': nrm((L, G), 0.1),
        'rwkv_a2': nrm((L, RWKV_A_RANK, G), 0.5 * RWKV_A_RANK ** -0.5),
        'rwkv_g2': nrm((L, RWKV_GATE_RANK, G), RWKV_GATE_RANK ** -0.5),
        'rwkv_kk': 0.85 + nrm((L, G), 0.02),
        'rwkv_ka': 1.0 + nrm((L, G), 0.02),
        'rwkv_rk': -0.04 + nrm((L, RWKV_HEADS, HEAD_DIM), 0.02),
        'rwkv_gn_g': 1.0 + nrm((L, G), 0.02),
        'rwkv_gn_b': nrm((L, G), 0.02),
        'attn_sinks': nrm((L, ATT_HEADS), 0.5),
        'rel_bias': nrm((N_BUCKETS, ATT_HEADS), 0.5),
        's5_lambda_re': -0.5 + nrm((L, S5_GROUPS, S5_STATE), 0.01),
        's5_lambda_im': math.pi * jnp.arange(S5_STATE, dtype=F32)[None, None] + nrm((L, S5_GROUPS, S5_STATE), 0.01),
        's5_log_dt': unif((L, S5_GROUPS), math.log(1e-3), math.log(1e-1)),
        's5_b_re': nrm((L, S5_GROUPS, S5_STATE, S5_CH), (2 * S5_CH) ** -0.5),
        's5_b_im': nrm((L, S5_GROUPS, S5_STATE, S5_CH), (2 * S5_CH) ** -0.5),
        's5_c_re': nrm((L, S5_GROUPS, S5_CH, S5_STATE), S5_STATE ** -0.5),
        's5_c_im': nrm((L, S5_GROUPS, S5_CH, S5_STATE), S5_STATE ** -0.5),
        's5_d': nrm((L, S5_GROUPS, S5_CH), 0.5),
        's5_glu_w': nrm((L, G, G), G ** -0.5),
        's5_glu_b': nrm((L, G), 0.02),
        'router_group_w': nrm((L, D, N_EXPERT_GROUPS), D ** -0.5),
        'router_group_b': nrm((L, N_EXPERT_GROUPS), 0.01),
        'router_expert_w': nrm((L, D, N_EXPERTS), D ** -0.5),
        'router_expert_b': nrm((L, N_EXPERTS), 0.01),
        'moe_w1': nrm((L, N_EXPERTS, D, D_EXPERT), BETA * D ** -0.5),
        'moe_w3': nrm((L, N_EXPERTS, D, D_EXPERT), BETA * D ** -0.5),
        'moe_w2': nrm((L, N_EXPERTS, D_EXPERT, D), BETA * D_EXPERT ** -0.5),
    }


def reference(x, c, w_ada, b_ada, ln_g, ln_b, w_in, w_out, conv_w, rwkv_mu, rwkv_w0, rwkv_w2,
              rwkv_a0, rwkv_a2, rwkv_g2, rwkv_kk, rwkv_ka, rwkv_rk, rwkv_gn_g, rwkv_gn_b,
              attn_sinks, rel_bias, s5_lambda_re, s5_lambda_im, s5_log_dt, s5_b_re, s5_b_im,
              s5_c_re, s5_c_im, s5_d, s5_glu_w, s5_glu_b, router_group_w, router_group_b,
              router_expert_w, router_expert_b, moe_w1, moe_w3, moe_w2):
    dt = x.dtype
    G = GROUP_WIDTH
    for l in range(DEPTH):
        mod = (jax.nn.silu(c) @ w_ada[l] + b_ada[l]).astype(F32)
        sh1, sc1, gt1, sh2, sc2, gt2 = jnp.split(mod, 6, axis=-1)
        h = ada_input(x, sh1, sc1)
        p = h @ w_in[l]
        y_conv = short_conv_mixer(p[..., 0:G], p[..., G:2 * G], p[..., 2 * G:3 * G], conv_w[l])
        y_rwkv = rwkv7_mixer(p[..., RW_OFF:ATT_OFF], rwkv_mu[l], rwkv_w0[l], rwkv_w2[l],
                             rwkv_a0[l], rwkv_a2[l], rwkv_g2[l], rwkv_kk[l], rwkv_ka[l],
                             rwkv_rk[l], rwkv_gn_g[l], rwkv_gn_b[l])
        y_att = swa_sink_attention(p[..., ATT_OFF:ATT_OFF + ATT_Q],
                                   p[..., ATT_OFF + ATT_Q:ATT_OFF + ATT_Q + ATT_KV],
                                   p[..., ATT_OFF + ATT_Q + ATT_KV:S5_OFF],
                                   attn_sinks[l], rel_bias)
        y_ssm = s5_mixer(p[..., S5_OFF:N_IN], s5_lambda_re[l], s5_lambda_im[l], s5_log_dt[l],
                         s5_b_re[l], s5_b_im[l], s5_c_re[l], s5_c_im[l], s5_d[l],
                         s5_glu_w[l], s5_glu_b[l])
        y_mix = jnp.concatenate([t.astype(dt) for t in (y_conv, y_rwkv, y_att, y_ssm)], axis=-1) @ w_out[l]
        x = layer_norm(ALPHA * x.astype(F32) + (1.0 + gt1)[:, None, :] * y_mix.astype(F32),
                       ln_g[l, 0], ln_b[l, 0]).astype(dt)
        h = ada_input(x, sh2, sc2)
        y_moe = hierarchical_moe(h, router_group_w[l], router_group_b[l], router_expert_w[l],
                                 router_expert_b[l], moe_w1[l], moe_w3[l], moe_w2[l])
        x = layer_norm(ALPHA * x.astype(F32) + (1.0 + gt2)[:, None, :] * y_moe.astype(F32),
                       ln_g[l, 1], ln_b[l, 1]).astype(dt)
    return x
```

```python
import functools
import math

import numpy as np
import jax
import jax.numpy as jnp
from jax import lax
from jax.experimental import pallas as pl
from jax.experimental.pallas import tpu as pltpu

F32 = jnp.float32
BF16 = jnp.bfloat16
I32 = jnp.int32

HEAD_DIM = 64
CONV_WIDTH = 3
RWKV_DECAY_RANK = 96
RWKV_A_RANK = 96
RWKV_GATE_RANK = 128
RWKV_GN_EPS = 64e-5
ATT_BLOCK = 128
WINDOW = 128
N_BUCKETS = 32
NEG_INF = -1e30
S5_CH = 16
S5_STATE = 64
N_EXPERT_GROUPS = 4
EXPERTS_PER_GROUP = 8
N_EXPERTS = N_EXPERT_GROUPS * EXPERTS_PER_GROUP
TOP_K = 2
LN_EPS = 1e-5

LANES = 128
SUBLANES = 8
WKV_CHUNK = 64
MOE_ROWS = 256
VMEM_LIMIT = 56 * 2 ** 20


def _cparams(sem):
    return pltpu.CompilerParams(dimension_semantics=sem, vmem_limit_bytes=VMEM_LIMIT)


def _dot(a, b):
    return jnp.dot(a.astype(BF16), b.astype(BF16), preferred_element_type=F32)


def _dot_nt(a, b):
    return lax.dot_general(a.astype(BF16), b.astype(BF16), (((1,), (1,)), ((), ())),
                           preferred_element_type=F32)


def _dot_tn(a, b):
    return jnp.dot(a.T.astype(BF16), b.astype(BF16), preferred_element_type=F32)


def _dot_split(x, e):
    hi = x.astype(BF16)
    lo = (x - hi.astype(F32)).astype(BF16)
    return jnp.dot(hi, e, preferred_element_type=F32) + jnp.dot(lo, e, preferred_element_type=F32)


def _sigmoid(x):
    return 1.0 / (1.0 + jnp.exp(-x))


def _layer_norm(x):
    mean = jnp.mean(x, axis=-1, keepdims=True)
    xc = x - mean
    var = jnp.mean(xc * xc, axis=-1, keepdims=True)
    return xc * lax.rsqrt(var + LN_EPS)


def _shift_rows(p, carry_row, n):
    row = lax.broadcasted_iota(I32, (p.shape[0], 1), 0)
    out = pltpu.roll(p, n, 0)
    for i in range(n):
        out = jnp.where(row == i, carry_row[SUBLANES - n + i:SUBLANES - n + i + 1, :], out)
    return out


def _mod_kernel(c_ref, w_ref, b_ref, o_ref):
    c = c_ref[...]
    a = c * _sigmoid(c)
    o_ref[0] = _dot(a, w_ref[0]) + b_ref[0]


def _modulation(c, w_ada, b_ada):
    depth, d, n = w_ada.shape
    bsz = c.shape[0]
    tn = 1536
    cp = jnp.zeros((SUBLANES, d), F32).at[:bsz].set(c)
    out = pl.pallas_call(
        _mod_kernel,
        grid=(depth, n // tn),
        in_specs=[pl.BlockSpec((SUBLANES, d), lambda l, j: (0, 0)),
                  pl.BlockSpec((1, d, tn), lambda l, j: (l, 0, j)),
                  pl.BlockSpec((1, 1, tn), lambda l, j: (l, 0, j))],
        out_specs=pl.BlockSpec((1, SUBLANES, tn), lambda l, j: (l, 0, j)),
        out_shape=jax.ShapeDtypeStruct((depth, SUBLANES, n), F32),
        compiler_params=_cparams(("parallel", "parallel")),
        name="adaln_modulation",
    )(cp, w_ada, b_ada.reshape(depth, 1, n))
    return out[:, :bsz]


def _adaln_kernel(x_ref, sh_ref, sc_ref, h_ref):
    h_ref[0] = (_layer_norm(x_ref[0]) * (1.0 + sc_ref[0]) + sh_ref[0]).astype(BF16)


def _adaln(x, shift, scale):
    bsz, s, d = x.shape
    tm = 512
    return pl.pallas_call(
        _adaln_kernel,
        grid=(bsz, s // tm),
        in_specs=[pl.BlockSpec((1, tm, d), lambda b, i: (b, i, 0)),
                  pl.BlockSpec((1, 1, d), lambda b, i: (b, 0, 0)),
                  pl.BlockSpec((1, 1, d), lambda b, i: (b, 0, 0))],
        out_specs=pl.BlockSpec((1, tm, d), lambda b, i: (b, i, 0)),
        out_shape=jax.ShapeDtypeStruct((bsz, s, d), BF16),
        compiler_params=_cparams(("parallel", "parallel")),
        name="adaln_input",
    )(x, shift[:, None, :], scale[:, None, :])


def _conv_kernel(h_ref, w_ref, cw_ref, y_ref, carry_ref):
    tm = h_ref.shape[1]
    g = y_ref.shape[2]

    @pl.when(pl.program_id(1) == 0)
    def _():
        carry_ref[...] = jnp.zeros_like(carry_ref)

    p = _dot(h_ref[0], w_ref[...])
    b_gate, c_gate, hh = p[:, :g], p[:, g:2 * g], p[:, 2 * g:]
    z = c_gate * hh
    carry = carry_ref[...]
    z1 = _shift_rows(z, carry, 1)
    z2 = _shift_rows(z, carry, 2)
    cw = cw_ref[...]
    out = cw[0:1] * z2 + cw[1:2] * z1 + cw[2:3] * z
    y_ref[0] = (b_gate * out).astype(BF16)
    carry_ref[...] = z[tm - SUBLANES:, :]


def _conv_mixer(h, w, conv_w):
    bsz, s, d = h.shape
    g = conv_w.shape[1]
    tm = 512
    return pl.pallas_call(
        _conv_kernel,
        grid=(bsz, s // tm),
        in_specs=[pl.BlockSpec((1, tm, d), lambda b, i: (b, i, 0)),
                  pl.BlockSpec((d, 3 * g), lambda b, i: (0, 0)),
                  pl.BlockSpec((CONV_WIDTH, g), lambda b, i: (0, 0))],
        out_specs=pl.BlockSpec((1, tm, g), lambda b, i: (b, i, 0)),
        out_shape=jax.ShapeDtypeStruct((bsz, s, g), BF16),
        scratch_shapes=[pltpu.VMEM((SUBLANES, g), F32)],
        compiler_params=_cparams(("parallel", "arbitrary")),
        name="conv_mixer",
    )(h, w, conv_w)


def _t5_bucket(rel):
    n = jnp.maximum(rel, 0)
    max_exact = N_BUCKETS // 2
    n_f = jnp.maximum(n, 1).astype(F32)
    large = max_exact + (jnp.log(n_f / max_exact) / math.log(WINDOW / max_exact)
                         * (N_BUCKETS - max_exact)).astype(I32)
    return jnp.where(n < max_exact, n, jnp.minimum(large, N_BUCKETS - 1))


def _att_kernel(sink_ref, h_ref, wq_ref, wkv_ref, bias_ref, y_ref, kvc_ref):
    tm = h_ref.shape[1]
    n_heads = bias_ref.shape[0]
    kvw = wkv_ref.shape[1] // 2
    n_kv = kvw // HEAD_DIM
    rep = n_heads // n_kv
    blk = ATT_BLOCK
    first_tile = pl.program_id(1) == 0

    @pl.when(first_tile)
    def _():
        kvc_ref[...] = jnp.zeros_like(kvc_ref)

    x = h_ref[0]
    q = _dot(x, wq_ref[...]) * (HEAD_DIM ** -0.5)
    kv = _dot(x, wkv_ref[...])
    kvext = jnp.concatenate([kvc_ref[...], kv], axis=0)
    kvc_ref[...] = kv[tm - blk:, :]
    col = lax.broadcasted_iota(I32, (blk, 2 * blk), 1)
    for j in range(tm // blk):
        qb = q[j * blk:(j + 1) * blk]
        kw = kvext[j * blk:j * blk + 2 * blk, :kvw]
        vw = kvext[j * blk:j * blk + 2 * blk, kvw:]
        outs = []
        for hh in range(n_heads):
            gi = hh // rep
            kg = kw[:, gi * HEAD_DIM:(gi + 1) * HEAD_DIM]
            vg = vw[:, gi * HEAD_DIM:(gi + 1) * HEAD_DIM]
            qh = qb[:, hh * HEAD_DIM:(hh + 1) * HEAD_DIM]
            sc = _dot_nt(qh, kg) + bias_ref[hh]
            if j == 0:
                sc = jnp.where(jnp.logical_and(first_tile, col < blk), NEG_INF, sc)
            sink = sink_ref[hh]
            m = jnp.maximum(jnp.max(sc, axis=-1, keepdims=True), sink)
            e = jnp.exp(sc - m)
            den = jnp.sum(e, axis=-1, keepdims=True) + jnp.exp(sink - m)
            outs.append(_dot(e / den, vg))
        y_ref[0, j * blk:(j + 1) * blk, :] = jnp.concatenate(outs, axis=1).astype(BF16)


def _att_mixer(h, wq, wkv, sinks, rel_bias):
    bsz, s, d = h.shape
    n_heads = sinks.shape[0]
    tm = 512
    qi = jnp.arange(ATT_BLOCK)[:, None]
    kj = jnp.arange(2 * ATT_BLOCK)[None, :]
    rel = qi + ATT_BLOCK - kj
    valid = (rel >= 0) & (rel < WINDOW)
    bias = jnp.transpose(rel_bias.astype(F32)[_t5_bucket(rel)], (2, 0, 1))
    bias = jnp.where(valid[None], bias, NEG_INF)
    return pl.pallas_call(
        _att_kernel,
        grid=(bsz, s // tm),
        in_specs=[pl.BlockSpec(memory_space=pltpu.SMEM),
                  pl.BlockSpec((1, tm, d), lambda b, i: (b, i, 0)),
                  pl.BlockSpec(wq.shape, lambda b, i: (0, 0)),
                  pl.BlockSpec(wkv.shape, lambda b, i: (0, 0)),
                  pl.BlockSpec(bias.shape, lambda b, i: (0, 0, 0))],
        out_specs=pl.BlockSpec((1, tm, wq.shape[1]), lambda b, i: (b, i, 0)),
        out_shape=jax.ShapeDtypeStruct((bsz, s, wq.shape[1]), BF16),
        scratch_shapes=[pltpu.VMEM((ATT_BLOCK, wkv.shape[1]), F32)],
        compiler_params=_cparams(("parallel", "arbitrary")),
        name="swa_mixer",
    )(sinks.astype(F32), h, wq, wkv, bias)


S5_GROUPS_PER_BLOCK = LANES // S5_CH


def _s5_tables(lam_re, lam_im, log_dt, b_re, b_im, c_re, c_im):
    n_groups, p = lam_re.shape
    lr, li = lam_re.astype(F32), lam_im.astype(F32)
    delta = jnp.exp(log_dt.astype(F32))[:, None]
    mag = jnp.exp(lr * delta)
    ab_re, ab_im = mag * jnp.cos(li * delta), mag * jnp.sin(li * delta)
    den = lr * lr + li * li
    z_re = ((ab_re - 1.0) * lr + ab_im * li) / den
    z_im = (ab_im * lr - (ab_re - 1.0) * li) / den
    br, bi = b_re.astype(F32), b_im.astype(F32)
    bb_re = z_re[..., None] * br - z_im[..., None] * bi
    bb_im = z_re[..., None] * bi + z_im[..., None] * br
    nblk = n_groups // S5_GROUPS_PER_BLOCK
    eye = jnp.eye(S5_GROUPS_PER_BLOCK, dtype=F32)

    def in_blocks(bb):
        bb = bb.reshape(nblk, S5_GROUPS_PER_BLOCK, p, S5_CH)
        return jnp.einsum('qgpc,gh->qgchp', bb, eye).reshape(nblk, LANES, S5_GROUPS_PER_BLOCK * p)

    def out_blocks(cc):
        cc = cc.astype(F32).reshape(nblk, S5_GROUPS_PER_BLOCK, S5_CH, p)
        return jnp.einsum('qgcp,gh->qgphc', cc, eye).reshape(nblk, S5_GROUPS_PER_BLOCK * p, LANES)

    def power(m):
        mg = jnp.exp(m * lr * delta)
        return (mg * jnp.cos(m * li * delta)).reshape(1, -1), (mg * jnp.sin(m * li * delta)).reshape(1, -1)

    row = jnp.arange(SUBLANES, dtype=F32)[:, None]
    tabs = []
    for sft in (1, 2, 4):
        pr, pi = power(float(sft))
        keep = row >= sft
        tabs += [jnp.where(keep, pr, 0.0), jnp.where(keep, pi, 0.0)]
    n_state = n_groups * p
    lrd = (lr * delta).reshape(1, n_state)
    lid = (li * delta).reshape(1, n_state)
    mg = jnp.exp((row + 1.0) * lrd)
    tabs += [mg * jnp.cos((row + 1.0) * lid), mg * jnp.sin((row + 1.0) * lid)]
    tables = jnp.stack(tabs, axis=0)
    return (in_blocks(bb_re).astype(BF16), in_blocks(bb_im).astype(BF16),
            out_blocks(c_re).astype(BF16), out_blocks(c_im).astype(BF16), tables)


def _s5_kernel(h_ref, w_ref, bre_ref, bim_ref, cre_ref, cim_ref, tab_ref, d_ref, gw_ref, gb_ref,
               y_ref, xr_ref, xi_ref, cr_ref, ci_ref):
    tm = h_ref.shape[1]
    nblk = bre_ref.shape[0]
    sw = bre_ref.shape[2]

    @pl.when(pl.program_id(1) == 0)
    def _():
        cr_ref[...] = jnp.zeros_like(cr_ref)
        ci_ref[...] = jnp.zeros_like(ci_ref)

    u = _dot(h_ref[0], w_ref[...])
    ub = u.astype(BF16)
    for q in range(nblk):
        uq = ub[:, q * LANES:(q + 1) * LANES]
        xr_ref[:, q * sw:(q + 1) * sw] = jnp.dot(uq, bre_ref[q], preferred_element_type=F32)
        xi_ref[:, q * sw:(q + 1) * sw] = jnp.dot(uq, bim_ref[q], preferred_element_type=F32)

    def tile(i, carry):
        cr, ci = carry
        rows = pl.ds(pl.multiple_of(i * SUBLANES, SUBLANES), SUBLANES)
        xr = xr_ref[rows, :]
        xi = xi_ref[rows, :]
        for k, sft in enumerate((1, 2, 4)):
            mr = tab_ref[2 * k]
            mi = tab_ref[2 * k + 1]
            sr = pltpu.roll(xr, sft, 0)
            si = pltpu.roll(xi, sft, 0)
            xr, xi = xr + mr * sr - mi * si, xi + mr * si + mi * sr
        pr = tab_ref[6]
        pi = tab_ref[7]
        xr, xi = xr + pr * cr - pi * ci, xi + pr * ci + pi * cr
        xr_ref[rows, :] = xr
        xi_ref[rows, :] = xi
        return xr[SUBLANES - 1:SUBLANES, :], xi[SUBLANES - 1:SUBLANES, :]

    cr, ci = lax.fori_loop(0, tm // SUBLANES, tile, (cr_ref[0:1, :], ci_ref[0:1, :]))
    cr_ref[0:1, :] = cr
    ci_ref[0:1, :] = ci

    ys = []
    for q in range(nblk):
        xr = xr_ref[:, q * sw:(q + 1) * sw].astype(BF16)
        xi = xi_ref[:, q * sw:(q + 1) * sw].astype(BF16)
        ys.append(jnp.dot(xr, cre_ref[q], preferred_element_type=F32)
                  - jnp.dot(xi, cim_ref[q], preferred_element_type=F32))
    y = jnp.concatenate(ys, axis=1) + d_ref[...] * u
    y = 0.5 * y * (1.0 + jnp.tanh(math.sqrt(2.0 / math.pi) * (y + 0.044715 * (y * y * y))))
    y_ref[0] = (y * _sigmoid(_dot(y, gw_ref[...]) + gb_ref[...])).astype(BF16)


def _s5_mixer(h, w, lam_re, lam_im, log_dt, b_re, b_im, c_re, c_im, d_skip, glu_w, glu_b):
    bsz, s, d = h.shape
    g = w.shape[1]
    tm = 256
    bre, bim, cre, cim, tables = _s5_tables(lam_re, lam_im, log_dt, b_re, b_im, c_re, c_im)
    n_state = tables.shape[2]
    full = lambda a: pl.BlockSpec(a.shape, lambda b, i: (0,) * a.ndim)
    dvec = d_skip.astype(F32).reshape(1, g)
    gw = glu_w.astype(BF16)
    gb = glu_b.astype(F32).reshape(1, g)
    return pl.pallas_call(
        _s5_kernel,
        grid=(bsz, s // tm),
        in_specs=[pl.BlockSpec((1, tm, d), lambda b, i: (b, i, 0)),
                  full(w), full(bre), full(bim), full(cre), full(cim), full(tables),
                  full(dvec), full(gw), full(gb)],
        out_specs=pl.BlockSpec((1, tm, g), lambda b, i: (b, i, 0)),
        out_shape=jax.ShapeDtypeStruct((bsz, s, g), BF16),
        scratch_shapes=[pltpu.VMEM((tm, n_state), F32), pltpu.VMEM((tm, n_state), F32),
                        pltpu.VMEM((SUBLANES, n_state), F32), pltpu.VMEM((SUBLANES, n_state), F32)],
        compiler_params=_cparams(("parallel", "arbitrary")),
        name="s5_mixer",
    )(h, w, bre, bim, cre, cim, tables, dvec, gw, gb)


def _rwkv_kernel(h_ref, wrkv_ref, wlo_ref, mu1_ref, mu2_ref, w0_ref, w2_ref, a0_ref, a2_ref, g2_ref,
                 kk_ref, ka_ref, rk_ref, gng_ref, gnb_ref, eblk_ref,
                 y_ref,
                 cp_ref, cl_ref, hs_ref, r_s, k_s, v_s, a_s, b_s, ld_s, y_s):
    tm = h_ref.shape[1]
    g = y_ref.shape[2]
    npair = g // LANES
    ch = WKV_CHUNK

    @pl.when(pl.program_id(1) == 0)
    def _():
        cp_ref[...] = jnp.zeros_like(cp_ref)
        cl_ref[...] = jnp.zeros_like(cl_ref)
        hs_ref[...] = jnp.zeros_like(hs_ref)

    x = h_ref[0]
    p = _dot(x, wrkv_ref[...])
    plo = _dot(x, wlo_ref[...])
    pprev = _shift_rows(p, cp_ref[...], 1)
    lprev = _shift_rows(plo, cl_ref[...], 1)
    cp_ref[...] = p[tm - SUBLANES:, :]
    cl_ref[...] = plo[tm - SUBLANES:, :]
    p = p + (pprev - p) * mu1_ref[...]
    plo = plo + (lprev - plo) * mu2_ref[...]
    r, k, v = p[:, :g], p[:, g:2 * g], p[:, 2 * g:]
    w_lo, a_lo, g_lo = plo[:, :LANES], plo[:, LANES:2 * LANES], plo[:, 2 * LANES:]
    wraw = w0_ref[...] + _dot(jnp.tanh(w_lo), w2_ref[...])
    nz = -wraw
    softplus = jnp.maximum(nz, 0.0) + jnp.log(1.0 + jnp.exp(-jnp.abs(nz)))
    w = -softplus - 0.5
    ld_s[...] = -jnp.exp(w)
    a = _sigmoid(a0_ref[...] + _dot(a_lo, a2_ref[...]))
    gate = _dot(_sigmoid(g_lo), g2_ref[...])
    eblk = eblk_ref[...]
    kk = k * kk_ref[...]
    kk = kk / jnp.maximum(jnp.sqrt(_dot_split(kk * kk, eblk)), 1e-12)
    k = k * (1.0 + (a - 1.0) * ka_ref[...])
    r_s[...] = r
    k_s[...] = k
    v_s[...] = v
    a_s[...] = -kk
    b_s[...] = kk * a

    lane = lax.broadcasted_iota(I32, (1, LANES), 1)
    m0 = (lane < HEAD_DIM).astype(F32)
    m1 = 1.0 - m0
    ri = lax.broadcasted_iota(I32, (2 * ch, 2 * ch), 0)
    ci = lax.broadcasted_iota(I32, (2 * ch, 2 * ch), 1)
    same = (ri < ch) == (ci < ch)
    rloc = jnp.bitwise_and(ri, ch - 1)
    cloc = jnp.bitwise_and(ci, ch - 1)
    strict = jnp.where(jnp.logical_and(same, cloc < rloc), 1.0, 0.0)
    incl = jnp.where(jnp.logical_and(same, cloc <= rloc), 1.0, 0.0)
    eye = jnp.where(ri == ci, 1.0, 0.0)
    tri = jnp.where(lax.broadcasted_iota(I32, (ch, ch), 1) <= lax.broadcasted_iota(I32, (ch, ch), 0),
                    1.0, 0.0).astype(BF16)

    def bd(t):
        return jnp.concatenate([t * m0, t * m1], axis=0)

    def chunk(c, _):
        rows = pl.ds(pl.multiple_of(c * ch, ch), ch)
        ld = ld_s[rows, :]
        ld_hi = ld.astype(BF16)
        ld_lo = (ld - ld_hi.astype(F32)).astype(BF16)
        cum = (jnp.dot(tri, ld_hi, preferred_element_type=F32)
               + jnp.dot(tri, ld_lo, preferred_element_type=F32))
        gam = jnp.exp(cum)
        gprev = jnp.exp(cum - ld)
        ginv = jnp.exp(-cum)
        g_last = gam[ch - 1:ch, :]
        at_all = a_s[rows, :] * gprev
        rt_all = r_s[rows, :] * gam
        bt_all = b_s[rows, :] * ginv
        kt_all = k_s[rows, :] * ginv
        v_all = v_s[rows, :]
        for q in range(npair):
            sl = slice(q * LANES, (q + 1) * LANES)
            at, rt, bt, kt, vv, gl = at_all[:, sl], rt_all[:, sl], bt_all[:, sl], kt_all[:, sl], v_all[:, sl], g_last[:, sl]
            at_bd, rt_bd, v_bd = bd(at), bd(rt), bd(vv)
            bh_bd, kh_bd = bd(bt * gl), bd(kt * gl)
            gmat = _dot_nt(jnp.concatenate([at_bd, rt_bd], axis=0),
                           jnp.concatenate([bt, bt, kt, kt], axis=0))
            n_ab = gmat[:2 * ch, :2 * ch] * strict
            a_ak = gmat[:2 * ch, 2 * ch:] * strict
            m_rb = gmat[2 * ch:, :2 * ch] * incl
            m_rk = gmat[2 * ch:, 2 * ch:] * incl
            tinv = eye + n_ab
            npow = n_ab
            for _ in range(5):
                npow = _dot(npow, npow)
                tinv = tinv + _dot(tinv, npow)
            wu = _dot(tinv, jnp.concatenate([at_bd, _dot(a_ak, v_bd)], axis=1))
            wmat, u0 = wu[:, :2 * ch], wu[:, 2 * ch:]
            pmat = eye * gl + _dot_tn(bh_bd, wmat)
            qmat = _dot_tn(bh_bd, u0) + _dot_tn(kh_bd, v_bd)
            ry = rt_bd + _dot(m_rb, wmat)
            y0 = _dot(m_rb, u0) + _dot(m_rk, v_bd)
            hst = hs_ref[q]
            yy = _dot(ry, hst) + y0
            hs_ref[q] = _dot(pmat, hst) + qmat
            y_s[rows, sl] = yy[:ch] + yy[ch:]
        return 0

    lax.fori_loop(0, tm // ch, chunk, 0)

    y = y_s[...]
    inv_n = 1.0 / HEAD_DIM
    mean = _dot_split(y, eblk) * inv_n
    yc = y - mean
    var = _dot_split(yc * yc, eblk) * inv_n
    yn = yc * lax.rsqrt(var + RWKV_GN_EPS) * gng_ref[...] + gnb_ref[...]
    r = r_s[...]
    k = k_s[...]
    v = v_s[...]
    bonus = _dot_split(r * k * rk_ref[...], eblk) * v
    y_ref[0] = ((yn + bonus) * gate).astype(BF16)


def _rwkv_mixer(h, w_rkv, w_lora, mu, w0, w2, a0, a2, g2, k_k, k_a, r_k, gn_g, gn_b):
    bsz, s, d = h.shape
    g = w0.shape[0]
    tm = 256
    row = lambda t: t.astype(F32).reshape(1, -1)
    pad_rows = lambda t: jnp.zeros((LANES, g), F32).at[:t.shape[0]].set(t.astype(F32)).astype(BF16)
    mu1 = row(mu[:3 * g])
    mu2 = jnp.concatenate([
        jnp.zeros((LANES,), F32).at[:RWKV_DECAY_RANK].set(mu[3 * g:3 * g + RWKV_DECAY_RANK]),
        jnp.zeros((LANES,), F32).at[:RWKV_A_RANK].set(mu[3 * g + RWKV_DECAY_RANK:3 * g + RWKV_DECAY_RANK + RWKV_A_RANK]),
        mu[3 * g + RWKV_DECAY_RANK + RWKV_A_RANK:]]).reshape(1, -1)
    head = np.arange(g) // HEAD_DIM
    eblk = jnp.asarray(head[:, None] == head[None, :], BF16)
    args = (h, w_rkv, w_lora, mu1, mu2, row(w0), pad_rows(w2), row(a0), pad_rows(a2), g2.astype(BF16),
            row(k_k), row(k_a), row(r_k), row(gn_g), row(gn_b), eblk)
    full = lambda a: pl.BlockSpec(a.shape, lambda b, i: (0,) * a.ndim)
    return pl.pallas_call(
        _rwkv_kernel,
        grid=(bsz, s // tm),
        in_specs=[pl.BlockSpec((1, tm, d), lambda b, i: (b, i, 0))] + [full(a) for a in args[1:]],
        out_specs=pl.BlockSpec((1, tm, g), lambda b, i: (b, i, 0)),
        out_shape=jax.ShapeDtypeStruct((bsz, s, g), BF16),
        scratch_shapes=[pltpu.VMEM((SUBLANES, 3 * g), F32), pltpu.VMEM((SUBLANES, 3 * LANES), F32),
                        pltpu.VMEM((g // LANES, 2 * WKV_CHUNK, LANES), F32)]
                       + [pltpu.VMEM((tm, g), F32) for _ in range(7)],
        compiler_params=_cparams(("parallel", "arbitrary")),
        name="rwkv7_mixer",
    )(*args)


def _mixout_kernel(alpha, ya_ref, yb_ref, yc_ref, yd_ref, wo_ref, x_ref, gt_ref, lng_ref, lnb_ref,
                   sh_ref, sc_ref, wr_ref, br_ref, x1_ref, h2_ref, lg_ref):
    g = ya_ref.shape[2]
    y = (jnp.dot(ya_ref[0], wo_ref[0:g, :], preferred_element_type=F32)
         + jnp.dot(yb_ref[0], wo_ref[g:2 * g, :], preferred_element_type=F32)
         + jnp.dot(yc_ref[0], wo_ref[2 * g:3 * g, :], preferred_element_type=F32)
         + jnp.dot(yd_ref[0], wo_ref[3 * g:, :], preferred_element_type=F32))
    x1 = _layer_norm(alpha * x_ref[0] + (1.0 + gt_ref[0]) * y) * lng_ref[...] + lnb_ref[...]
    x1_ref[0] = x1
    h2 = _layer_norm(x1) * (1.0 + sc_ref[0]) + sh_ref[0]
    h2_ref[0] = h2
    lg_ref[0] = _dot(h2, wr_ref[...]) + br_ref[...]


def _mix_out(alpha, ys, w_out, x, gate, ln_g, ln_b, shift2, scale2, w_router, b_router):
    bsz, s, d = x.shape
    g = ys[0].shape[2]
    tm = 256
    tok = lambda w: pl.BlockSpec((1, tm, w), lambda b, i: (b, i, 0))
    per_b = pl.BlockSpec((1, 1, d), lambda b, i: (b, 0, 0))
    full = lambda a: pl.BlockSpec(a.shape, lambda b, i: (0,) * a.ndim)
    row = lambda t: t.astype(F32).reshape(1, -1)
    args = (*ys, w_out, x, gate[:, None, :], row(ln_g), row(ln_b), shift2[:, None, :], scale2[:, None, :],
            w_router, b_router)
    return pl.pallas_call(
        functools.partial(_mixout_kernel, alpha),
        grid=(bsz, s // tm),
        in_specs=[tok(g)] * 4 + [full(w_out), tok(d), per_b, full(args[7]), full(args[8]), per_b, per_b,
                                 full(w_router), full(b_router)],
        out_specs=[tok(d), tok(d), tok(LANES)],
        out_shape=[jax.ShapeDtypeStruct((bsz, s, d), F32), jax.ShapeDtypeStruct((bsz, s, d), F32),
                   jax.ShapeDtypeStruct((bsz, s, LANES), F32)],
        compiler_params=_cparams(("parallel", "parallel")),
        name="mix_out_ln_router",
    )(*args)


def _route_kernel(lg_ref, out_ref, cnt_ref, carry_ref):
    tm = lg_ref.shape[0]

    @pl.when(pl.program_id(0) == 0)
    def _():
        carry_ref[...] = jnp.zeros_like(carry_ref)

    lg = lg_ref[...]
    lane = lax.broadcasted_iota(I32, (tm, LANES), 1)
    lane_f = lane.astype(F32)

    def top1(vals, mask):
        mv = jnp.where(mask, vals, -jnp.inf)
        m = jnp.max(mv, axis=-1, keepdims=True)
        idx = jnp.min(jnp.where(jnp.logical_and(mask, mv == m), lane_f, float(LANES)), axis=-1, keepdims=True)
        return m, idx.astype(I32)

    gmask = lane < N_EXPERT_GROUPS
    gm, gidx = top1(lg, gmask)
    g_val = 1.0 / jnp.sum(jnp.where(gmask, jnp.exp(lg - gm), 0.0), axis=-1, keepdims=True)
    elo = N_EXPERT_GROUPS + gidx * EXPERTS_PER_GROUP
    emask = jnp.logical_and(lane >= elo, lane < elo + EXPERTS_PER_GROUP)
    m1, i1 = top1(lg, emask)
    m2, i2 = top1(lg, jnp.logical_and(emask, lane != i1))
    e21 = jnp.exp(m2 - m1)
    w1 = g_val / (1.0 + e21)
    w2 = g_val * e21 / (1.0 + e21)
    e1 = i1 - N_EXPERT_GROUPS
    e2 = i2 - N_EXPERT_GROUPS
    oh1 = (lane == e1)
    oh2 = (lane == e2)
    ohs = jnp.where(jnp.logical_or(oh1, oh2), 1.0, 0.0)
    ri = lax.broadcasted_iota(I32, (tm, tm), 0)
    ci = lax.broadcasted_iota(I32, (tm, tm), 1)
    tri = jnp.where(ci < ri, 1.0, 0.0).astype(BF16)
    before = jnp.dot(tri, ohs.astype(BF16), preferred_element_type=F32) + carry_ref[0:1, :]
    rank1 = jnp.sum(jnp.where(oh1, before, 0.0), axis=-1, keepdims=True).astype(I32)
    rank2 = jnp.sum(jnp.where(oh2, before, 0.0), axis=-1, keepdims=True).astype(I32)
    total = carry_ref[0:1, :] + jnp.sum(ohs, axis=0, keepdims=True)
    carry_ref[0:1, :] = total
    cnt_ref[...] = jnp.broadcast_to(total, cnt_ref.shape).astype(I32)
    packed = jnp.where(lane == 0, e1, 0)
    packed = jnp.where(lane == 1, e2, packed)
    packed = jnp.where(lane == 2, pltpu.bitcast(jnp.broadcast_to(w1, (tm, LANES)), I32), packed)
    packed = jnp.where(lane == 3, pltpu.bitcast(jnp.broadcast_to(w2, (tm, LANES)), I32), packed)
    packed = jnp.where(lane == 4, rank1, packed)
    packed = jnp.where(lane == 5, rank2, packed)
    out_ref[...] = packed


def _route(logits):
    t = logits.shape[0]
    tm = 512
    return pl.pallas_call(
        _route_kernel,
        grid=(t // tm,),
        in_specs=[pl.BlockSpec((tm, LANES), lambda i: (i, 0))],
        out_specs=[pl.BlockSpec((tm, LANES), lambda i: (i, 0)),
                   pl.BlockSpec((SUBLANES, LANES), lambda i: (0, 0))],
        out_shape=[jax.ShapeDtypeStruct((t, LANES), I32), jax.ShapeDtypeStruct((SUBLANES, LANES), I32)],
        scratch_shapes=[pltpu.VMEM((SUBLANES, LANES), F32)],
        compiler_params=_cparams(("arbitrary",)),
        name="moe_route",
    )(logits)


def _expert_kernel(layer, be_ref, rt_ref, h_hbm, w1_ref, w3_ref, w2_ref, o_ref,
                   xbuf, sem, w1b, w3b, w2b):
    del layer
    b = pl.program_id(0)
    rows = xbuf.shape[0]
    base = b * rows

    def row_copy(r, tok):
        return pltpu.make_async_copy(h_hbm.at[pl.ds(tok, 1), :], xbuf.at[pl.ds(r, 1), :], sem)

    def issue(r, c):
        row_copy(r, rt_ref[base + r]).start()
        return c

    lax.fori_loop(0, rows, issue, 0)

    prev = be_ref[jnp.maximum(b - 1, 0)]

    @pl.when(jnp.logical_or(b == 0, be_ref[b] != prev))
    def _():
        w1b[...] = w1_ref[0, 0].astype(BF16)
        w3b[...] = w3_ref[0, 0].astype(BF16)
        w2b[...] = w2_ref[0, 0].astype(BF16)

    def wait(r, c):
        row_copy(r, 0).wait()
        return c

    lax.fori_loop(0, rows, wait, 0)

    x = xbuf[...].astype(BF16)
    a = jnp.dot(x, w1b[...], preferred_element_type=F32)
    gte = jnp.dot(x, w3b[...], preferred_element_type=F32)
    mid = (a * _sigmoid(a)) * gte
    o_ref[...] = jnp.dot(mid.astype(BF16), w2b[...], preferred_element_type=F32)


def _expert_ffn(layer, h2, blk_expert, row_tok, w1, w3, w2):
    t, d = h2.shape
    de = w1.shape[3]
    nr = row_tok.shape[0]
    nb = nr // MOE_ROWS
    grid_spec = pltpu.PrefetchScalarGridSpec(
        num_scalar_prefetch=2,
        grid=(nb,),
        in_specs=[pl.BlockSpec(memory_space=pl.ANY),
                  pl.BlockSpec((1, 1, d, de), lambda b, be, rt: (layer, be[b], 0, 0)),
                  pl.BlockSpec((1, 1, d, de), lambda b, be, rt: (layer, be[b], 0, 0)),
                  pl.BlockSpec((1, 1, de, d), lambda b, be, rt: (layer, be[b], 0, 0))],
        out_specs=pl.BlockSpec((MOE_ROWS, d), lambda b, be, rt: (b, 0)),
        scratch_shapes=[pltpu.VMEM((MOE_ROWS, d), F32), pltpu.SemaphoreType.DMA(()),
                        pltpu.VMEM((d, de), BF16), pltpu.VMEM((d, de), BF16), pltpu.VMEM((de, d), BF16)],
    )
    return pl.pallas_call(
        functools.partial(_expert_kernel, layer),
        grid_spec=grid_spec,
        out_shape=jax.ShapeDtypeStruct((nr, d), F32),
        compiler_params=_cparams(("arbitrary",)),
        name="moe_expert_ffn",
    )(blk_expert, row_tok, h2, w1, w3, w2)


def _combine_kernel(alpha, dst_ref, yr_hbm, wt_ref, x_ref, gt_ref, lng_ref, lnb_ref, o_ref, ybuf, sem):
    tm = x_ref.shape[0]
    base = pl.program_id(0) * tm

    def row_copy(k, r, src):
        return pltpu.make_async_copy(yr_hbm.at[pl.ds(src, 1), :], ybuf.at[k, pl.ds(r, 1), :], sem)

    def issue(r, c):
        row_copy(0, r, dst_ref[2 * (base + r)]).start()
        row_copy(1, r, dst_ref[2 * (base + r) + 1]).start()
        return c

    lax.fori_loop(0, tm, issue, 0)

    def wait(r, c):
        row_copy(0, r, 0).wait()
        row_copy(1, r, 0).wait()
        return c

    lax.fori_loop(0, tm, wait, 0)
    wt = wt_ref[...]
    y = ybuf[0] * wt[:, 0:1] + ybuf[1] * wt[:, 1:2]
    o_ref[...] = _layer_norm(alpha * x_ref[...] + (1.0 + gt_ref[0]) * y) * lng_ref[...] + lnb_ref[...]


def _combine(alpha, dest, yr, wts, x1, gate, ln_g, ln_b, seq):
    t, d = x1.shape
    tm = 256
    per_seq = seq // tm
    grid_spec = pltpu.PrefetchScalarGridSpec(
        num_scalar_prefetch=1,
        grid=(t // tm,),
        in_specs=[pl.BlockSpec(memory_space=pl.ANY),
                  pl.BlockSpec((tm, LANES), lambda i, dst: (i, 0)),
                  pl.BlockSpec((tm, d), lambda i, dst: (i, 0)),
                  pl.BlockSpec((1, 1, d), lambda i, dst: (i // per_seq, 0, 0)),
                  pl.BlockSpec((1, d), lambda i, dst: (0, 0)),
                  pl.BlockSpec((1, d), lambda i, dst: (0, 0))],
        out_specs=pl.BlockSpec((tm, d), lambda i, dst: (i, 0)),
        scratch_shapes=[pltpu.VMEM((TOP_K, tm, d), F32), pltpu.SemaphoreType.DMA(())],
    )
    return pl.pallas_call(
        functools.partial(_combine_kernel, alpha),
        grid_spec=grid_spec,
        out_shape=jax.ShapeDtypeStruct((t, d), F32),
        compiler_params=_cparams(("arbitrary",)),
        name="moe_combine_ln",
    )(dest, yr, wts, x1, gate[:, None, :], ln_g.astype(F32).reshape(1, d), ln_b.astype(F32).reshape(1, d))


def _moe(layer, alpha, h2, logits, x1, gate, ln_g, ln_b, w1, w3, w2):
    bsz, s, d = x1.shape
    t = bsz * s
    packed, counts = _route(logits.reshape(t, LANES))
    eid = packed[:, 0:TOP_K]
    wts = lax.bitcast_convert_type(packed, F32)[:, 0:LANES]
    wts = jnp.concatenate([wts[:, 2:4], jnp.zeros((t, LANES - TOP_K), F32)], axis=1)
    rank = packed[:, 4:4 + TOP_K]
    cnt = counts[0, :N_EXPERTS]
    n_assign = t * TOP_K
    nb = (n_assign + N_EXPERTS * (MOE_ROWS - 1) + MOE_ROWS - 1) // MOE_ROWS
    nr = nb * MOE_ROWS
    padded = ((cnt + MOE_ROWS - 1) // MOE_ROWS) * MOE_ROWS
    pad_end = jnp.cumsum(padded)
    pad_start = pad_end - padded
    dest = (pad_start[eid] + rank).reshape(n_assign)
    tok = jnp.repeat(jnp.arange(t, dtype=I32), TOP_K)
    row_tok = jnp.zeros((nr,), I32).at[dest].set(tok)
    blk_expert = jnp.minimum(jnp.searchsorted(pad_end, jnp.arange(nb, dtype=I32) * MOE_ROWS, side='right'),
                             N_EXPERTS - 1).astype(I32)
    yr = _expert_ffn(layer, h2.reshape(t, d), blk_expert, row_tok, w1, w3, w2)
    x2 = _combine(alpha, dest.astype(I32), yr, wts, x1.reshape(t, d), gate, ln_g, ln_b, s)
    return x2.reshape(bsz, s, d)


def kernel(x, c, w_ada, b_ada, ln_g, ln_b, w_in, w_out, conv_w, rwkv_mu, rwkv_w0, rwkv_w2, rwkv_a0, rwkv_a2, rwkv_g2, rwkv_kk, rwkv_ka, rwkv_rk, rwkv_gn_g, rwkv_gn_b, attn_sinks, rel_bias, s5_lambda_re, s5_lambda_im, s5_log_dt, s5_b_re, s5_b_im, s5_c_re, s5_c_im, s5_d, s5_glu_w, s5_glu_b, router_group_w, router_group_b, router_expert_w, router_expert_b, moe_w1, moe_w3, moe_w2):
    depth = w_ada.shape[0]
    d = x.shape[-1]
    g = d // 4
    alpha = (2 * depth) ** 0.25
    n_heads = g // HEAD_DIM
    att_kv = max(1, n_heads // 4) * HEAD_DIM
    rw_off = 3 * g
    lora = RWKV_DECAY_RANK + RWKV_A_RANK + RWKV_GATE_RANK
    att_off = rw_off + 3 * g + lora
    s5_off = att_off + g + 2 * att_kv

    mod = _modulation(c, w_ada, b_ada)
    for l in range(depth):
        sh1, sc1, gt1, sh2, sc2, gt2 = jnp.split(mod[l], 6, axis=-1)
        wl = w_in[l]
        w_conv = wl[:, :rw_off].astype(BF16)
        w_rkv = wl[:, rw_off:rw_off + 3 * g].astype(BF16)
        lo = rw_off + 3 * g
        zcol = lambda n: jnp.zeros((d, n), F32)
        w_lora = jnp.concatenate([
            wl[:, lo:lo + RWKV_DECAY_RANK], zcol(LANES - RWKV_DECAY_RANK),
            wl[:, lo + RWKV_DECAY_RANK:lo + RWKV_DECAY_RANK + RWKV_A_RANK], zcol(LANES - RWKV_A_RANK),
            wl[:, lo + RWKV_DECAY_RANK + RWKV_A_RANK:att_off]], axis=1).astype(BF16)
        w_q = wl[:, att_off:att_off + g].astype(BF16)
        w_kv = wl[:, att_off + g:s5_off].astype(BF16)
        w_s5 = wl[:, s5_off:].astype(BF16)

        h = _adaln(x, sh1, sc1)
        y_conv = _conv_mixer(h, w_conv, conv_w[l].astype(F32))
        y_rwkv = _rwkv_mixer(h, w_rkv, w_lora, rwkv_mu[l], rwkv_w0[l], rwkv_w2[l], rwkv_a0[l], rwkv_a2[l],
                             rwkv_g2[l], rwkv_kk[l], rwkv_ka[l], rwkv_rk[l], rwkv_gn_g[l], rwkv_gn_b[l])
        y_att = _att_mixer(h, w_q, w_kv, attn_sinks[l], rel_bias)
        y_ssm = _s5_mixer(h, w_s5, s5_lambda_re[l], s5_lambda_im[l], s5_log_dt[l], s5_b_re[l], s5_b_im[l],
                          s5_c_re[l], s5_c_im[l], s5_d[l], s5_glu_w[l], s5_glu_b[l])
        w_router = jnp.zeros((d, LANES), F32)
        w_router = w_router.at[:, :N_EXPERT_GROUPS].set(router_group_w[l])
        w_router = w_router.at[:, N_EXPERT_GROUPS:N_EXPERT_GROUPS + N_EXPERTS].set(router_expert_w[l]).astype(BF16)
        b_router = jnp.zeros((1, LANES), F32)
        b_router = b_router.at[0, :N_EXPERT_GROUPS].set(router_group_b[l])
        b_router = b_router.at[0, N_EXPERT_GROUPS:N_EXPERT_GROUPS + N_EXPERTS].set(router_expert_b[l])
        x1, h2, logits = _mix_out(alpha, (y_conv, y_rwkv, y_att, y_ssm), w_out[l].astype(BF16), x, gt1,
                                  ln_g[l, 0], ln_b[l, 0], sh2, sc2, w_router, b_router)
        x = _moe(l, alpha, h2, logits, x1, gt2, ln_g[l, 1], ln_b[l, 1], moe_w1, moe_w3, moe_w2)
    return x
```

```python
import functools
import math

import numpy as np
import jax
import jax.numpy as jnp
from jax import lax
from jax.experimental import pallas as pl
from jax.experimental.pallas import tpu as pltpu

F32 = jnp.float32
BF16 = jnp.bfloat16
I32 = jnp.int32

HEAD_DIM = 64
CONV_WIDTH = 3
RWKV_DECAY_RANK = 96
RWKV_A_RANK = 96
RWKV_GATE_RANK = 128
RWKV_GN_EPS = 64e-5
ATT_BLOCK = 128
WINDOW = 128
N_BUCKETS = 32
NEG_INF = -1e30
S5_CH = 16
S5_STATE = 64
N_EXPERT_GROUPS = 4
EXPERTS_PER_GROUP = 8
N_EXPERTS = N_EXPERT_GROUPS * EXPERTS_PER_GROUP
TOP_K = 2
LN_EPS = 1e-5

LANES = 128
SUBLANES = 8
WKV_CHUNK = 64
MOE_ROWS = 256
VMEM_LIMIT = 56 * 2 ** 20


def _cparams(sem, flags=None):
    return pltpu.CompilerParams(dimension_semantics=sem, vmem_limit_bytes=VMEM_LIMIT, flags=flags)


def _dot(a, b):
    return jnp.dot(a.astype(BF16), b.astype(BF16), preferred_element_type=F32)


def _dot_nt(a, b):
    return lax.dot_general(a.astype(BF16), b.astype(BF16), (((1,), (1,)), ((), ())),
                           preferred_element_type=F32)


def _dot_tn(a, b):
    return jnp.dot(a.T.astype(BF16), b.astype(BF16), preferred_element_type=F32)


def _dot_split(x, e):
    hi = x.astype(BF16)
    lo = (x - hi.astype(F32)).astype(BF16)
    return jnp.dot(hi, e, preferred_element_type=F32) + jnp.dot(lo, e, preferred_element_type=F32)


def _sigmoid(x):
    return 1.0 / (1.0 + jnp.exp(-x))


def _layer_norm(x):
    mean = jnp.mean(x, axis=-1, keepdims=True)
    xc = x - mean
    var = jnp.mean(xc * xc, axis=-1, keepdims=True)
    return xc * lax.rsqrt(var + LN_EPS)


def _shift_rows(p, carry_row, n):
    row = lax.broadcasted_iota(I32, (p.shape[0], 1), 0)
    out = pltpu.roll(p, n, 0)
    for i in range(n):
        out = jnp.where(row == i, carry_row[SUBLANES - n + i:SUBLANES - n + i + 1, :], out)
    return out


def _mod_kernel(c_ref, w_ref, b_ref, o_ref):
    c = c_ref[...]
    a = c * _sigmoid(c)
    o_ref[0] = _dot(a, w_ref[0]) + b_ref[0]


def _modulation(c, w_ada, b_ada):
    depth, d, n = w_ada.shape
    bsz = c.shape[0]
    tn = 1536
    cp = jnp.zeros((SUBLANES, d), F32).at[:bsz].set(c)
    out = pl.pallas_call(
        _mod_kernel,
        grid=(depth, n // tn),
        in_specs=[pl.BlockSpec((SUBLANES, d), lambda l, j: (0, 0)),
                  pl.BlockSpec((1, d, tn), lambda l, j: (l, 0, j)),
                  pl.BlockSpec((1, 1, tn), lambda l, j: (l, 0, j))],
        out_specs=pl.BlockSpec((1, SUBLANES, tn), lambda l, j: (l, 0, j)),
        out_shape=jax.ShapeDtypeStruct((depth, SUBLANES, n), F32),
        compiler_params=_cparams(("parallel", "parallel")),
        name="adaln_modulation",
    )(cp, w_ada, b_ada.reshape(depth, 1, n))
    return out[:, :bsz]


def _adaln_kernel(x_ref, sh_ref, sc_ref, h_ref):
    h_ref[0] = (_layer_norm(x_ref[0]) * (1.0 + sc_ref[0]) + sh_ref[0]).astype(BF16)


def _adaln(x, shift, scale):
    bsz, s, d = x.shape
    tm = 512
    return pl.pallas_call(
        _adaln_kernel,
        grid=(bsz, s // tm),
        in_specs=[pl.BlockSpec((1, tm, d), lambda b, i: (b, i, 0)),
                  pl.BlockSpec((1, 1, d), lambda b, i: (b, 0, 0)),
                  pl.BlockSpec((1, 1, d), lambda b, i: (b, 0, 0))],
        out_specs=pl.BlockSpec((1, tm, d), lambda b, i: (b, i, 0)),
        out_shape=jax.ShapeDtypeStruct((bsz, s, d), BF16),
        compiler_params=_cparams(("parallel", "parallel")),
        name="adaln_input",
    )(x, shift[:, None, :], scale[:, None, :])


def _conv_kernel(h_ref, w_ref, cw_ref, y_ref, carry_ref):
    tm = h_ref.shape[1]
    g = y_ref.shape[2]

    @pl.when(pl.program_id(1) == 0)
    def _():
        carry_ref[...] = jnp.zeros_like(carry_ref)

    p = _dot(h_ref[0], w_ref[...])
    b_gate, c_gate, hh = p[:, :g], p[:, g:2 * g], p[:, 2 * g:]
    z = c_gate * hh
    carry = carry_ref[...]
    z1 = _shift_rows(z, carry, 1)
    z2 = _shift_rows(z, carry, 2)
    cw = cw_ref[...]
    out = cw[0:1] * z2 + cw[1:2] * z1 + cw[2:3] * z
    y_ref[0] = (b_gate * out).astype(BF16)
    carry_ref[...] = z[tm - SUBLANES:, :]


def _conv_mixer(h, w, conv_w):
    bsz, s, d = h.shape
    g = conv_w.shape[1]
    tm = 512
    return pl.pallas_call(
        _conv_kernel,
        grid=(bsz, s // tm),
        in_specs=[pl.BlockSpec((1, tm, d), lambda b, i: (b, i, 0)),
                  pl.BlockSpec((d, 3 * g), lambda b, i: (0, 0)),
                  pl.BlockSpec((CONV_WIDTH, g), lambda b, i: (0, 0))],
        out_specs=pl.BlockSpec((1, tm, g), lambda b, i: (b, i, 0)),
        out_shape=jax.ShapeDtypeStruct((bsz, s, g), BF16),
        scratch_shapes=[pltpu.VMEM((SUBLANES, g), F32)],
        compiler_params=_cparams(("parallel", "arbitrary")),
        name="conv_mixer",
    )(h, w, conv_w)


def _t5_bucket(rel):
    n = jnp.maximum(rel, 0)
    max_exact = N_BUCKETS // 2
    n_f = jnp.maximum(n, 1).astype(F32)
    large = max_exact + (jnp.log(n_f / max_exact) / math.log(WINDOW / max_exact)
                         * (N_BUCKETS - max_exact)).astype(I32)
    return jnp.where(n < max_exact, n, jnp.minimum(large, N_BUCKETS - 1))


def _att_kernel(sink_ref, h_ref, wq_ref, wkv_ref, bias_ref, y_ref, kvc_ref):
    tm = h_ref.shape[1]
    n_heads = bias_ref.shape[0]
    kvw = wkv_ref.shape[1] // 2
    n_kv = kvw // HEAD_DIM
    rep = n_heads // n_kv
    blk = ATT_BLOCK
    first_tile = pl.program_id(1) == 0

    @pl.when(first_tile)
    def _():
        kvc_ref[...] = jnp.zeros_like(kvc_ref)

    x = h_ref[0]
    q = _dot(x, wq_ref[...]) * (HEAD_DIM ** -0.5)
    kv = _dot(x, wkv_ref[...])
    kvext = jnp.concatenate([kvc_ref[...], kv], axis=0)
    kvc_ref[...] = kv[tm - blk:, :]
    col = lax.broadcasted_iota(I32, (blk, 2 * blk), 1)
    qb16 = q.astype(BF16)
    kv16 = kvext.astype(BF16)

    def scores(j):
        qb = qb16[j * blk:(j + 1) * blk]
        kw = kv16[j * blk:j * blk + 2 * blk, :kvw]
        kgs = [kw[:, gi * HEAD_DIM:(gi + 1) * HEAD_DIM] for gi in range(n_kv)]
        scs = [_dot_nt(qb[:, hh * HEAD_DIM:(hh + 1) * HEAD_DIM], kgs[hh // rep]) + bias_ref[hh]
               for hh in range(n_heads)]
        if j == 0:
            scs = [jnp.where(jnp.logical_and(first_tile, col < blk), NEG_INF, sc) for sc in scs]
        return scs

    def probs(scs):
        out = []
        for hh, sc in enumerate(scs):
            sink = sink_ref[hh]
            m = jnp.maximum(jnp.max(sc, axis=-1, keepdims=True), sink)
            e = jnp.exp(sc - m)
            den = jnp.sum(e, axis=-1, keepdims=True) + jnp.exp(sink - m)
            out.append((e / den).astype(BF16))
        return out

    def values(j, ps):
        vw = kv16[j * blk:j * blk + 2 * blk, kvw:]
        vgs = [vw[:, gi * HEAD_DIM:(gi + 1) * HEAD_DIM] for gi in range(n_kv)]
        outs = [jnp.dot(p, vgs[hh // rep], preferred_element_type=F32) for hh, p in enumerate(ps)]
        y_ref[0, j * blk:(j + 1) * blk, :] = jnp.concatenate(outs, axis=1).astype(BF16)

    nblk = tm // blk
    pending = scores(0)
    for j in range(nblk):
        nxt = scores(j + 1) if j + 1 < nblk else None
        values(j, probs(pending))
        pending = nxt


def _att_mixer(h, wq, wkv, sinks, rel_bias):
    bsz, s, d = h.shape
    n_heads = sinks.shape[0]
    tm = 512
    qi = jnp.arange(ATT_BLOCK)[:, None]
    kj = jnp.arange(2 * ATT_BLOCK)[None, :]
    rel = qi + ATT_BLOCK - kj
    valid = (rel >= 0) & (rel < WINDOW)
    bias = jnp.transpose(rel_bias.astype(F32)[_t5_bucket(rel)], (2, 0, 1))
    bias = jnp.where(valid[None], bias, NEG_INF)
    return pl.pallas_call(
        _att_kernel,
        grid=(bsz, s // tm),
        in_specs=[pl.BlockSpec(memory_space=pltpu.SMEM),
                  pl.BlockSpec((1, tm, d), lambda b, i: (b, i, 0)),
                  pl.BlockSpec(wq.shape, lambda b, i: (0, 0)),
                  pl.BlockSpec(wkv.shape, lambda b, i: (0, 0)),
                  pl.BlockSpec(bias.shape, lambda b, i: (0, 0, 0))],
        out_specs=pl.BlockSpec((1, tm, wq.shape[1]), lambda b, i: (b, i, 0)),
        out_shape=jax.ShapeDtypeStruct((bsz, s, wq.shape[1]), BF16),
        scratch_shapes=[pltpu.VMEM((ATT_BLOCK, wkv.shape[1]), F32)],
        compiler_params=_cparams(("parallel", "arbitrary")),
        name="swa_mixer",
    )(sinks.astype(F32), h, wq, wkv, bias)


S5_GROUPS_PER_BLOCK = LANES // S5_CH


def _s5_tables(lam_re, lam_im, log_dt, b_re, b_im, c_re, c_im):
    n_groups, p = lam_re.shape
    lr, li = lam_re.astype(F32), lam_im.astype(F32)
    delta = jnp.exp(log_dt.astype(F32))[:, None]
    mag = jnp.exp(lr * delta)
    ab_re, ab_im = mag * jnp.cos(li * delta), mag * jnp.sin(li * delta)
    den = lr * lr + li * li
    z_re = ((ab_re - 1.0) * lr + ab_im * li) / den
    z_im = (ab_im * lr - (ab_re - 1.0) * li) / den
    br, bi = b_re.astype(F32), b_im.astype(F32)
    bb_re = z_re[..., None] * br - z_im[..., None] * bi
    bb_im = z_re[..., None] * bi + z_im[..., None] * br
    nblk = n_groups // S5_GROUPS_PER_BLOCK
    eye = jnp.eye(S5_GROUPS_PER_BLOCK, dtype=F32)

    def in_blocks(bb):
        bb = bb.reshape(nblk, S5_GROUPS_PER_BLOCK, p, S5_CH)
        return jnp.einsum('qgpc,gh->qgchp', bb, eye).reshape(nblk, LANES, S5_GROUPS_PER_BLOCK * p)

    def out_blocks(cc):
        cc = cc.astype(F32).reshape(nblk, S5_GROUPS_PER_BLOCK, S5_CH, p)
        return jnp.einsum('qgcp,gh->qgphc', cc, eye).reshape(nblk, S5_GROUPS_PER_BLOCK * p, LANES)

    def power(m):
        mg = jnp.exp(m * lr * delta)
        return (mg * jnp.cos(m * li * delta)).reshape(1, -1), (mg * jnp.sin(m * li * delta)).reshape(1, -1)

    row = jnp.arange(SUBLANES, dtype=F32)[:, None]
    tabs = []
    for sft in (1, 2, 4):
        pr, pi = power(float(sft))
        keep = row >= sft
        tabs += [jnp.where(keep, pr, 0.0), jnp.where(keep, pi, 0.0)]
    n_state = n_groups * p
    lrd = (lr * delta).reshape(1, n_state)
    lid = (li * delta).reshape(1, n_state)
    mg = jnp.exp((row + 1.0) * lrd)
    tabs += [mg * jnp.cos((row + 1.0) * lid), mg * jnp.sin((row + 1.0) * lid)]
    tables = jnp.stack(tabs, axis=0)
    return (in_blocks(bb_re).astype(BF16), in_blocks(bb_im).astype(BF16),
            out_blocks(c_re).astype(BF16), out_blocks(c_im).astype(BF16), tables)


def _s5_kernel(h_ref, w_ref, bre_ref, bim_ref, cre_ref, cim_ref, tab_ref, d_ref, gw_ref, gb_ref,
               y_ref, xr_ref, xi_ref, cr_ref, ci_ref):
    tm = h_ref.shape[1]
    nblk = bre_ref.shape[0]
    sw = bre_ref.shape[2]

    @pl.when(pl.program_id(1) == 0)
    def _():
        cr_ref[...] = jnp.zeros_like(cr_ref)
        ci_ref[...] = jnp.zeros_like(ci_ref)

    u = _dot(h_ref[0], w_ref[...])
    ub = u.astype(BF16)
    for q in range(nblk):
        uq = ub[:, q * LANES:(q + 1) * LANES]
        xr_ref[:, q * sw:(q + 1) * sw] = jnp.dot(uq, bre_ref[q], preferred_element_type=F32)
        xi_ref[:, q * sw:(q + 1) * sw] = jnp.dot(uq, bim_ref[q], preferred_element_type=F32)

    def tile(i, carry):
        cr, ci = carry
        rows = pl.ds(pl.multiple_of(i * SUBLANES, SUBLANES), SUBLANES)
        xr = xr_ref[rows, :]
        xi = xi_ref[rows, :]
        for k, sft in enumerate((1, 2, 4)):
            mr = tab_ref[2 * k]
            mi = tab_ref[2 * k + 1]
            sr = pltpu.roll(xr, sft, 0)
            si = pltpu.roll(xi, sft, 0)
            xr, xi = xr + mr * sr - mi * si, xi + mr * si + mi * sr
        pr = tab_ref[6]
        pi = tab_ref[7]
        xr, xi = xr + pr * cr - pi * ci, xi + pr * ci + pi * cr
        xr_ref[rows, :] = xr
        xi_ref[rows, :] = xi
        return xr[SUBLANES - 1:SUBLANES, :], xi[SUBLANES - 1:SUBLANES, :]

    cr, ci = lax.fori_loop(0, tm // SUBLANES, tile, (cr_ref[0:1, :], ci_ref[0:1, :]))
    cr_ref[0:1, :] = cr
    ci_ref[0:1, :] = ci

    ys = []
    for q in range(nblk):
        xr = xr_ref[:, q * sw:(q + 1) * sw].astype(BF16)
        xi = xi_ref[:, q * sw:(q + 1) * sw].astype(BF16)
        ys.append(jnp.dot(xr, cre_ref[q], preferred_element_type=F32)
                  - jnp.dot(xi, cim_ref[q], preferred_element_type=F32))
    y = jnp.concatenate(ys, axis=1) + d_ref[...] * u
    y = 0.5 * y * (1.0 + jnp.tanh(math.sqrt(2.0 / math.pi) * (y + 0.044715 * (y * y * y))))
    y_ref[0] = (y * _sigmoid(_dot(y, gw_ref[...]) + gb_ref[...])).astype(BF16)


def _s5_mixer(h, w, lam_re, lam_im, log_dt, b_re, b_im, c_re, c_im, d_skip, glu_w, glu_b):
    bsz, s, d = h.shape
    g = w.shape[1]
    tm = 256
    bre, bim, cre, cim, tables = _s5_tables(lam_re, lam_im, log_dt, b_re, b_im, c_re, c_im)
    n_state = tables.shape[2]
    full = lambda a: pl.BlockSpec(a.shape, lambda b, i: (0,) * a.ndim)
    dvec = d_skip.astype(F32).reshape(1, g)
    gw = glu_w.astype(BF16)
    gb = glu_b.astype(F32).reshape(1, g)
    return pl.pallas_call(
        _s5_kernel,
        grid=(bsz, s // tm),
        in_specs=[pl.BlockSpec((1, tm, d), lambda b, i: (b, i, 0)),
                  full(w), full(bre), full(bim), full(cre), full(cim), full(tables),
                  full(dvec), full(gw), full(gb)],
        out_specs=pl.BlockSpec((1, tm, g), lambda b, i: (b, i, 0)),
        out_shape=jax.ShapeDtypeStruct((bsz, s, g), BF16),
        scratch_shapes=[pltpu.VMEM((tm, n_state), F32), pltpu.VMEM((tm, n_state), F32),
                        pltpu.VMEM((SUBLANES, n_state), F32), pltpu.VMEM((SUBLANES, n_state), F32)],
        compiler_params=_cparams(("parallel", "arbitrary")),
        name="s5_mixer",
    )(h, w, bre, bim, cre, cim, tables, dvec, gw, gb)


def _rwkv_kernel(h_ref, wrkv_ref, wlo_ref, mu1_ref, mu2_ref, w0_ref, w2_ref, a0_ref, a2_ref, g2_ref,
                 kk_ref, ka_ref, rk_ref, gng_ref, gnb_ref, eblk_ref,
                 y_ref,
                 cp_ref, cl_ref, hs_ref, r_s, k_s, v_s, a_s, b_s, ld_s, y_s):
    tm = h_ref.shape[1]
    g = y_ref.shape[2]
    npair = g // LANES
    ch = WKV_CHUNK

    @pl.when(pl.program_id(1) == 0)
    def _():
        cp_ref[...] = jnp.zeros_like(cp_ref)
        cl_ref[...] = jnp.zeros_like(cl_ref)
        hs_ref[...] = jnp.zeros_like(hs_ref)

    x = h_ref[0]
    p = _dot(x, wrkv_ref[...])
    plo = _dot(x, wlo_ref[...])
    pprev = _shift_rows(p, cp_ref[...], 1)
    lprev = _shift_rows(plo, cl_ref[...], 1)
    cp_ref[...] = p[tm - SUBLANES:, :]
    cl_ref[...] = plo[tm - SUBLANES:, :]
    p = p + (pprev - p) * mu1_ref[...]
    plo = plo + (lprev - plo) * mu2_ref[...]
    r, k, v = p[:, :g], p[:, g:2 * g], p[:, 2 * g:]
    w_lo, a_lo, g_lo = plo[:, :LANES], plo[:, LANES:2 * LANES], plo[:, 2 * LANES:]
    wraw = w0_ref[...] + _dot(jnp.tanh(w_lo), w2_ref[...])
    nz = -wraw
    softplus = jnp.maximum(nz, 0.0) + jnp.log(1.0 + jnp.exp(-jnp.abs(nz)))
    w = -softplus - 0.5
    ld_s[...] = -jnp.exp(w)
    a = _sigmoid(a0_ref[...] + _dot(a_lo, a2_ref[...]))
    gate = _dot(_sigmoid(g_lo), g2_ref[...])
    eblk = eblk_ref[...]
    kk = k * kk_ref[...]
    kk = kk / jnp.maximum(jnp.sqrt(_dot_split(kk * kk, eblk)), 1e-12)
    k = k * (1.0 + (a - 1.0) * ka_ref[...])
    r_s[...] = r
    k_s[...] = k
    v_s[...] = v
    a_s[...] = -kk
    b_s[...] = kk * a

    lane = lax.broadcasted_iota(I32, (1, LANES), 1)
    m0 = (lane < HEAD_DIM).astype(F32)
    m1 = 1.0 - m0
    ri = lax.broadcasted_iota(I32, (2 * ch, 2 * ch), 0)
    ci = lax.broadcasted_iota(I32, (2 * ch, 2 * ch), 1)
    same = (ri < ch) == (ci < ch)
    rloc = jnp.bitwise_and(ri, ch - 1)
    cloc = jnp.bitwise_and(ci, ch - 1)
    strict = jnp.where(jnp.logical_and(same, cloc < rloc), 1.0, 0.0)
    incl = jnp.where(jnp.logical_and(same, cloc <= rloc), 1.0, 0.0)
    eye = jnp.where(ri == ci, 1.0, 0.0)
    tri = jnp.where(lax.broadcasted_iota(I32, (ch, ch), 1) <= lax.broadcasted_iota(I32, (ch, ch), 0),
                    1.0, 0.0).astype(BF16)

    def bd(t):
        return jnp.concatenate([t * m0, t * m1], axis=0)

    nchunk = tm // ch
    per_chunk = []
    for c in range(nchunk):
        rows = slice(c * ch, (c + 1) * ch)
        ld = ld_s[rows, :]
        ld_hi = ld.astype(BF16)
        ld_lo = (ld - ld_hi.astype(F32)).astype(BF16)
        cum = (jnp.dot(tri, ld_hi, preferred_element_type=F32)
               + jnp.dot(tri, ld_lo, preferred_element_type=F32))
        gam = jnp.exp(cum)
        ginv = jnp.exp(-cum)
        per_chunk.append(dict(at=a_s[rows, :] * jnp.exp(cum - ld), rt=r_s[rows, :] * gam,
                              bt=b_s[rows, :] * ginv, kt=k_s[rows, :] * ginv, v=v_s[rows, :],
                              gl=gam[ch - 1:ch, :]))
    inst = [(c, q) for c in range(nchunk) for q in range(npair)]

    def part(name):
        return [per_chunk[c][name][:, q * LANES:(q + 1) * LANES] for c, q in inst]

    bt, kt, gl = part("bt"), part("kt"), part("gl")
    at_bd = [bd(t) for t in part("at")]
    rt_bd = [bd(t) for t in part("rt")]
    v_bd = [bd(t) for t in part("v")]
    bh_t = [bd(b * g_).T for b, g_ in zip(bt, gl)]
    kh_t = [bd(k_ * g_).T for k_, g_ in zip(kt, gl)]
    gmat = [_dot_nt(jnp.concatenate([a_, r_], axis=0), jnp.concatenate([b, b, k_, k_], axis=0))
            for a_, r_, b, k_ in zip(at_bd, rt_bd, bt, kt)]
    n_ab = [gm[:2 * ch, :2 * ch] * strict for gm in gmat]
    a_ak = [gm[:2 * ch, 2 * ch:] * strict for gm in gmat]
    m_rb = [gm[2 * ch:, :2 * ch] * incl for gm in gmat]
    m_rk = [gm[2 * ch:, 2 * ch:] * incl for gm in gmat]
    tinv = [eye + n for n in n_ab]
    npow = n_ab
    for _ in range(5):
        npow = [_dot(n, n) for n in npow]
        tinv = [t + _dot(t, n) for t, n in zip(tinv, npow)]
    akv = [_dot(a_, v_) for a_, v_ in zip(a_ak, v_bd)]
    wu = [_dot(t, jnp.concatenate([a_, x_], axis=1)) for t, a_, x_ in zip(tinv, at_bd, akv)]
    wmat = [t[:, :2 * ch] for t in wu]
    u0 = [t[:, 2 * ch:] for t in wu]
    pmat = [eye * g_ + _dot(b, w_) for g_, b, w_ in zip(gl, bh_t, wmat)]
    qmat = [_dot(b, u_) + _dot(k_, v_) for b, u_, k_, v_ in zip(bh_t, u0, kh_t, v_bd)]
    ry = [r_ + _dot(m_, w_) for r_, m_, w_ in zip(rt_bd, m_rb, wmat)]
    y0 = [_dot(mb, u_) + _dot(mk, v_) for mb, u_, mk, v_ in zip(m_rb, u0, m_rk, v_bd)]
    state = [hs_ref[q] for q in range(npair)]
    y_chunks = []
    for c in range(nchunk):
        ids = [c * npair + q for q in range(npair)]
        yy = [_dot(ry[i], st) + y0[i] for i, st in zip(ids, state)]
        state = [_dot(pmat[i], st) + qmat[i] for i, st in zip(ids, state)]
        y_chunks.append(jnp.concatenate([t[:ch] + t[ch:] for t in yy], axis=1))
    hs_ref[...] = jnp.stack(state, axis=0)

    y = jnp.concatenate(y_chunks, axis=0)
    inv_n = 1.0 / HEAD_DIM
    mean = _dot_split(y, eblk) * inv_n
    yc = y - mean
    var = _dot_split(yc * yc, eblk) * inv_n
    yn = yc * lax.rsqrt(var + RWKV_GN_EPS) * gng_ref[...] + gnb_ref[...]
    r = r_s[...]
    k = k_s[...]
    v = v_s[...]
    bonus = _dot_split(r * k * rk_ref[...], eblk) * v
    y_ref[0] = ((yn + bonus) * gate).astype(BF16)


def _rwkv_mixer(h, w_rkv, w_lora, mu, w0, w2, a0, a2, g2, k_k, k_a, r_k, gn_g, gn_b):
    bsz, s, d = h.shape
    g = w0.shape[0]
    tm = 256
    row = lambda t: t.astype(F32).reshape(1, -1)
    pad_rows = lambda t: jnp.zeros((LANES, g), F32).at[:t.shape[0]].set(t.astype(F32)).astype(BF16)
    mu1 = row(mu[:3 * g])
    mu2 = jnp.concatenate([
        jnp.zeros((LANES,), F32).at[:RWKV_DECAY_RANK].set(mu[3 * g:3 * g + RWKV_DECAY_RANK]),
        jnp.zeros((LANES,), F32).at[:RWKV_A_RANK].set(mu[3 * g + RWKV_DECAY_RANK:3 * g + RWKV_DECAY_RANK + RWKV_A_RANK]),
        mu[3 * g + RWKV_DECAY_RANK + RWKV_A_RANK:]]).reshape(1, -1)
    head = np.arange(g) // HEAD_DIM
    eblk = jnp.asarray(head[:, None] == head[None, :], BF16)
    args = (h, w_rkv, w_lora, mu1, mu2, row(w0), pad_rows(w2), row(a0), pad_rows(a2), g2.astype(BF16),
            row(k_k), row(k_a), row(r_k), row(gn_g), row(gn_b), eblk)
    full = lambda a: pl.BlockSpec(a.shape, lambda b, i: (0,) * a.ndim)
    return pl.pallas_call(
        _rwkv_kernel,
        grid=(bsz, s // tm),
        in_specs=[pl.BlockSpec((1, tm, d), lambda b, i: (b, i, 0))] + [full(a) for a in args[1:]],
        out_specs=pl.BlockSpec((1, tm, g), lambda b, i: (b, i, 0)),
        out_shape=jax.ShapeDtypeStruct((bsz, s, g), BF16),
        scratch_shapes=[pltpu.VMEM((SUBLANES, 3 * g), F32), pltpu.VMEM((SUBLANES, 3 * LANES), F32),
                        pltpu.VMEM((g // LANES, 2 * WKV_CHUNK, LANES), F32)]
                       + [pltpu.VMEM((tm, g), F32) for _ in range(7)],
        compiler_params=_cparams(("parallel", "arbitrary")),
        name="rwkv7_mixer",
    )(*args)


def _mixout_kernel(alpha, ya_ref, yb_ref, yc_ref, yd_ref, wo_ref, x_ref, gt_ref, lng_ref, lnb_ref,
                   sh_ref, sc_ref, wr_ref, br_ref, x1_ref, h2_ref, lg_ref):
    g = ya_ref.shape[2]
    y = (jnp.dot(ya_ref[0], wo_ref[0:g, :], preferred_element_type=F32)
         + jnp.dot(yb_ref[0], wo_ref[g:2 * g, :], preferred_element_type=F32)
         + jnp.dot(yc_ref[0], wo_ref[2 * g:3 * g, :], preferred_element_type=F32)
         + jnp.dot(yd_ref[0], wo_ref[3 * g:, :], preferred_element_type=F32))
    x1 = _layer_norm(alpha * x_ref[0] + (1.0 + gt_ref[0]) * y) * lng_ref[...] + lnb_ref[...]
    x1_ref[0] = x1
    h2 = _layer_norm(x1) * (1.0 + sc_ref[0]) + sh_ref[0]
    h2_ref[0] = h2
    lg_ref[0] = _dot(h2, wr_ref[...]) + br_ref[...]


def _mix_out(alpha, ys, w_out, x, gate, ln_g, ln_b, shift2, scale2, w_router, b_router):
    bsz, s, d = x.shape
    g = ys[0].shape[2]
    tm = 256
    tok = lambda w: pl.BlockSpec((1, tm, w), lambda b, i: (b, i, 0))
    per_b = pl.BlockSpec((1, 1, d), lambda b, i: (b, 0, 0))
    full = lambda a: pl.BlockSpec(a.shape, lambda b, i: (0,) * a.ndim)
    row = lambda t: t.astype(F32).reshape(1, -1)
    args = (*ys, w_out, x, gate[:, None, :], row(ln_g), row(ln_b), shift2[:, None, :], scale2[:, None, :],
            w_router, b_router)
    return pl.pallas_call(
        functools.partial(_mixout_kernel, alpha),
        grid=(bsz, s // tm),
        in_specs=[tok(g)] * 4 + [full(w_out), tok(d), per_b, full(args[7]), full(args[8]), per_b, per_b,
                                 full(w_router), full(b_router)],
        out_specs=[tok(d), tok(d), tok(LANES)],
        out_shape=[jax.ShapeDtypeStruct((bsz, s, d), F32), jax.ShapeDtypeStruct((bsz, s, d), F32),
                   jax.ShapeDtypeStruct((bsz, s, LANES), F32)],
        compiler_params=_cparams(("parallel", "parallel")),
        name="mix_out_ln_router",
    )(*args)


def _route_kernel(lg_ref, out_ref, cnt_ref, carry_ref):
    tm = lg_ref.shape[0]

    @pl.when(pl.program_id(0) == 0)
    def _():
        carry_ref[...] = jnp.zeros_like(carry_ref)

    lg = lg_ref[...]
    lane = lax.broadcasted_iota(I32, (tm, LANES), 1)
    lane_f = lane.astype(F32)

    def top1(vals, mask):
        mv = jnp.where(mask, vals, -jnp.inf)
        m = jnp.max(mv, axis=-1, keepdims=True)
        idx = jnp.min(jnp.where(jnp.logical_and(mask, mv == m), lane_f, float(LANES)), axis=-1, keepdims=True)
        return m, idx.astype(I32)

    gmask = lane < N_EXPERT_GROUPS
    gm, gidx = top1(lg, gmask)
    g_val = 1.0 / jnp.sum(jnp.where(gmask, jnp.exp(lg - gm), 0.0), axis=-1, keepdims=True)
    elo = N_EXPERT_GROUPS + gidx * EXPERTS_PER_GROUP
    emask = jnp.logical_and(lane >= elo, lane < elo + EXPERTS_PER_GROUP)
    m1, i1 = top1(lg, emask)
    m2, i2 = top1(lg, jnp.logical_and(emask, lane != i1))
    e21 = jnp.exp(m2 - m1)
    w1 = g_val / (1.0 + e21)
    w2 = g_val * e21 / (1.0 + e21)
    e1 = i1 - N_EXPERT_GROUPS
    e2 = i2 - N_EXPERT_GROUPS
    oh1 = (lane == e1)
    oh2 = (lane == e2)
    ohs = jnp.where(jnp.logical_or(oh1, oh2), 1.0, 0.0)
    ri = lax.broadcasted_iota(I32, (tm, tm), 0)
    ci = lax.broadcasted_iota(I32, (tm, tm), 1)
    tri = jnp.where(ci < ri, 1.0, 0.0).astype(BF16)
    before = jnp.dot(tri, ohs.astype(BF16), preferred_element_type=F32) + carry_ref[0:1, :]
    rank1 = jnp.sum(jnp.where(oh1, before, 0.0), axis=-1, keepdims=True).astype(I32)
    rank2 = jnp.sum(jnp.where(oh2, before, 0.0), axis=-1, keepdims=True).astype(I32)
    total = carry_ref[0:1, :] + jnp.sum(ohs, axis=0, keepdims=True)
    carry_ref[0:1, :] = total
    cnt_ref[...] = jnp.broadcast_to(total, cnt_ref.shape).astype(I32)
    packed = jnp.where(lane == 0, e1, 0)
    packed = jnp.where(lane == 1, e2, packed)
    packed = jnp.where(lane == 2, pltpu.bitcast(jnp.broadcast_to(w1, (tm, LANES)), I32), packed)
    packed = jnp.where(lane == 3, pltpu.bitcast(jnp.broadcast_to(w2, (tm, LANES)), I32), packed)
    packed = jnp.where(lane == 4, rank1, packed)
    packed = jnp.where(lane == 5, rank2, packed)
    out_ref[...] = packed


def _route(logits):
    t = logits.shape[0]
    tm = 512
    return pl.pallas_call(
        _route_kernel,
        grid=(t // tm,),
        in_specs=[pl.BlockSpec((tm, LANES), lambda i: (i, 0))],
        out_specs=[pl.BlockSpec((tm, LANES), lambda i: (i, 0)),
                   pl.BlockSpec((SUBLANES, LANES), lambda i: (0, 0))],
        out_shape=[jax.ShapeDtypeStruct((t, LANES), I32), jax.ShapeDtypeStruct((SUBLANES, LANES), I32)],
        scratch_shapes=[pltpu.VMEM((SUBLANES, LANES), F32)],
        compiler_params=_cparams(("arbitrary",)),
        name="moe_route",
    )(logits)


def _expert_kernel(be_ref, tok_ref, tokn_ref, dst_ref, h_hbm, w1_ref, w3_ref, w2_ref, y_hbm,
                   xbuf, ybuf, gsem, ssem, w1b, w3b, w2b):
    b = pl.program_id(0)
    nb = pl.num_programs(0)
    rows = xbuf.shape[1]
    slot = lax.rem(b, 2)
    other = 1 - slot

    def gather_copy(s, r, tok):
        return pltpu.make_async_copy(h_hbm.at[pl.ds(tok, 1), :], xbuf.at[s, pl.ds(r, 1), :], gsem.at[s])

    def scatter_copy(s, r, dst):
        return pltpu.make_async_copy(ybuf.at[s, pl.ds(r, 1), :], y_hbm.at[pl.ds(dst, 1), :], ssem.at[s])

    def gather_start(s, ref):
        for r in range(rows):
            gather_copy(s, r, ref[r]).start(priority=r % 2)

    def gather_wait(s):
        for r in range(rows):
            gather_copy(s, r, 0).wait()

    def scatter_wait(s):
        for r in range(rows):
            scatter_copy(s, r, 0).wait()

    @pl.when(b == 0)
    def _():
        ybuf[...] = jnp.zeros_like(ybuf)
        n_out = y_hbm.shape[0]
        for s in range(2):
            dump = pltpu.make_async_copy(ybuf.at[s], y_hbm.at[pl.ds(n_out - (2 - s) * rows, rows), :], ssem.at[s])
            dump.start()
            dump.wait()
        gather_start(0, tok_ref)

    gather_start(other, tokn_ref)

    prev = be_ref[jnp.maximum(b - 1, 0)]

    @pl.when(jnp.logical_or(b == 0, be_ref[b] != prev))
    def _():
        w1b[...] = w1_ref[0, 0].astype(BF16)
        w3b[...] = w3_ref[0, 0].astype(BF16)
        w2b[...] = w2_ref[0, 0].astype(BF16)

    gather_wait(slot)
    x = xbuf[slot].astype(BF16)
    a = jnp.dot(x, w1b[...], preferred_element_type=F32)
    gte = jnp.dot(x, w3b[...], preferred_element_type=F32)
    mid = (a * _sigmoid(a)) * gte
    y = jnp.dot(mid.astype(BF16), w2b[...], preferred_element_type=F32)

    @pl.when(b >= 2)
    def _():
        scatter_wait(slot)

    ybuf[slot] = y
    for r in range(rows):
        scatter_copy(slot, r, dst_ref[r]).start(priority=r % 2)

    @pl.when(b == nb - 1)
    def _():
        gather_wait(other)
        scatter_wait(slot)

        @pl.when(nb >= 2)
        def _():
            scatter_wait(other)


def _expert_ffn(layer, h2, blk_expert, row_tok, row_dst, n_out, w1, w3, w2):
    t, d = h2.shape
    de = w1.shape[3]
    nr = row_tok.shape[0]
    nb = nr // MOE_ROWS
    rows_spec = lambda fn: pl.BlockSpec((MOE_ROWS,), fn, memory_space=pltpu.SMEM)
    grid_spec = pltpu.PrefetchScalarGridSpec(
        num_scalar_prefetch=1,
        grid=(nb,),
        in_specs=[rows_spec(lambda b, be: (b,)),
                  rows_spec(lambda b, be: (jnp.minimum(b + 1, nb - 1),)),
                  rows_spec(lambda b, be: (b,)),
                  pl.BlockSpec(memory_space=pl.ANY),
                  pl.BlockSpec((1, 1, d, de), lambda b, be: (layer, be[b], 0, 0)),
                  pl.BlockSpec((1, 1, d, de), lambda b, be: (layer, be[b], 0, 0)),
                  pl.BlockSpec((1, 1, de, d), lambda b, be: (layer, be[b], 0, 0))],
        out_specs=pl.BlockSpec(memory_space=pl.ANY),
        scratch_shapes=[pltpu.VMEM((2, MOE_ROWS, d), F32), pltpu.VMEM((2, MOE_ROWS, d), F32),
                        pltpu.SemaphoreType.DMA((2,)), pltpu.SemaphoreType.DMA((2,)),
                        pltpu.VMEM((d, de), BF16), pltpu.VMEM((d, de), BF16), pltpu.VMEM((de, d), BF16)],
    )
    return pl.pallas_call(
        _expert_kernel,
        grid_spec=grid_spec,
        out_shape=jax.ShapeDtypeStruct((n_out, d), F32),
        compiler_params=_cparams(("arbitrary",)),
        name="moe_expert_ffn",
    )(blk_expert, row_tok, row_tok, row_dst, h2, w1, w3, w2)


def _combine_kernel(alpha, ya_ref, yb_ref, wt_ref, x_ref, gt_ref, lng_ref, lnb_ref, o_ref):
    wt = wt_ref[...]
    y = ya_ref[...] * wt[:, 0:1] + yb_ref[...] * wt[:, 1:2]
    o_ref[...] = _layer_norm(alpha * x_ref[...] + (1.0 + gt_ref[0]) * y) * lng_ref[...] + lnb_ref[...]


def _combine(alpha, y_slots, wts, x1, gate, ln_g, ln_b, seq):
    t, d = x1.shape
    tm = 256
    per_seq = seq // tm
    nt = t // tm
    return pl.pallas_call(
        functools.partial(_combine_kernel, alpha),
        grid=(nt,),
        in_specs=[pl.BlockSpec((tm, d), lambda i: (i, 0)),
                  pl.BlockSpec((tm, d), lambda i: (i + nt, 0)),
                  pl.BlockSpec((tm, LANES), lambda i: (i, 0)),
                  pl.BlockSpec((tm, d), lambda i: (i, 0)),
                  pl.BlockSpec((1, 1, d), lambda i: (i // per_seq, 0, 0)),
                  pl.BlockSpec((1, d), lambda i: (0, 0)),
                  pl.BlockSpec((1, d), lambda i: (0, 0))],
        out_specs=pl.BlockSpec((tm, d), lambda i: (i, 0)),
        out_shape=jax.ShapeDtypeStruct((t, d), F32),
        compiler_params=_cparams(("parallel",)),
        name="moe_combine_ln",
    )(y_slots, y_slots, wts, x1, gate[:, None, :], ln_g.astype(F32).reshape(1, d), ln_b.astype(F32).reshape(1, d))


def _moe(layer, alpha, h2, logits, x1, gate, ln_g, ln_b, w1, w3, w2):
    bsz, s, d = x1.shape
    t = bsz * s
    packed, counts = _route(logits.reshape(t, LANES))
    eid = packed[:, 0:TOP_K]
    wts = lax.bitcast_convert_type(packed, F32)[:, 0:LANES]
    wts = jnp.concatenate([wts[:, 2:4], jnp.zeros((t, LANES - TOP_K), F32)], axis=1)
    rank = packed[:, 4:4 + TOP_K]
    cnt = counts[0, :N_EXPERTS]
    n_assign = t * TOP_K
    nb = (n_assign + N_EXPERTS * (MOE_ROWS - 1) + MOE_ROWS - 1) // MOE_ROWS
    nr = nb * MOE_ROWS
    padded = ((cnt + MOE_ROWS - 1) // MOE_ROWS) * MOE_ROWS
    pad_end = jnp.cumsum(padded)
    pad_start = pad_end - padded
    experts = jnp.arange(N_EXPERTS, dtype=I32)
    dest = (jnp.sum(jnp.where(eid[..., None] == experts, pad_start, 0), axis=-1) + rank).reshape(n_assign)
    row_assign = jnp.full((nr,), -1, I32).at[dest].set(jnp.arange(n_assign, dtype=I32))
    row = jnp.arange(nr, dtype=I32)
    real = row_assign >= 0
    row_tok = jnp.where(real, row_assign // TOP_K, 0)
    dump = n_assign + ((row // MOE_ROWS) % 2) * MOE_ROWS + row % MOE_ROWS
    row_dst = jnp.where(real, row_assign // TOP_K + t * (row_assign % TOP_K), dump)
    blk_start = jnp.arange(nb, dtype=I32) * MOE_ROWS
    blk_expert = jnp.minimum(jnp.sum(pad_end[None, :] <= blk_start[:, None], axis=1), N_EXPERTS - 1).astype(I32)
    y_slots = _expert_ffn(layer, h2.reshape(t, d), blk_expert, row_tok, row_dst, n_assign + 2 * MOE_ROWS,
                          w1, w3, w2)
    x2 = _combine(alpha, y_slots, wts, x1.reshape(t, d), gate, ln_g, ln_b, s)
    return x2.reshape(bsz, s, d)


def kernel(x, c, w_ada, b_ada, ln_g, ln_b, w_in, w_out, conv_w, rwkv_mu, rwkv_w0, rwkv_w2, rwkv_a0, rwkv_a2, rwkv_g2, rwkv_kk, rwkv_ka, rwkv_rk, rwkv_gn_g, rwkv_gn_b, attn_sinks, rel_bias, s5_lambda_re, s5_lambda_im, s5_log_dt, s5_b_re, s5_b_im, s5_c_re, s5_c_im, s5_d, s5_glu_w, s5_glu_b, router_group_w, router_group_b, router_expert_w, router_expert_b, moe_w1, moe_w3, moe_w2):
    depth = w_ada.shape[0]
    d = x.shape[-1]
    g = d // 4
    alpha = (2 * depth) ** 0.25
    n_heads = g // HEAD_DIM
    att_kv = max(1, n_heads // 4) * HEAD_DIM
    rw_off = 3 * g
    lora = RWKV_DECAY_RANK + RWKV_A_RANK + RWKV_GATE_RANK
    att_off = rw_off + 3 * g + lora
    s5_off = att_off + g + 2 * att_kv

    mod = _modulation(c, w_ada, b_ada)
    for l in range(depth):
        sh1, sc1, gt1, sh2, sc2, gt2 = jnp.split(mod[l], 6, axis=-1)
        wl = w_in[l]
        w_conv = wl[:, :rw_off].astype(BF16)
        w_rkv = wl[:, rw_off:rw_off + 3 * g].astype(BF16)
        lo = rw_off + 3 * g
        zcol = lambda n: jnp.zeros((d, n), F32)
        w_lora = jnp.concatenate([
            wl[:, lo:lo + RWKV_DECAY_RANK], zcol(LANES - RWKV_DECAY_RANK),
            wl[:, lo + RWKV_DECAY_RANK:lo + RWKV_DECAY_RANK + RWKV_A_RANK], zcol(LANES - RWKV_A_RANK),
            wl[:, lo + RWKV_DECAY_RANK + RWKV_A_RANK:att_off]], axis=1).astype(BF16)
        w_q = wl[:, att_off:att_off + g].astype(BF16)
        w_kv = wl[:, att_off + g:s5_off].astype(BF16)
        w_s5 = wl[:, s5_off:].astype(BF16)

        h = _adaln(x, sh1, sc1)
        y_conv = _conv_mixer(h, w_conv, conv_w[l].astype(F32))
        y_rwkv = _rwkv_mixer(h, w_rkv, w_lora, rwkv_mu[l], rwkv_w0[l], rwkv_w2[l], rwkv_a0[l], rwkv_a2[l],
                             rwkv_g2[l], rwkv_kk[l], rwkv_ka[l], rwkv_rk[l], rwkv_gn_g[l], rwkv_gn_b[l])
        y_att = _att_mixer(h, w_q, w_kv, attn_sinks[l], rel_bias)
        y_ssm = _s5_mixer(h, w_s5, s5_lambda_re[l], s5_lambda_im[l], s5_log_dt[l], s5_b_re[l], s5_b_im[l],
                          s5_c_re[l], s5_c_im[l], s5_d[l], s5_glu_w[l], s5_glu_b[l])
        w_router = jnp.zeros((d, LANES), F32)
        w_router = w_router.at[:, :N_EXPERT_GROUPS].set(router_group_w[l])
        w_router = w_router.at[:, N_EXPERT_GROUPS:N_EXPERT_GROUPS + N_EXPERTS].set(router_expert_w[l]).astype(BF16)
        b_router = jnp.zeros((1, LANES), F32)
        b_router = b_router.at[0, :N_EXPERT_GROUPS].set(router_group_b[l])
        b_router = b_router.at[0, N_EXPERT_GROUPS:N_EXPERT_GROUPS + N_EXPERTS].set(router_expert_b[l])
        x1, h2, logits = _mix_out(alpha, (y_conv, y_rwkv, y_att, y_ssm), w_out[l].astype(BF16), x, gt1,
                                  ln_g[l, 0], ln_b[l, 0], sh2, sc2, w_router, b_router)
        x = _moe(l, alpha, h2, logits, x1, gt2, ln_g[l, 1], ln_b[l, 1], moe_w1, moe_w3, moe_w2)
    return x
```

```python
import functools
import math

import numpy as np
import jax
import jax.numpy as jnp
from jax import lax
from jax.experimental import pallas as pl
from jax.experimental.pallas import tpu as pltpu

F32 = jnp.float32
BF16 = jnp.bfloat16
I32 = jnp.int32

HEAD_DIM = 64
CONV_WIDTH = 3
RWKV_DECAY_RANK = 96
RWKV_A_RANK = 96
RWKV_GATE_RANK = 128
RWKV_GN_EPS = 64e-5
ATT_BLOCK = 128
WINDOW = 128
N_BUCKETS = 32
NEG_INF = -1e30
S5_CH = 16
S5_STATE = 64
N_EXPERT_GROUPS = 4
EXPERTS_PER_GROUP = 8
N_EXPERTS = N_EXPERT_GROUPS * EXPERTS_PER_GROUP
TOP_K = 2
LN_EPS = 1e-5

LANES = 128
SUBLANES = 8
WKV_CHUNK = 64
MOE_TILE = 256
MOE_UNIT = SUBLANES
MOE_TILE_ROWS = 768
MOE_BLOCK_UNITS = 32
VMEM_LIMIT = 56 * 2 ** 20


def _cparams(sem, flags=None):
    return pltpu.CompilerParams(dimension_semantics=sem, vmem_limit_bytes=VMEM_LIMIT, flags=flags)


def _dot(a, b):
    return jnp.dot(a.astype(BF16), b.astype(BF16), preferred_element_type=F32)


def _dot_nt(a, b):
    return lax.dot_general(a.astype(BF16), b.astype(BF16), (((1,), (1,)), ((), ())),
                           preferred_element_type=F32)


def _dot_tn(a, b):
    return jnp.dot(a.T.astype(BF16), b.astype(BF16), preferred_element_type=F32)


def _dot_split(x, e, lhs=False):
    hi = x.astype(BF16)
    lo = (x - hi.astype(F32)).astype(BF16)
    if lhs:
        return jnp.dot(e, hi, preferred_element_type=F32) + jnp.dot(e, lo, preferred_element_type=F32)
    return jnp.dot(hi, e, preferred_element_type=F32) + jnp.dot(lo, e, preferred_element_type=F32)


def _sigmoid(x):
    return 1.0 / (1.0 + jnp.exp(-x))


def _layer_norm(x):
    mean = jnp.mean(x, axis=-1, keepdims=True)
    xc = x - mean
    var = jnp.mean(xc * xc, axis=-1, keepdims=True)
    return xc * lax.rsqrt(var + LN_EPS)


def _shift_rows(p, carry_row, n):
    row = lax.broadcasted_iota(I32, (p.shape[0], 1), 0)
    out = pltpu.roll(p, n, 0)
    for i in range(n):
        out = jnp.where(row == i, carry_row[SUBLANES - n + i:SUBLANES - n + i + 1, :], out)
    return out


def _mod_kernel(c_ref, w_ref, b_ref, o_ref):
    c = c_ref[...]
    a = c * _sigmoid(c)
    o_ref[0] = _dot(a, w_ref[0]) + b_ref[0]


def _modulation(c, w_ada, b_ada):
    depth, d, n = w_ada.shape
    bsz = c.shape[0]
    tn = 1536
    cp = jnp.zeros((SUBLANES, d), F32).at[:bsz].set(c)
    out = pl.pallas_call(
        _mod_kernel,
        grid=(depth, n // tn),
        in_specs=[pl.BlockSpec((SUBLANES, d), lambda l, j: (0, 0)),
                  pl.BlockSpec((1, d, tn), lambda l, j: (l, 0, j)),
                  pl.BlockSpec((1, 1, tn), lambda l, j: (l, 0, j))],
        out_specs=pl.BlockSpec((1, SUBLANES, tn), lambda l, j: (l, 0, j)),
        out_shape=jax.ShapeDtypeStruct((depth, SUBLANES, n), F32),
        compiler_params=_cparams(("parallel", "parallel")),
        name="adaln_modulation",
    )(cp, w_ada, b_ada.reshape(depth, 1, n))
    return out[:, :bsz]


def _adaln_kernel(x_ref, sh_ref, sc_ref, h_ref):
    h_ref[0] = (_layer_norm(x_ref[0]) * (1.0 + sc_ref[0]) + sh_ref[0]).astype(BF16)


def _adaln(x, shift, scale):
    bsz, s, d = x.shape
    tm = 512
    return pl.pallas_call(
        _adaln_kernel,
        grid=(bsz, s // tm),
        in_specs=[pl.BlockSpec((1, tm, d), lambda b, i: (b, i, 0)),
                  pl.BlockSpec((1, 1, d), lambda b, i: (b, 0, 0)),
                  pl.BlockSpec((1, 1, d), lambda b, i: (b, 0, 0))],
        out_specs=pl.BlockSpec((1, tm, d), lambda b, i: (b, i, 0)),
        out_shape=jax.ShapeDtypeStruct((bsz, s, d), BF16),
        compiler_params=_cparams(("parallel", "parallel")),
        name="adaln_input",
    )(x, shift[:, None, :], scale[:, None, :])


def _conv_kernel(h_ref, w_ref, cw_ref, y_ref, carry_ref):
    tm = h_ref.shape[1]
    g = y_ref.shape[2]

    @pl.when(pl.program_id(1) == 0)
    def _():
        carry_ref[...] = jnp.zeros_like(carry_ref)

    p = _dot(h_ref[0], w_ref[...])
    b_gate, c_gate, hh = p[:, :g], p[:, g:2 * g], p[:, 2 * g:]
    z = c_gate * hh
    carry = carry_ref[...]
    z1 = _shift_rows(z, carry, 1)
    z2 = _shift_rows(z, carry, 2)
    cw = cw_ref[...]
    out = cw[0:1] * z2 + cw[1:2] * z1 + cw[2:3] * z
    y_ref[0] = (b_gate * out).astype(BF16)
    carry_ref[...] = z[tm - SUBLANES:, :]


def _conv_mixer(h, w, conv_w):
    bsz, s, d = h.shape
    g = conv_w.shape[1]
    tm = 512
    return pl.pallas_call(
        _conv_kernel,
        grid=(bsz, s // tm),
        in_specs=[pl.BlockSpec((1, tm, d), lambda b, i: (b, i, 0)),
                  pl.BlockSpec((d, 3 * g), lambda b, i: (0, 0)),
                  pl.BlockSpec((CONV_WIDTH, g), lambda b, i: (0, 0))],
        out_specs=pl.BlockSpec((1, tm, g), lambda b, i: (b, i, 0)),
        out_shape=jax.ShapeDtypeStruct((bsz, s, g), BF16),
        scratch_shapes=[pltpu.VMEM((SUBLANES, g), F32)],
        compiler_params=_cparams(("parallel", "arbitrary")),
        name="conv_mixer",
    )(h, w, conv_w)


def _t5_bucket(rel):
    n = jnp.maximum(rel, 0)
    max_exact = N_BUCKETS // 2
    n_f = jnp.maximum(n, 1).astype(F32)
    large = max_exact + (jnp.log(n_f / max_exact) / math.log(WINDOW / max_exact)
                         * (N_BUCKETS - max_exact)).astype(I32)
    return jnp.where(n < max_exact, n, jnp.minimum(large, N_BUCKETS - 1))


def _att_kernel(sink_ref, h_ref, wq_ref, wkv_ref, bias_ref, y_ref, kvc_ref):
    tm = h_ref.shape[1]
    n_heads = bias_ref.shape[0]
    kvw = wkv_ref.shape[1] // 2
    n_kv = kvw // HEAD_DIM
    rep = n_heads // n_kv
    blk = ATT_BLOCK
    first_tile = pl.program_id(1) == 0

    @pl.when(first_tile)
    def _():
        kvc_ref[...] = jnp.zeros_like(kvc_ref)

    x = h_ref[0]
    q = _dot(x, wq_ref[...]) * (HEAD_DIM ** -0.5)
    kv = _dot(x, wkv_ref[...])
    kvext = jnp.concatenate([kvc_ref[...], kv], axis=0)
    kvc_ref[...] = kv[tm - blk:, :]
    col = lax.broadcasted_iota(I32, (blk, 2 * blk), 1)
    qb16 = q.astype(BF16)
    kv16 = kvext.astype(BF16)

    def scores(j):
        qb = qb16[j * blk:(j + 1) * blk]
        kw = kv16[j * blk:j * blk + 2 * blk, :kvw]
        kgs = [kw[:, gi * HEAD_DIM:(gi + 1) * HEAD_DIM] for gi in range(n_kv)]
        scs = [_dot_nt(qb[:, hh * HEAD_DIM:(hh + 1) * HEAD_DIM], kgs[hh // rep]) + bias_ref[hh]
               for hh in range(n_heads)]
        if j == 0:
            scs = [jnp.where(jnp.logical_and(first_tile, col < blk), NEG_INF, sc) for sc in scs]
        return scs

    def probs(scs):
        out = []
        for hh, sc in enumerate(scs):
            sink = sink_ref[hh]
            m = jnp.maximum(jnp.max(sc, axis=-1, keepdims=True), sink)
            e = jnp.exp(sc - m)
            den = jnp.sum(e, axis=-1, keepdims=True) + jnp.exp(sink - m)
            out.append((e / den).astype(BF16))
        return out

    def values(j, ps):
        vw = kv16[j * blk:j * blk + 2 * blk, kvw:]
        vgs = [vw[:, gi * HEAD_DIM:(gi + 1) * HEAD_DIM] for gi in range(n_kv)]
        outs = [jnp.dot(p, vgs[hh // rep], preferred_element_type=F32) for hh, p in enumerate(ps)]
        y_ref[0, j * blk:(j + 1) * blk, :] = jnp.concatenate(outs, axis=1).astype(BF16)

    nblk = tm // blk
    pending = scores(0)
    for j in range(nblk):
        nxt = scores(j + 1) if j + 1 < nblk else None
        values(j, probs(pending))
        pending = nxt


def _att_mixer(h, wq, wkv, sinks, rel_bias):
    bsz, s, d = h.shape
    n_heads = sinks.shape[0]
    tm = 512
    qi = jnp.arange(ATT_BLOCK)[:, None]
    kj = jnp.arange(2 * ATT_BLOCK)[None, :]
    rel = qi + ATT_BLOCK - kj
    valid = (rel >= 0) & (rel < WINDOW)
    bias = jnp.transpose(rel_bias.astype(F32)[_t5_bucket(rel)], (2, 0, 1))
    bias = jnp.where(valid[None], bias, NEG_INF)
    return pl.pallas_call(
        _att_kernel,
        grid=(bsz, s // tm),
        in_specs=[pl.BlockSpec(memory_space=pltpu.SMEM),
                  pl.BlockSpec((1, tm, d), lambda b, i: (b, i, 0)),
                  pl.BlockSpec(wq.shape, lambda b, i: (0, 0)),
                  pl.BlockSpec(wkv.shape, lambda b, i: (0, 0)),
                  pl.BlockSpec(bias.shape, lambda b, i: (0, 0, 0))],
        out_specs=pl.BlockSpec((1, tm, wq.shape[1]), lambda b, i: (b, i, 0)),
        out_shape=jax.ShapeDtypeStruct((bsz, s, wq.shape[1]), BF16),
        scratch_shapes=[pltpu.VMEM((ATT_BLOCK, wkv.shape[1]), F32)],
        compiler_params=_cparams(("parallel", "arbitrary")),
        name="swa_mixer",
    )(sinks.astype(F32), h, wq, wkv, bias)


S5_GROUPS_PER_BLOCK = LANES // S5_CH


def _s5_tables(lam_re, lam_im, log_dt, b_re, b_im, c_re, c_im):
    n_groups, p = lam_re.shape
    lr, li = lam_re.astype(F32), lam_im.astype(F32)
    delta = jnp.exp(log_dt.astype(F32))[:, None]
    mag = jnp.exp(lr * delta)
    ab_re, ab_im = mag * jnp.cos(li * delta), mag * jnp.sin(li * delta)
    den = lr * lr + li * li
    z_re = ((ab_re - 1.0) * lr + ab_im * li) / den
    z_im = (ab_im * lr - (ab_re - 1.0) * li) / den
    br, bi = b_re.astype(F32), b_im.astype(F32)
    bb_re = z_re[..., None] * br - z_im[..., None] * bi
    bb_im = z_re[..., None] * bi + z_im[..., None] * br
    nblk = n_groups // S5_GROUPS_PER_BLOCK
    eye = jnp.eye(S5_GROUPS_PER_BLOCK, dtype=F32)

    def in_blocks(bb):
        bb = bb.reshape(nblk, S5_GROUPS_PER_BLOCK, p, S5_CH)
        return jnp.einsum('qgpc,gh->qgchp', bb, eye).reshape(nblk, LANES, S5_GROUPS_PER_BLOCK * p)

    def out_blocks(cc):
        cc = cc.astype(F32).reshape(nblk, S5_GROUPS_PER_BLOCK, S5_CH, p)
        return jnp.einsum('qgcp,gh->qgphc', cc, eye).reshape(nblk, S5_GROUPS_PER_BLOCK * p, LANES)

    def power(m):
        mg = jnp.exp(m * lr * delta)
        return (mg * jnp.cos(m * li * delta)).reshape(1, -1), (mg * jnp.sin(m * li * delta)).reshape(1, -1)

    row = jnp.arange(SUBLANES, dtype=F32)[:, None]
    tabs = []
    for sft in (1, 2, 4):
        pr, pi = power(float(sft))
        keep = row >= sft
        tabs += [jnp.where(keep, pr, 0.0), jnp.where(keep, pi, 0.0)]
    n_state = n_groups * p
    lrd = (lr * delta).reshape(1, n_state)
    lid = (li * delta).reshape(1, n_state)
    mg = jnp.exp((row + 1.0) * lrd)
    tabs += [mg * jnp.cos((row + 1.0) * lid), mg * jnp.sin((row + 1.0) * lid)]
    tables = jnp.stack(tabs, axis=0)
    return (in_blocks(bb_re).astype(BF16), in_blocks(bb_im).astype(BF16),
            out_blocks(c_re).astype(BF16), out_blocks(c_im).astype(BF16), tables)


def _s5_kernel(h_ref, w_ref, bre_ref, bim_ref, cre_ref, cim_ref, tab_ref, d_ref, gw_ref, gb_ref,
               y_ref, xr_ref, xi_ref, cr_ref, ci_ref):
    tm = h_ref.shape[1]
    nblk = bre_ref.shape[0]
    sw = bre_ref.shape[2]

    @pl.when(pl.program_id(1) == 0)
    def _():
        cr_ref[...] = jnp.zeros_like(cr_ref)
        ci_ref[...] = jnp.zeros_like(ci_ref)

    u = _dot(h_ref[0], w_ref[...])
    ub = u.astype(BF16)
    for q in range(nblk):
        uq = ub[:, q * LANES:(q + 1) * LANES]
        xr_ref[:, q * sw:(q + 1) * sw] = jnp.dot(uq, bre_ref[q], preferred_element_type=F32)
        xi_ref[:, q * sw:(q + 1) * sw] = jnp.dot(uq, bim_ref[q], preferred_element_type=F32)

    def tile(i, carry):
        cr, ci = carry
        rows = pl.ds(pl.multiple_of(i * SUBLANES, SUBLANES), SUBLANES)
        xr = xr_ref[rows, :]
        xi = xi_ref[rows, :]
        for k, sft in enumerate((1, 2, 4)):
            mr = tab_ref[2 * k]
            mi = tab_ref[2 * k + 1]
            sr = pltpu.roll(xr, sft, 0)
            si = pltpu.roll(xi, sft, 0)
            xr, xi = xr + mr * sr - mi * si, xi + mr * si + mi * sr
        pr = tab_ref[6]
        pi = tab_ref[7]
        xr, xi = xr + pr * cr - pi * ci, xi + pr * ci + pi * cr
        xr_ref[rows, :] = xr
        xi_ref[rows, :] = xi
        return xr[SUBLANES - 1:SUBLANES, :], xi[SUBLANES - 1:SUBLANES, :]

    cr, ci = lax.fori_loop(0, tm // SUBLANES, tile, (cr_ref[0:1, :], ci_ref[0:1, :]))
    cr_ref[0:1, :] = cr
    ci_ref[0:1, :] = ci

    ys = []
    for q in range(nblk):
        xr = xr_ref[:, q * sw:(q + 1) * sw].astype(BF16)
        xi = xi_ref[:, q * sw:(q + 1) * sw].astype(BF16)
        ys.append(jnp.dot(xr, cre_ref[q], preferred_element_type=F32)
                  - jnp.dot(xi, cim_ref[q], preferred_element_type=F32))
    y = jnp.concatenate(ys, axis=1) + d_ref[...] * u
    y = 0.5 * y * (1.0 + jnp.tanh(math.sqrt(2.0 / math.pi) * (y + 0.044715 * (y * y * y))))
    y_ref[0] = (y * _sigmoid(_dot(y, gw_ref[...]) + gb_ref[...])).astype(BF16)


def _s5_mixer(h, w, lam_re, lam_im, log_dt, b_re, b_im, c_re, c_im, d_skip, glu_w, glu_b):
    bsz, s, d = h.shape
    g = w.shape[1]
    tm = 256
    bre, bim, cre, cim, tables = _s5_tables(lam_re, lam_im, log_dt, b_re, b_im, c_re, c_im)
    n_state = tables.shape[2]
    full = lambda a: pl.BlockSpec(a.shape, lambda b, i: (0,) * a.ndim)
    dvec = d_skip.astype(F32).reshape(1, g)
    gw = glu_w.astype(BF16)
    gb = glu_b.astype(F32).reshape(1, g)
    return pl.pallas_call(
        _s5_kernel,
        grid=(bsz, s // tm),
        in_specs=[pl.BlockSpec((1, tm, d), lambda b, i: (b, i, 0)),
                  full(w), full(bre), full(bim), full(cre), full(cim), full(tables),
                  full(dvec), full(gw), full(gb)],
        out_specs=pl.BlockSpec((1, tm, g), lambda b, i: (b, i, 0)),
        out_shape=jax.ShapeDtypeStruct((bsz, s, g), BF16),
        scratch_shapes=[pltpu.VMEM((tm, n_state), F32), pltpu.VMEM((tm, n_state), F32),
                        pltpu.VMEM((SUBLANES, n_state), F32), pltpu.VMEM((SUBLANES, n_state), F32)],
        compiler_params=_cparams(("parallel", "arbitrary")),
        name="s5_mixer",
    )(h, w, bre, bim, cre, cim, tables, dvec, gw, gb)


def _rwkv_kernel(h_ref, wrkv_ref, wlo_ref, mu1_ref, mu2_ref, w0_ref, w2_ref, a0_ref, a2_ref, g2_ref,
                 kk_ref, ka_ref, rk_ref, gng_ref, gnb_ref, eblk_ref,
                 y_ref,
                 cp_ref, cl_ref, hs_ref, r_s, k_s, v_s, a_s, b_s, ld_s, y_s):
    tm = h_ref.shape[1]
    g = y_ref.shape[2]
    npair = g // LANES
    ch = WKV_CHUNK

    @pl.when(pl.program_id(1) == 0)
    def _():
        cp_ref[...] = jnp.zeros_like(cp_ref)
        cl_ref[...] = jnp.zeros_like(cl_ref)
        hs_ref[...] = jnp.zeros_like(hs_ref)

    x = h_ref[0]
    p = _dot(x, wrkv_ref[...])
    plo = _dot(x, wlo_ref[...])
    pprev = _shift_rows(p, cp_ref[...], 1)
    lprev = _shift_rows(plo, cl_ref[...], 1)
    cp_ref[...] = p[tm - SUBLANES:, :]
    cl_ref[...] = plo[tm - SUBLANES:, :]
    p = p + (pprev - p) * mu1_ref[...]
    plo = plo + (lprev - plo) * mu2_ref[...]
    r, k, v = p[:, :g], p[:, g:2 * g], p[:, 2 * g:]
    w_lo, a_lo, g_lo = plo[:, :LANES], plo[:, LANES:2 * LANES], plo[:, 2 * LANES:]
    wraw = w0_ref[...] + _dot(jnp.tanh(w_lo), w2_ref[...])
    nz = -wraw
    softplus = jnp.maximum(nz, 0.0) + jnp.log(1.0 + jnp.exp(-jnp.abs(nz)))
    w = -softplus - 0.5
    ld_s[...] = -jnp.exp(w)
    a = _sigmoid(a0_ref[...] + _dot(a_lo, a2_ref[...]))
    gate = _dot(_sigmoid(g_lo), g2_ref[...])
    eblk = eblk_ref[...]
    kk = k * kk_ref[...]
    kk = kk / jnp.maximum(jnp.sqrt(_dot_split(kk * kk, eblk)), 1e-12)
    k = k * (1.0 + (a - 1.0) * ka_ref[...])
    r_s[...] = r
    k_s[...] = k
    v_s[...] = v
    a_s[...] = -kk
    b_s[...] = kk * a

    lane = lax.broadcasted_iota(I32, (1, LANES), 1)
    m0 = (lane < HEAD_DIM).astype(F32)
    m1 = 1.0 - m0
    ri = lax.broadcasted_iota(I32, (2 * ch, 2 * ch), 0)
    ci = lax.broadcasted_iota(I32, (2 * ch, 2 * ch), 1)
    same = (ri < ch) == (ci < ch)
    rloc = jnp.bitwise_and(ri, ch - 1)
    cloc = jnp.bitwise_and(ci, ch - 1)
    strict = jnp.where(jnp.logical_and(same, cloc < rloc), 1.0, 0.0)
    incl = jnp.where(jnp.logical_and(same, cloc <= rloc), 1.0, 0.0)
    eye = jnp.where(ri == ci, 1.0, 0.0)
    tri = jnp.where(lax.broadcasted_iota(I32, (ch, ch), 1) <= lax.broadcasted_iota(I32, (ch, ch), 0),
                    1.0, 0.0).astype(BF16)

    def bd(t):
        return jnp.concatenate([t * m0, t * m1], axis=0)

    nchunk = tm // ch
    per_chunk = []
    for c in range(nchunk):
        rows = slice(c * ch, (c + 1) * ch)
        ld = ld_s[rows, :]
        ld_hi = ld.astype(BF16)
        ld_lo = (ld - ld_hi.astype(F32)).astype(BF16)
        cum = (jnp.dot(tri, ld_hi, preferred_element_type=F32)
               + jnp.dot(tri, ld_lo, preferred_element_type=F32))
        gam = jnp.exp(cum)
        ginv = jnp.exp(-cum)
        per_chunk.append(dict(at=a_s[rows, :] * jnp.exp(cum - ld), rt=r_s[rows, :] * gam,
                              bt=b_s[rows, :] * ginv, kt=k_s[rows, :] * ginv, v=v_s[rows, :],
                              gl=gam[ch - 1:ch, :]))
    inst = [(c, q) for c in range(nchunk) for q in range(npair)]

    def part(name):
        return [per_chunk[c][name][:, q * LANES:(q + 1) * LANES] for c, q in inst]

    bt, kt, gl = part("bt"), part("kt"), part("gl")
    at_bd = [bd(t) for t in part("at")]
    rt_bd = [bd(t) for t in part("rt")]
    v_bd = [bd(t) for t in part("v")]
    bh_t = [bd(b * g_).T for b, g_ in zip(bt, gl)]
    kh_t = [bd(k_ * g_).T for k_, g_ in zip(kt, gl)]
    gmat = [_dot_nt(jnp.concatenate([a_, r_], axis=0), jnp.concatenate([b, b, k_, k_], axis=0))
            for a_, r_, b, k_ in zip(at_bd, rt_bd, bt, kt)]
    n_ab = [gm[:2 * ch, :2 * ch] * strict for gm in gmat]
    a_ak = [gm[:2 * ch, 2 * ch:] * strict for gm in gmat]
    m_rb = [gm[2 * ch:, :2 * ch] * incl for gm in gmat]
    m_rk = [gm[2 * ch:, 2 * ch:] * incl for gm in gmat]
    tinv = [eye + n for n in n_ab]
    npow = n_ab
    for _ in range(5):
        npow = [_dot(n, n) for n in npow]
        tinv = [t + _dot(t, n) for t, n in zip(tinv, npow)]
    akv = [_dot(a_, v_) for a_, v_ in zip(a_ak, v_bd)]
    wu = [_dot(t, jnp.concatenate([a_, x_], axis=1)) for t, a_, x_ in zip(tinv, at_bd, akv)]
    wmat = [t[:, :2 * ch] for t in wu]
    u0 = [t[:, 2 * ch:] for t in wu]
    pmat = [eye * g_ + _dot(b, w_) for g_, b, w_ in zip(gl, bh_t, wmat)]
    qmat = [_dot(b, u_) + _dot(k_, v_) for b, u_, k_, v_ in zip(bh_t, u0, kh_t, v_bd)]
    ry = [r_ + _dot(m_, w_) for r_, m_, w_ in zip(rt_bd, m_rb, wmat)]
    y0 = [_dot(mb, u_) + _dot(mk, v_) for mb, u_, mk, v_ in zip(m_rb, u0, m_rk, v_bd)]
    state = [hs_ref[q] for q in range(npair)]
    y_chunks = []
    for c in range(nchunk):
        ids = [c * npair + q for q in range(npair)]
        yy = [_dot(ry[i], st) + y0[i] for i, st in zip(ids, state)]
        state = [_dot(pmat[i], st) + qmat[i] for i, st in zip(ids, state)]
        y_chunks.append(jnp.concatenate([t[:ch] + t[ch:] for t in yy], axis=1))
    hs_ref[...] = jnp.stack(state, axis=0)

    y = jnp.concatenate(y_chunks, axis=0)
    inv_n = 1.0 / HEAD_DIM
    mean = _dot_split(y, eblk) * inv_n
    yc = y - mean
    var = _dot_split(yc * yc, eblk) * inv_n
    yn = yc * lax.rsqrt(var + RWKV_GN_EPS) * gng_ref[...] + gnb_ref[...]
    r = r_s[...]
    k = k_s[...]
    v = v_s[...]
    bonus = _dot_split(r * k * rk_ref[...], eblk) * v
    y_ref[0] = ((yn + bonus) * gate).astype(BF16)


def _rwkv_mixer(h, w_rkv, w_lora, mu, w0, w2, a0, a2, g2, k_k, k_a, r_k, gn_g, gn_b):
    bsz, s, d = h.shape
    g = w0.shape[0]
    tm = 256
    row = lambda t: t.astype(F32).reshape(1, -1)
    pad_rows = lambda t: jnp.zeros((LANES, g), F32).at[:t.shape[0]].set(t.astype(F32)).astype(BF16)
    mu1 = row(mu[:3 * g])
    mu2 = jnp.concatenate([
        jnp.zeros((LANES,), F32).at[:RWKV_DECAY_RANK].set(mu[3 * g:3 * g + RWKV_DECAY_RANK]),
        jnp.zeros((LANES,), F32).at[:RWKV_A_RANK].set(mu[3 * g + RWKV_DECAY_RANK:3 * g + RWKV_DECAY_RANK + RWKV_A_RANK]),
        mu[3 * g + RWKV_DECAY_RANK + RWKV_A_RANK:]]).reshape(1, -1)
    head = np.arange(g) // HEAD_DIM
    eblk = jnp.asarray(head[:, None] == head[None, :], BF16)
    args = (h, w_rkv, w_lora, mu1, mu2, row(w0), pad_rows(w2), row(a0), pad_rows(a2), g2.astype(BF16),
            row(k_k), row(k_a), row(r_k), row(gn_g), row(gn_b), eblk)
    full = lambda a: pl.BlockSpec(a.shape, lambda b, i: (0,) * a.ndim)
    return pl.pallas_call(
        _rwkv_kernel,
        grid=(bsz, s // tm),
        in_specs=[pl.BlockSpec((1, tm, d), lambda b, i: (b, i, 0))] + [full(a) for a in args[1:]],
        out_specs=pl.BlockSpec((1, tm, g), lambda b, i: (b, i, 0)),
        out_shape=jax.ShapeDtypeStruct((bsz, s, g), BF16),
        scratch_shapes=[pltpu.VMEM((SUBLANES, 3 * g), F32), pltpu.VMEM((SUBLANES, 3 * LANES), F32),
                        pltpu.VMEM((g // LANES, 2 * WKV_CHUNK, LANES), F32)]
                       + [pltpu.VMEM((tm, g), F32) for _ in range(7)],
        compiler_params=_cparams(("parallel", "arbitrary")),
        name="rwkv7_mixer",
    )(*args)


def _mixout_kernel(alpha, ya_ref, yb_ref, yc_ref, yd_ref, wo_ref, x_ref, gt_ref, lng_ref, lnb_ref,
                   sh_ref, sc_ref, wr_ref, br_ref, x1_ref, h2_ref, lg_ref):
    g = ya_ref.shape[2]
    y = (jnp.dot(ya_ref[0], wo_ref[0:g, :], preferred_element_type=F32)
         + jnp.dot(yb_ref[0], wo_ref[g:2 * g, :], preferred_element_type=F32)
         + jnp.dot(yc_ref[0], wo_ref[2 * g:3 * g, :], preferred_element_type=F32)
         + jnp.dot(yd_ref[0], wo_ref[3 * g:, :], preferred_element_type=F32))
    x1 = _layer_norm(alpha * x_ref[0] + (1.0 + gt_ref[0]) * y) * lng_ref[...] + lnb_ref[...]
    x1_ref[0] = x1
    h2 = _layer_norm(x1) * (1.0 + sc_ref[0]) + sh_ref[0]
    h2_ref[0] = h2
    lg_ref[0] = _dot(h2, wr_ref[...]) + br_ref[...]


def _mix_out(alpha, ys, w_out, x, gate, ln_g, ln_b, shift2, scale2, w_router, b_router):
    bsz, s, d = x.shape
    g = ys[0].shape[2]
    tm = 256
    tok = lambda w: pl.BlockSpec((1, tm, w), lambda b, i: (b, i, 0))
    per_b = pl.BlockSpec((1, 1, d), lambda b, i: (b, 0, 0))
    full = lambda a: pl.BlockSpec(a.shape, lambda b, i: (0,) * a.ndim)
    row = lambda t: t.astype(F32).reshape(1, -1)
    args = (*ys, w_out, x, gate[:, None, :], row(ln_g), row(ln_b), shift2[:, None, :], scale2[:, None, :],
            w_router, b_router)
    return pl.pallas_call(
        functools.partial(_mixout_kernel, alpha),
        grid=(bsz, s // tm),
        in_specs=[tok(g)] * 4 + [full(w_out), tok(d), per_b, full(args[7]), full(args[8]), per_b, per_b,
                                 full(w_router), full(b_router)],
        out_specs=[tok(d), tok(d), tok(LANES)],
        out_shape=[jax.ShapeDtypeStruct((bsz, s, d), F32), jax.ShapeDtypeStruct((bsz, s, d), F32),
                   jax.ShapeDtypeStruct((bsz, s, LANES), F32)],
        compiler_params=_cparams(("parallel", "parallel")),
        name="mix_out_ln_router",
    )(*args)


def _dispatch_kernel(h_ref, lg_ref, xs_ref, info_ref, seg_ref):
    tm = lg_ref.shape[0]
    rt = xs_ref.shape[0]
    half = h_ref.shape[1] // 2
    lg = lg_ref[...]
    lane = lax.broadcasted_iota(I32, (tm, LANES), 1)
    lane_f = lane.astype(F32)

    def top1(vals, mask):
        mv = jnp.where(mask, vals, -jnp.inf)
        m = jnp.max(mv, axis=-1, keepdims=True)
        idx = jnp.min(jnp.where(jnp.logical_and(mask, mv == m), lane_f, float(LANES)), axis=-1, keepdims=True)
        return m, idx.astype(I32)

    gmask = lane < N_EXPERT_GROUPS
    gm, gidx = top1(lg, gmask)
    g_val = 1.0 / jnp.sum(jnp.where(gmask, jnp.exp(lg - gm), 0.0), axis=-1, keepdims=True)
    elo = N_EXPERT_GROUPS + gidx * EXPERTS_PER_GROUP
    emask = jnp.logical_and(lane >= elo, lane < elo + EXPERTS_PER_GROUP)
    m1, i1 = top1(lg, emask)
    m2, i2 = top1(lg, jnp.logical_and(emask, lane != i1))
    e21 = jnp.exp(m2 - m1)
    w1 = g_val / (1.0 + e21)
    w2 = g_val * e21 / (1.0 + e21)
    e1 = i1 - N_EXPERT_GROUPS
    e2 = i2 - N_EXPERT_GROUPS
    oh1 = (lane == e1)
    oh2 = (lane == e2)
    ohs = jnp.where(jnp.logical_or(oh1, oh2), 1.0, 0.0)
    cnt = jnp.sum(ohs, axis=0, keepdims=True)
    units = jnp.floor((cnt + (MOE_UNIT - 1.0)) * (1.0 / MOE_UNIT))
    li = lax.broadcasted_iota(I32, (LANES, LANES), 0)
    lj = lax.broadcasted_iota(I32, (LANES, LANES), 1)
    upper = jnp.where(li < lj, 1.0, 0.0).astype(BF16)
    ustart = jnp.dot(jnp.broadcast_to(units, (SUBLANES, LANES)).astype(BF16), upper,
                     preferred_element_type=F32)[0:1, :]
    ri = lax.broadcasted_iota(I32, (tm, tm), 0)
    ci = lax.broadcasted_iota(I32, (tm, tm), 1)
    tri = jnp.where(ci < ri, 1.0, 0.0).astype(BF16)
    before = jnp.dot(tri, ohs.astype(BF16), preferred_element_type=F32)
    first = before + MOE_UNIT * ustart
    pos1 = jnp.sum(jnp.where(oh1, first, 0.0), axis=-1, keepdims=True)
    pos2 = jnp.sum(jnp.where(oh2, first, 0.0), axis=-1, keepdims=True)
    posm = jnp.where(lane == 0, pos1, jnp.where(lane == 1, pos2, -1.0))
    post = posm.T
    prow = lax.broadcasted_iota(I32, (rt, tm), 0).astype(F32)
    sel1 = prow == post[0:1, :]
    sel2 = prow == post[1:2, :]
    hb = h_ref[...].astype(BF16)
    xs = jnp.dot(jnp.where(jnp.logical_or(sel1, sel2), 1.0, 0.0).astype(BF16), hb, preferred_element_type=F32)
    bits = pltpu.bitcast(xs, jnp.uint32)
    xs_ref[:, :half] = jnp.bitwise_or(jnp.right_shift(bits[:, :half], jnp.uint32(16)), bits[:, half:])
    def terms(w):
        a = w.astype(BF16).astype(F32)
        b = (w - a).astype(BF16).astype(F32)
        return a, b, ((w - a) - b).astype(BF16).astype(F32)
    t1 = terms(w1)
    t2 = terms(w2)
    wm = jnp.zeros((tm, LANES), F32)
    for k, t in enumerate(t1 + t2):
        wm = jnp.where(lane == k, t, wm)
    wmb = wm.astype(BF16)
    s1 = jnp.dot(jnp.where(sel1, 1.0, 0.0).astype(BF16), wmb, preferred_element_type=F32)
    s2 = jnp.dot(jnp.where(sel2, 1.0, 0.0).astype(BF16), wmb, preferred_element_type=F32)
    wrow = (s1[:, 0:1] + s1[:, 1:2] + s1[:, 2:3]) + (s2[:, 3:4] + s2[:, 4:5] + s2[:, 5:6])
    mlane = lax.broadcasted_iota(I32, (rt, LANES), 1)
    wbits = pltpu.bitcast(jnp.broadcast_to(wrow, (rt, LANES)), jnp.uint32)
    xs_ref[:, half:] = jnp.where(mlane == 0, wbits, jnp.uint32(0))
    info_ref[...] = jnp.where(lane == 0, pos1, jnp.where(lane == 1, pos2, 0.0)).astype(I32)
    srow = lax.broadcasted_iota(I32, (SUBLANES, LANES), 0)
    total = jnp.sum(units, axis=-1, keepdims=True)
    seg = jnp.where(srow == 0, units, jnp.where(srow == 1, ustart, jnp.where(srow == 2, total, 0.0)))
    seg_ref[0] = seg.astype(I32)


def _dispatch(h2, logits):
    t, d = h2.shape
    nt = t // MOE_TILE
    return pl.pallas_call(
        _dispatch_kernel,
        grid=(nt,),
        in_specs=[pl.BlockSpec((MOE_TILE, d), lambda i: (i, 0)),
                  pl.BlockSpec((MOE_TILE, LANES), lambda i: (i, 0))],
        out_specs=[pl.BlockSpec((MOE_TILE_ROWS, d // 2 + LANES), lambda i: (i, 0)),
                   pl.BlockSpec((MOE_TILE, LANES), lambda i: (i, 0)),
                   pl.BlockSpec((1, SUBLANES, LANES), lambda i: (i, 0, 0))],
        out_shape=[jax.ShapeDtypeStruct((nt * MOE_TILE_ROWS, d // 2 + LANES), jnp.uint32),
                   jax.ShapeDtypeStruct((t, LANES), I32),
                   jax.ShapeDtypeStruct((nt, SUBLANES, LANES), I32)],
        compiler_params=_cparams(("parallel",)),
        name="moe_dispatch",
    )(h2, logits)


def _worklist_kernel(units_ref, ustart_ref, uidx_ref, slot_ref, be_ref, nb_ref):
    n_seg = units_ref.shape[0]
    n_tiles = n_seg // N_EXPERTS
    n_slots = uidx_ref.shape[0]
    n_blocks = be_ref.shape[0]
    tile_units = MOE_TILE_ROWS // MOE_UNIT

    def clear(i, c):
        uidx_ref[i] = -1
        return c

    lax.fori_loop(0, n_slots, clear, 0)

    def clear_slot(i, c):
        slot_ref[i] = -1
        return c

    lax.fori_loop(0, slot_ref.shape[0], clear_slot, 0)

    def expert(e, blk0):
        def tile(tau, pos):
            n = units_ref[tau * N_EXPERTS + e]
            base = tau * tile_units + ustart_ref[tau * N_EXPERTS + e]

            def unit(j, p):
                uidx_ref[p] = base + j
                slot_ref[base + j] = p
                return p + 1

            return lax.fori_loop(0, n, unit, pos)

        end = lax.fori_loop(0, n_tiles, tile, blk0 * MOE_BLOCK_UNITS)
        nblk = lax.shift_right_logical(end - blk0 * MOE_BLOCK_UNITS + (MOE_BLOCK_UNITS - 1),
                                       int(math.log2(MOE_BLOCK_UNITS)))

        def mark(i, c):
            be_ref[blk0 + i] = e
            return c

        lax.fori_loop(0, nblk, mark, 0)
        return blk0 + nblk

    used = lax.fori_loop(0, N_EXPERTS, expert, 0)
    nb_ref[0] = used

    def tail(i, c):
        be_ref[i] = N_EXPERTS - 1
        return c

    lax.fori_loop(used, n_blocks, tail, 0)


def _worklist(units, ustart, n_blocks):
    smem = pl.BlockSpec(memory_space=pltpu.SMEM)
    n_tiles = units.shape[0] // N_EXPERTS
    return pl.pallas_call(
        _worklist_kernel,
        in_specs=[smem, smem],
        out_specs=[smem, smem, smem, smem],
        out_shape=[jax.ShapeDtypeStruct((n_blocks * MOE_BLOCK_UNITS,), I32),
                   jax.ShapeDtypeStruct((n_tiles * (MOE_TILE_ROWS // MOE_UNIT),), I32),
                   jax.ShapeDtypeStruct((n_blocks,), I32), jax.ShapeDtypeStruct((1,), I32)],
        name="moe_worklist",
    )(units, ustart)


def _expert_kernel(be_ref, uidx_ref, nb_ref, xs_hbm, w1_ref, w3_ref, w2_ref, o_ref,
                   xbuf, gsem, w1b, w3b, w2b):
    b = pl.program_id(0)
    used = nb_ref[0]
    half = xs_hbm.shape[1] - LANES
    slot = lax.rem(b, 2)
    other = 1 - slot

    def gather_copy(s, j, unit):
        return pltpu.make_async_copy(xs_hbm.at[pl.ds(pl.multiple_of(unit * MOE_UNIT, MOE_UNIT), MOE_UNIT), :],
                                     xbuf.at[s, pl.ds(j * MOE_UNIT, MOE_UNIT), :], gsem.at[s])

    def gather_start(blk, s):
        for j in range(MOE_BLOCK_UNITS):
            gather_copy(s, j, jnp.maximum(uidx_ref[blk * MOE_BLOCK_UNITS + j], 0)).start(priority=j % 2)

    def gather_wait(s):
        for j in range(MOE_BLOCK_UNITS):
            gather_copy(s, j, 0).wait()

    @pl.when(b == 0)
    def _():
        gather_start(0, 0)

    @pl.when(b + 1 < used)
    def _():
        gather_start(b + 1, other)

    @pl.when(b >= used)
    def _():
        o_ref[...] = jnp.zeros_like(o_ref)

    @pl.when(b < used)
    def _():
        prev = be_ref[jnp.maximum(b - 1, 0)]

        @pl.when(jnp.logical_or(b == 0, be_ref[b] != prev))
        def _():
            w1b[...] = w1_ref[0, 0].astype(BF16)
            w3b[...] = w3_ref[0, 0].astype(BF16)
            w2b[...] = w2_ref[0, 0].astype(BF16)

        gather_wait(slot)
        xw = xbuf[slot]
        word = xw[:, :half]
        lo = pltpu.bitcast(jnp.left_shift(word, jnp.uint32(16)), F32)
        hi = pltpu.bitcast(jnp.bitwise_and(word, jnp.uint32(0xFFFF0000)), F32)
        x = jnp.concatenate([lo, hi], axis=1).astype(BF16)
        wrow = pltpu.bitcast(xw[:, half:], F32)[:, 0:1]
        a = jnp.dot(x, w1b[...], preferred_element_type=F32)
        gte = jnp.dot(x, w3b[...], preferred_element_type=F32)
        mid = (a * _sigmoid(a)) * gte
        o_ref[...] = jnp.dot(mid.astype(BF16), w2b[...], preferred_element_type=F32) * wrow


def _expert_ffn(layer, xs, blk_expert, unit_idx, n_used, w1, w3, w2):
    d, de = w1.shape[2], w1.shape[3]
    nb = blk_expert.shape[0]
    rows = MOE_BLOCK_UNITS * MOE_UNIT
    grid_spec = pltpu.PrefetchScalarGridSpec(
        num_scalar_prefetch=3,
        grid=(nb,),
        in_specs=[pl.BlockSpec(memory_space=pl.ANY),
                  pl.BlockSpec((1, 1, d, de), lambda b, be, ui, nu: (layer, be[b], 0, 0)),
                  pl.BlockSpec((1, 1, d, de), lambda b, be, ui, nu: (layer, be[b], 0, 0)),
                  pl.BlockSpec((1, 1, de, d), lambda b, be, ui, nu: (layer, be[b], 0, 0))],
        out_specs=pl.BlockSpec((rows, d), lambda b, be, ui, nu: (b, 0)),
        scratch_shapes=[pltpu.VMEM((2, rows, xs.shape[1]), jnp.uint32), pltpu.SemaphoreType.DMA((2,)),
                        pltpu.VMEM((d, de), BF16), pltpu.VMEM((d, de), BF16), pltpu.VMEM((de, d), BF16)],
    )
    return pl.pallas_call(
        _expert_kernel,
        grid_spec=grid_spec,
        out_shape=jax.ShapeDtypeStruct((nb * rows, d), F32),
        compiler_params=_cparams(("arbitrary",)),
        name="moe_expert_ffn",
    )(blk_expert, unit_idx, n_used, xs, w1, w3, w2)


def _combine_kernel(alpha, nu_ref, slot_ref, ys_hbm, info_ref, x_ref, gt_ref, lng_ref, lnb_ref, o_ref, ybuf, sem):
    i = pl.program_id(0)
    nt = pl.num_programs(0)
    tm = x_ref.shape[0]
    rt = ybuf.shape[1]
    slot = lax.rem(i, 2)
    other = 1 - slot

    def unit_copy(tile, s, j):
        src = pl.multiple_of(slot_ref[tile * (rt // MOE_UNIT) + j] * MOE_UNIT, MOE_UNIT)
        dst = pl.multiple_of(j * MOE_UNIT, MOE_UNIT)
        return pltpu.make_async_copy(ys_hbm.at[pl.ds(src, MOE_UNIT), :], ybuf.at[s, pl.ds(dst, MOE_UNIT), :], sem.at[s])

    def start(tile, s):
        def body(j, c):
            unit_copy(tile, s, j).start()
            return c
        lax.fori_loop(0, nu_ref[tile], body, 0)

    def wait(tile, s):
        def body(j, c):
            unit_copy(tile, s, j).wait()
            return c
        lax.fori_loop(0, nu_ref[tile], body, 0)

    @pl.when(i == 0)
    def _():
        ybuf[...] = jnp.zeros_like(ybuf)
        start(0, 0)

    @pl.when(i + 1 < nt)
    def _():
        start(i + 1, other)

    wait(i, slot)
    info = info_ref[...]
    col = lax.broadcasted_iota(I32, (tm, rt), 1)
    pick = jnp.where(jnp.logical_or(col == info[:, 0:1], col == info[:, 1:2]), 1.0, 0.0).astype(BF16)
    ys = ybuf[slot]
    y = _dot_split(ys, pick, lhs=True)
    o_ref[...] = _layer_norm(alpha * x_ref[...] + (1.0 + gt_ref[0]) * y) * lng_ref[...] + lnb_ref[...]


def _combine(alpha, ys, tile_units, unit_slot, info, x1, gate, ln_g, ln_b, seq):
    t, d = x1.shape
    tm = MOE_TILE
    per_seq = seq // tm
    grid_spec = pltpu.PrefetchScalarGridSpec(
        num_scalar_prefetch=2,
        grid=(t // tm,),
        in_specs=[pl.BlockSpec(memory_space=pl.ANY),
                  pl.BlockSpec((tm, LANES), lambda i, nu, us: (i, 0)),
                  pl.BlockSpec((tm, d), lambda i, nu, us: (i, 0)),
                  pl.BlockSpec((1, 1, d), lambda i, nu, us: (i // per_seq, 0, 0)),
                  pl.BlockSpec((1, d), lambda i, nu, us: (0, 0)),
                  pl.BlockSpec((1, d), lambda i, nu, us: (0, 0))],
        out_specs=pl.BlockSpec((tm, d), lambda i, nu, us: (i, 0)),
        scratch_shapes=[pltpu.VMEM((2, MOE_TILE_ROWS, d), F32), pltpu.SemaphoreType.DMA((2,))],
    )
    return pl.pallas_call(
        functools.partial(_combine_kernel, alpha),
        grid_spec=grid_spec,
        out_shape=jax.ShapeDtypeStruct((t, d), F32),
        compiler_params=_cparams(("arbitrary",)),
        name="moe_combine_ln",
    )(tile_units, unit_slot, ys, info, x1, gate[:, None, :], ln_g.astype(F32).reshape(1, d),
      ln_b.astype(F32).reshape(1, d))


def _moe(layer, alpha, h2, logits, x1, gate, ln_g, ln_b, w1, w3, w2):
    bsz, s, d = x1.shape
    t = bsz * s
    nt = t // MOE_TILE
    xs, info, seg = _dispatch(h2.reshape(t, d), logits.reshape(t, LANES))
    units = seg[:, 0, :N_EXPERTS].reshape(nt * N_EXPERTS)
    ustart = seg[:, 1, :N_EXPERTS].reshape(nt * N_EXPERTS)
    tile_units = seg[:, 2, 0]
    max_units = nt * (TOP_K * MOE_TILE // MOE_UNIT + N_EXPERTS * (MOE_UNIT - 1) // MOE_UNIT)
    n_blocks = max_units // MOE_BLOCK_UNITS + N_EXPERTS
    unit_idx, unit_slot, blk_expert, n_used = _worklist(units, ustart, n_blocks)
    ys = _expert_ffn(layer, xs, blk_expert, unit_idx, n_used, w1, w3, w2)
    x2 = _combine(alpha, ys, tile_units, unit_slot, info, x1.reshape(t, d), gate, ln_g, ln_b, s)
    return x2.reshape(bsz, s, d)


def kernel(x, c, w_ada, b_ada, ln_g, ln_b, w_in, w_out, conv_w, rwkv_mu, rwkv_w0, rwkv_w2, rwkv_a0, rwkv_a2, rwkv_g2, rwkv_kk, rwkv_ka, rwkv_rk, rwkv_gn_g, rwkv_gn_b, attn_sinks, rel_bias, s5_lambda_re, s5_lambda_im, s5_log_dt, s5_b_re, s5_b_im, s5_c_re, s5_c_im, s5_d, s5_glu_w, s5_glu_b, router_group_w, router_group_b, router_expert_w, router_expert_b, moe_w1, moe_w3, moe_w2):
    depth = w_ada.shape[0]
    d = x.shape[-1]
    g = d // 4
    alpha = (2 * depth) ** 0.25
    n_heads = g // HEAD_DIM
    att_kv = max(1, n_heads // 4) * HEAD_DIM
    rw_off = 3 * g
    lora = RWKV_DECAY_RANK + RWKV_A_RANK + RWKV_GATE_RANK
    att_off = rw_off + 3 * g + lora
    s5_off = att_off + g + 2 * att_kv

    mod = _modulation(c, w_ada, b_ada)
    for l in range(depth):
        sh1, sc1, gt1, sh2, sc2, gt2 = jnp.split(mod[l], 6, axis=-1)
        wl = w_in[l]
        w_conv = wl[:, :rw_off].astype(BF16)
        w_rkv = wl[:, rw_off:rw_off + 3 * g].astype(BF16)
        lo = rw_off + 3 * g
        zcol = lambda n: jnp.zeros((d, n), F32)
        w_lora = jnp.concatenate([
            wl[:, lo:lo + RWKV_DECAY_RANK], zcol(LANES - RWKV_DECAY_RANK),
            wl[:, lo + RWKV_DECAY_RANK:lo + RWKV_DECAY_RANK + RWKV_A_RANK], zcol(LANES - RWKV_A_RANK),
            wl[:, lo + RWKV_DECAY_RANK + RWKV_A_RANK:att_off]], axis=1).astype(BF16)
        w_q = wl[:, att_off:att_off + g].astype(BF16)
        w_kv = wl[:, att_off + g:s5_off].astype(BF16)
        w_s5 = wl[:, s5_off:].astype(BF16)

        h = _adaln(x, sh1, sc1)
        y_conv = _conv_mixer(h, w_conv, conv_w[l].astype(F32))
        y_rwkv = _rwkv_mixer(h, w_rkv, w_lora, rwkv_mu[l], rwkv_w0[l], rwkv_w2[l], rwkv_a0[l], rwkv_a2[l],
                             rwkv_g2[l], rwkv_kk[l], rwkv_ka[l], rwkv_rk[l], rwkv_gn_g[l], rwkv_gn_b[l])
        y_att = _att_mixer(h, w_q, w_kv, attn_sinks[l], rel_bias)
        y_ssm = _s5_mixer(h, w_s5, s5_lambda_re[l], s5_lambda_im[l], s5_log_dt[l], s5_b_re[l], s5_b_im[l],
                          s5_c_re[l], s5_c_im[l], s5_d[l], s5_glu_w[l], s5_glu_b[l])
        w_router = jnp.zeros((d, LANES), F32)
        w_router = w_router.at[:, :N_EXPERT_GROUPS].set(router_group_w[l])
        w_router = w_router.at[:, N_EXPERT_GROUPS:N_EXPERT_GROUPS + N_EXPERTS].set(router_expert_w[l]).astype(BF16)
        b_router = jnp.zeros((1, LANES), F32)
        b_router = b_router.at[0, :N_EXPERT_GROUPS].set(router_group_b[l])
        b_router = b_router.at[0, N_EXPERT_GROUPS:N_EXPERT_GROUPS + N_EXPERTS].set(router_expert_b[l])
        x1, h2, logits = _mix_out(alpha, (y_conv, y_rwkv, y_att, y_ssm), w_out[l].astype(BF16), x, gt1,
                                  ln_g[l, 0], ln_b[l, 0], sh2, sc2, w_router, b_router)
        x = _moe(l, alpha, h2, logits, x1, gt2, ln_g[l, 1], ln_b[l, 1], moe_w1, moe_w3, moe_w2)
    return x
```

```python
import functools
import math

import numpy as np
import jax
import jax.numpy as jnp
from jax import lax
from jax.experimental import pallas as pl
from jax.experimental.pallas import tpu as pltpu

F32 = jnp.float32
BF16 = jnp.bfloat16
I32 = jnp.int32

HEAD_DIM = 64
CONV_WIDTH = 3
RWKV_DECAY_RANK = 96
RWKV_A_RANK = 96
RWKV_GATE_RANK = 128
RWKV_GN_EPS = 64e-5
ATT_BLOCK = 128
WINDOW = 128
N_BUCKETS = 32
NEG_INF = -1e30
S5_CH = 16
S5_STATE = 64
N_EXPERT_GROUPS = 4
EXPERTS_PER_GROUP = 8
N_EXPERTS = N_EXPERT_GROUPS * EXPERTS_PER_GROUP
TOP_K = 2
LN_EPS = 1e-5

LANES = 128
SUBLANES = 8
WKV_CHUNK = 64
MOE_TILE = 256
MOE_UNIT = SUBLANES
MOE_TILE_ROWS = 768
MOE_BLOCK_UNITS = 32
VMEM_LIMIT = 56 * 2 ** 20


def _cparams(sem, flags=None):
    return pltpu.CompilerParams(dimension_semantics=sem, vmem_limit_bytes=VMEM_LIMIT, flags=flags)


def _dot(a, b):
    return jnp.dot(a.astype(BF16), b.astype(BF16), preferred_element_type=F32)


def _dot_nt(a, b):
    return lax.dot_general(a.astype(BF16), b.astype(BF16), (((1,), (1,)), ((), ())),
                           preferred_element_type=F32)


def _dot_tn(a, b):
    return jnp.dot(a.T.astype(BF16), b.astype(BF16), preferred_element_type=F32)


def _dot_split(x, e, lhs=False):
    hi = x.astype(BF16)
    lo = (x - hi.astype(F32)).astype(BF16)
    if lhs:
        return jnp.dot(e, hi, preferred_element_type=F32) + jnp.dot(e, lo, preferred_element_type=F32)
    return jnp.dot(hi, e, preferred_element_type=F32) + jnp.dot(lo, e, preferred_element_type=F32)


def _sigmoid(x):
    return 1.0 / (1.0 + jnp.exp(-x))


def _layer_norm(x):
    mean = jnp.mean(x, axis=-1, keepdims=True)
    xc = x - mean
    var = jnp.mean(xc * xc, axis=-1, keepdims=True)
    return xc * lax.rsqrt(var + LN_EPS)


def _shift_rows(p, carry_row, n):
    row = lax.broadcasted_iota(I32, (p.shape[0], 1), 0)
    out = pltpu.roll(p, n, 0)
    for i in range(n):
        out = jnp.where(row == i, carry_row[SUBLANES - n + i:SUBLANES - n + i + 1, :], out)
    return out


def _mod_kernel(c_ref, w_ref, b_ref, o_ref):
    c = c_ref[...]
    a = c * _sigmoid(c)
    o_ref[0] = _dot(a, w_ref[0]) + b_ref[0]


def _modulation(c, w_ada, b_ada):
    depth, d, n = w_ada.shape
    bsz = c.shape[0]
    tn = 1536
    cp = jnp.zeros((SUBLANES, d), F32).at[:bsz].set(c)
    out = pl.pallas_call(
        _mod_kernel,
        grid=(depth, n // tn),
        in_specs=[pl.BlockSpec((SUBLANES, d), lambda l, j: (0, 0)),
                  pl.BlockSpec((1, d, tn), lambda l, j: (l, 0, j)),
                  pl.BlockSpec((1, 1, tn), lambda l, j: (l, 0, j))],
        out_specs=pl.BlockSpec((1, SUBLANES, tn), lambda l, j: (l, 0, j)),
        out_shape=jax.ShapeDtypeStruct((depth, SUBLANES, n), F32),
        compiler_params=_cparams(("parallel", "parallel")),
        name="adaln_modulation",
    )(cp, w_ada, b_ada.reshape(depth, 1, n))
    return out[:, :bsz]


def _adaln_kernel(x_ref, sh_ref, sc_ref, h_ref):
    h_ref[0] = (_layer_norm(x_ref[0]) * (1.0 + sc_ref[0]) + sh_ref[0]).astype(BF16)


def _adaln(x, shift, scale):
    bsz, s, d = x.shape
    tm = 512
    return pl.pallas_call(
        _adaln_kernel,
        grid=(bsz, s // tm),
        in_specs=[pl.BlockSpec((1, tm, d), lambda b, i: (b, i, 0)),
                  pl.BlockSpec((1, 1, d), lambda b, i: (b, 0, 0)),
                  pl.BlockSpec((1, 1, d), lambda b, i: (b, 0, 0))],
        out_specs=pl.BlockSpec((1, tm, d), lambda b, i: (b, i, 0)),
        out_shape=jax.ShapeDtypeStruct((bsz, s, d), BF16),
        compiler_params=_cparams(("parallel", "parallel")),
        name="adaln_input",
    )(x, shift[:, None, :], scale[:, None, :])


def _conv_kernel(h_ref, w_ref, cw_ref, y_ref, carry_ref):
    tm = h_ref.shape[1]
    g = y_ref.shape[2]

    @pl.when(pl.program_id(1) == 0)
    def _():
        carry_ref[...] = jnp.zeros_like(carry_ref)

    p = _dot(h_ref[0], w_ref[...])
    b_gate, c_gate, hh = p[:, :g], p[:, g:2 * g], p[:, 2 * g:]
    z = c_gate * hh
    carry = carry_ref[...]
    z1 = _shift_rows(z, carry, 1)
    z2 = _shift_rows(z, carry, 2)
    cw = cw_ref[...]
    out = cw[0:1] * z2 + cw[1:2] * z1 + cw[2:3] * z
    y_ref[0] = (b_gate * out).astype(BF16)
    carry_ref[...] = z[tm - SUBLANES:, :]


def _conv_mixer(h, w, conv_w):
    bsz, s, d = h.shape
    g = conv_w.shape[1]
    tm = 512
    return pl.pallas_call(
        _conv_kernel,
        grid=(bsz, s // tm),
        in_specs=[pl.BlockSpec((1, tm, d), lambda b, i: (b, i, 0)),
                  pl.BlockSpec((d, 3 * g), lambda b, i: (0, 0)),
                  pl.BlockSpec((CONV_WIDTH, g), lambda b, i: (0, 0))],
        out_specs=pl.BlockSpec((1, tm, g), lambda b, i: (b, i, 0)),
        out_shape=jax.ShapeDtypeStruct((bsz, s, g), BF16),
        scratch_shapes=[pltpu.VMEM((SUBLANES, g), F32)],
        compiler_params=_cparams(("parallel", "arbitrary")),
        name="conv_mixer",
    )(h, w, conv_w)


def _t5_bucket(rel):
    n = jnp.maximum(rel, 0)
    max_exact = N_BUCKETS // 2
    n_f = jnp.maximum(n, 1).astype(F32)
    large = max_exact + (jnp.log(n_f / max_exact) / math.log(WINDOW / max_exact)
                         * (N_BUCKETS - max_exact)).astype(I32)
    return jnp.where(n < max_exact, n, jnp.minimum(large, N_BUCKETS - 1))


def _att_kernel(sink_ref, h_ref, wq_ref, wkv_ref, bias_ref, y_ref, kvc_ref):
    tm = h_ref.shape[1]
    n_heads = bias_ref.shape[0]
    kvw = wkv_ref.shape[1] // 2
    n_kv = kvw // HEAD_DIM
    rep = n_heads // n_kv
    blk = ATT_BLOCK
    first_tile = pl.program_id(1) == 0

    @pl.when(first_tile)
    def _():
        kvc_ref[...] = jnp.zeros_like(kvc_ref)

    x = h_ref[0]
    q = _dot(x, wq_ref[...]) * (HEAD_DIM ** -0.5)
    kv = _dot(x, wkv_ref[...])
    kvext = jnp.concatenate([kvc_ref[...], kv], axis=0)
    kvc_ref[...] = kv[tm - blk:, :]
    col = lax.broadcasted_iota(I32, (blk, 2 * blk), 1)
    qb16 = q.astype(BF16)
    kv16 = kvext.astype(BF16)

    def scores(j):
        qb = qb16[j * blk:(j + 1) * blk]
        kw = kv16[j * blk:j * blk + 2 * blk, :kvw]
        kgs = [kw[:, gi * HEAD_DIM:(gi + 1) * HEAD_DIM] for gi in range(n_kv)]
        scs = [_dot_nt(qb[:, hh * HEAD_DIM:(hh + 1) * HEAD_DIM], kgs[hh // rep]) + bias_ref[hh]
               for hh in range(n_heads)]
        if j == 0:
            scs = [jnp.where(jnp.logical_and(first_tile, col < blk), NEG_INF, sc) for sc in scs]
        return scs

    def probs(scs):
        out = []
        for hh, sc in enumerate(scs):
            sink = sink_ref[hh]
            m = jnp.maximum(jnp.max(sc, axis=-1, keepdims=True), sink)
            e = jnp.exp(sc - m)
            den = jnp.sum(e, axis=-1, keepdims=True) + jnp.exp(sink - m)
            out.append((e / den).astype(BF16))
        return out

    def values(j, ps):
        vw = kv16[j * blk:j * blk + 2 * blk, kvw:]
        vgs = [vw[:, gi * HEAD_DIM:(gi + 1) * HEAD_DIM] for gi in range(n_kv)]
        outs = [jnp.dot(p, vgs[hh // rep], preferred_element_type=F32) for hh, p in enumerate(ps)]
        y_ref[0, j * blk:(j + 1) * blk, :] = jnp.concatenate(outs, axis=1).astype(BF16)

    nblk = tm // blk
    pending = scores(0)
    for j in range(nblk):
        nxt = scores(j + 1) if j + 1 < nblk else None
        values(j, probs(pending))
        pending = nxt


def _att_mixer(h, wq, wkv, sinks, rel_bias):
    bsz, s, d = h.shape
    n_heads = sinks.shape[0]
    tm = 512
    qi = jnp.arange(ATT_BLOCK)[:, None]
    kj = jnp.arange(2 * ATT_BLOCK)[None, :]
    rel = qi + ATT_BLOCK - kj
    valid = (rel >= 0) & (rel < WINDOW)
    onehot = (_t5_bucket(rel)[..., None] == jnp.arange(N_BUCKETS)).astype(F32)
    bias = jnp.einsum('qkb,bh->hqk', onehot, rel_bias.astype(F32), precision=lax.Precision.HIGHEST)
    bias = jnp.where(valid[None], bias, NEG_INF)
    return pl.pallas_call(
        _att_kernel,
        grid=(bsz, s // tm),
        in_specs=[pl.BlockSpec(memory_space=pltpu.SMEM),
                  pl.BlockSpec((1, tm, d), lambda b, i: (b, i, 0)),
                  pl.BlockSpec(wq.shape, lambda b, i: (0, 0)),
                  pl.BlockSpec(wkv.shape, lambda b, i: (0, 0)),
                  pl.BlockSpec(bias.shape, lambda b, i: (0, 0, 0))],
        out_specs=pl.BlockSpec((1, tm, wq.shape[1]), lambda b, i: (b, i, 0)),
        out_shape=jax.ShapeDtypeStruct((bsz, s, wq.shape[1]), BF16),
        scratch_shapes=[pltpu.VMEM((ATT_BLOCK, wkv.shape[1]), F32)],
        compiler_params=_cparams(("parallel", "arbitrary")),
        name="swa_mixer",
    )(sinks.astype(F32), h, wq, wkv, bias)


S5_GROUPS_PER_BLOCK = LANES // S5_CH


def _s5_tables(lam_re, lam_im, log_dt, b_re, b_im, c_re, c_im):
    n_groups, p = lam_re.shape
    lr, li = lam_re.astype(F32), lam_im.astype(F32)
    delta = jnp.exp(log_dt.astype(F32))[:, None]
    mag = jnp.exp(lr * delta)
    ab_re, ab_im = mag * jnp.cos(li * delta), mag * jnp.sin(li * delta)
    den = lr * lr + li * li
    z_re = ((ab_re - 1.0) * lr + ab_im * li) / den
    z_im = (ab_im * lr - (ab_re - 1.0) * li) / den
    br, bi = b_re.astype(F32), b_im.astype(F32)
    bb_re = z_re[..., None] * br - z_im[..., None] * bi
    bb_im = z_re[..., None] * bi + z_im[..., None] * br
    nblk = n_groups // S5_GROUPS_PER_BLOCK
    eye = jnp.eye(S5_GROUPS_PER_BLOCK, dtype=F32)

    def in_blocks(bb):
        bb = bb.reshape(nblk, S5_GROUPS_PER_BLOCK, p, S5_CH)
        return jnp.einsum('qgpc,gh->qgchp', bb, eye).reshape(nblk, LANES, S5_GROUPS_PER_BLOCK * p)

    def out_blocks(cc):
        cc = cc.astype(F32).reshape(nblk, S5_GROUPS_PER_BLOCK, S5_CH, p)
        return jnp.einsum('qgcp,gh->qgphc', cc, eye).reshape(nblk, S5_GROUPS_PER_BLOCK * p, LANES)

    def power(m):
        mg = jnp.exp(m * lr * delta)
        return (mg * jnp.cos(m * li * delta)).reshape(1, -1), (mg * jnp.sin(m * li * delta)).reshape(1, -1)

    row = jnp.arange(SUBLANES, dtype=F32)[:, None]
    tabs = []
    for sft in (1, 2, 4):
        pr, pi = power(float(sft))
        keep = row >= sft
        tabs += [jnp.where(keep, pr, 0.0), jnp.where(keep, pi, 0.0)]
    n_state = n_groups * p
    lrd = (lr * delta).reshape(1, n_state)
    lid = (li * delta).reshape(1, n_state)
    mg = jnp.exp((row + 1.0) * lrd)
    tabs += [mg * jnp.cos((row + 1.0) * lid), mg * jnp.sin((row + 1.0) * lid)]
    tables = jnp.stack(tabs, axis=0)
    return (in_blocks(bb_re).astype(BF16), in_blocks(bb_im).astype(BF16),
            out_blocks(c_re).astype(BF16), out_blocks(c_im).astype(BF16), tables)


def _s5_kernel(h_ref, w_ref, bre_ref, bim_ref, cre_ref, cim_ref, tab_ref, d_ref, gw_ref, gb_ref,
               y_ref, xr_ref, xi_ref, cr_ref, ci_ref):
    tm = h_ref.shape[1]
    nblk = bre_ref.shape[0]
    sw = bre_ref.shape[2]

    @pl.when(pl.program_id(1) == 0)
    def _():
        cr_ref[...] = jnp.zeros_like(cr_ref)
        ci_ref[...] = jnp.zeros_like(ci_ref)

    u = _dot(h_ref[0], w_ref[...])
    ub = u.astype(BF16)
    for q in range(nblk):
        uq = ub[:, q * LANES:(q + 1) * LANES]
        xr_ref[:, q * sw:(q + 1) * sw] = jnp.dot(uq, bre_ref[q], preferred_element_type=F32)
        xi_ref[:, q * sw:(q + 1) * sw] = jnp.dot(uq, bim_ref[q], preferred_element_type=F32)

    def tile(i, carry):
        cr, ci = carry
        rows = pl.ds(pl.multiple_of(i * SUBLANES, SUBLANES), SUBLANES)
        xr = xr_ref[rows, :]
        xi = xi_ref[rows, :]
        for k, sft in enumerate((1, 2, 4)):
            mr = tab_ref[2 * k]
            mi = tab_ref[2 * k + 1]
            sr = pltpu.roll(xr, sft, 0)
            si = pltpu.roll(xi, sft, 0)
            xr, xi = xr + mr * sr - mi * si, xi + mr * si + mi * sr
        pr = tab_ref[6]
        pi = tab_ref[7]
        xr, xi = xr + pr * cr - pi * ci, xi + pr * ci + pi * cr
        xr_ref[rows, :] = xr
        xi_ref[rows, :] = xi
        return xr[SUBLANES - 1:SUBLANES, :], xi[SUBLANES - 1:SUBLANES, :]

    cr, ci = lax.fori_loop(0, tm // SUBLANES, tile, (cr_ref[0:1, :], ci_ref[0:1, :]))
    cr_ref[0:1, :] = cr
    ci_ref[0:1, :] = ci

    ys = []
    for q in range(nblk):
        xr = xr_ref[:, q * sw:(q + 1) * sw].astype(BF16)
        xi = xi_ref[:, q * sw:(q + 1) * sw].astype(BF16)
        ys.append(jnp.dot(xr, cre_ref[q], preferred_element_type=F32)
                  - jnp.dot(xi, cim_ref[q], preferred_element_type=F32))
    y = jnp.concatenate(ys, axis=1) + d_ref[...] * u
    y = 0.5 * y * (1.0 + jnp.tanh(math.sqrt(2.0 / math.pi) * (y + 0.044715 * (y * y * y))))
    y_ref[0] = (y * _sigmoid(_dot(y, gw_ref[...]) + gb_ref[...])).astype(BF16)


def _s5_mixer(h, w, lam_re, lam_im, log_dt, b_re, b_im, c_re, c_im, d_skip, glu_w, glu_b):
    bsz, s, d = h.shape
    g = w.shape[1]
    tm = 256
    bre, bim, cre, cim, tables = _s5_tables(lam_re, lam_im, log_dt, b_re, b_im, c_re, c_im)
    n_state = tables.shape[2]
    full = lambda a: pl.BlockSpec(a.shape, lambda b, i: (0,) * a.ndim)
    dvec = d_skip.astype(F32).reshape(1, g)
    gw = glu_w.astype(BF16)
    gb = glu_b.astype(F32).reshape(1, g)
    return pl.pallas_call(
        _s5_kernel,
        grid=(bsz, s // tm),
        in_specs=[pl.BlockSpec((1, tm, d), lambda b, i: (b, i, 0)),
                  full(w), full(bre), full(bim), full(cre), full(cim), full(tables),
                  full(dvec), full(gw), full(gb)],
        out_specs=pl.BlockSpec((1, tm, g), lambda b, i: (b, i, 0)),
        out_shape=jax.ShapeDtypeStruct((bsz, s, g), BF16),
        scratch_shapes=[pltpu.VMEM((tm, n_state), F32), pltpu.VMEM((tm, n_state), F32),
                        pltpu.VMEM((SUBLANES, n_state), F32), pltpu.VMEM((SUBLANES, n_state), F32)],
        compiler_params=_cparams(("parallel", "arbitrary")),
        name="s5_mixer",
    )(h, w, bre, bim, cre, cim, tables, dvec, gw, gb)


def _rwkv_kernel(h_ref, wrkv_ref, wlo_ref, mu1_ref, mu2_ref, w0_ref, w2_ref, a0_ref, a2_ref, g2_ref,
                 kk_ref, ka_ref, rk_ref, gng_ref, gnb_ref, eblk_ref,
                 y_ref,
                 cp_ref, cl_ref, hs_ref, r_s, k_s, v_s, a_s, b_s, ld_s, y_s):
    tm = h_ref.shape[1]
    g = y_ref.shape[2]
    npair = g // LANES
    ch = WKV_CHUNK

    @pl.when(pl.program_id(1) == 0)
    def _():
        cp_ref[...] = jnp.zeros_like(cp_ref)
        cl_ref[...] = jnp.zeros_like(cl_ref)
        hs_ref[...] = jnp.zeros_like(hs_ref)

    x = h_ref[0]
    p = _dot(x, wrkv_ref[...])
    plo = _dot(x, wlo_ref[...])
    pprev = _shift_rows(p, cp_ref[...], 1)
    lprev = _shift_rows(plo, cl_ref[...], 1)
    cp_ref[...] = p[tm - SUBLANES:, :]
    cl_ref[...] = plo[tm - SUBLANES:, :]
    p = p + (pprev - p) * mu1_ref[...]
    plo = plo + (lprev - plo) * mu2_ref[...]
    r, k, v = p[:, :g], p[:, g:2 * g], p[:, 2 * g:]
    w_lo, a_lo, g_lo = plo[:, :LANES], plo[:, LANES:2 * LANES], plo[:, 2 * LANES:]
    wraw = w0_ref[...] + _dot(jnp.tanh(w_lo), w2_ref[...])
    nz = -wraw
    softplus = jnp.maximum(nz, 0.0) + jnp.log(1.0 + jnp.exp(-jnp.abs(nz)))
    w = -softplus - 0.5
    ld_s[...] = -jnp.exp(w)
    a = _sigmoid(a0_ref[...] + _dot(a_lo, a2_ref[...]))
    gate = _dot(_sigmoid(g_lo), g2_ref[...])
    eblk = eblk_ref[...]
    kk = k * kk_ref[...]
    kk = kk / jnp.maximum(jnp.sqrt(_dot_split(kk * kk, eblk)), 1e-12)
    k = k * (1.0 + (a - 1.0) * ka_ref[...])
    r_s[...] = r
    k_s[...] = k
    v_s[...] = v
    a_s[...] = -kk
    b_s[...] = kk * a

    lane = lax.broadcasted_iota(I32, (1, LANES), 1)
    m0 = (lane < HEAD_DIM).astype(F32)
    m1 = 1.0 - m0
    ri = lax.broadcasted_iota(I32, (2 * ch, 2 * ch), 0)
    ci = lax.broadcasted_iota(I32, (2 * ch, 2 * ch), 1)
    same = (ri < ch) == (ci < ch)
    rloc = jnp.bitwise_and(ri, ch - 1)
    cloc = jnp.bitwise_and(ci, ch - 1)
    strict = jnp.where(jnp.logical_and(same, cloc < rloc), 1.0, 0.0)
    incl = jnp.where(jnp.logical_and(same, cloc <= rloc), 1.0, 0.0)
    eye = jnp.where(ri == ci, 1.0, 0.0)
    tri = jnp.where(lax.broadcasted_iota(I32, (ch, ch), 1) <= lax.broadcasted_iota(I32, (ch, ch), 0),
                    1.0, 0.0).astype(BF16)

    def bd(t):
        return jnp.concatenate([t * m0, t * m1], axis=0)

    nchunk = tm // ch
    per_chunk = []
    for c in range(nchunk):
        rows = slice(c * ch, (c + 1) * ch)
        ld = ld_s[rows, :]
        ld_hi = ld.astype(BF16)
        ld_lo = (ld - ld_hi.astype(F32)).astype(BF16)
        cum = (jnp.dot(tri, ld_hi, preferred_element_type=F32)
               + jnp.dot(tri, ld_lo, preferred_element_type=F32))
        gam = jnp.exp(cum)
        ginv = jnp.exp(-cum)
        per_chunk.append(dict(at=a_s[rows, :] * jnp.exp(cum - ld), rt=r_s[rows, :] * gam,
                              bt=b_s[rows, :] * ginv, kt=k_s[rows, :] * ginv, v=v_s[rows, :],
                              gl=gam[ch - 1:ch, :]))
    inst = [(c, q) for c in range(nchunk) for q in range(npair)]

    def part(name):
        return [per_chunk[c][name][:, q * LANES:(q + 1) * LANES] for c, q in inst]

    bt, kt, gl = part("bt"), part("kt"), part("gl")
    at_bd = [bd(t) for t in part("at")]
    rt_bd = [bd(t) for t in part("rt")]
    v_bd = [bd(t) for t in part("v")]
    bh_t = [bd(b * g_).T for b, g_ in zip(bt, gl)]
    kh_t = [bd(k_ * g_).T for k_, g_ in zip(kt, gl)]
    gmat = [_dot_nt(jnp.concatenate([a_, r_], axis=0), jnp.concatenate([b, b, k_, k_], axis=0))
            for a_, r_, b, k_ in zip(at_bd, rt_bd, bt, kt)]
    n_ab = [gm[:2 * ch, :2 * ch] * strict for gm in gmat]
    a_ak = [gm[:2 * ch, 2 * ch:] * strict for gm in gmat]
    m_rb = [gm[2 * ch:, :2 * ch] * incl for gm in gmat]
    m_rk = [gm[2 * ch:, 2 * ch:] * incl for gm in gmat]
    tinv = [eye + n for n in n_ab]
    npow = n_ab
    for step in range(1, 6):
        if step == 1:
            npow = [_dot(n, n) for n in npow]
        both = [_dot(n, jnp.concatenate([t, n], axis=1)) for n, t in zip(npow, tinv)]
        tinv = [t + b[:, :2 * ch] for t, b in zip(tinv, both)]
        npow = [b[:, 2 * ch:] for b in both]
    va = [_dot(jnp.concatenate([a_, k_, m_], axis=0), v_) for a_, k_, m_, v_ in zip(a_ak, kh_t, m_rk, v_bd)]
    wu = [_dot(t, jnp.concatenate([a_, x_[:2 * ch]], axis=1)) for t, a_, x_ in zip(tinv, at_bd, va)]
    pq = [_dot(jnp.concatenate([b, m_], axis=0), w_) for b, m_, w_ in zip(bh_t, m_rb, wu)]
    pmat = [eye * g_ + t[:2 * ch, :2 * ch] for g_, t in zip(gl, pq)]
    qmat = [t[:2 * ch, 2 * ch:] + x_[2 * ch:4 * ch] for t, x_ in zip(pq, va)]
    ry = [r_ + t[2 * ch:, :2 * ch] for r_, t in zip(rt_bd, pq)]
    y0 = [t[2 * ch:, 2 * ch:] + x_[4 * ch:] for t, x_ in zip(pq, va)]
    state = [hs_ref[q] for q in range(npair)]
    y_chunks = []
    for c in range(nchunk):
        ids = [c * npair + q for q in range(npair)]
        both = [_dot(jnp.concatenate([ry[i], pmat[i]], axis=0), st) for i, st in zip(ids, state)]
        yy = [t[:2 * ch] + y0[i] for i, t in zip(ids, both)]
        state = [t[2 * ch:] + qmat[i] for i, t in zip(ids, both)]
        y_chunks.append(jnp.concatenate([t[:ch] + t[ch:] for t in yy], axis=1))
    hs_ref[...] = jnp.stack(state, axis=0)

    y = jnp.concatenate(y_chunks, axis=0)
    inv_n = 1.0 / HEAD_DIM
    mean = _dot_split(y, eblk) * inv_n
    yc = y - mean
    var = _dot_split(yc * yc, eblk) * inv_n
    yn = yc * lax.rsqrt(var + RWKV_GN_EPS) * gng_ref[...] + gnb_ref[...]
    r = r_s[...]
    k = k_s[...]
    v = v_s[...]
    bonus = _dot_split(r * k * rk_ref[...], eblk) * v
    y_ref[0] = ((yn + bonus) * gate).astype(BF16)


def _rwkv_mixer(h, w_rkv, w_lora, mu, w0, w2, a0, a2, g2, k_k, k_a, r_k, gn_g, gn_b):
    bsz, s, d = h.shape
    g = w0.shape[0]
    tm = 256
    row = lambda t: t.astype(F32).reshape(1, -1)
    pad_rows = lambda t: jnp.zeros((LANES, g), F32).at[:t.shape[0]].set(t.astype(F32)).astype(BF16)
    mu1 = row(mu[:3 * g])
    mu2 = jnp.concatenate([
        jnp.zeros((LANES,), F32).at[:RWKV_DECAY_RANK].set(mu[3 * g:3 * g + RWKV_DECAY_RANK]),
        jnp.zeros((LANES,), F32).at[:RWKV_A_RANK].set(mu[3 * g + RWKV_DECAY_RANK:3 * g + RWKV_DECAY_RANK + RWKV_A_RANK]),
        mu[3 * g + RWKV_DECAY_RANK + RWKV_A_RANK:]]).reshape(1, -1)
    head = np.arange(g) // HEAD_DIM
    eblk = jnp.asarray(head[:, None] == head[None, :], BF16)
    args = (h, w_rkv, w_lora, mu1, mu2, row(w0), pad_rows(w2), row(a0), pad_rows(a2), g2.astype(BF16),
            row(k_k), row(k_a), row(r_k), row(gn_g), row(gn_b), eblk)
    full = lambda a: pl.BlockSpec(a.shape, lambda b, i: (0,) * a.ndim)
    return pl.pallas_call(
        _rwkv_kernel,
        grid=(bsz, s // tm),
        in_specs=[pl.BlockSpec((1, tm, d), lambda b, i: (b, i, 0))] + [full(a) for a in args[1:]],
        out_specs=pl.BlockSpec((1, tm, g), lambda b, i: (b, i, 0)),
        out_shape=jax.ShapeDtypeStruct((bsz, s, g), BF16),
        scratch_shapes=[pltpu.VMEM((SUBLANES, 3 * g), F32), pltpu.VMEM((SUBLANES, 3 * LANES), F32),
                        pltpu.VMEM((g // LANES, 2 * WKV_CHUNK, LANES), F32)]
                       + [pltpu.VMEM((tm, g), F32) for _ in range(7)],
        compiler_params=_cparams(("parallel", "arbitrary")),
        name="rwkv7_mixer",
    )(*args)


def _mixout_kernel(alpha, ya_ref, yb_ref, yc_ref, yd_ref, wo_ref, x_ref, gt_ref, lng_ref, lnb_ref,
                   sh_ref, sc_ref, wr_ref, br_ref, x1_ref, h2_ref, lg_ref):
    g = ya_ref.shape[2]
    y = (jnp.dot(ya_ref[0], wo_ref[0:g, :], preferred_element_type=F32)
         + jnp.dot(yb_ref[0], wo_ref[g:2 * g, :], preferred_element_type=F32)
         + jnp.dot(yc_ref[0], wo_ref[2 * g:3 * g, :], preferred_element_type=F32)
         + jnp.dot(yd_ref[0], wo_ref[3 * g:, :], preferred_element_type=F32))
    x1 = _layer_norm(alpha * x_ref[0] + (1.0 + gt_ref[0]) * y) * lng_ref[...] + lnb_ref[...]
    x1_ref[0] = x1
    h2 = _layer_norm(x1) * (1.0 + sc_ref[0]) + sh_ref[0]
    h2_ref[0] = h2
    lg_ref[0] = _dot(h2, wr_ref[...]) + br_ref[...]


def _mix_out(alpha, ys, w_out, x, gate, ln_g, ln_b, shift2, scale2, w_router, b_router):
    bsz, s, d = x.shape
    g = ys[0].shape[2]
    tm = 256
    tok = lambda w: pl.BlockSpec((1, tm, w), lambda b, i: (b, i, 0))
    per_b = pl.BlockSpec((1, 1, d), lambda b, i: (b, 0, 0))
    full = lambda a: pl.BlockSpec(a.shape, lambda b, i: (0,) * a.ndim)
    row = lambda t: t.astype(F32).reshape(1, -1)
    args = (*ys, w_out, x, gate[:, None, :], row(ln_g), row(ln_b), shift2[:, None, :], scale2[:, None, :],
            w_router, b_router)
    return pl.pallas_call(
        functools.partial(_mixout_kernel, alpha),
        grid=(bsz, s // tm),
        in_specs=[tok(g)] * 4 + [full(w_out), tok(d), per_b, full(args[7]), full(args[8]), per_b, per_b,
                                 full(w_router), full(b_router)],
        out_specs=[tok(d), tok(d), tok(LANES)],
        out_shape=[jax.ShapeDtypeStruct((bsz, s, d), F32), jax.ShapeDtypeStruct((bsz, s, d), F32),
                   jax.ShapeDtypeStruct((bsz, s, LANES), F32)],
        compiler_params=_cparams(("parallel", "parallel")),
        name="mix_out_ln_router",
    )(*args)


def _dispatch_kernel(h_ref, lg_ref, xs_ref, info_ref, seg_ref):
    tm = lg_ref.shape[0]
    rt = xs_ref.shape[0]
    half = h_ref.shape[1] // 2
    lg = lg_ref[...]
    lane = lax.broadcasted_iota(I32, (tm, LANES), 1)
    lane_f = lane.astype(F32)

    def top1(vals, mask):
        mv = jnp.where(mask, vals, -jnp.inf)
        m = jnp.max(mv, axis=-1, keepdims=True)
        idx = jnp.min(jnp.where(jnp.logical_and(mask, mv == m), lane_f, float(LANES)), axis=-1, keepdims=True)
        return m, idx.astype(I32)

    gmask = lane < N_EXPERT_GROUPS
    gm, gidx = top1(lg, gmask)
    g_val = 1.0 / jnp.sum(jnp.where(gmask, jnp.exp(lg - gm), 0.0), axis=-1, keepdims=True)
    elo = N_EXPERT_GROUPS + gidx * EXPERTS_PER_GROUP
    emask = jnp.logical_and(lane >= elo, lane < elo + EXPERTS_PER_GROUP)
    m1, i1 = top1(lg, emask)
    m2, i2 = top1(lg, jnp.logical_and(emask, lane != i1))
    e21 = jnp.exp(m2 - m1)
    w1 = g_val / (1.0 + e21)
    w2 = g_val * e21 / (1.0 + e21)
    e1 = i1 - N_EXPERT_GROUPS
    e2 = i2 - N_EXPERT_GROUPS
    oh1 = (lane == e1)
    oh2 = (lane == e2)
    ohs = jnp.where(jnp.logical_or(oh1, oh2), 1.0, 0.0)
    cnt = jnp.sum(ohs, axis=0, keepdims=True)
    units = jnp.floor((cnt + (MOE_UNIT - 1.0)) * (1.0 / MOE_UNIT))
    li = lax.broadcasted_iota(I32, (LANES, LANES), 0)
    lj = lax.broadcasted_iota(I32, (LANES, LANES), 1)
    upper = jnp.where(li < lj, 1.0, 0.0).astype(BF16)
    ustart = jnp.dot(jnp.broadcast_to(units, (SUBLANES, LANES)).astype(BF16), upper,
                     preferred_element_type=F32)[0:1, :]
    ri = lax.broadcasted_iota(I32, (tm, tm), 0)
    ci = lax.broadcasted_iota(I32, (tm, tm), 1)
    tri = jnp.where(ci < ri, 1.0, 0.0).astype(BF16)
    before = jnp.dot(tri, ohs.astype(BF16), preferred_element_type=F32)
    first = before + MOE_UNIT * ustart
    pos1 = jnp.sum(jnp.where(oh1, first, 0.0), axis=-1, keepdims=True)
    pos2 = jnp.sum(jnp.where(oh2, first, 0.0), axis=-1, keepdims=True)
    posm = jnp.where(lane == 0, pos1, jnp.where(lane == 1, pos2, -1.0))
    post = posm.T
    prow = lax.broadcasted_iota(I32, (rt, tm), 0).astype(F32)
    sel1 = prow == post[0:1, :]
    sel2 = prow == post[1:2, :]
    hb = h_ref[...].astype(BF16)
    xs = jnp.dot(jnp.where(jnp.logical_or(sel1, sel2), 1.0, 0.0).astype(BF16), hb, preferred_element_type=F32)
    bits = pltpu.bitcast(xs, jnp.uint32)
    xs_ref[:, :half] = jnp.bitwise_or(jnp.right_shift(bits[:, :half], jnp.uint32(16)), bits[:, half:])
    def terms(w):
        a = w.astype(BF16).astype(F32)
        b = (w - a).astype(BF16).astype(F32)
        return a, b, ((w - a) - b).astype(BF16).astype(F32)
    t1 = terms(w1)
    t2 = terms(w2)
    wm = jnp.zeros((tm, LANES), F32)
    for k, t in enumerate(t1 + t2):
        wm = jnp.where(lane == k, t, wm)
    wmb = wm.astype(BF16)
    s1 = jnp.dot(jnp.where(sel1, 1.0, 0.0).astype(BF16), wmb, preferred_element_type=F32)
    s2 = jnp.dot(jnp.where(sel2, 1.0, 0.0).astype(BF16), wmb, preferred_element_type=F32)
    wrow = (s1[:, 0:1] + s1[:, 1:2] + s1[:, 2:3]) + (s2[:, 3:4] + s2[:, 4:5] + s2[:, 5:6])
    mlane = lax.broadcasted_iota(I32, (rt, LANES), 1)
    wbits = pltpu.bitcast(jnp.broadcast_to(wrow, (rt, LANES)), jnp.uint32)
    xs_ref[:, half:] = jnp.where(mlane == 0, wbits, jnp.uint32(0))
    info_ref[...] = jnp.where(lane == 0, pos1, jnp.where(lane == 1, pos2, 0.0)).astype(I32)
    srow = lax.broadcasted_iota(I32, (SUBLANES, LANES), 0)
    total = jnp.sum(units, axis=-1, keepdims=True)
    seg = jnp.where(srow == 0, units, jnp.where(srow == 1, ustart, jnp.where(srow == 2, total, 0.0)))
    seg_ref[0] = seg.astype(I32)


def _dispatch(h2, logits):
    t, d = h2.shape
    nt = t // MOE_TILE
    return pl.pallas_call(
        _dispatch_kernel,
        grid=(nt,),
        in_specs=[pl.BlockSpec((MOE_TILE, d), lambda i: (i, 0)),
                  pl.BlockSpec((MOE_TILE, LANES), lambda i: (i, 0))],
        out_specs=[pl.BlockSpec((MOE_TILE_ROWS, d // 2 + LANES), lambda i: (i, 0)),
                   pl.BlockSpec((MOE_TILE, LANES), lambda i: (i, 0)),
                   pl.BlockSpec((1, SUBLANES, LANES), lambda i: (i, 0, 0))],
        out_shape=[jax.ShapeDtypeStruct((nt * MOE_TILE_ROWS, d // 2 + LANES), jnp.uint32),
                   jax.ShapeDtypeStruct((t, LANES), I32),
                   jax.ShapeDtypeStruct((nt, SUBLANES, LANES), I32)],
        compiler_params=_cparams(("parallel",)),
        name="moe_dispatch",
    )(h2, logits)


def _worklist_kernel(units_ref, ustart_ref, uidx_ref, slot_ref, be_ref, nb_ref):
    n_seg = units_ref.shape[0]
    n_tiles = n_seg // N_EXPERTS
    n_slots = uidx_ref.shape[0]
    n_blocks = be_ref.shape[0]
    tile_units = MOE_TILE_ROWS // MOE_UNIT

    def pad_slot(i, c):
        uidx_ref[i] = -1
        return c

    def unused_units(tau, c):
        last = tau * N_EXPERTS + (N_EXPERTS - 1)

        def mark_unused(u, c2):
            slot_ref[tau * tile_units + u] = -1
            return c2

        return lax.fori_loop(ustart_ref[last] + units_ref[last], tile_units, mark_unused, c)

    lax.fori_loop(0, n_tiles, unused_units, 0)

    def expert(e, blk0):
        def tile(tau, pos):
            n = units_ref[tau * N_EXPERTS + e]
            base = tau * tile_units + ustart_ref[tau * N_EXPERTS + e]

            def unit(j, p):
                uidx_ref[p] = base + j
                slot_ref[base + j] = p
                return p + 1

            return lax.fori_loop(0, n, unit, pos)

        end = lax.fori_loop(0, n_tiles, tile, blk0 * MOE_BLOCK_UNITS)
        nblk = lax.shift_right_logical(end - blk0 * MOE_BLOCK_UNITS + (MOE_BLOCK_UNITS - 1),
                                       int(math.log2(MOE_BLOCK_UNITS)))

        def mark(i, c):
            be_ref[blk0 + i] = e
            return c

        lax.fori_loop(0, nblk, mark, 0)
        lax.fori_loop(end, (blk0 + nblk) * MOE_BLOCK_UNITS, pad_slot, 0)
        return blk0 + nblk

    used = lax.fori_loop(0, N_EXPERTS, expert, 0)
    nb_ref[0] = used

    def tail(i, c):
        be_ref[i] = N_EXPERTS - 1
        return c

    lax.fori_loop(used, n_blocks, tail, 0)
    lax.fori_loop(used * MOE_BLOCK_UNITS, n_slots, pad_slot, 0)


def _worklist(units, ustart, n_blocks):
    smem = pl.BlockSpec(memory_space=pltpu.SMEM)
    n_tiles = units.shape[0] // N_EXPERTS
    return pl.pallas_call(
        _worklist_kernel,
        in_specs=[smem, smem],
        out_specs=[smem, smem, smem, smem],
        out_shape=[jax.ShapeDtypeStruct((n_blocks * MOE_BLOCK_UNITS,), I32),
                   jax.ShapeDtypeStruct((n_tiles * (MOE_TILE_ROWS // MOE_UNIT),), I32),
                   jax.ShapeDtypeStruct((n_blocks,), I32), jax.ShapeDtypeStruct((1,), I32)],
        name="moe_worklist",
    )(units, ustart)


def _expert_kernel(be_ref, uidx_ref, nb_ref, xs_hbm, w1_ref, w3_ref, w2_ref, o_ref,
                   xbuf, gsem, w1b, w3b, w2b):
    b = pl.program_id(0)
    used = nb_ref[0]
    half = xs_hbm.shape[1] - LANES
    slot = lax.rem(b, 2)
    other = 1 - slot

    def gather_copy(s, j, unit):
        return pltpu.make_async_copy(xs_hbm.at[pl.ds(pl.multiple_of(unit * MOE_UNIT, MOE_UNIT), MOE_UNIT), :],
                                     xbuf.at[s, pl.ds(j * MOE_UNIT, MOE_UNIT), :], gsem.at[s])

    def gather_start(blk, s):
        for j in range(MOE_BLOCK_UNITS):
            gather_copy(s, j, jnp.maximum(uidx_ref[blk * MOE_BLOCK_UNITS + j], 0)).start(priority=j % 2)

    def gather_wait(s):
        for j in range(MOE_BLOCK_UNITS):
            gather_copy(s, j, 0).wait()

    @pl.when(b == 0)
    def _():
        gather_start(0, 0)

    @pl.when(b + 1 < used)
    def _():
        gather_start(b + 1, other)

    @pl.when(b >= used)
    def _():
        o_ref[...] = jnp.zeros_like(o_ref)

    @pl.when(b < used)
    def _():
        prev = be_ref[jnp.maximum(b - 1, 0)]

        @pl.when(jnp.logical_or(b == 0, be_ref[b] != prev))
        def _():
            w1b[...] = w1_ref[0, 0].astype(BF16)
            w3b[...] = w3_ref[0, 0].astype(BF16)
            w2b[...] = w2_ref[0, 0].astype(BF16)

        gather_wait(slot)
        xw = xbuf[slot]
        word = xw[:, :half]
        lo = pltpu.bitcast(jnp.left_shift(word, jnp.uint32(16)), F32)
        hi = pltpu.bitcast(jnp.bitwise_and(word, jnp.uint32(0xFFFF0000)), F32)
        x = jnp.concatenate([lo, hi], axis=1).astype(BF16)
        wrow = pltpu.bitcast(xw[:, half:], F32)[:, 0:1]
        a = jnp.dot(x, w1b[...], preferred_element_type=F32)
        gte = jnp.dot(x, w3b[...], preferred_element_type=F32)
        mid = (a * _sigmoid(a)) * gte
        o_ref[...] = jnp.dot(mid.astype(BF16), w2b[...], preferred_element_type=F32) * wrow


def _expert_ffn(layer, xs, blk_expert, unit_idx, n_used, w1, w3, w2):
    d, de = w1.shape[2], w1.shape[3]
    nb = blk_expert.shape[0]
    rows = MOE_BLOCK_UNITS * MOE_UNIT
    grid_spec = pltpu.PrefetchScalarGridSpec(
        num_scalar_prefetch=3,
        grid=(nb,),
        in_specs=[pl.BlockSpec(memory_space=pl.ANY),
                  pl.BlockSpec((1, 1, d, de), lambda b, be, ui, nu: (layer, be[b], 0, 0)),
                  pl.BlockSpec((1, 1, d, de), lambda b, be, ui, nu: (layer, be[b], 0, 0)),
                  pl.BlockSpec((1, 1, de, d), lambda b, be, ui, nu: (layer, be[b], 0, 0))],
        out_specs=pl.BlockSpec((rows, d), lambda b, be, ui, nu: (b, 0)),
        scratch_shapes=[pltpu.VMEM((2, rows, xs.shape[1]), jnp.uint32), pltpu.SemaphoreType.DMA((2,)),
                        pltpu.VMEM((d, de), BF16), pltpu.VMEM((d, de), BF16), pltpu.VMEM((de, d), BF16)],
    )
    return pl.pallas_call(
        _expert_kernel,
        grid_spec=grid_spec,
        out_shape=jax.ShapeDtypeStruct((nb * rows, d), F32),
        compiler_params=_cparams(("arbitrary",)),
        name="moe_expert_ffn",
    )(blk_expert, unit_idx, n_used, xs, w1, w3, w2)


def _combine_kernel(alpha, with_next, nu_ref, slot_ref, ys_hbm, info_ref, x_ref, gt_ref, lng_ref, lnb_ref, *rest):
    if with_next:
        sh_ref, sc_ref, o_ref, hn_ref, ybuf, sem = rest
    else:
        o_ref, ybuf, sem = rest
    i = pl.program_id(0)
    nt = pl.num_programs(0)
    tm = x_ref.shape[0]
    rt = ybuf.shape[1]
    slot = lax.rem(i, 2)
    other = 1 - slot

    def unit_copy(tile, s, j):
        src = pl.multiple_of(slot_ref[tile * (rt // MOE_UNIT) + j] * MOE_UNIT, MOE_UNIT)
        dst = pl.multiple_of(j * MOE_UNIT, MOE_UNIT)
        return pltpu.make_async_copy(ys_hbm.at[pl.ds(src, MOE_UNIT), :], ybuf.at[s, pl.ds(dst, MOE_UNIT), :], sem.at[s])

    def start(tile, s):
        def body(j, c):
            unit_copy(tile, s, j).start()
            return c
        lax.fori_loop(0, nu_ref[tile], body, 0)

    def wait(tile, s):
        def body(j, c):
            unit_copy(tile, s, j).wait()
            return c
        lax.fori_loop(0, nu_ref[tile], body, 0)

    @pl.when(i == 0)
    def _():
        ybuf[...] = jnp.zeros_like(ybuf)
        start(0, 0)

    @pl.when(i + 1 < nt)
    def _():
        start(i + 1, other)

    wait(i, slot)
    info = info_ref[...]
    col = lax.broadcasted_iota(I32, (tm, rt), 1)
    pick = jnp.where(jnp.logical_or(col == info[:, 0:1], col == info[:, 1:2]), 1.0, 0.0).astype(BF16)
    ys = ybuf[slot]
    y = _dot_split(ys, pick, lhs=True)
    x2 = _layer_norm(alpha * x_ref[...] + (1.0 + gt_ref[0]) * y) * lng_ref[...] + lnb_ref[...]
    o_ref[...] = x2
    if with_next:
        hn_ref[...] = (_layer_norm(x2) * (1.0 + sc_ref[0]) + sh_ref[0]).astype(BF16)


def _combine(alpha, ys, tile_units, unit_slot, info, x1, gate, ln_g, ln_b, seq, next_mod=None):
    t, d = x1.shape
    tm = MOE_TILE
    per_seq = seq // tm
    tok = pl.BlockSpec((tm, d), lambda i, nu, us: (i, 0))
    per_b = pl.BlockSpec((1, 1, d), lambda i, nu, us: (i // per_seq, 0, 0))
    row = pl.BlockSpec((1, d), lambda i, nu, us: (0, 0))
    with_next = next_mod is not None
    args = [tile_units, unit_slot, ys, info, x1, gate[:, None, :], ln_g.astype(F32).reshape(1, d),
            ln_b.astype(F32).reshape(1, d)]
    in_specs = [pl.BlockSpec(memory_space=pl.ANY), pl.BlockSpec((tm, LANES), lambda i, nu, us: (i, 0)),
                tok, per_b, row, row]
    out_specs, out_shape = tok, jax.ShapeDtypeStruct((t, d), F32)
    if with_next:
        args += [next_mod[0][:, None, :], next_mod[1][:, None, :]]
        in_specs += [per_b, per_b]
        out_specs, out_shape = [tok, tok], [out_shape, jax.ShapeDtypeStruct((t, d), BF16)]
    grid_spec = pltpu.PrefetchScalarGridSpec(
        num_scalar_prefetch=2,
        grid=(t // tm,),
        in_specs=in_specs,
        out_specs=out_specs,
        scratch_shapes=[pltpu.VMEM((2, MOE_TILE_ROWS, d), F32), pltpu.SemaphoreType.DMA((2,))],
    )
    return pl.pallas_call(
        functools.partial(_combine_kernel, alpha, with_next),
        grid_spec=grid_spec,
        out_shape=out_shape,
        compiler_params=_cparams(("arbitrary",)),
        name="moe_combine_ln",
    )(*args)


def _moe(layer, alpha, h2, logits, x1, gate, ln_g, ln_b, w1, w3, w2, next_mod):
    bsz, s, d = x1.shape
    t = bsz * s
    nt = t // MOE_TILE
    xs, info, seg = _dispatch(h2.reshape(t, d), logits.reshape(t, LANES))
    units = seg[:, 0, :N_EXPERTS].reshape(nt * N_EXPERTS)
    ustart = seg[:, 1, :N_EXPERTS].reshape(nt * N_EXPERTS)
    tile_units = seg[:, 2, 0]
    max_units = nt * (TOP_K * MOE_TILE // MOE_UNIT + N_EXPERTS * (MOE_UNIT - 1) // MOE_UNIT)
    n_blocks = max_units // MOE_BLOCK_UNITS + N_EXPERTS
    unit_idx, unit_slot, blk_expert, n_used = _worklist(units, ustart, n_blocks)
    ys = _expert_ffn(layer, xs, blk_expert, unit_idx, n_used, w1, w3, w2)
    out = _combine(alpha, ys, tile_units, unit_slot, info, x1.reshape(t, d), gate, ln_g, ln_b, s, next_mod)
    if next_mod is None:
        return out.reshape(bsz, s, d), None
    return out[0].reshape(bsz, s, d), out[1].reshape(bsz, s, d)


def kernel(x, c, w_ada, b_ada, ln_g, ln_b, w_in, w_out, conv_w, rwkv_mu, rwkv_w0, rwkv_w2, rwkv_a0, rwkv_a2, rwkv_g2, rwkv_kk, rwkv_ka, rwkv_rk, rwkv_gn_g, rwkv_gn_b, attn_sinks, rel_bias, s5_lambda_re, s5_lambda_im, s5_log_dt, s5_b_re, s5_b_im, s5_c_re, s5_c_im, s5_d, s5_glu_w, s5_glu_b, router_group_w, router_group_b, router_expert_w, router_expert_b, moe_w1, moe_w3, moe_w2):
    depth = w_ada.shape[0]
    d = x.shape[-1]
    g = d // 4
    alpha = (2 * depth) ** 0.25
    n_heads = g // HEAD_DIM
    att_kv = max(1, n_heads // 4) * HEAD_DIM
    rw_off = 3 * g
    lora = RWKV_DECAY_RANK + RWKV_A_RANK + RWKV_GATE_RANK
    att_off = rw_off + 3 * g + lora
    s5_off = att_off + g + 2 * att_kv

    mod = _modulation(c, w_ada, b_ada)
    for l in range(depth):
        sh1, sc1, gt1, sh2, sc2, gt2 = jnp.split(mod[l], 6, axis=-1)
        wl = w_in[l]
        w_conv = wl[:, :rw_off].astype(BF16)
        w_rkv = wl[:, rw_off:rw_off + 3 * g].astype(BF16)
        lo = rw_off + 3 * g
        zcol = lambda n: jnp.zeros((d, n), F32)
        w_lora = jnp.concatenate([
            wl[:, lo:lo + RWKV_DECAY_RANK], zcol(LANES - RWKV_DECAY_RANK),
            wl[:, lo + RWKV_DECAY_RANK:lo + RWKV_DECAY_RANK + RWKV_A_RANK], zcol(LANES - RWKV_A_RANK),
            wl[:, lo + RWKV_DECAY_RANK + RWKV_A_RANK:att_off]], axis=1).astype(BF16)
        w_q = wl[:, att_off:att_off + g].astype(BF16)
        w_kv = wl[:, att_off + g:s5_off].astype(BF16)
        w_s5 = wl[:, s5_off:].astype(BF16)

        if l == 0:
            h = _adaln(x, sh1, sc1)
        y_conv = _conv_mixer(h, w_conv, conv_w[l].astype(F32))
        y_rwkv = _rwkv_mixer(h, w_rkv, w_lora, rwkv_mu[l], rwkv_w0[l], rwkv_w2[l], rwkv_a0[l], rwkv_a2[l],
                             rwkv_g2[l], rwkv_kk[l], rwkv_ka[l], rwkv_rk[l], rwkv_gn_g[l], rwkv_gn_b[l])
        y_att = _att_mixer(h, w_q, w_kv, attn_sinks[l], rel_bias)
        y_ssm = _s5_mixer(h, w_s5, s5_lambda_re[l], s5_lambda_im[l], s5_log_dt[l], s5_b_re[l], s5_b_im[l],
                          s5_c_re[l], s5_c_im[l], s5_d[l], s5_glu_w[l], s5_glu_b[l])
        w_router = jnp.zeros((d, LANES), F32)
        w_router = w_router.at[:, :N_EXPERT_GROUPS].set(router_group_w[l])
        w_router = w_router.at[:, N_EXPERT_GROUPS:N_EXPERT_GROUPS + N_EXPERTS].set(router_expert_w[l]).astype(BF16)
        b_router = jnp.zeros((1, LANES), F32)
        b_router = b_router.at[0, :N_EXPERT_GROUPS].set(router_group_b[l])
        b_router = b_router.at[0, N_EXPERT_GROUPS:N_EXPERT_GROUPS + N_EXPERTS].set(router_expert_b[l])
        x1, h2, logits = _mix_out(alpha, (y_conv, y_rwkv, y_att, y_ssm), w_out[l].astype(BF16), x, gt1,
                                  ln_g[l, 0], ln_b[l, 0], sh2, sc2, w_router, b_router)
        next_mod = None
        if l + 1 < depth:
            nsh1, nsc1 = jnp.split(mod[l + 1], 6, axis=-1)[:2]
            next_mod = (nsh1, nsc1)
        x, h = _moe(l, alpha, h2, logits, x1, gt2, ln_g[l, 1], ln_b[l, 1], moe_w1, moe_w3, moe_w2, next_mod)
    return x
```

```python
import functools
import math

import numpy as np
import jax
import jax.numpy as jnp
from jax import lax
from jax.experimental import pallas as pl
from jax.experimental.pallas import tpu as pltpu

F32 = jnp.float32
BF16 = jnp.bfloat16
I32 = jnp.int32

HEAD_DIM = 64
CONV_WIDTH = 3
RWKV_DECAY_RANK = 96
RWKV_A_RANK = 96
RWKV_GATE_RANK = 128
RWKV_GN_EPS = 64e-5
ATT_BLOCK = 128
WINDOW = 128
N_BUCKETS = 32
NEG_INF = -1e30
S5_CH = 16
S5_STATE = 64
N_EXPERT_GROUPS = 4
EXPERTS_PER_GROUP = 8
N_EXPERTS = N_EXPERT_GROUPS * EXPERTS_PER_GROUP
TOP_K = 2
LN_EPS = 1e-5

LANES = 128
SUBLANES = 8
WKV_CHUNK = 64
MOE_TILE = 256
MOE_UNIT = SUBLANES
MOE_TILE_ROWS = 768
MOE_BLOCK_UNITS = 32
VMEM_LIMIT = 56 * 2 ** 20


def _cparams(sem, flags=None):
    return pltpu.CompilerParams(dimension_semantics=sem, vmem_limit_bytes=VMEM_LIMIT, flags=flags)


def _dot(a, b):
    return jnp.dot(a.astype(BF16), b.astype(BF16), preferred_element_type=F32)


def _dot_nt(a, b):
    return lax.dot_general(a.astype(BF16), b.astype(BF16), (((1,), (1,)), ((), ())),
                           preferred_element_type=F32)


def _dot_tn(a, b):
    return jnp.dot(a.T.astype(BF16), b.astype(BF16), preferred_element_type=F32)


def _dot_split(x, e, lhs=False):
    hi = x.astype(BF16)
    lo = (x - hi.astype(F32)).astype(BF16)
    if lhs:
        return jnp.dot(e, hi, preferred_element_type=F32) + jnp.dot(e, lo, preferred_element_type=F32)
    return jnp.dot(hi, e, preferred_element_type=F32) + jnp.dot(lo, e, preferred_element_type=F32)


def _sigmoid(x):
    return 1.0 / (1.0 + jnp.exp(-x))


def _layer_norm(x):
    mean = jnp.mean(x, axis=-1, keepdims=True)
    xc = x - mean
    var = jnp.mean(xc * xc, axis=-1, keepdims=True)
    return xc * lax.rsqrt(var + LN_EPS)


def _shift_rows(p, carry_row, n):
    row = lax.broadcasted_iota(I32, (p.shape[0], 1), 0)
    out = pltpu.roll(p, n, 0)
    for i in range(n):
        out = jnp.where(row == i, carry_row[SUBLANES - n + i:SUBLANES - n + i + 1, :], out)
    return out


def _mod_kernel(c_ref, w_ref, b_ref, o_ref):
    c = c_ref[...]
    a = c * _sigmoid(c)
    o_ref[0] = _dot(a, w_ref[0]) + b_ref[0]


def _modulation(c, w_ada, b_ada):
    depth, d, n = w_ada.shape
    bsz = c.shape[0]
    tn = 1536
    cp = jnp.zeros((SUBLANES, d), F32).at[:bsz].set(c)
    out = pl.pallas_call(
        _mod_kernel,
        grid=(depth, n // tn),
        in_specs=[pl.BlockSpec((SUBLANES, d), lambda l, j: (0, 0)),
                  pl.BlockSpec((1, d, tn), lambda l, j: (l, 0, j)),
                  pl.BlockSpec((1, 1, tn), lambda l, j: (l, 0, j))],
        out_specs=pl.BlockSpec((1, SUBLANES, tn), lambda l, j: (l, 0, j)),
        out_shape=jax.ShapeDtypeStruct((depth, SUBLANES, n), F32),
        compiler_params=_cparams(("parallel", "parallel")),
        name="adaln_modulation",
    )(cp, w_ada, b_ada.reshape(depth, 1, n))
    return out[:, :bsz]


def _adaln_kernel(x_ref, sh_ref, sc_ref, h_ref):
    h_ref[0] = (_layer_norm(x_ref[0]) * (1.0 + sc_ref[0]) + sh_ref[0]).astype(BF16)


def _adaln(x, shift, scale):
    bsz, s, d = x.shape
    tm = 512
    return pl.pallas_call(
        _adaln_kernel,
        grid=(bsz, s // tm),
        in_specs=[pl.BlockSpec((1, tm, d), lambda b, i: (b, i, 0)),
                  pl.BlockSpec((1, 1, d), lambda b, i: (b, 0, 0)),
                  pl.BlockSpec((1, 1, d), lambda b, i: (b, 0, 0))],
        out_specs=pl.BlockSpec((1, tm, d), lambda b, i: (b, i, 0)),
        out_shape=jax.ShapeDtypeStruct((bsz, s, d), BF16),
        compiler_params=_cparams(("parallel", "parallel")),
        name="adaln_input",
    )(x, shift[:, None, :], scale[:, None, :])


def _conv_kernel(h_ref, w_ref, cw_ref, y_ref, carry_ref):
    tm = h_ref.shape[1]
    g = y_ref.shape[2]

    @pl.when(pl.program_id(1) == 0)
    def _():
        carry_ref[...] = jnp.zeros_like(carry_ref)

    p = _dot(h_ref[0], w_ref[...])
    b_gate, c_gate, hh = p[:, :g], p[:, g:2 * g], p[:, 2 * g:]
    z = c_gate * hh
    carry = carry_ref[...]
    z1 = _shift_rows(z, carry, 1)
    z2 = _shift_rows(z, carry, 2)
    cw = cw_ref[...]
    out = cw[0:1] * z2 + cw[1:2] * z1 + cw[2:3] * z
    y_ref[0] = (b_gate * out).astype(BF16)
    carry_ref[...] = z[tm - SUBLANES:, :]


def _conv_mixer(h, w, conv_w):
    bsz, s, d = h.shape
    g = conv_w.shape[1]
    tm = 512
    return pl.pallas_call(
        _conv_kernel,
        grid=(bsz, s // tm),
        in_specs=[pl.BlockSpec((1, tm, d), lambda b, i: (b, i, 0)),
                  pl.BlockSpec((d, 3 * g), lambda b, i: (0, 0)),
                  pl.BlockSpec((CONV_WIDTH, g), lambda b, i: (0, 0))],
        out_specs=pl.BlockSpec((1, tm, g), lambda b, i: (b, i, 0)),
        out_shape=jax.ShapeDtypeStruct((bsz, s, g), BF16),
        scratch_shapes=[pltpu.VMEM((SUBLANES, g), F32)],
        compiler_params=_cparams(("parallel", "arbitrary")),
        name="conv_mixer",
    )(h, w, conv_w)


def _t5_bucket(rel):
    n = jnp.maximum(rel, 0)
    max_exact = N_BUCKETS // 2
    n_f = jnp.maximum(n, 1).astype(F32)
    large = max_exact + (jnp.log(n_f / max_exact) / math.log(WINDOW / max_exact)
                         * (N_BUCKETS - max_exact)).astype(I32)
    return jnp.where(n < max_exact, n, jnp.minimum(large, N_BUCKETS - 1))


def _att_kernel(sink_ref, h_ref, wq_ref, wkv_ref, bias_ref, y_ref, kvc_ref):
    tm = h_ref.shape[1]
    n_heads = bias_ref.shape[0]
    kvw = wkv_ref.shape[1] // 2
    n_kv = kvw // HEAD_DIM
    rep = n_heads // n_kv
    blk = ATT_BLOCK
    first_tile = pl.program_id(1) == 0

    @pl.when(first_tile)
    def _():
        kvc_ref[...] = jnp.zeros_like(kvc_ref)

    x = h_ref[0]
    q = _dot(x, wq_ref[...]) * (HEAD_DIM ** -0.5)
    kv = _dot(x, wkv_ref[...])
    kvext = jnp.concatenate([kvc_ref[...], kv], axis=0)
    kvc_ref[...] = kv[tm - blk:, :]
    col = lax.broadcasted_iota(I32, (blk, 2 * blk), 1)
    qb16 = q.astype(BF16)
    kv16 = kvext.astype(BF16)

    def scores(j):
        qb = qb16[j * blk:(j + 1) * blk]
        kw = kv16[j * blk:j * blk + 2 * blk, :kvw]
        kgs = [kw[:, gi * HEAD_DIM:(gi + 1) * HEAD_DIM] for gi in range(n_kv)]
        scs = [_dot_nt(qb[:, hh * HEAD_DIM:(hh + 1) * HEAD_DIM], kgs[hh // rep]) + bias_ref[hh]
               for hh in range(n_heads)]
        if j == 0:
            scs = [jnp.where(jnp.logical_and(first_tile, col < blk), NEG_INF, sc) for sc in scs]
        return scs

    def probs(scs):
        out = []
        for hh, sc in enumerate(scs):
            sink = sink_ref[hh]
            m = jnp.maximum(jnp.max(sc, axis=-1, keepdims=True), sink)
            e = jnp.exp(sc - m)
            den = jnp.sum(e, axis=-1, keepdims=True) + jnp.exp(sink - m)
            out.append((e / den).astype(BF16))
        return out

    def values(j, ps):
        vw = kv16[j * blk:j * blk + 2 * blk, kvw:]
        vgs = [vw[:, gi * HEAD_DIM:(gi + 1) * HEAD_DIM] for gi in range(n_kv)]
        outs = [jnp.dot(p, vgs[hh // rep], preferred_element_type=F32) for hh, p in enumerate(ps)]
        y_ref[0, j * blk:(j + 1) * blk, :] = jnp.concatenate(outs, axis=1).astype(BF16)

    nblk = tm // blk
    pending = scores(0)
    for j in range(nblk):
        nxt = scores(j + 1) if j + 1 < nblk else None
        values(j, probs(pending))
        pending = nxt


def _att_mixer(h, wq, wkv, sinks, rel_bias):
    bsz, s, d = h.shape
    n_heads = sinks.shape[0]
    tm = 512
    qi = jnp.arange(ATT_BLOCK)[:, None]
    kj = jnp.arange(2 * ATT_BLOCK)[None, :]
    rel = qi + ATT_BLOCK - kj
    valid = (rel >= 0) & (rel < WINDOW)
    onehot = (_t5_bucket(rel)[..., None] == jnp.arange(N_BUCKETS)).astype(F32)
    bias = jnp.einsum('qkb,bh->hqk', onehot, rel_bias.astype(F32), precision=lax.Precision.HIGHEST)
    bias = jnp.where(valid[None], bias, NEG_INF)
    return pl.pallas_call(
        _att_kernel,
        grid=(bsz, s // tm),
        in_specs=[pl.BlockSpec(memory_space=pltpu.SMEM),
                  pl.BlockSpec((1, tm, d), lambda b, i: (b, i, 0)),
                  pl.BlockSpec(wq.shape, lambda b, i: (0, 0)),
                  pl.BlockSpec(wkv.shape, lambda b, i: (0, 0)),
                  pl.BlockSpec(bias.shape, lambda b, i: (0, 0, 0))],
        out_specs=pl.BlockSpec((1, tm, wq.shape[1]), lambda b, i: (b, i, 0)),
        out_shape=jax.ShapeDtypeStruct((bsz, s, wq.shape[1]), BF16),
        scratch_shapes=[pltpu.VMEM((ATT_BLOCK, wkv.shape[1]), F32)],
        compiler_params=_cparams(("parallel", "arbitrary")),
        name="swa_mixer",
    )(sinks.astype(F32), h, wq, wkv, bias)


S5_GROUPS_PER_BLOCK = LANES // S5_CH


def _s5_tables(lam_re, lam_im, log_dt, b_re, b_im, c_re, c_im):
    n_groups, p = lam_re.shape
    lr, li = lam_re.astype(F32), lam_im.astype(F32)
    delta = jnp.exp(log_dt.astype(F32))[:, None]
    mag = jnp.exp(lr * delta)
    ab_re, ab_im = mag * jnp.cos(li * delta), mag * jnp.sin(li * delta)
    den = lr * lr + li * li
    z_re = ((ab_re - 1.0) * lr + ab_im * li) / den
    z_im = (ab_im * lr - (ab_re - 1.0) * li) / den
    br, bi = b_re.astype(F32), b_im.astype(F32)
    bb_re = z_re[..., None] * br - z_im[..., None] * bi
    bb_im = z_re[..., None] * bi + z_im[..., None] * br
    nblk = n_groups // S5_GROUPS_PER_BLOCK
    eye = jnp.eye(S5_GROUPS_PER_BLOCK, dtype=F32)

    def in_blocks(bb):
        bb = bb.reshape(nblk, S5_GROUPS_PER_BLOCK, p, S5_CH)
        return jnp.einsum('qgpc,gh->qgchp', bb, eye).reshape(nblk, LANES, S5_GROUPS_PER_BLOCK * p)

    def out_blocks(cc):
        cc = cc.astype(F32).reshape(nblk, S5_GROUPS_PER_BLOCK, S5_CH, p)
        return jnp.einsum('qgcp,gh->qgphc', cc, eye).reshape(nblk, S5_GROUPS_PER_BLOCK * p, LANES)

    def power(m):
        mg = jnp.exp(m * lr * delta)
        return (mg * jnp.cos(m * li * delta)).reshape(1, -1), (mg * jnp.sin(m * li * delta)).reshape(1, -1)

    row = jnp.arange(SUBLANES, dtype=F32)[:, None]
    tabs = []
    for sft in (1, 2, 4):
        pr, pi = power(float(sft))
        keep = row >= sft
        tabs += [jnp.where(keep, pr, 0.0), jnp.where(keep, pi, 0.0)]
    n_state = n_groups * p
    lrd = (lr * delta).reshape(1, n_state)
    lid = (li * delta).reshape(1, n_state)
    mg = jnp.exp((row + 1.0) * lrd)
    tabs += [mg * jnp.cos((row + 1.0) * lid), mg * jnp.sin((row + 1.0) * lid)]
    tables = jnp.stack(tabs, axis=0)
    return (in_blocks(bb_re).astype(BF16), in_blocks(bb_im).astype(BF16),
            out_blocks(c_re).astype(BF16), out_blocks(c_im).astype(BF16), tables)


def _s5_kernel(h_ref, w_ref, bre_ref, bim_ref, cre_ref, cim_ref, tab_ref, d_ref, gw_ref, gb_ref,
               y_ref, xr_ref, xi_ref, cr_ref, ci_ref):
    nseq = h_ref.shape[0]
    tm = h_ref.shape[1]
    nblk = bre_ref.shape[0]
    sw = bre_ref.shape[2]

    @pl.when(pl.program_id(0) == 0)
    def _():
        cr_ref[...] = jnp.zeros_like(cr_ref)
        ci_ref[...] = jnp.zeros_like(ci_ref)

    def project(b):
        u = _dot(h_ref[b], w_ref[...])
        ub = u.astype(BF16)
        for q in range(nblk):
            uq = ub[:, q * LANES:(q + 1) * LANES]
            xr_ref[b, :, q * sw:(q + 1) * sw] = jnp.dot(uq, bre_ref[q], preferred_element_type=F32)
            xi_ref[b, :, q * sw:(q + 1) * sw] = jnp.dot(uq, bim_ref[q], preferred_element_type=F32)
        return u

    def scan(b):
        cr = cr_ref[b, 0:1, :]
        ci = ci_ref[b, 0:1, :]
        for i in range(tm // SUBLANES):
            rows = slice(i * SUBLANES, (i + 1) * SUBLANES)
            xr = xr_ref[b, rows, :]
            xi = xi_ref[b, rows, :]
            for k, sft in enumerate((1, 2, 4)):
                mr = tab_ref[2 * k]
                mi = tab_ref[2 * k + 1]
                sr = pltpu.roll(xr, sft, 0)
                si = pltpu.roll(xi, sft, 0)
                xr, xi = xr + mr * sr - mi * si, xi + mr * si + mi * sr
            pr = tab_ref[6]
            pi = tab_ref[7]
            xr, xi = xr + pr * cr - pi * ci, xi + pr * ci + pi * cr
            xr_ref[b, rows, :] = xr
            xi_ref[b, rows, :] = xi
            cr = xr[SUBLANES - 1:SUBLANES, :]
            ci = xi[SUBLANES - 1:SUBLANES, :]
        cr_ref[b, 0:1, :] = cr
        ci_ref[b, 0:1, :] = ci

    def readout(b, u):
        ys = []
        for q in range(nblk):
            xr = xr_ref[b, :, q * sw:(q + 1) * sw].astype(BF16)
            xi = xi_ref[b, :, q * sw:(q + 1) * sw].astype(BF16)
            ys.append(jnp.dot(xr, cre_ref[q], preferred_element_type=F32)
                      - jnp.dot(xi, cim_ref[q], preferred_element_type=F32))
        y = jnp.concatenate(ys, axis=1) + d_ref[...] * u
        y = 0.5 * y * (1.0 + jnp.tanh(math.sqrt(2.0 / math.pi) * (y + 0.044715 * (y * y * y))))
        y_ref[b] = (y * _sigmoid(_dot(y, gw_ref[...]) + gb_ref[...])).astype(BF16)

    us = [project(b) for b in range(nseq)]
    for b in range(nseq):
        scan(b)
        readout(b, us[b])


def _s5_mixer(h, w, lam_re, lam_im, log_dt, b_re, b_im, c_re, c_im, d_skip, glu_w, glu_b):
    bsz, s, d = h.shape
    g = w.shape[1]
    tm = 256
    bre, bim, cre, cim, tables = _s5_tables(lam_re, lam_im, log_dt, b_re, b_im, c_re, c_im)
    n_state = tables.shape[2]
    full = lambda a: pl.BlockSpec(a.shape, lambda i: (0,) * a.ndim)
    dvec = d_skip.astype(F32).reshape(1, g)
    gw = glu_w.astype(BF16)
    gb = glu_b.astype(F32).reshape(1, g)
    return pl.pallas_call(
        _s5_kernel,
        grid=(s // tm,),
        in_specs=[pl.BlockSpec((bsz, tm, d), lambda i: (0, i, 0)),
                  full(w), full(bre), full(bim), full(cre), full(cim), full(tables),
                  full(dvec), full(gw), full(gb)],
        out_specs=pl.BlockSpec((bsz, tm, g), lambda i: (0, i, 0)),
        out_shape=jax.ShapeDtypeStruct((bsz, s, g), BF16),
        scratch_shapes=[pltpu.VMEM((bsz, tm, n_state), F32), pltpu.VMEM((bsz, tm, n_state), F32),
                        pltpu.VMEM((bsz, SUBLANES, n_state), F32), pltpu.VMEM((bsz, SUBLANES, n_state), F32)],
        compiler_params=_cparams(("arbitrary",)),
        name="s5_mixer",
    )(h, w, bre, bim, cre, cim, tables, dvec, gw, gb)


def _rwkv_kernel(h_ref, wrkv_ref, wlo_ref, mu1_ref, mu2_ref, w0_ref, w2_ref, a0_ref, a2_ref, g2_ref,
                 kk_ref, ka_ref, rk_ref, gng_ref, gnb_ref, eblk_ref,
                 y_ref,
                 cp_ref, cl_ref, hs_ref, r_s, k_s, v_s, a_s, b_s, ld_s, y_s):
    tm = h_ref.shape[1]
    g = y_ref.shape[2]
    npair = g // LANES
    ch = WKV_CHUNK

    @pl.when(pl.program_id(1) == 0)
    def _():
        cp_ref[...] = jnp.zeros_like(cp_ref)
        cl_ref[...] = jnp.zeros_like(cl_ref)
        hs_ref[...] = jnp.zeros_like(hs_ref)

    x = h_ref[0]
    p = _dot(x, wrkv_ref[...])
    plo = _dot(x, wlo_ref[...])
    pprev = _shift_rows(p, cp_ref[...], 1)
    lprev = _shift_rows(plo, cl_ref[...], 1)
    cp_ref[...] = p[tm - SUBLANES:, :]
    cl_ref[...] = plo[tm - SUBLANES:, :]
    p = p + (pprev - p) * mu1_ref[...]
    plo = plo + (lprev - plo) * mu2_ref[...]
    r, k, v = p[:, :g], p[:, g:2 * g], p[:, 2 * g:]
    w_lo, a_lo, g_lo = plo[:, :LANES], plo[:, LANES:2 * LANES], plo[:, 2 * LANES:]
    wraw = w0_ref[...] + _dot(jnp.tanh(w_lo), w2_ref[...])
    nz = -wraw
    softplus = jnp.maximum(nz, 0.0) + jnp.log(1.0 + jnp.exp(-jnp.abs(nz)))
    w = -softplus - 0.5
    ld_s[...] = -jnp.exp(w)
    a = _sigmoid(a0_ref[...] + _dot(a_lo, a2_ref[...]))
    gate = _dot(_sigmoid(g_lo), g2_ref[...])
    eblk = eblk_ref[...]
    kk = k * kk_ref[...]
    kk = kk / jnp.maximum(jnp.sqrt(_dot_split(kk * kk, eblk)), 1e-12)
    k = k * (1.0 + (a - 1.0) * ka_ref[...])
    r_s[...] = r
    k_s[...] = k
    v_s[...] = v
    a_s[...] = -kk
    b_s[...] = kk * a

    lane = lax.broadcasted_iota(I32, (1, LANES), 1)
    m0 = (lane < HEAD_DIM).astype(F32)
    m1 = 1.0 - m0
    ri = lax.broadcasted_iota(I32, (2 * ch, 2 * ch), 0)
    ci = lax.broadcasted_iota(I32, (2 * ch, 2 * ch), 1)
    same = (ri < ch) == (ci < ch)
    rloc = jnp.bitwise_and(ri, ch - 1)
    cloc = jnp.bitwise_and(ci, ch - 1)
    strict = jnp.where(jnp.logical_and(same, cloc < rloc), 1.0, 0.0)
    incl = jnp.where(jnp.logical_and(same, cloc <= rloc), 1.0, 0.0)
    eye = jnp.where(ri == ci, 1.0, 0.0)
    tri = jnp.where(lax.broadcasted_iota(I32, (ch, ch), 1) <= lax.broadcasted_iota(I32, (ch, ch), 0),
                    1.0, 0.0).astype(BF16)

    def bd(t):
        return jnp.concatenate([t * m0, t * m1], axis=0)

    nchunk = tm // ch
    per_chunk = []
    for c in range(nchunk):
        rows = slice(c * ch, (c + 1) * ch)
        ld = ld_s[rows, :]
        ld_hi = ld.astype(BF16)
        ld_lo = (ld - ld_hi.astype(F32)).astype(BF16)
        cum = (jnp.dot(tri, ld_hi, preferred_element_type=F32)
               + jnp.dot(tri, ld_lo, preferred_element_type=F32))
        gam = jnp.exp(cum)
        ginv = jnp.exp(-cum)
        per_chunk.append(dict(at=a_s[rows, :] * jnp.exp(cum - ld), rt=r_s[rows, :] * gam,
                              bt=b_s[rows, :] * ginv, kt=k_s[rows, :] * ginv, v=v_s[rows, :],
                              gl=gam[ch - 1:ch, :]))
    inst = [(c, q) for c in range(nchunk) for q in range(npair)]

    def part(name):
        return [per_chunk[c][name][:, q * LANES:(q + 1) * LANES] for c, q in inst]

    bt, kt, gl = part("bt"), part("kt"), part("gl")
    at_bd = [bd(t) for t in part("at")]
    rt_bd = [bd(t) for t in part("rt")]
    v_bd = [bd(t) for t in part("v")]
    bh_t = [bd(b * g_).T for b, g_ in zip(bt, gl)]
    kh_t = [bd(k_ * g_).T for k_, g_ in zip(kt, gl)]
    gmat = [_dot_nt(jnp.concatenate([a_, r_], axis=0), jnp.concatenate([b, b, k_, k_], axis=0))
            for a_, r_, b, k_ in zip(at_bd, rt_bd, bt, kt)]
    n_ab = [gm[:2 * ch, :2 * ch] * strict for gm in gmat]
    a_ak = [gm[:2 * ch, 2 * ch:] * strict for gm in gmat]
    m_rb = [gm[2 * ch:, :2 * ch] * incl for gm in gmat]
    m_rk = [gm[2 * ch:, 2 * ch:] * incl for gm in gmat]
    tinv = [eye + n for n in n_ab]
    npow = n_ab
    for step in range(1, 6):
        if step == 1:
            npow = [_dot(n, n) for n in npow]
        both = [_dot(n, jnp.concatenate([t, n], axis=1)) for n, t in zip(npow, tinv)]
        tinv = [t + b[:, :2 * ch] for t, b in zip(tinv, both)]
        npow = [b[:, 2 * ch:] for b in both]
    va = [_dot(jnp.concatenate([a_, k_, m_], axis=0), v_) for a_, k_, m_, v_ in zip(a_ak, kh_t, m_rk, v_bd)]
    wu = [_dot(t, jnp.concatenate([a_, x_[:2 * ch]], axis=1)) for t, a_, x_ in zip(tinv, at_bd, va)]
    pq = [_dot(jnp.concatenate([b, m_], axis=0), w_) for b, m_, w_ in zip(bh_t, m_rb, wu)]
    pmat = [eye * g_ + t[:2 * ch, :2 * ch] for g_, t in zip(gl, pq)]
    qmat = [t[:2 * ch, 2 * ch:] + x_[2 * ch:4 * ch] for t, x_ in zip(pq, va)]
    ry = [r_ + t[2 * ch:, :2 * ch] for r_, t in zip(rt_bd, pq)]
    y0 = [t[2 * ch:, 2 * ch:] + x_[4 * ch:] for t, x_ in zip(pq, va)]
    state = [hs_ref[q] for q in range(npair)]
    y_chunks = []
    for c in range(nchunk):
        ids = [c * npair + q for q in range(npair)]
        both = [_dot(jnp.concatenate([ry[i], pmat[i]], axis=0), st) for i, st in zip(ids, state)]
        yy = [t[:2 * ch] + y0[i] for i, t in zip(ids, both)]
        state = [t[2 * ch:] + qmat[i] for i, t in zip(ids, both)]
        y_chunks.append(jnp.concatenate([t[:ch] + t[ch:] for t in yy], axis=1))
    hs_ref[...] = jnp.stack(state, axis=0)

    y = jnp.concatenate(y_chunks, axis=0)
    inv_n = 1.0 / HEAD_DIM
    mean = _dot_split(y, eblk) * inv_n
    yc = y - mean
    var = _dot_split(yc * yc, eblk) * inv_n
    yn = yc * lax.rsqrt(var + RWKV_GN_EPS) * gng_ref[...] + gnb_ref[...]
    r = r_s[...]
    k = k_s[...]
    v = v_s[...]
    bonus = _dot_split(r * k * rk_ref[...], eblk) * v
    y_ref[0] = ((yn + bonus) * gate).astype(BF16)


def _rwkv_mixer(h, w_rkv, w_lora, mu, w0, w2, a0, a2, g2, k_k, k_a, r_k, gn_g, gn_b):
    bsz, s, d = h.shape
    g = w0.shape[0]
    tm = 256
    row = lambda t: t.astype(F32).reshape(1, -1)
    pad_rows = lambda t: jnp.zeros((LANES, g), F32).at[:t.shape[0]].set(t.astype(F32)).astype(BF16)
    mu1 = row(mu[:3 * g])
    mu2 = jnp.concatenate([
        jnp.zeros((LANES,), F32).at[:RWKV_DECAY_RANK].set(mu[3 * g:3 * g + RWKV_DECAY_RANK]),
        jnp.zeros((LANES,), F32).at[:RWKV_A_RANK].set(mu[3 * g + RWKV_DECAY_RANK:3 * g + RWKV_DECAY_RANK + RWKV_A_RANK]),
        mu[3 * g + RWKV_DECAY_RANK + RWKV_A_RANK:]]).reshape(1, -1)
    head = np.arange(g) // HEAD_DIM
    eblk = jnp.asarray(head[:, None] == head[None, :], BF16)
    args = (h, w_rkv, w_lora, mu1, mu2, row(w0), pad_rows(w2), row(a0), pad_rows(a2), g2.astype(BF16),
            row(k_k), row(k_a), row(r_k), row(gn_g), row(gn_b), eblk)
    full = lambda a: pl.BlockSpec(a.shape, lambda b, i: (0,) * a.ndim)
    return pl.pallas_call(
        _rwkv_kernel,
        grid=(bsz, s // tm),
        in_specs=[pl.BlockSpec((1, tm, d), lambda b, i: (b, i, 0))] + [full(a) for a in args[1:]],
        out_specs=pl.BlockSpec((1, tm, g), lambda b, i: (b, i, 0)),
        out_shape=jax.ShapeDtypeStruct((bsz, s, g), BF16),
        scratch_shapes=[pltpu.VMEM((SUBLANES, 3 * g), F32), pltpu.VMEM((SUBLANES, 3 * LANES), F32),
                        pltpu.VMEM((g // LANES, 2 * WKV_CHUNK, LANES), F32)]
                       + [pltpu.VMEM((tm, g), F32) for _ in range(7)],
        compiler_params=_cparams(("parallel", "arbitrary")),
        name="rwkv7_mixer",
    )(*args)


def _mixout_kernel(alpha, ya_ref, yb_ref, yc_ref, yd_ref, wo_ref, x_ref, gt_ref, lng_ref, lnb_ref,
                   sh_ref, sc_ref, wr_ref, br_ref, x1_ref, xs_ref, info_ref, seg_ref):
    g = ya_ref.shape[2]
    y = (jnp.dot(ya_ref[0], wo_ref[0:g, :], preferred_element_type=F32)
         + jnp.dot(yb_ref[0], wo_ref[g:2 * g, :], preferred_element_type=F32)
         + jnp.dot(yc_ref[0], wo_ref[2 * g:3 * g, :], preferred_element_type=F32)
         + jnp.dot(yd_ref[0], wo_ref[3 * g:, :], preferred_element_type=F32))
    x1 = _layer_norm(alpha * x_ref[0] + (1.0 + gt_ref[0]) * y) * lng_ref[...] + lnb_ref[...]
    x1_ref[0] = x1
    h2 = _layer_norm(x1) * (1.0 + sc_ref[0]) + sh_ref[0]
    _route_sort(h2, _dot(h2, wr_ref[...]) + br_ref[...], xs_ref, info_ref, seg_ref)


def _mix_out(alpha, ys, w_out, x, gate, ln_g, ln_b, shift2, scale2, w_router, b_router):
    bsz, s, d = x.shape
    g = ys[0].shape[2]
    tm = MOE_TILE
    per_seq = s // tm
    nt = bsz * per_seq
    tok = lambda w: pl.BlockSpec((1, tm, w), lambda b, i: (b, i, 0))
    per_b = pl.BlockSpec((1, 1, d), lambda b, i: (b, 0, 0))
    full = lambda a: pl.BlockSpec(a.shape, lambda b, i: (0,) * a.ndim)
    row = lambda t: t.astype(F32).reshape(1, -1)
    args = (*ys, w_out, x, gate[:, None, :], row(ln_g), row(ln_b), shift2[:, None, :], scale2[:, None, :],
            w_router, b_router)
    return pl.pallas_call(
        functools.partial(_mixout_kernel, alpha),
        grid=(bsz, s // tm),
        in_specs=[tok(g)] * 4 + [full(w_out), tok(d), per_b, full(args[7]), full(args[8]), per_b, per_b,
                                 full(w_router), full(b_router)],
        out_specs=[tok(d),
                   pl.BlockSpec((MOE_TILE_ROWS, d // 2 + LANES), lambda b, i: (b * per_seq + i, 0)),
                   pl.BlockSpec((tm, LANES), lambda b, i: (b * per_seq + i, 0)),
                   pl.BlockSpec((1, SUBLANES, LANES), lambda b, i: (b * per_seq + i, 0, 0))],
        out_shape=[jax.ShapeDtypeStruct((bsz, s, d), F32),
                   jax.ShapeDtypeStruct((nt * MOE_TILE_ROWS, d // 2 + LANES), jnp.uint32),
                   jax.ShapeDtypeStruct((bsz * s, LANES), I32),
                   jax.ShapeDtypeStruct((nt, SUBLANES, LANES), I32)],
        compiler_params=_cparams(("parallel", "parallel")),
        name="mix_out_route_sort",
    )(*args)


def _route_sort(h2, lg, xs_ref, info_ref, seg_ref):
    tm = lg.shape[0]
    rt = xs_ref.shape[0]
    half = h2.shape[1] // 2
    lane = lax.broadcasted_iota(I32, (tm, LANES), 1)
    lane_f = lane.astype(F32)

    def top1(vals, mask):
        mv = jnp.where(mask, vals, -jnp.inf)
        m = jnp.max(mv, axis=-1, keepdims=True)
        idx = jnp.min(jnp.where(jnp.logical_and(mask, mv == m), lane_f, float(LANES)), axis=-1, keepdims=True)
        return m, idx.astype(I32)

    gmask = lane < N_EXPERT_GROUPS
    gm, gidx = top1(lg, gmask)
    g_val = 1.0 / jnp.sum(jnp.where(gmask, jnp.exp(lg - gm), 0.0), axis=-1, keepdims=True)
    elo = N_EXPERT_GROUPS + gidx * EXPERTS_PER_GROUP
    emask = jnp.logical_and(lane >= elo, lane < elo + EXPERTS_PER_GROUP)
    m1, i1 = top1(lg, emask)
    m2, i2 = top1(lg, jnp.logical_and(emask, lane != i1))
    e21 = jnp.exp(m2 - m1)
    w1 = g_val / (1.0 + e21)
    w2 = g_val * e21 / (1.0 + e21)
    e1 = i1 - N_EXPERT_GROUPS
    e2 = i2 - N_EXPERT_GROUPS
    oh1 = (lane == e1)
    oh2 = (lane == e2)
    ohs = jnp.where(jnp.logical_or(oh1, oh2), 1.0, 0.0)
    cnt = jnp.sum(ohs, axis=0, keepdims=True)
    units = jnp.floor((cnt + (MOE_UNIT - 1.0)) * (1.0 / MOE_UNIT))
    li = lax.broadcasted_iota(I32, (LANES, LANES), 0)
    lj = lax.broadcasted_iota(I32, (LANES, LANES), 1)
    upper = jnp.where(li < lj, 1.0, 0.0).astype(BF16)
    ustart = jnp.dot(jnp.broadcast_to(units, (SUBLANES, LANES)).astype(BF16), upper,
                     preferred_element_type=F32)[0:1, :]
    ri = lax.broadcasted_iota(I32, (tm, tm), 0)
    ci = lax.broadcasted_iota(I32, (tm, tm), 1)
    tri = jnp.where(ci < ri, 1.0, 0.0).astype(BF16)
    before = jnp.dot(tri, ohs.astype(BF16), preferred_element_type=F32)
    first = before + MOE_UNIT * ustart
    pos1 = jnp.sum(jnp.where(oh1, first, 0.0), axis=-1, keepdims=True)
    pos2 = jnp.sum(jnp.where(oh2, first, 0.0), axis=-1, keepdims=True)
    posm = jnp.where(lane == 0, pos1, jnp.where(lane == 1, pos2, -1.0))
    post = posm.T
    prow = lax.broadcasted_iota(I32, (rt, tm), 0).astype(F32)
    sel1 = prow == post[0:1, :]
    sel2 = prow == post[1:2, :]
    hb = h2.astype(BF16)
    xs = jnp.dot(jnp.where(jnp.logical_or(sel1, sel2), 1.0, 0.0).astype(BF16), hb, preferred_element_type=F32)
    bits = pltpu.bitcast(xs, jnp.uint32)
    xs_ref[:, :half] = jnp.bitwise_or(jnp.right_shift(bits[:, :half], jnp.uint32(16)), bits[:, half:])
    def terms(w):
        a = w.astype(BF16).astype(F32)
        b = (w - a).astype(BF16).astype(F32)
        return a, b, ((w - a) - b).astype(BF16).astype(F32)
    t1 = terms(w1)
    t2 = terms(w2)
    wm = jnp.zeros((tm, LANES), F32)
    for k, t in enumerate(t1 + t2):
        wm = jnp.where(lane == k, t, wm)
    wmb = wm.astype(BF16)
    s1 = jnp.dot(jnp.where(sel1, 1.0, 0.0).astype(BF16), wmb, preferred_element_type=F32)
    s2 = jnp.dot(jnp.where(sel2, 1.0, 0.0).astype(BF16), wmb, preferred_element_type=F32)
    wrow = (s1[:, 0:1] + s1[:, 1:2] + s1[:, 2:3]) + (s2[:, 3:4] + s2[:, 4:5] + s2[:, 5:6])
    mlane = lax.broadcasted_iota(I32, (rt, LANES), 1)
    wbits = pltpu.bitcast(jnp.broadcast_to(wrow, (rt, LANES)), jnp.uint32)
    xs_ref[:, half:] = jnp.where(mlane == 0, wbits, jnp.uint32(0))
    info_ref[...] = jnp.where(lane == 0, pos1, jnp.where(lane == 1, pos2, 0.0)).astype(I32)
    srow = lax.broadcasted_iota(I32, (SUBLANES, LANES), 0)
    total = jnp.sum(units, axis=-1, keepdims=True)
    seg = jnp.where(srow == 0, units, jnp.where(srow == 1, ustart, jnp.where(srow == 2, total, 0.0)))
    seg_ref[0] = seg.astype(I32)


def _worklist_kernel(units_ref, ustart_ref, uidx_ref, slot_ref, be_ref, nxt_ref, nb_ref):
    n_seg = units_ref.shape[0]
    n_tiles = n_seg // N_EXPERTS
    n_slots = uidx_ref.shape[0]
    n_blocks = be_ref.shape[0]
    tile_units = MOE_TILE_ROWS // MOE_UNIT

    def pad_slot(i, c):
        uidx_ref[i] = -1
        return c

    def unused_units(tau, c):
        last = tau * N_EXPERTS + (N_EXPERTS - 1)

        def mark_unused(u, c2):
            slot_ref[tau * tile_units + u] = -1
            return c2

        return lax.fori_loop(ustart_ref[last] + units_ref[last], tile_units, mark_unused, c)

    lax.fori_loop(0, n_tiles, unused_units, 0)

    def expert(e, blk0):
        def tile(tau, pos):
            n = units_ref[tau * N_EXPERTS + e]
            base = tau * tile_units + ustart_ref[tau * N_EXPERTS + e]

            def unit(j, p):
                uidx_ref[p] = base + j
                slot_ref[base + j] = p
                return p + 1

            return lax.fori_loop(0, n, unit, pos)

        end = lax.fori_loop(0, n_tiles, tile, blk0 * MOE_BLOCK_UNITS)
        nblk = lax.shift_right_logical(end - blk0 * MOE_BLOCK_UNITS + (MOE_BLOCK_UNITS - 1),
                                       int(math.log2(MOE_BLOCK_UNITS)))

        def mark(i, c):
            be_ref[blk0 + i] = e
            return c

        lax.fori_loop(0, nblk, mark, 0)
        lax.fori_loop(end, (blk0 + nblk) * MOE_BLOCK_UNITS, pad_slot, 0)
        return blk0 + nblk

    used = lax.fori_loop(0, N_EXPERTS, expert, 0)
    nb_ref[0] = used

    def tail(i, c):
        be_ref[i] = N_EXPERTS - 1
        nxt_ref[i] = -1
        return c

    lax.fori_loop(used, n_blocks, tail, 0)
    lax.fori_loop(used * MOE_BLOCK_UNITS, n_slots, pad_slot, 0)

    def following(i, carry):
        after_e, after_next = carry
        idx = used - 1 - i
        e = be_ref[idx]
        nxt = jnp.where(e == after_e, after_next, after_e)
        nxt_ref[idx] = nxt
        return e, nxt

    lax.fori_loop(0, used, following, (jnp.int32(-1), jnp.int32(-1)))


def _worklist(units, ustart, n_blocks):
    smem = pl.BlockSpec(memory_space=pltpu.SMEM)
    n_tiles = units.shape[0] // N_EXPERTS
    return pl.pallas_call(
        _worklist_kernel,
        in_specs=[smem, smem],
        out_specs=[smem, smem, smem, smem, smem],
        out_shape=[jax.ShapeDtypeStruct((n_blocks * MOE_BLOCK_UNITS,), I32),
                   jax.ShapeDtypeStruct((n_tiles * (MOE_TILE_ROWS // MOE_UNIT),), I32),
                   jax.ShapeDtypeStruct((n_blocks,), I32), jax.ShapeDtypeStruct((n_blocks,), I32),
                   jax.ShapeDtypeStruct((1,), I32)],
        name="moe_worklist",
    )(units, ustart)


def _expert_kernel(layer, be_ref, nxt_ref, uidx_ref, nb_ref, xs_hbm, w1_hbm, w3_hbm, w2_hbm, o_ref,
                   xbuf, gsem, w1f, w3f, w2f, wsem, wslot, w1b, w3b, w2b):
    b = pl.program_id(0)
    used = nb_ref[0]
    half = xs_hbm.shape[1] - LANES
    slot = lax.rem(b, 2)
    other = 1 - slot

    def weight_copies(e, s):
        return (pltpu.make_async_copy(w1_hbm.at[layer, e], w1f.at[s], wsem.at[s]),
                pltpu.make_async_copy(w3_hbm.at[layer, e], w3f.at[s], wsem.at[s]),
                pltpu.make_async_copy(w2_hbm.at[layer, e], w2f.at[s], wsem.at[s]))

    def gather_copy(s, j, unit):
        return pltpu.make_async_copy(xs_hbm.at[pl.ds(pl.multiple_of(unit * MOE_UNIT, MOE_UNIT), MOE_UNIT), :],
                                     xbuf.at[s, pl.ds(j * MOE_UNIT, MOE_UNIT), :], gsem.at[s])

    def gather_start(blk, s):
        for j in range(MOE_BLOCK_UNITS):
            gather_copy(s, j, jnp.maximum(uidx_ref[blk * MOE_BLOCK_UNITS + j], 0)).start(priority=j % 2)

    def gather_wait(s):
        for j in range(MOE_BLOCK_UNITS):
            gather_copy(s, j, 0).wait()

    @pl.when(b == 0)
    def _():
        gather_start(0, 0)
        wslot[0] = 1
        for cp in weight_copies(be_ref[0], 0):
            cp.start()

    @pl.when(b + 1 < used)
    def _():
        gather_start(b + 1, other)

    @pl.when(b >= used)
    def _():
        o_ref[...] = jnp.zeros_like(o_ref)

    @pl.when(b < used)
    def _():
        prev = be_ref[jnp.maximum(b - 1, 0)]

        @pl.when(jnp.logical_or(b == 0, be_ref[b] != prev))
        def _():
            s = 1 - wslot[0]
            wslot[0] = s
            for cp in weight_copies(0, s):
                cp.wait()
            w1b[...] = w1f[s].astype(BF16)
            w3b[...] = w3f[s].astype(BF16)
            w2b[...] = w2f[s].astype(BF16)

            @pl.when(nxt_ref[b] >= 0)
            def _():
                for cp in weight_copies(nxt_ref[b], 1 - s):
                    cp.start()

        gather_wait(slot)
        xw = xbuf[slot]
        word = xw[:, :half]
        lo = pltpu.bitcast(jnp.left_shift(word, jnp.uint32(16)), F32)
        hi = pltpu.bitcast(jnp.bitwise_and(word, jnp.uint32(0xFFFF0000)), F32)
        x = jnp.concatenate([lo, hi], axis=1).astype(BF16)
        wrow = pltpu.bitcast(xw[:, half:], F32)[:, 0:1]
        a = jnp.dot(x, w1b[...], preferred_element_type=F32)
        gte = jnp.dot(x, w3b[...], preferred_element_type=F32)
        mid = (a * _sigmoid(a)) * gte
        o_ref[...] = jnp.dot(mid.astype(BF16), w2b[...], preferred_element_type=F32) * wrow


def _expert_ffn(layer, xs, blk_expert, blk_next, unit_idx, n_used, w1, w3, w2):
    d, de = w1.shape[2], w1.shape[3]
    nb = blk_expert.shape[0]
    rows = MOE_BLOCK_UNITS * MOE_UNIT
    hbm = pl.BlockSpec(memory_space=pl.ANY)
    grid_spec = pltpu.PrefetchScalarGridSpec(
        num_scalar_prefetch=4,
        grid=(nb,),
        in_specs=[hbm, hbm, hbm, hbm],
        out_specs=pl.BlockSpec((rows, d), lambda b, be, nx, ui, nu: (b, 0)),
        scratch_shapes=[pltpu.VMEM((2, rows, xs.shape[1]), jnp.uint32), pltpu.SemaphoreType.DMA((2,)),
                        pltpu.VMEM((2, d, de), F32), pltpu.VMEM((2, d, de), F32), pltpu.VMEM((2, de, d), F32),
                        pltpu.SemaphoreType.DMA((2,)), pltpu.SMEM((1,), I32),
                        pltpu.VMEM((d, de), BF16), pltpu.VMEM((d, de), BF16), pltpu.VMEM((de, d), BF16)],
    )
    return pl.pallas_call(
        functools.partial(_expert_kernel, layer),
        grid_spec=grid_spec,
        out_shape=jax.ShapeDtypeStruct((nb * rows, d), F32),
        compiler_params=_cparams(("arbitrary",)),
        name="moe_expert_ffn",
    )(blk_expert, blk_next, unit_idx, n_used, xs, w1, w3, w2)


def _combine_kernel(alpha, with_next, nu_ref, slot_ref, ys_hbm, info_ref, x_ref, gt_ref, lng_ref, lnb_ref, *rest):
    if with_next:
        sh_ref, sc_ref, o_ref, hn_ref, ybuf, sem = rest
    else:
        o_ref, ybuf, sem = rest
    i = pl.program_id(0)
    nt = pl.num_programs(0)
    tm = x_ref.shape[0]
    rt = ybuf.shape[1]
    slot = lax.rem(i, 2)
    other = 1 - slot

    def unit_copy(tile, s, j):
        src = pl.multiple_of(slot_ref[tile * (rt // MOE_UNIT) + j] * MOE_UNIT, MOE_UNIT)
        dst = pl.multiple_of(j * MOE_UNIT, MOE_UNIT)
        return pltpu.make_async_copy(ys_hbm.at[pl.ds(src, MOE_UNIT), :], ybuf.at[s, pl.ds(dst, MOE_UNIT), :], sem.at[s])

    def start(tile, s):
        def body(j, c):
            unit_copy(tile, s, j).start()
            return c
        lax.fori_loop(0, nu_ref[tile], body, 0)

    def wait(tile, s):
        def body(j, c):
            unit_copy(tile, s, j).wait()
            return c
        lax.fori_loop(0, nu_ref[tile], body, 0)

    @pl.when(i == 0)
    def _():
        ybuf[...] = jnp.zeros_like(ybuf)
        start(0, 0)

    @pl.when(i + 1 < nt)
    def _():
        start(i + 1, other)

    wait(i, slot)
    info = info_ref[...]
    col = lax.broadcasted_iota(I32, (tm, rt), 1)
    pick = jnp.where(jnp.logical_or(col == info[:, 0:1], col == info[:, 1:2]), 1.0, 0.0).astype(BF16)
    ys = ybuf[slot]
    y = _dot_split(ys, pick, lhs=True)
    x2 = _layer_norm(alpha * x_ref[...] + (1.0 + gt_ref[0]) * y) * lng_ref[...] + lnb_ref[...]
    o_ref[...] = x2
    if with_next:
        hn_ref[...] = (_layer_norm(x2) * (1.0 + sc_ref[0]) + sh_ref[0]).astype(BF16)


def _combine(alpha, ys, tile_units, unit_slot, info, x1, gate, ln_g, ln_b, seq, next_mod=None):
    t, d = x1.shape
    tm = MOE_TILE
    per_seq = seq // tm
    tok = pl.BlockSpec((tm, d), lambda i, nu, us: (i, 0))
    per_b = pl.BlockSpec((1, 1, d), lambda i, nu, us: (i // per_seq, 0, 0))
    row = pl.BlockSpec((1, d), lambda i, nu, us: (0, 0))
    with_next = next_mod is not None
    args = [tile_units, unit_slot, ys, info, x1, gate[:, None, :], ln_g.astype(F32).reshape(1, d),
            ln_b.astype(F32).reshape(1, d)]
    in_specs = [pl.BlockSpec(memory_space=pl.ANY), pl.BlockSpec((tm, LANES), lambda i, nu, us: (i, 0)),
                tok, per_b, row, row]
    out_specs, out_shape = tok, jax.ShapeDtypeStruct((t, d), F32)
    if with_next:
        args += [next_mod[0][:, None, :], next_mod[1][:, None, :]]
        in_specs += [per_b, per_b]
        out_specs, out_shape = [tok, tok], [out_shape, jax.ShapeDtypeStruct((t, d), BF16)]
    grid_spec = pltpu.PrefetchScalarGridSpec(
        num_scalar_prefetch=2,
        grid=(t // tm,),
        in_specs=in_specs,
        out_specs=out_specs,
        scratch_shapes=[pltpu.VMEM((2, MOE_TILE_ROWS, d), F32), pltpu.SemaphoreType.DMA((2,))],
    )
    return pl.pallas_call(
        functools.partial(_combine_kernel, alpha, with_next),
        grid_spec=grid_spec,
        out_shape=out_shape,
        compiler_params=_cparams(("arbitrary",)),
        name="moe_combine_ln",
    )(*args)


def _moe(layer, alpha, xs, info, seg, x1, gate, ln_g, ln_b, w1, w3, w2, next_mod):
    bsz, s, d = x1.shape
    t = bsz * s
    nt = t // MOE_TILE
    units = seg[:, 0, :N_EXPERTS].reshape(nt * N_EXPERTS)
    ustart = seg[:, 1, :N_EXPERTS].reshape(nt * N_EXPERTS)
    tile_units = seg[:, 2, 0]
    max_units = nt * (TOP_K * MOE_TILE // MOE_UNIT + N_EXPERTS * (MOE_UNIT - 1) // MOE_UNIT)
    n_blocks = max_units // MOE_BLOCK_UNITS + N_EXPERTS
    unit_idx, unit_slot, blk_expert, blk_next, n_used = _worklist(units, ustart, n_blocks)
    ys = _expert_ffn(layer, xs, blk_expert, blk_next, unit_idx, n_used, w1, w3, w2)
    out = _combine(alpha, ys, tile_units, unit_slot, info, x1.reshape(t, d), gate, ln_g, ln_b, s, next_mod)
    if next_mod is None:
        return out.reshape(bsz, s, d), None
    return out[0].reshape(bsz, s, d), out[1].reshape(bsz, s, d)


def kernel(x, c, w_ada, b_ada, ln_g, ln_b, w_in, w_out, conv_w, rwkv_mu, rwkv_w0, rwkv_w2, rwkv_a0, rwkv_a2, rwkv_g2, rwkv_kk, rwkv_ka, rwkv_rk, rwkv_gn_g, rwkv_gn_b, attn_sinks, rel_bias, s5_lambda_re, s5_lambda_im, s5_log_dt, s5_b_re, s5_b_im, s5_c_re, s5_c_im, s5_d, s5_glu_w, s5_glu_b, router_group_w, router_group_b, router_expert_w, router_expert_b, moe_w1, moe_w3, moe_w2):
    depth = w_ada.shape[0]
    d = x.shape[-1]
    g = d // 4
    alpha = (2 * depth) ** 0.25
    n_heads = g // HEAD_DIM
    att_kv = max(1, n_heads // 4) * HEAD_DIM
    rw_off = 3 * g
    lora = RWKV_DECAY_RANK + RWKV_A_RANK + RWKV_GATE_RANK
    att_off = rw_off + 3 * g + lora
    s5_off = att_off + g + 2 * att_kv

    mod = _modulation(c, w_ada, b_ada)
    for l in range(depth):
        sh1, sc1, gt1, sh2, sc2, gt2 = jnp.split(mod[l], 6, axis=-1)
        wl = w_in[l]
        w_conv = wl[:, :rw_off].astype(BF16)
        w_rkv = wl[:, rw_off:rw_off + 3 * g].astype(BF16)
        lo = rw_off + 3 * g
        zcol = lambda n: jnp.zeros((d, n), F32)
        w_lora = jnp.concatenate([
            wl[:, lo:lo + RWKV_DECAY_RANK], zcol(LANES - RWKV_DECAY_RANK),
            wl[:, lo + RWKV_DECAY_RANK:lo + RWKV_DECAY_RANK + RWKV_A_RANK], zcol(LANES - RWKV_A_RANK),
            wl[:, lo + RWKV_DECAY_RANK + RWKV_A_RANK:att_off]], axis=1).astype(BF16)
        w_q = wl[:, att_off:att_off + g].astype(BF16)
        w_kv = wl[:, att_off + g:s5_off].astype(BF16)
        w_s5 = wl[:, s5_off:].astype(BF16)

        if l == 0:
            h = _adaln(x, sh1, sc1)
        y_conv = _conv_mixer(h, w_conv, conv_w[l].astype(F32))
        y_rwkv = _rwkv_mixer(h, w_rkv, w_lora, rwkv_mu[l], rwkv_w0[l], rwkv_w2[l], rwkv_a0[l], rwkv_a2[l],
                             rwkv_g2[l], rwkv_kk[l], rwkv_ka[l], rwkv_rk[l], rwkv_gn_g[l], rwkv_gn_b[l])
        y_att = _att_mixer(h, w_q, w_kv, attn_sinks[l], rel_bias)
        y_ssm = _s5_mixer(h, w_s5, s5_lambda_re[l], s5_lambda_im[l], s5_log_dt[l], s5_b_re[l], s5_b_im[l],
                          s5_c_re[l], s5_c_im[l], s5_d[l], s5_glu_w[l], s5_glu_b[l])
        w_router = jnp.zeros((d, LANES), F32)
        w_router = w_router.at[:, :N_EXPERT_GROUPS].set(router_group_w[l])
        w_router = w_router.at[:, N_EXPERT_GROUPS:N_EXPERT_GROUPS + N_EXPERTS].set(router_expert_w[l]).astype(BF16)
        b_router = jnp.zeros((1, LANES), F32)
        b_router = b_router.at[0, :N_EXPERT_GROUPS].set(router_group_b[l])
        b_router = b_router.at[0, N_EXPERT_GROUPS:N_EXPERT_GROUPS + N_EXPERTS].set(router_expert_b[l])
        x1, xs, info, seg = _mix_out(alpha, (y_conv, y_rwkv, y_att, y_ssm), w_out[l].astype(BF16), x, gt1,
                                     ln_g[l, 0], ln_b[l, 0], sh2, sc2, w_router, b_router)
        next_mod = None
        if l + 1 < depth:
            nsh1, nsc1 = jnp.split(mod[l + 1], 6, axis=-1)[:2]
            next_mod = (nsh1, nsc1)
        x, h = _moe(l, alpha, xs, info, seg, x1, gt2, ln_g[l, 1], ln_b[l, 1], moe_w1, moe_w3, moe_w2, next_mod)
    return x
```

```python
import functools
import math

import numpy as np
import jax
import jax.numpy as jnp
from jax import lax
from jax.experimental import pallas as pl
from jax.experimental.pallas import tpu as pltpu

F32 = jnp.float32
BF16 = jnp.bfloat16
I32 = jnp.int32

HEAD_DIM = 64
CONV_WIDTH = 3
RWKV_DECAY_RANK = 96
RWKV_A_RANK = 96
RWKV_GATE_RANK = 128
RWKV_GN_EPS = 64e-5
ATT_BLOCK = 128
WINDOW = 128
N_BUCKETS = 32
NEG_INF = -1e30
S5_CH = 16
S5_STATE = 64
N_EXPERT_GROUPS = 4
EXPERTS_PER_GROUP = 8
N_EXPERTS = N_EXPERT_GROUPS * EXPERTS_PER_GROUP
TOP_K = 2
LN_EPS = 1e-5

LANES = 128
SUBLANES = 8
WKV_CHUNK = 64
MOE_TILE = 256
MOE_UNIT = SUBLANES
MOE_TILE_ROWS = 768
MOE_BLOCK_UNITS = 32
VMEM_LIMIT = 56 * 2 ** 20


def _cparams(sem, flags=None):
    return pltpu.CompilerParams(dimension_semantics=sem, vmem_limit_bytes=VMEM_LIMIT, flags=flags)


def _dot(a, b):
    return jnp.dot(a.astype(BF16), b.astype(BF16), preferred_element_type=F32)


def _dot_nt(a, b):
    return lax.dot_general(a.astype(BF16), b.astype(BF16), (((1,), (1,)), ((), ())),
                           preferred_element_type=F32)


def _dot_tn(a, b):
    return jnp.dot(a.T.astype(BF16), b.astype(BF16), preferred_element_type=F32)


def _dot_split(x, e, lhs=False):
    hi = x.astype(BF16)
    lo = (x - hi.astype(F32)).astype(BF16)
    if lhs:
        return jnp.dot(e, hi, preferred_element_type=F32) + jnp.dot(e, lo, preferred_element_type=F32)
    return jnp.dot(hi, e, preferred_element_type=F32) + jnp.dot(lo, e, preferred_element_type=F32)


def _pack_bf16_pairs(x):
    n = x.shape[1] // 2
    bits = pltpu.bitcast(x.astype(BF16).astype(F32), jnp.uint32)
    return jnp.bitwise_or(jnp.right_shift(bits[:, :n], jnp.uint32(16)), bits[:, n:])


def _unpack_bf16_pairs(word):
    lo = pltpu.bitcast(jnp.left_shift(word, jnp.uint32(16)), F32)
    hi = pltpu.bitcast(jnp.bitwise_and(word, jnp.uint32(0xFFFF0000)), F32)
    return jnp.concatenate([lo, hi], axis=1).astype(BF16)


def _sigmoid(x):
    return 1.0 / (1.0 + jnp.exp(-x))


def _layer_norm(x):
    mean = jnp.mean(x, axis=-1, keepdims=True)
    xc = x - mean
    var = jnp.mean(xc * xc, axis=-1, keepdims=True)
    return xc * lax.rsqrt(var + LN_EPS)


def _shift_rows(p, carry_row, n):
    row = lax.broadcasted_iota(I32, (p.shape[0], 1), 0)
    out = pltpu.roll(p, n, 0)
    for i in range(n):
        out = jnp.where(row == i, carry_row[SUBLANES - n + i:SUBLANES - n + i + 1, :], out)
    return out


def _mod_kernel(c_ref, w_ref, b_ref, o_ref):
    c = c_ref[...]
    a = c * _sigmoid(c)
    o_ref[0] = _dot(a, w_ref[0]) + b_ref[0]


def _modulation(c, w_ada, b_ada):
    depth, d, n = w_ada.shape
    bsz = c.shape[0]
    tn = 1536
    cp = jnp.zeros((SUBLANES, d), F32).at[:bsz].set(c)
    out = pl.pallas_call(
        _mod_kernel,
        grid=(depth, n // tn),
        in_specs=[pl.BlockSpec((SUBLANES, d), lambda l, j: (0, 0)),
                  pl.BlockSpec((1, d, tn), lambda l, j: (l, 0, j)),
                  pl.BlockSpec((1, 1, tn), lambda l, j: (l, 0, j))],
        out_specs=pl.BlockSpec((1, SUBLANES, tn), lambda l, j: (l, 0, j)),
        out_shape=jax.ShapeDtypeStruct((depth, SUBLANES, n), F32),
        compiler_params=_cparams(("parallel", "parallel")),
        name="adaln_modulation",
    )(cp, w_ada, b_ada.reshape(depth, 1, n))
    return out[:, :bsz]


def _adaln_kernel(x_ref, sh_ref, sc_ref, h_ref):
    h_ref[0] = (_layer_norm(x_ref[0]) * (1.0 + sc_ref[0]) + sh_ref[0]).astype(BF16)


def _adaln(x, shift, scale):
    bsz, s, d = x.shape
    tm = 512
    return pl.pallas_call(
        _adaln_kernel,
        grid=(bsz, s // tm),
        in_specs=[pl.BlockSpec((1, tm, d), lambda b, i: (b, i, 0)),
                  pl.BlockSpec((1, 1, d), lambda b, i: (b, 0, 0)),
                  pl.BlockSpec((1, 1, d), lambda b, i: (b, 0, 0))],
        out_specs=pl.BlockSpec((1, tm, d), lambda b, i: (b, i, 0)),
        out_shape=jax.ShapeDtypeStruct((bsz, s, d), BF16),
        compiler_params=_cparams(("parallel", "parallel")),
        name="adaln_input",
    )(x, shift[:, None, :], scale[:, None, :])


def _conv_kernel(h_ref, w_ref, cw_ref, y_ref, carry_ref):
    tm = h_ref.shape[1]
    g = y_ref.shape[2]

    @pl.when(pl.program_id(1) == 0)
    def _():
        carry_ref[...] = jnp.zeros_like(carry_ref)

    p = _dot(h_ref[0], w_ref[...])
    b_gate, c_gate, hh = p[:, :g], p[:, g:2 * g], p[:, 2 * g:]
    z = c_gate * hh
    carry = carry_ref[...]
    z1 = _shift_rows(z, carry, 1)
    z2 = _shift_rows(z, carry, 2)
    cw = cw_ref[...]
    out = cw[0:1] * z2 + cw[1:2] * z1 + cw[2:3] * z
    y_ref[0] = (b_gate * out).astype(BF16)
    carry_ref[...] = z[tm - SUBLANES:, :]


def _conv_mixer(h, w, conv_w):
    bsz, s, d = h.shape
    g = conv_w.shape[1]
    tm = 512
    return pl.pallas_call(
        _conv_kernel,
        grid=(bsz, s // tm),
        in_specs=[pl.BlockSpec((1, tm, d), lambda b, i: (b, i, 0)),
                  pl.BlockSpec((d, 3 * g), lambda b, i: (0, 0)),
                  pl.BlockSpec((CONV_WIDTH, g), lambda b, i: (0, 0))],
        out_specs=pl.BlockSpec((1, tm, g), lambda b, i: (b, i, 0)),
        out_shape=jax.ShapeDtypeStruct((bsz, s, g), BF16),
        scratch_shapes=[pltpu.VMEM((SUBLANES, g), F32)],
        compiler_params=_cparams(("parallel", "arbitrary")),
        name="conv_mixer",
    )(h, w, conv_w)


def _t5_bucket(rel):
    n = jnp.maximum(rel, 0)
    max_exact = N_BUCKETS // 2
    n_f = jnp.maximum(n, 1).astype(F32)
    large = max_exact + (jnp.log(n_f / max_exact) / math.log(WINDOW / max_exact)
                         * (N_BUCKETS - max_exact)).astype(I32)
    return jnp.where(n < max_exact, n, jnp.minimum(large, N_BUCKETS - 1))


def _att_kernel(sink_ref, h_ref, wq_ref, wkv_ref, bias_ref, y_ref, kvc_ref):
    tm = h_ref.shape[1]
    n_heads = bias_ref.shape[0]
    kvw = wkv_ref.shape[1] // 2
    n_kv = kvw // HEAD_DIM
    rep = n_heads // n_kv
    blk = ATT_BLOCK
    first_tile = pl.program_id(1) == 0

    @pl.when(first_tile)
    def _():
        kvc_ref[...] = jnp.zeros_like(kvc_ref)

    x = h_ref[0]
    q = _dot(x, wq_ref[...]) * (HEAD_DIM ** -0.5)
    kv = _dot(x, wkv_ref[...])
    kvext = jnp.concatenate([kvc_ref[...], kv], axis=0)
    kvc_ref[...] = kv[tm - blk:, :]
    col = lax.broadcasted_iota(I32, (blk, 2 * blk), 1)
    qb16 = q.astype(BF16)
    kv16 = kvext.astype(BF16)

    def scores(j):
        qb = qb16[j * blk:(j + 1) * blk]
        kw = kv16[j * blk:j * blk + 2 * blk, :kvw]
        kgs = [kw[:, gi * HEAD_DIM:(gi + 1) * HEAD_DIM] for gi in range(n_kv)]
        scs = [_dot_nt(qb[:, hh * HEAD_DIM:(hh + 1) * HEAD_DIM], kgs[hh // rep]) + bias_ref[hh]
               for hh in range(n_heads)]
        if j == 0:
            scs = [jnp.where(jnp.logical_and(first_tile, col < blk), NEG_INF, sc) for sc in scs]
        return scs

    def probs(scs):
        out = []
        for hh, sc in enumerate(scs):
            sink = sink_ref[hh]
            m = jnp.maximum(jnp.max(sc, axis=-1, keepdims=True), sink)
            e = jnp.exp(sc - m)
            den = jnp.sum(e, axis=-1, keepdims=True) + jnp.exp(sink - m)
            out.append((e / den).astype(BF16))
        return out

    def values(j, ps):
        vw = kv16[j * blk:j * blk + 2 * blk, kvw:]
        vgs = [vw[:, gi * HEAD_DIM:(gi + 1) * HEAD_DIM] for gi in range(n_kv)]
        outs = [jnp.dot(p, vgs[hh // rep], preferred_element_type=F32) for hh, p in enumerate(ps)]
        y_ref[0, j * blk:(j + 1) * blk, :] = jnp.concatenate(outs, axis=1).astype(BF16)

    nblk = tm // blk
    pending = scores(0)
    for j in range(nblk):
        nxt = scores(j + 1) if j + 1 < nblk else None
        values(j, probs(pending))
        pending = nxt


def _att_mixer(h, wq, wkv, sinks, rel_bias):
    bsz, s, d = h.shape
    n_heads = sinks.shape[0]
    tm = 512
    qi = jnp.arange(ATT_BLOCK)[:, None]
    kj = jnp.arange(2 * ATT_BLOCK)[None, :]
    rel = qi + ATT_BLOCK - kj
    valid = (rel >= 0) & (rel < WINDOW)
    onehot = (_t5_bucket(rel)[..., None] == jnp.arange(N_BUCKETS)).astype(F32)
    bias = jnp.einsum('qkb,bh->hqk', onehot, rel_bias.astype(F32), precision=lax.Precision.HIGHEST)
    bias = jnp.where(valid[None], bias, NEG_INF)
    return pl.pallas_call(
        _att_kernel,
        grid=(bsz, s // tm),
        in_specs=[pl.BlockSpec(memory_space=pltpu.SMEM),
                  pl.BlockSpec((1, tm, d), lambda b, i: (b, i, 0)),
                  pl.BlockSpec(wq.shape, lambda b, i: (0, 0)),
                  pl.BlockSpec(wkv.shape, lambda b, i: (0, 0)),
                  pl.BlockSpec(bias.shape, lambda b, i: (0, 0, 0))],
        out_specs=pl.BlockSpec((1, tm, wq.shape[1]), lambda b, i: (b, i, 0)),
        out_shape=jax.ShapeDtypeStruct((bsz, s, wq.shape[1]), BF16),
        scratch_shapes=[pltpu.VMEM((ATT_BLOCK, wkv.shape[1]), F32)],
        compiler_params=_cparams(("parallel", "arbitrary")),
        name="swa_mixer",
    )(sinks.astype(F32), h, wq, wkv, bias)


S5_GROUPS_PER_BLOCK = LANES // S5_CH


def _s5_tables(lam_re, lam_im, log_dt, b_re, b_im, c_re, c_im):
    n_groups, p = lam_re.shape
    lr, li = lam_re.astype(F32), lam_im.astype(F32)
    delta = jnp.exp(log_dt.astype(F32))[:, None]
    mag = jnp.exp(lr * delta)
    ab_re, ab_im = mag * jnp.cos(li * delta), mag * jnp.sin(li * delta)
    den = lr * lr + li * li
    z_re = ((ab_re - 1.0) * lr + ab_im * li) / den
    z_im = (ab_im * lr - (ab_re - 1.0) * li) / den
    br, bi = b_re.astype(F32), b_im.astype(F32)
    bb_re = z_re[..., None] * br - z_im[..., None] * bi
    bb_im = z_re[..., None] * bi + z_im[..., None] * br
    nblk = n_groups // S5_GROUPS_PER_BLOCK
    eye = jnp.eye(S5_GROUPS_PER_BLOCK, dtype=F32)

    def in_blocks(bb):
        bb = bb.reshape(nblk, S5_GROUPS_PER_BLOCK, p, S5_CH)
        return jnp.einsum('qgpc,gh->qgchp', bb, eye).reshape(nblk, LANES, S5_GROUPS_PER_BLOCK * p)

    def out_blocks(cc):
        cc = cc.astype(F32).reshape(nblk, S5_GROUPS_PER_BLOCK, S5_CH, p)
        return jnp.einsum('qgcp,gh->qgphc', cc, eye).reshape(nblk, S5_GROUPS_PER_BLOCK * p, LANES)

    def power(m):
        mg = jnp.exp(m * lr * delta)
        return (mg * jnp.cos(m * li * delta)).reshape(1, -1), (mg * jnp.sin(m * li * delta)).reshape(1, -1)

    row = jnp.arange(SUBLANES, dtype=F32)[:, None]
    tabs = []
    for sft in (1, 2, 4):
        pr, pi = power(float(sft))
        keep = row >= sft
        tabs += [jnp.where(keep, pr, 0.0), jnp.where(keep, pi, 0.0)]
    n_state = n_groups * p
    lrd = (lr * delta).reshape(1, n_state)
    lid = (li * delta).reshape(1, n_state)
    mg = jnp.exp((row + 1.0) * lrd)
    tabs += [mg * jnp.cos((row + 1.0) * lid), mg * jnp.sin((row + 1.0) * lid)]
    tables = jnp.stack(tabs, axis=0)
    return (in_blocks(bb_re).astype(BF16), in_blocks(bb_im).astype(BF16),
            out_blocks(c_re).astype(BF16), out_blocks(c_im).astype(BF16), tables)


def _s5_kernel(h_ref, w_ref, bre_ref, bim_ref, cre_ref, cim_ref, tab_ref, d_ref, gw_ref, gb_ref,
               y_ref, xr_ref, xi_ref, cr_ref, ci_ref):
    nseq = h_ref.shape[0]
    tm = h_ref.shape[1]
    nblk = bre_ref.shape[0]
    sw = bre_ref.shape[2]

    @pl.when(pl.program_id(0) == 0)
    def _():
        cr_ref[...] = jnp.zeros_like(cr_ref)
        ci_ref[...] = jnp.zeros_like(ci_ref)

    def project(b):
        u = _dot(h_ref[b], w_ref[...])
        ub = u.astype(BF16)
        for q in range(nblk):
            uq = ub[:, q * LANES:(q + 1) * LANES]
            xr_ref[b, :, q * sw:(q + 1) * sw] = jnp.dot(uq, bre_ref[q], preferred_element_type=F32)
            xi_ref[b, :, q * sw:(q + 1) * sw] = jnp.dot(uq, bim_ref[q], preferred_element_type=F32)
        return u

    def scan(b):
        cr = cr_ref[b, 0:1, :]
        ci = ci_ref[b, 0:1, :]
        for i in range(tm // SUBLANES):
            rows = slice(i * SUBLANES, (i + 1) * SUBLANES)
            xr = xr_ref[b, rows, :]
            xi = xi_ref[b, rows, :]
            for k, sft in enumerate((1, 2, 4)):
                mr = tab_ref[2 * k]
                mi = tab_ref[2 * k + 1]
                sr = pltpu.roll(xr, sft, 0)
                si = pltpu.roll(xi, sft, 0)
                xr, xi = xr + mr * sr - mi * si, xi + mr * si + mi * sr
            pr = tab_ref[6]
            pi = tab_ref[7]
            xr, xi = xr + pr * cr - pi * ci, xi + pr * ci + pi * cr
            xr_ref[b, rows, :] = xr
            xi_ref[b, rows, :] = xi
            cr = xr[SUBLANES - 1:SUBLANES, :]
            ci = xi[SUBLANES - 1:SUBLANES, :]
        cr_ref[b, 0:1, :] = cr
        ci_ref[b, 0:1, :] = ci

    def readout(b, u):
        ys = []
        for q in range(nblk):
            xr = xr_ref[b, :, q * sw:(q + 1) * sw].astype(BF16)
            xi = xi_ref[b, :, q * sw:(q + 1) * sw].astype(BF16)
            ys.append(jnp.dot(xr, cre_ref[q], preferred_element_type=F32)
                      - jnp.dot(xi, cim_ref[q], preferred_element_type=F32))
        y = jnp.concatenate(ys, axis=1) + d_ref[...] * u
        y = 0.5 * y * (1.0 + jnp.tanh(math.sqrt(2.0 / math.pi) * (y + 0.044715 * (y * y * y))))
        y_ref[b] = (y * _sigmoid(_dot(y, gw_ref[...]) + gb_ref[...])).astype(BF16)

    us = [project(b) for b in range(nseq)]
    for b in range(nseq):
        scan(b)
        readout(b, us[b])


def _s5_mixer(h, w, lam_re, lam_im, log_dt, b_re, b_im, c_re, c_im, d_skip, glu_w, glu_b):
    bsz, s, d = h.shape
    g = w.shape[1]
    tm = 256
    bre, bim, cre, cim, tables = _s5_tables(lam_re, lam_im, log_dt, b_re, b_im, c_re, c_im)
    n_state = tables.shape[2]
    full = lambda a: pl.BlockSpec(a.shape, lambda i: (0,) * a.ndim)
    dvec = d_skip.astype(F32).reshape(1, g)
    gw = glu_w.astype(BF16)
    gb = glu_b.astype(F32).reshape(1, g)
    return pl.pallas_call(
        _s5_kernel,
        grid=(s // tm,),
        in_specs=[pl.BlockSpec((bsz, tm, d), lambda i: (0, i, 0)),
                  full(w), full(bre), full(bim), full(cre), full(cim), full(tables),
                  full(dvec), full(gw), full(gb)],
        out_specs=pl.BlockSpec((bsz, tm, g), lambda i: (0, i, 0)),
        out_shape=jax.ShapeDtypeStruct((bsz, s, g), BF16),
        scratch_shapes=[pltpu.VMEM((bsz, tm, n_state), F32), pltpu.VMEM((bsz, tm, n_state), F32),
                        pltpu.VMEM((bsz, SUBLANES, n_state), F32), pltpu.VMEM((bsz, SUBLANES, n_state), F32)],
        compiler_params=_cparams(("arbitrary",)),
        name="s5_mixer",
    )(h, w, bre, bim, cre, cim, tables, dvec, gw, gb)


def _rwkv_kernel(h_ref, wrkv_ref, wlo_ref, mu1_ref, mu2_ref, w0_ref, w2_ref, a0_ref, a2_ref, g2_ref,
                 kk_ref, ka_ref, rk_ref, gng_ref, gnb_ref, eblk_ref,
                 y_ref,
                 cp_ref, cl_ref, hs_ref, r_s, k_s, v_s, a_s, b_s, ld_s, y_s):
    tm = h_ref.shape[1]
    g = y_ref.shape[2]
    npair = g // LANES
    ch = WKV_CHUNK

    @pl.when(pl.program_id(1) == 0)
    def _():
        cp_ref[...] = jnp.zeros_like(cp_ref)
        cl_ref[...] = jnp.zeros_like(cl_ref)
        hs_ref[...] = jnp.zeros_like(hs_ref)

    x = h_ref[0]
    p = _dot(x, wrkv_ref[...])
    plo = _dot(x, wlo_ref[...])
    pprev = _shift_rows(p, cp_ref[...], 1)
    lprev = _shift_rows(plo, cl_ref[...], 1)
    cp_ref[...] = p[tm - SUBLANES:, :]
    cl_ref[...] = plo[tm - SUBLANES:, :]
    p = p + (pprev - p) * mu1_ref[...]
    plo = plo + (lprev - plo) * mu2_ref[...]
    r, k, v = p[:, :g], p[:, g:2 * g], p[:, 2 * g:]
    w_lo, a_lo, g_lo = plo[:, :LANES], plo[:, LANES:2 * LANES], plo[:, 2 * LANES:]
    wraw = w0_ref[...] + _dot(jnp.tanh(w_lo), w2_ref[...])
    nz = -wraw
    softplus = jnp.maximum(nz, 0.0) + jnp.log(1.0 + jnp.exp(-jnp.abs(nz)))
    w = -softplus - 0.5
    ld_s[...] = -jnp.exp(w)
    a = _sigmoid(a0_ref[...] + _dot(a_lo, a2_ref[...]))
    gate = _dot(_sigmoid(g_lo), g2_ref[...])
    eblk = eblk_ref[...]
    kk = k * kk_ref[...]
    kk = kk / jnp.maximum(jnp.sqrt(_dot_split(kk * kk, eblk)), 1e-12)
    k = k * (1.0 + (a - 1.0) * ka_ref[...])
    r_s[...] = r
    k_s[...] = k
    v_s[...] = v
    a_s[...] = -kk
    b_s[...] = kk * a

    lane = lax.broadcasted_iota(I32, (1, LANES), 1)
    m0 = (lane < HEAD_DIM).astype(F32)
    m1 = 1.0 - m0
    ri = lax.broadcasted_iota(I32, (2 * ch, 2 * ch), 0)
    ci = lax.broadcasted_iota(I32, (2 * ch, 2 * ch), 1)
    same = (ri < ch) == (ci < ch)
    rloc = jnp.bitwise_and(ri, ch - 1)
    cloc = jnp.bitwise_and(ci, ch - 1)
    strict = jnp.where(jnp.logical_and(same, cloc < rloc), 1.0, 0.0)
    incl = jnp.where(jnp.logical_and(same, cloc <= rloc), 1.0, 0.0)
    eye = jnp.where(ri == ci, 1.0, 0.0)
    tri = jnp.where(lax.broadcasted_iota(I32, (ch, ch), 1) <= lax.broadcasted_iota(I32, (ch, ch), 0),
                    1.0, 0.0).astype(BF16)

    def bd(t):
        return jnp.concatenate([t * m0, t * m1], axis=0)

    nchunk = tm // ch
    per_chunk = []
    for c in range(nchunk):
        rows = slice(c * ch, (c + 1) * ch)
        ld = ld_s[rows, :]
        ld_hi = ld.astype(BF16)
        ld_lo = (ld - ld_hi.astype(F32)).astype(BF16)
        cum = (jnp.dot(tri, ld_hi, preferred_element_type=F32)
               + jnp.dot(tri, ld_lo, preferred_element_type=F32))
        gam = jnp.exp(cum)
        ginv = jnp.exp(-cum)
        per_chunk.append(dict(at=a_s[rows, :] * jnp.exp(cum - ld), rt=r_s[rows, :] * gam,
                              bt=b_s[rows, :] * ginv, kt=k_s[rows, :] * ginv, v=v_s[rows, :],
                              gl=gam[ch - 1:ch, :]))
    inst = [(c, q) for c in range(nchunk) for q in range(npair)]

    def part(name):
        return [per_chunk[c][name][:, q * LANES:(q + 1) * LANES] for c, q in inst]

    bt, kt, gl = part("bt"), part("kt"), part("gl")
    at_bd = [bd(t) for t in part("at")]
    rt_bd = [bd(t) for t in part("rt")]
    v_bd = [bd(t) for t in part("v")]
    bh_t = [bd(b * g_).T for b, g_ in zip(bt, gl)]
    kh_t = [bd(k_ * g_).T for k_, g_ in zip(kt, gl)]
    gmat = [_dot_nt(jnp.concatenate([a_, r_], axis=0), jnp.concatenate([b, b, k_, k_], axis=0))
            for a_, r_, b, k_ in zip(at_bd, rt_bd, bt, kt)]
    n_ab = [gm[:2 * ch, :2 * ch] * strict for gm in gmat]
    a_ak = [gm[:2 * ch, 2 * ch:] * strict for gm in gmat]
    m_rb = [gm[2 * ch:, :2 * ch] * incl for gm in gmat]
    m_rk = [gm[2 * ch:, 2 * ch:] * incl for gm in gmat]
    tinv = [eye + n for n in n_ab]
    npow = n_ab
    for step in range(1, 6):
        if step == 1:
            npow = [_dot(n, n) for n in npow]
        both = [_dot(n, jnp.concatenate([t, n], axis=1)) for n, t in zip(npow, tinv)]
        tinv = [t + b[:, :2 * ch] for t, b in zip(tinv, both)]
        npow = [b[:, 2 * ch:] for b in both]
    va = [_dot(jnp.concatenate([a_, k_, m_], axis=0), v_) for a_, k_, m_, v_ in zip(a_ak, kh_t, m_rk, v_bd)]
    wu = [_dot(t, jnp.concatenate([a_, x_[:2 * ch]], axis=1)) for t, a_, x_ in zip(tinv, at_bd, va)]
    pq = [_dot(jnp.concatenate([b, m_], axis=0), w_) for b, m_, w_ in zip(bh_t, m_rb, wu)]
    pmat = [eye * g_ + t[:2 * ch, :2 * ch] for g_, t in zip(gl, pq)]
    qmat = [t[:2 * ch, 2 * ch:] + x_[2 * ch:4 * ch] for t, x_ in zip(pq, va)]
    ry = [r_ + t[2 * ch:, :2 * ch] for r_, t in zip(rt_bd, pq)]
    y0 = [t[2 * ch:, 2 * ch:] + x_[4 * ch:] for t, x_ in zip(pq, va)]
    state = [hs_ref[q] for q in range(npair)]
    y_chunks = []
    for c in range(nchunk):
        ids = [c * npair + q for q in range(npair)]
        both = [_dot(jnp.concatenate([ry[i], pmat[i]], axis=0), st) for i, st in zip(ids, state)]
        yy = [t[:2 * ch] + y0[i] for i, t in zip(ids, both)]
        state = [t[2 * ch:] + qmat[i] for i, t in zip(ids, both)]
        y_chunks.append(jnp.concatenate([t[:ch] + t[ch:] for t in yy], axis=1))
    hs_ref[...] = jnp.stack(state, axis=0)

    y = jnp.concatenate(y_chunks, axis=0)
    inv_n = 1.0 / HEAD_DIM
    mean = _dot_split(y, eblk) * inv_n
    yc = y - mean
    var = _dot_split(yc * yc, eblk) * inv_n
    yn = yc * lax.rsqrt(var + RWKV_GN_EPS) * gng_ref[...] + gnb_ref[...]
    r = r_s[...]
    k = k_s[...]
    v = v_s[...]
    bonus = _dot_split(r * k * rk_ref[...], eblk) * v
    y_ref[0] = ((yn + bonus) * gate).astype(BF16)


def _rwkv_mixer(h, w_rkv, w_lora, mu, w0, w2, a0, a2, g2, k_k, k_a, r_k, gn_g, gn_b):
    bsz, s, d = h.shape
    g = w0.shape[0]
    tm = 256
    row = lambda t: t.astype(F32).reshape(1, -1)
    pad_rows = lambda t: jnp.zeros((LANES, g), F32).at[:t.shape[0]].set(t.astype(F32)).astype(BF16)
    mu1 = row(mu[:3 * g])
    mu2 = jnp.concatenate([
        jnp.zeros((LANES,), F32).at[:RWKV_DECAY_RANK].set(mu[3 * g:3 * g + RWKV_DECAY_RANK]),
        jnp.zeros((LANES,), F32).at[:RWKV_A_RANK].set(mu[3 * g + RWKV_DECAY_RANK:3 * g + RWKV_DECAY_RANK + RWKV_A_RANK]),
        mu[3 * g + RWKV_DECAY_RANK + RWKV_A_RANK:]]).reshape(1, -1)
    head = np.arange(g) // HEAD_DIM
    eblk = jnp.asarray(head[:, None] == head[None, :], BF16)
    args = (h, w_rkv, w_lora, mu1, mu2, row(w0), pad_rows(w2), row(a0), pad_rows(a2), g2.astype(BF16),
            row(k_k), row(k_a), row(r_k), row(gn_g), row(gn_b), eblk)
    full = lambda a: pl.BlockSpec(a.shape, lambda b, i: (0,) * a.ndim)
    return pl.pallas_call(
        _rwkv_kernel,
        grid=(bsz, s // tm),
        in_specs=[pl.BlockSpec((1, tm, d), lambda b, i: (b, i, 0))] + [full(a) for a in args[1:]],
        out_specs=pl.BlockSpec((1, tm, g), lambda b, i: (b, i, 0)),
        out_shape=jax.ShapeDtypeStruct((bsz, s, g), BF16),
        scratch_shapes=[pltpu.VMEM((SUBLANES, 3 * g), F32), pltpu.VMEM((SUBLANES, 3 * LANES), F32),
                        pltpu.VMEM((g // LANES, 2 * WKV_CHUNK, LANES), F32)]
                       + [pltpu.VMEM((tm, g), F32) for _ in range(7)],
        compiler_params=_cparams(("parallel", "arbitrary")),
        name="rwkv7_mixer",
    )(*args)


def _mixout_kernel(alpha, ya_ref, yb_ref, yc_ref, yd_ref, wo_ref, x_ref, gt_ref, lng_ref, lnb_ref,
                   sh_ref, sc_ref, wr_ref, br_ref, x1_ref, xs_ref, info_ref, seg_ref, h2s_ref, lgs_ref):
    g = ya_ref.shape[1]

    @pl.when(pl.program_id(0) == 0)
    def _():
        h2s_ref[...] = jnp.zeros_like(h2s_ref)
        lgs_ref[...] = jnp.zeros_like(lgs_ref)

    h2_prev = h2s_ref[...]
    lg_prev = lgs_ref[...]
    y = (jnp.dot(ya_ref[...], wo_ref[0:g, :], preferred_element_type=F32)
         + jnp.dot(yb_ref[...], wo_ref[g:2 * g, :], preferred_element_type=F32)
         + jnp.dot(yc_ref[...], wo_ref[2 * g:3 * g, :], preferred_element_type=F32)
         + jnp.dot(yd_ref[...], wo_ref[3 * g:, :], preferred_element_type=F32))
    _route_sort(h2_prev, lg_prev, xs_ref, info_ref, seg_ref)
    x1 = _layer_norm(alpha * x_ref[...] + (1.0 + gt_ref[0]) * y) * lng_ref[...] + lnb_ref[...]
    x1_ref[...] = x1
    h2 = _layer_norm(x1) * (1.0 + sc_ref[0]) + sh_ref[0]
    h2s_ref[...] = h2.astype(BF16)
    lgs_ref[...] = _dot(h2, wr_ref[...]) + br_ref[...]


def _mix_out(alpha, ys, w_out, x, gate, ln_g, ln_b, shift2, scale2, w_router, b_router):
    bsz, s, d = x.shape
    g = ys[0].shape[2]
    tm = MOE_TILE
    per_seq = s // tm
    nt = bsz * per_seq
    cur = lambda t: jnp.minimum(t, nt - 1)
    prev = lambda t: jnp.maximum(t - 1, 0)
    tok = lambda w: pl.BlockSpec((tm, w), lambda t: (cur(t), 0))
    per_b = pl.BlockSpec((1, 1, d), lambda t: (cur(t) // per_seq, 0, 0))
    full = lambda a: pl.BlockSpec(a.shape, lambda t: (0,) * a.ndim)
    row = lambda v: v.astype(F32).reshape(1, -1)
    args = (*[v.reshape(bsz * s, g) for v in ys], w_out, x.reshape(bsz * s, d), gate[:, None, :], row(ln_g),
            row(ln_b), shift2[:, None, :], scale2[:, None, :], w_router, b_router)
    x1, xs, info, seg = pl.pallas_call(
        functools.partial(_mixout_kernel, alpha),
        grid=(nt + 1,),
        in_specs=[tok(g)] * 4 + [full(w_out), tok(d), per_b, full(args[7]), full(args[8]), per_b, per_b,
                                 full(w_router), full(b_router)],
        out_specs=[tok(d),
                   pl.BlockSpec((MOE_TILE_ROWS, d // 2 + LANES), lambda t: (prev(t), 0)),
                   pl.BlockSpec((tm, LANES), lambda t: (prev(t), 0)),
                   pl.BlockSpec((1, SUBLANES, LANES), lambda t: (prev(t), 0, 0))],
        out_shape=[jax.ShapeDtypeStruct((bsz * s, d), F32),
                   jax.ShapeDtypeStruct((nt * MOE_TILE_ROWS, d // 2 + LANES), jnp.uint32),
                   jax.ShapeDtypeStruct((bsz * s, LANES), I32),
                   jax.ShapeDtypeStruct((nt, SUBLANES, LANES), I32)],
        scratch_shapes=[pltpu.VMEM((tm, d), BF16), pltpu.VMEM((tm, LANES), F32)],
        compiler_params=_cparams(("arbitrary",)),
        name="mix_out_route_sort",
    )(*args)
    return x1.reshape(bsz, s, d), xs, info, seg


def _route_sort(h2, lg, xs_ref, info_ref, seg_ref):
    tm = lg.shape[0]
    rt = xs_ref.shape[0]
    half = h2.shape[1] // 2
    lane = lax.broadcasted_iota(I32, (tm, LANES), 1)
    lane_f = lane.astype(F32)

    def top1(vals, mask):
        mv = jnp.where(mask, vals, -jnp.inf)
        m = jnp.max(mv, axis=-1, keepdims=True)
        idx = jnp.min(jnp.where(jnp.logical_and(mask, mv == m), lane_f, float(LANES)), axis=-1, keepdims=True)
        return m, idx.astype(I32)

    gmask = lane < N_EXPERT_GROUPS
    gm, gidx = top1(lg, gmask)
    g_val = 1.0 / jnp.sum(jnp.where(gmask, jnp.exp(lg - gm), 0.0), axis=-1, keepdims=True)
    elo = N_EXPERT_GROUPS + gidx * EXPERTS_PER_GROUP
    emask = jnp.logical_and(lane >= elo, lane < elo + EXPERTS_PER_GROUP)
    m1, i1 = top1(lg, emask)
    m2, i2 = top1(lg, jnp.logical_and(emask, lane != i1))
    e21 = jnp.exp(m2 - m1)
    w1 = g_val / (1.0 + e21)
    w2 = g_val * e21 / (1.0 + e21)
    e1 = i1 - N_EXPERT_GROUPS
    e2 = i2 - N_EXPERT_GROUPS
    oh1 = (lane == e1)
    oh2 = (lane == e2)
    ohs = jnp.where(jnp.logical_or(oh1, oh2), 1.0, 0.0)
    cnt = jnp.sum(ohs, axis=0, keepdims=True)
    units = jnp.floor((cnt + (MOE_UNIT - 1.0)) * (1.0 / MOE_UNIT))
    li = lax.broadcasted_iota(I32, (LANES, LANES), 0)
    lj = lax.broadcasted_iota(I32, (LANES, LANES), 1)
    upper = jnp.where(li < lj, 1.0, 0.0).astype(BF16)
    ustart = jnp.dot(jnp.broadcast_to(units, (SUBLANES, LANES)).astype(BF16), upper,
                     preferred_element_type=F32)[0:1, :]
    ri = lax.broadcasted_iota(I32, (tm, tm), 0)
    ci = lax.broadcasted_iota(I32, (tm, tm), 1)
    tri = jnp.where(ci < ri, 1.0, 0.0).astype(BF16)
    before = jnp.dot(tri, ohs.astype(BF16), preferred_element_type=F32)
    first = before + MOE_UNIT * ustart
    pos1 = jnp.sum(jnp.where(oh1, first, 0.0), axis=-1, keepdims=True)
    pos2 = jnp.sum(jnp.where(oh2, first, 0.0), axis=-1, keepdims=True)
    posm = jnp.where(lane == 0, pos1, jnp.where(lane == 1, pos2, -1.0))
    post = posm.T
    prow = lax.broadcasted_iota(I32, (rt, tm), 0).astype(F32)
    sel1 = prow == post[0:1, :]
    sel2 = prow == post[1:2, :]
    hb = h2.astype(BF16)
    xs = jnp.dot(jnp.where(jnp.logical_or(sel1, sel2), 1.0, 0.0).astype(BF16), hb, preferred_element_type=F32)
    bits = pltpu.bitcast(xs, jnp.uint32)
    xs_ref[:, :half] = jnp.bitwise_or(jnp.right_shift(bits[:, :half], jnp.uint32(16)), bits[:, half:])
    def terms(w):
        a = w.astype(BF16).astype(F32)
        b = (w - a).astype(BF16).astype(F32)
        return a, b, ((w - a) - b).astype(BF16).astype(F32)
    t1 = terms(w1)
    t2 = terms(w2)
    wm = jnp.zeros((tm, LANES), F32)
    for k, t in enumerate(t1 + t2):
        wm = jnp.where(lane == k, t, wm)
    wmb = wm.astype(BF16)
    s1 = jnp.dot(jnp.where(sel1, 1.0, 0.0).astype(BF16), wmb, preferred_element_type=F32)
    s2 = jnp.dot(jnp.where(sel2, 1.0, 0.0).astype(BF16), wmb, preferred_element_type=F32)
    wrow = (s1[:, 0:1] + s1[:, 1:2] + s1[:, 2:3]) + (s2[:, 3:4] + s2[:, 4:5] + s2[:, 5:6])
    mlane = lax.broadcasted_iota(I32, (rt, LANES), 1)
    wbits = pltpu.bitcast(jnp.broadcast_to(wrow, (rt, LANES)), jnp.uint32)
    xs_ref[:, half:] = jnp.where(mlane == 0, wbits, jnp.uint32(0))
    info_ref[...] = jnp.where(lane == 0, pos1, jnp.where(lane == 1, pos2, 0.0)).astype(I32)
    srow = lax.broadcasted_iota(I32, (SUBLANES, LANES), 0)
    total = jnp.sum(units, axis=-1, keepdims=True)
    seg = jnp.where(srow == 0, units, jnp.where(srow == 1, ustart, jnp.where(srow == 2, total, 0.0)))
    seg_ref[0] = seg.astype(I32)


def _worklist_kernel(units_ref, ustart_ref, uidx_ref, slot_ref, be_ref, nxt_ref, nb_ref):
    n_seg = units_ref.shape[0]
    n_tiles = n_seg // N_EXPERTS
    n_slots = uidx_ref.shape[0]
    n_blocks = be_ref.shape[0]
    tile_units = MOE_TILE_ROWS // MOE_UNIT

    def pad_slot(i, c):
        uidx_ref[i] = -1
        return c

    def unused_units(tau, c):
        last = tau * N_EXPERTS + (N_EXPERTS - 1)

        def mark_unused(u, c2):
            slot_ref[tau * tile_units + u] = -1
            return c2

        return lax.fori_loop(ustart_ref[last] + units_ref[last], tile_units, mark_unused, c)

    lax.fori_loop(0, n_tiles, unused_units, 0)

    def expert(e, blk0):
        def tile(tau, pos):
            n = units_ref[tau * N_EXPERTS + e]
            base = tau * tile_units + ustart_ref[tau * N_EXPERTS + e]

            def unit(j, p):
                uidx_ref[p] = base + j
                slot_ref[base + j] = p
                return p + 1

            return lax.fori_loop(0, n, unit, pos)

        end = lax.fori_loop(0, n_tiles, tile, blk0 * MOE_BLOCK_UNITS)
        nblk = lax.shift_right_logical(end - blk0 * MOE_BLOCK_UNITS + (MOE_BLOCK_UNITS - 1),
                                       int(math.log2(MOE_BLOCK_UNITS)))

        def mark(i, c):
            be_ref[blk0 + i] = e
            return c

        lax.fori_loop(0, nblk, mark, 0)
        lax.fori_loop(end, (blk0 + nblk) * MOE_BLOCK_UNITS, pad_slot, 0)
        return blk0 + nblk

    used = lax.fori_loop(0, N_EXPERTS, expert, 0)
    nb_ref[0] = used

    def tail(i, c):
        be_ref[i] = N_EXPERTS - 1
        nxt_ref[i] = -1
        return c

    lax.fori_loop(used, n_blocks, tail, 0)
    lax.fori_loop(used * MOE_BLOCK_UNITS, n_slots, pad_slot, 0)

    def following(i, carry):
        after_e, after_next = carry
        idx = used - 1 - i
        e = be_ref[idx]
        nxt = jnp.where(e == after_e, after_next, after_e)
        nxt_ref[idx] = nxt
        return e, nxt

    lax.fori_loop(0, used, following, (jnp.int32(-1), jnp.int32(-1)))


def _worklist(units, ustart, n_blocks):
    smem = pl.BlockSpec(memory_space=pltpu.SMEM)
    n_tiles = units.shape[0] // N_EXPERTS
    return pl.pallas_call(
        _worklist_kernel,
        in_specs=[smem, smem],
        out_specs=[smem, smem, smem, smem, smem],
        out_shape=[jax.ShapeDtypeStruct((n_blocks * MOE_BLOCK_UNITS,), I32),
                   jax.ShapeDtypeStruct((n_tiles * (MOE_TILE_ROWS // MOE_UNIT),), I32),
                   jax.ShapeDtypeStruct((n_blocks,), I32), jax.ShapeDtypeStruct((n_blocks,), I32),
                   jax.ShapeDtypeStruct((1,), I32)],
        name="moe_worklist",
    )(units, ustart)


def _expert_kernel(layer, be_ref, nxt_ref, uidx_ref, nb_ref, xs_hbm, w1_hbm, w3_hbm, w2_hbm, o_ref,
                   xbuf, gsem, w1f, w3f, w2f, wsem, wslot, w1b, w3b, w2b):
    b = pl.program_id(0)
    used = nb_ref[0]
    half = xs_hbm.shape[1] - LANES
    slot = lax.rem(b, 2)
    other = 1 - slot

    def weight_copies(e, s):
        return (pltpu.make_async_copy(w1_hbm.at[layer, e], w1f.at[s], wsem.at[s]),
                pltpu.make_async_copy(w3_hbm.at[layer, e], w3f.at[s], wsem.at[s]),
                pltpu.make_async_copy(w2_hbm.at[layer, e], w2f.at[s], wsem.at[s]))

    def gather_copy(s, j, unit):
        return pltpu.make_async_copy(xs_hbm.at[pl.ds(pl.multiple_of(unit * MOE_UNIT, MOE_UNIT), MOE_UNIT), :],
                                     xbuf.at[s, pl.ds(j * MOE_UNIT, MOE_UNIT), :], gsem.at[s])

    def gather_start(blk, s):
        for j in range(MOE_BLOCK_UNITS):
            gather_copy(s, j, jnp.maximum(uidx_ref[blk * MOE_BLOCK_UNITS + j], 0)).start(priority=j % 2)

    def gather_wait(s):
        for j in range(MOE_BLOCK_UNITS):
            gather_copy(s, j, 0).wait()

    @pl.when(b == 0)
    def _():
        gather_start(0, 0)
        wslot[0] = 1
        for cp in weight_copies(be_ref[0], 0):
            cp.start()

    @pl.when(b + 1 < used)
    def _():
        gather_start(b + 1, other)

    @pl.when(b >= used)
    def _():
        o_ref[...] = jnp.zeros_like(o_ref)

    @pl.when(b < used)
    def _():
        prev = be_ref[jnp.maximum(b - 1, 0)]

        @pl.when(jnp.logical_or(b == 0, be_ref[b] != prev))
        def _():
            s = 1 - wslot[0]
            wslot[0] = s
            for cp in weight_copies(0, s):
                cp.wait()
            w1b[...] = w1f[s].astype(BF16)
            w3b[...] = w3f[s].astype(BF16)
            w2b[...] = w2f[s].astype(BF16)

            @pl.when(nxt_ref[b] >= 0)
            def _():
                for cp in weight_copies(nxt_ref[b], 1 - s):
                    cp.start()

        gather_wait(slot)
        xw = xbuf[slot]
        x = _unpack_bf16_pairs(xw[:, :half])
        wrow = pltpu.bitcast(xw[:, half:], F32)[:, 0:1]
        a = jnp.dot(x, w1b[...], preferred_element_type=F32)
        gte = jnp.dot(x, w3b[...], preferred_element_type=F32)
        mid = (a * _sigmoid(a)) * gte
        o_ref[...] = _pack_bf16_pairs(jnp.dot(mid.astype(BF16), w2b[...], preferred_element_type=F32) * wrow)


def _expert_ffn(layer, xs, blk_expert, blk_next, unit_idx, n_used, w1, w3, w2):
    d, de = w1.shape[2], w1.shape[3]
    nb = blk_expert.shape[0]
    rows = MOE_BLOCK_UNITS * MOE_UNIT
    hbm = pl.BlockSpec(memory_space=pl.ANY)
    grid_spec = pltpu.PrefetchScalarGridSpec(
        num_scalar_prefetch=4,
        grid=(nb,),
        in_specs=[hbm, hbm, hbm, hbm],
        out_specs=pl.BlockSpec((rows, d // 2), lambda b, be, nx, ui, nu: (b, 0)),
        scratch_shapes=[pltpu.VMEM((2, rows, xs.shape[1]), jnp.uint32), pltpu.SemaphoreType.DMA((2,)),
                        pltpu.VMEM((2, d, de), F32), pltpu.VMEM((2, d, de), F32), pltpu.VMEM((2, de, d), F32),
                        pltpu.SemaphoreType.DMA((2,)), pltpu.SMEM((1,), I32),
                        pltpu.VMEM((d, de), BF16), pltpu.VMEM((d, de), BF16), pltpu.VMEM((de, d), BF16)],
    )
    return pl.pallas_call(
        functools.partial(_expert_kernel, layer),
        grid_spec=grid_spec,
        out_shape=jax.ShapeDtypeStruct((nb * rows, d // 2), jnp.uint32),
        compiler_params=_cparams(("arbitrary",)),
        name="moe_expert_ffn",
    )(blk_expert, blk_next, unit_idx, n_used, xs, w1, w3, w2)


def _combine_kernel(alpha, with_next, nu_ref, slot_ref, ys_hbm, info_ref, x_ref, gt_ref, lng_ref, lnb_ref, *rest):
    if with_next:
        sh_ref, sc_ref, o_ref, hn_ref, ybuf, sem = rest
    else:
        o_ref, ybuf, sem = rest
    i = pl.program_id(0)
    nt = pl.num_programs(0)
    tm = x_ref.shape[0]
    rt = ybuf.shape[1]
    slot = lax.rem(i, 2)
    other = 1 - slot

    def unit_copy(tile, s, j):
        src = pl.multiple_of(slot_ref[tile * (rt // MOE_UNIT) + j] * MOE_UNIT, MOE_UNIT)
        dst = pl.multiple_of(j * MOE_UNIT, MOE_UNIT)
        return pltpu.make_async_copy(ys_hbm.at[pl.ds(src, MOE_UNIT), :], ybuf.at[s, pl.ds(dst, MOE_UNIT), :], sem.at[s])

    def start(tile, s):
        def body(j, c):
            unit_copy(tile, s, j).start()
            return c
        lax.fori_loop(0, nu_ref[tile], body, 0)

    def wait(tile, s):
        def body(j, c):
            unit_copy(tile, s, j).wait()
            return c
        lax.fori_loop(0, nu_ref[tile], body, 0)

    @pl.when(i == 0)
    def _():
        ybuf[...] = jnp.zeros_like(ybuf)
        start(0, 0)

    @pl.when(i + 1 < nt)
    def _():
        start(i + 1, other)

    wait(i, slot)
    info = info_ref[...]
    col = lax.broadcasted_iota(I32, (tm, rt), 1)
    pick = jnp.where(jnp.logical_or(col == info[:, 0:1], col == info[:, 1:2]), 1.0, 0.0).astype(BF16)
    y = jnp.dot(pick, _unpack_bf16_pairs(ybuf[slot]), preferred_element_type=F32)
    x2 = _layer_norm(alpha * x_ref[...] + (1.0 + gt_ref[0]) * y) * lng_ref[...] + lnb_ref[...]
    o_ref[...] = x2
    if with_next:
        hn_ref[...] = (_layer_norm(x2) * (1.0 + sc_ref[0]) + sh_ref[0]).astype(BF16)


def _combine(alpha, ys, tile_units, unit_slot, info, x1, gate, ln_g, ln_b, seq, next_mod=None):
    t, d = x1.shape
    tm = MOE_TILE
    per_seq = seq // tm
    tok = pl.BlockSpec((tm, d), lambda i, nu, us: (i, 0))
    per_b = pl.BlockSpec((1, 1, d), lambda i, nu, us: (i // per_seq, 0, 0))
    row = pl.BlockSpec((1, d), lambda i, nu, us: (0, 0))
    with_next = next_mod is not None
    args = [tile_units, unit_slot, ys, info, x1, gate[:, None, :], ln_g.astype(F32).reshape(1, d),
            ln_b.astype(F32).reshape(1, d)]
    in_specs = [pl.BlockSpec(memory_space=pl.ANY), pl.BlockSpec((tm, LANES), lambda i, nu, us: (i, 0)),
                tok, per_b, row, row]
    out_specs, out_shape = tok, jax.ShapeDtypeStruct((t, d), F32)
    if with_next:
        args += [next_mod[0][:, None, :], next_mod[1][:, None, :]]
        in_specs += [per_b, per_b]
        out_specs, out_shape = [tok, tok], [out_shape, jax.ShapeDtypeStruct((t, d), BF16)]
    grid_spec = pltpu.PrefetchScalarGridSpec(
        num_scalar_prefetch=2,
        grid=(t // tm,),
        in_specs=in_specs,
        out_specs=out_specs,
        scratch_shapes=[pltpu.VMEM((2, MOE_TILE_ROWS, d // 2), jnp.uint32), pltpu.SemaphoreType.DMA((2,))],
    )
    return pl.pallas_call(
        functools.partial(_combine_kernel, alpha, with_next),
        grid_spec=grid_spec,
        out_shape=out_shape,
        compiler_params=_cparams(("arbitrary",)),
        name="moe_combine_ln",
    )(*args)


def _moe(layer, alpha, xs, info, seg, x1, gate, ln_g, ln_b, w1, w3, w2, next_mod):
    bsz, s, d = x1.shape
    t = bsz * s
    nt = t // MOE_TILE
    units = seg[:, 0, :N_EXPERTS].reshape(nt * N_EXPERTS)
    ustart = seg[:, 1, :N_EXPERTS].reshape(nt * N_EXPERTS)
    tile_units = seg[:, 2, 0]
    max_units = nt * (TOP_K * MOE_TILE // MOE_UNIT + N_EXPERTS * (MOE_UNIT - 1) // MOE_UNIT)
    n_blocks = max_units // MOE_BLOCK_UNITS + N_EXPERTS
    unit_idx, unit_slot, blk_expert, blk_next, n_used = _worklist(units, ustart, n_blocks)
    ys = _expert_ffn(layer, xs, blk_expert, blk_next, unit_idx, n_used, w1, w3, w2)
    out = _combine(alpha, ys, tile_units, unit_slot, info, x1.reshape(t, d), gate, ln_g, ln_b, s, next_mod)
    if next_mod is None:
        return out.reshape(bsz, s, d), None
    return out[0].reshape(bsz, s, d), out[1].reshape(bsz, s, d)


def kernel(x, c, w_ada, b_ada, ln_g, ln_b, w_in, w_out, conv_w, rwkv_mu, rwkv_w0, rwkv_w2, rwkv_a0, rwkv_a2, rwkv_g2, rwkv_kk, rwkv_ka, rwkv_rk, rwkv_gn_g, rwkv_gn_b, attn_sinks, rel_bias, s5_lambda_re, s5_lambda_im, s5_log_dt, s5_b_re, s5_b_im, s5_c_re, s5_c_im, s5_d, s5_glu_w, s5_glu_b, router_group_w, router_group_b, router_expert_w, router_expert_b, moe_w1, moe_w3, moe_w2):
    depth = w_ada.shape[0]
    d = x.shape[-1]
    g = d // 4
    alpha = (2 * depth) ** 0.25
    n_heads = g // HEAD_DIM
    att_kv = max(1, n_heads // 4) * HEAD_DIM
    rw_off = 3 * g
    lora = RWKV_DECAY_RANK + RWKV_A_RANK + RWKV_GATE_RANK
    att_off = rw_off + 3 * g + lora
    s5_off = att_off + g + 2 * att_kv

    mod = _modulation(c, w_ada, b_ada)
    for l in range(depth):
        sh1, sc1, gt1, sh2, sc2, gt2 = jnp.split(mod[l], 6, axis=-1)
        wl = w_in[l]
        w_conv = wl[:, :rw_off].astype(BF16)
        w_rkv = wl[:, rw_off:rw_off + 3 * g].astype(BF16)
        lo = rw_off + 3 * g
        zcol = lambda n: jnp.zeros((d, n), F32)
        w_lora = jnp.concatenate([
            wl[:, lo:lo + RWKV_DECAY_RANK], zcol(LANES - RWKV_DECAY_RANK),
            wl[:, lo + RWKV_DECAY_RANK:lo + RWKV_DECAY_RANK + RWKV_A_RANK], zcol(LANES - RWKV_A_RANK),
            wl[:, lo + RWKV_DECAY_RANK + RWKV_A_RANK:att_off]], axis=1).astype(BF16)
        w_q = wl[:, att_off:att_off + g].astype(BF16)
        w_kv = wl[:, att_off + g:s5_off].astype(BF16)
        w_s5 = wl[:, s5_off:].astype(BF16)

        if l == 0:
            h = _adaln(x, sh1, sc1)
        y_conv = _conv_mixer(h, w_conv, conv_w[l].astype(F32))
        y_rwkv = _rwkv_mixer(h, w_rkv, w_lora, rwkv_mu[l], rwkv_w0[l], rwkv_w2[l], rwkv_a0[l], rwkv_a2[l],
                             rwkv_g2[l], rwkv_kk[l], rwkv_ka[l], rwkv_rk[l], rwkv_gn_g[l], rwkv_gn_b[l])
        y_att = _att_mixer(h, w_q, w_kv, attn_sinks[l], rel_bias)
        y_ssm = _s5_mixer(h, w_s5, s5_lambda_re[l], s5_lambda_im[l], s5_log_dt[l], s5_b_re[l], s5_b_im[l],
                          s5_c_re[l], s5_c_im[l], s5_d[l], s5_glu_w[l], s5_glu_b[l])
        w_router = jnp.zeros((d, LANES), F32)
        w_router = w_router.at[:, :N_EXPERT_GROUPS].set(router_group_w[l])
        w_router = w_router.at[:, N_EXPERT_GROUPS:N_EXPERT_GROUPS + N_EXPERTS].set(router_expert_w[l]).astype(BF16)
        b_router = jnp.zeros((1, LANES), F32)
        b_router = b_router.at[0, :N_EXPERT_GROUPS].set(router_group_b[l])
        b_router = b_router.at[0, N_EXPERT_GROUPS:N_EXPERT_GROUPS + N_EXPERTS].set(router_expert_b[l])
        x1, xs, info, seg = _mix_out(alpha, (y_conv, y_rwkv, y_att, y_ssm), w_out[l].astype(BF16), x, gt1,
                                     ln_g[l, 0], ln_b[l, 0], sh2, sc2, w_router, b_router)
        next_mod = None
        if l + 1 < depth:
            nsh1, nsc1 = jnp.split(mod[l + 1], 6, axis=-1)[:2]
            next_mod = (nsh1, nsc1)
        x, h = _moe(l, alpha, xs, info, seg, x1, gt2, ln_g[l, 1], ln_b[l, 1], moe_w1, moe_w3, moe_w2, next_mod)
    return x
```

```python
import functools
import math

import numpy as np
import jax
import jax.numpy as jnp
from jax import lax
from jax.experimental import pallas as pl
from jax.experimental.pallas import tpu as pltpu

F32 = jnp.float32
BF16 = jnp.bfloat16
I32 = jnp.int32

HEAD_DIM = 64
CONV_WIDTH = 3
RWKV_DECAY_RANK = 96
RWKV_A_RANK = 96
RWKV_GATE_RANK = 128
RWKV_GN_EPS = 64e-5
ATT_BLOCK = 128
WINDOW = 128
N_BUCKETS = 32
NEG_INF = -1e30
S5_CH = 16
S5_STATE = 64
N_EXPERT_GROUPS = 4
EXPERTS_PER_GROUP = 8
N_EXPERTS = N_EXPERT_GROUPS * EXPERTS_PER_GROUP
TOP_K = 2
LN_EPS = 1e-5

LANES = 128
SUBLANES = 8
WKV_CHUNK = 64
MOE_TILE = 256
MOE_UNIT = SUBLANES
MOE_TILE_ROWS = 768
MOE_BLOCK_UNITS = 32
VMEM_LIMIT = 56 * 2 ** 20


def _cparams(sem, flags=None):
    return pltpu.CompilerParams(dimension_semantics=sem, vmem_limit_bytes=VMEM_LIMIT, flags=flags)


def _dot(a, b):
    return jnp.dot(a.astype(BF16), b.astype(BF16), preferred_element_type=F32)


def _dot_nt(a, b):
    return lax.dot_general(a.astype(BF16), b.astype(BF16), (((1,), (1,)), ((), ())),
                           preferred_element_type=F32)


def _dot_tn(a, b):
    return jnp.dot(a.T.astype(BF16), b.astype(BF16), preferred_element_type=F32)


def _dot_split(x, e, lhs=False):
    hi = x.astype(BF16)
    lo = (x - hi.astype(F32)).astype(BF16)
    if lhs:
        return jnp.dot(e, hi, preferred_element_type=F32) + jnp.dot(e, lo, preferred_element_type=F32)
    return jnp.dot(hi, e, preferred_element_type=F32) + jnp.dot(lo, e, preferred_element_type=F32)


def _pack_bf16_pairs(x):
    n = x.shape[1] // 2
    bits = pltpu.bitcast(x.astype(BF16).astype(F32), jnp.uint32)
    return jnp.bitwise_or(jnp.right_shift(bits[:, :n], jnp.uint32(16)), bits[:, n:])


def _unpack_bf16_pairs(word):
    lo = pltpu.bitcast(jnp.left_shift(word, jnp.uint32(16)), F32)
    hi = pltpu.bitcast(jnp.bitwise_and(word, jnp.uint32(0xFFFF0000)), F32)
    return jnp.concatenate([lo, hi], axis=1).astype(BF16)


def _sigmoid(x):
    return 1.0 / (1.0 + jnp.exp(-x))


def _layer_norm(x):
    mean = jnp.mean(x, axis=-1, keepdims=True)
    xc = x - mean
    var = jnp.mean(xc * xc, axis=-1, keepdims=True)
    return xc * lax.rsqrt(var + LN_EPS)


def _shift_rows(p, carry_row, n):
    row = lax.broadcasted_iota(I32, (p.shape[0], 1), 0)
    out = pltpu.roll(p, n, 0)
    for i in range(n):
        out = jnp.where(row == i, carry_row[SUBLANES - n + i:SUBLANES - n + i + 1, :], out)
    return out


def _mod_kernel(c_ref, w_ref, b_ref, o_ref):
    c = c_ref[...]
    a = c * _sigmoid(c)
    o_ref[0] = _dot(a, w_ref[0]) + b_ref[0]


def _modulation(c, w_ada, b_ada):
    depth, d, n = w_ada.shape
    bsz = c.shape[0]
    tn = 1536
    cp = jnp.zeros((SUBLANES, d), F32).at[:bsz].set(c)
    out = pl.pallas_call(
        _mod_kernel,
        grid=(depth, n // tn),
        in_specs=[pl.BlockSpec((SUBLANES, d), lambda l, j: (0, 0)),
                  pl.BlockSpec((1, d, tn), lambda l, j: (l, 0, j)),
                  pl.BlockSpec((1, 1, tn), lambda l, j: (l, 0, j))],
        out_specs=pl.BlockSpec((1, SUBLANES, tn), lambda l, j: (l, 0, j)),
        out_shape=jax.ShapeDtypeStruct((depth, SUBLANES, n), F32),
        compiler_params=_cparams(("parallel", "parallel")),
        name="adaln_modulation",
    )(cp, w_ada, b_ada.reshape(depth, 1, n))
    return out[:, :bsz]


def _adaln_kernel(x_ref, sh_ref, sc_ref, h_ref):
    h_ref[0] = (_layer_norm(x_ref[0]) * (1.0 + sc_ref[0]) + sh_ref[0]).astype(BF16)


def _adaln(x, shift, scale):
    bsz, s, d = x.shape
    tm = 512
    return pl.pallas_call(
        _adaln_kernel,
        grid=(bsz, s // tm),
        in_specs=[pl.BlockSpec((1, tm, d), lambda b, i: (b, i, 0)),
                  pl.BlockSpec((1, 1, d), lambda b, i: (b, 0, 0)),
                  pl.BlockSpec((1, 1, d), lambda b, i: (b, 0, 0))],
        out_specs=pl.BlockSpec((1, tm, d), lambda b, i: (b, i, 0)),
        out_shape=jax.ShapeDtypeStruct((bsz, s, d), BF16),
        compiler_params=_cparams(("parallel", "parallel")),
        name="adaln_input",
    )(x, shift[:, None, :], scale[:, None, :])


def _t5_bucket(rel):
    n = jnp.maximum(rel, 0)
    max_exact = N_BUCKETS // 2
    n_f = jnp.maximum(n, 1).astype(F32)
    large = max_exact + (jnp.log(n_f / max_exact) / math.log(WINDOW / max_exact)
                         * (N_BUCKETS - max_exact)).astype(I32)
    return jnp.where(n < max_exact, n, jnp.minimum(large, N_BUCKETS - 1))


def _att_kernel(sink_ref, h_ref, wq_ref, wkv_ref, bias_ref, y_ref, kvc_ref):
    tm = h_ref.shape[1]
    n_heads = bias_ref.shape[0]
    kvw = wkv_ref.shape[1] // 2
    n_kv = kvw // HEAD_DIM
    rep = n_heads // n_kv
    blk = ATT_BLOCK
    first_tile = pl.program_id(1) == 0

    @pl.when(first_tile)
    def _():
        kvc_ref[...] = jnp.zeros_like(kvc_ref)

    x = h_ref[0]
    q = _dot(x, wq_ref[...]) * (HEAD_DIM ** -0.5)
    kv = _dot(x, wkv_ref[...])
    kvext = jnp.concatenate([kvc_ref[...], kv], axis=0)
    kvc_ref[...] = kv[tm - blk:, :]
    col = lax.broadcasted_iota(I32, (blk, 2 * blk), 1)
    qb16 = q.astype(BF16)
    kv16 = kvext.astype(BF16)

    def scores(j):
        qb = qb16[j * blk:(j + 1) * blk]
        kw = kv16[j * blk:j * blk + 2 * blk, :kvw]
        kgs = [kw[:, gi * HEAD_DIM:(gi + 1) * HEAD_DIM] for gi in range(n_kv)]
        scs = [_dot_nt(qb[:, hh * HEAD_DIM:(hh + 1) * HEAD_DIM], kgs[hh // rep]) + bias_ref[hh]
               for hh in range(n_heads)]
        if j == 0:
            scs = [jnp.where(jnp.logical_and(first_tile, col < blk), NEG_INF, sc) for sc in scs]
        return scs

    def probs(scs):
        out = []
        for hh, sc in enumerate(scs):
            sink = sink_ref[hh]
            m = jnp.maximum(jnp.max(sc, axis=-1, keepdims=True), sink)
            e = jnp.exp(sc - m)
            den = jnp.sum(e, axis=-1, keepdims=True) + jnp.exp(sink - m)
            out.append((e / den).astype(BF16))
        return out

    def values(j, ps):
        vw = kv16[j * blk:j * blk + 2 * blk, kvw:]
        vgs = [vw[:, gi * HEAD_DIM:(gi + 1) * HEAD_DIM] for gi in range(n_kv)]
        outs = [jnp.dot(p, vgs[hh // rep], preferred_element_type=F32) for hh, p in enumerate(ps)]
        y_ref[0, j * blk:(j + 1) * blk, :] = jnp.concatenate(outs, axis=1).astype(BF16)

    nblk = tm // blk
    pending = scores(0)
    for j in range(nblk):
        nxt = scores(j + 1) if j + 1 < nblk else None
        values(j, probs(pending))
        pending = nxt


def _att_mixer(h, wq, wkv, sinks, rel_bias):
    bsz, s, d = h.shape
    n_heads = sinks.shape[0]
    tm = 512
    qi = jnp.arange(ATT_BLOCK)[:, None]
    kj = jnp.arange(2 * ATT_BLOCK)[None, :]
    rel = qi + ATT_BLOCK - kj
    valid = (rel >= 0) & (rel < WINDOW)
    onehot = (_t5_bucket(rel)[..., None] == jnp.arange(N_BUCKETS)).astype(F32)
    bias = jnp.einsum('qkb,bh->hqk', onehot, rel_bias.astype(F32), precision=lax.Precision.HIGHEST)
    bias = jnp.where(valid[None], bias, NEG_INF)
    return pl.pallas_call(
        _att_kernel,
        grid=(bsz, s // tm),
        in_specs=[pl.BlockSpec(memory_space=pltpu.SMEM),
                  pl.BlockSpec((1, tm, d), lambda b, i: (b, i, 0)),
                  pl.BlockSpec(wq.shape, lambda b, i: (0, 0)),
                  pl.BlockSpec(wkv.shape, lambda b, i: (0, 0)),
                  pl.BlockSpec(bias.shape, lambda b, i: (0, 0, 0))],
        out_specs=pl.BlockSpec((1, tm, wq.shape[1]), lambda b, i: (b, i, 0)),
        out_shape=jax.ShapeDtypeStruct((bsz, s, wq.shape[1]), BF16),
        scratch_shapes=[pltpu.VMEM((ATT_BLOCK, wkv.shape[1]), F32)],
        compiler_params=_cparams(("parallel", "arbitrary")),
        name="swa_mixer",
    )(sinks.astype(F32), h, wq, wkv, bias)


S5_GROUPS_PER_BLOCK = LANES // S5_CH


def _s5_tables(lam_re, lam_im, log_dt, b_re, b_im, c_re, c_im):
    n_groups, p = lam_re.shape
    lr, li = lam_re.astype(F32), lam_im.astype(F32)
    delta = jnp.exp(log_dt.astype(F32))[:, None]
    mag = jnp.exp(lr * delta)
    ab_re, ab_im = mag * jnp.cos(li * delta), mag * jnp.sin(li * delta)
    den = lr * lr + li * li
    z_re = ((ab_re - 1.0) * lr + ab_im * li) / den
    z_im = (ab_im * lr - (ab_re - 1.0) * li) / den
    br, bi = b_re.astype(F32), b_im.astype(F32)
    bb_re = z_re[..., None] * br - z_im[..., None] * bi
    bb_im = z_re[..., None] * bi + z_im[..., None] * br
    nblk = n_groups // S5_GROUPS_PER_BLOCK
    eye = jnp.eye(S5_GROUPS_PER_BLOCK, dtype=F32)

    def in_blocks(bb):
        bb = bb.reshape(nblk, S5_GROUPS_PER_BLOCK, p, S5_CH)
        return jnp.einsum('qgpc,gh->qgchp', bb, eye).reshape(nblk, LANES, S5_GROUPS_PER_BLOCK * p)

    def out_blocks(cc):
        cc = cc.astype(F32).reshape(nblk, S5_GROUPS_PER_BLOCK, S5_CH, p)
        return jnp.einsum('qgcp,gh->qgphc', cc, eye).reshape(nblk, S5_GROUPS_PER_BLOCK * p, LANES)

    def power(m):
        mg = jnp.exp(m * lr * delta)
        return (mg * jnp.cos(m * li * delta)).reshape(1, -1), (mg * jnp.sin(m * li * delta)).reshape(1, -1)

    row = jnp.arange(SUBLANES, dtype=F32)[:, None]
    tabs = []
    for sft in (1, 2, 4):
        pr, pi = power(float(sft))
        keep = row >= sft
        tabs += [jnp.where(keep, pr, 0.0), jnp.where(keep, pi, 0.0)]
    n_state = n_groups * p
    lrd = (lr * delta).reshape(1, n_state)
    lid = (li * delta).reshape(1, n_state)
    mg = jnp.exp((row + 1.0) * lrd)
    tabs += [mg * jnp.cos((row + 1.0) * lid), mg * jnp.sin((row + 1.0) * lid)]
    tables = jnp.stack(tabs, axis=0)
    return (in_blocks(bb_re).astype(BF16), in_blocks(bb_im).astype(BF16),
            out_blocks(c_re).astype(BF16), out_blocks(c_im).astype(BF16), tables)


def _s5_kernel(h_ref, w_ref, bre_ref, bim_ref, cre_ref, cim_ref, tab_ref, d_ref, gw_ref, gb_ref,
               wc_ref, cw_ref, y_ref, yc_ref, xr_ref, xi_ref, cr_ref, ci_ref, cc_ref):
    nseq = h_ref.shape[0]
    tm = h_ref.shape[1]
    nblk = bre_ref.shape[0]
    sw = bre_ref.shape[2]

    @pl.when(pl.program_id(0) == 0)
    def _():
        cr_ref[...] = jnp.zeros_like(cr_ref)
        ci_ref[...] = jnp.zeros_like(ci_ref)
        cc_ref[...] = jnp.zeros_like(cc_ref)

    def project(b):
        u = _dot(h_ref[b], w_ref[...])
        ub = u.astype(BF16)
        for q in range(nblk):
            uq = ub[:, q * LANES:(q + 1) * LANES]
            xr_ref[b, :, q * sw:(q + 1) * sw] = jnp.dot(uq, bre_ref[q], preferred_element_type=F32)
            xi_ref[b, :, q * sw:(q + 1) * sw] = jnp.dot(uq, bim_ref[q], preferred_element_type=F32)
        return u

    def scan(b):
        cr = cr_ref[b, 0:1, :]
        ci = ci_ref[b, 0:1, :]
        for i in range(tm // SUBLANES):
            rows = slice(i * SUBLANES, (i + 1) * SUBLANES)
            xr = xr_ref[b, rows, :]
            xi = xi_ref[b, rows, :]
            for k, sft in enumerate((1, 2, 4)):
                mr = tab_ref[2 * k]
                mi = tab_ref[2 * k + 1]
                sr = pltpu.roll(xr, sft, 0)
                si = pltpu.roll(xi, sft, 0)
                xr, xi = xr + mr * sr - mi * si, xi + mr * si + mi * sr
            pr = tab_ref[6]
            pi = tab_ref[7]
            xr, xi = xr + pr * cr - pi * ci, xi + pr * ci + pi * cr
            xr_ref[b, rows, :] = xr
            xi_ref[b, rows, :] = xi
            cr = xr[SUBLANES - 1:SUBLANES, :]
            ci = xi[SUBLANES - 1:SUBLANES, :]
        cr_ref[b, 0:1, :] = cr
        ci_ref[b, 0:1, :] = ci

    def readout(b, u):
        ys = []
        for q in range(nblk):
            xr = xr_ref[b, :, q * sw:(q + 1) * sw].astype(BF16)
            xi = xi_ref[b, :, q * sw:(q + 1) * sw].astype(BF16)
            ys.append(jnp.dot(xr, cre_ref[q], preferred_element_type=F32)
                      - jnp.dot(xi, cim_ref[q], preferred_element_type=F32))
        y = jnp.concatenate(ys, axis=1) + d_ref[...] * u
        y = 0.5 * y * (1.0 + jnp.tanh(math.sqrt(2.0 / math.pi) * (y + 0.044715 * (y * y * y))))
        y_ref[b] = (y * _sigmoid(_dot(y, gw_ref[...]) + gb_ref[...])).astype(BF16)

    def conv(b):
        g = yc_ref.shape[2]
        p = _dot(h_ref[b], wc_ref[...])
        b_gate, c_gate, hh = p[:, :g], p[:, g:2 * g], p[:, 2 * g:]
        z = c_gate * hh
        carry = cc_ref[b]
        cw = cw_ref[...]
        out = cw[0:1] * _shift_rows(z, carry, 2) + cw[1:2] * _shift_rows(z, carry, 1) + cw[2:3] * z
        yc_ref[b] = (b_gate * out).astype(BF16)
        cc_ref[b] = z[tm - SUBLANES:, :]

    us = [project(b) for b in range(nseq)]
    for b in range(nseq):
        conv(b)
    for b in range(nseq):
        scan(b)
        readout(b, us[b])


def _s5_conv_mixers(h, w, lam_re, lam_im, log_dt, b_re, b_im, c_re, c_im, d_skip, glu_w, glu_b, w_conv, conv_w):
    bsz, s, d = h.shape
    g = w.shape[1]
    tm = 256
    bre, bim, cre, cim, tables = _s5_tables(lam_re, lam_im, log_dt, b_re, b_im, c_re, c_im)
    n_state = tables.shape[2]
    full = lambda a: pl.BlockSpec(a.shape, lambda i: (0,) * a.ndim)
    dvec = d_skip.astype(F32).reshape(1, g)
    gw = glu_w.astype(BF16)
    gb = glu_b.astype(F32).reshape(1, g)
    return pl.pallas_call(
        _s5_kernel,
        grid=(s // tm,),
        in_specs=[pl.BlockSpec((bsz, tm, d), lambda i: (0, i, 0)),
                  full(w), full(bre), full(bim), full(cre), full(cim), full(tables),
                  full(dvec), full(gw), full(gb), full(w_conv), full(conv_w)],
        out_specs=[pl.BlockSpec((bsz, tm, g), lambda i: (0, i, 0)),
                   pl.BlockSpec((bsz, tm, g), lambda i: (0, i, 0))],
        out_shape=[jax.ShapeDtypeStruct((bsz, s, g), BF16), jax.ShapeDtypeStruct((bsz, s, g), BF16)],
        scratch_shapes=[pltpu.VMEM((bsz, tm, n_state), F32), pltpu.VMEM((bsz, tm, n_state), F32),
                        pltpu.VMEM((bsz, SUBLANES, n_state), F32), pltpu.VMEM((bsz, SUBLANES, n_state), F32),
                        pltpu.VMEM((bsz, SUBLANES, g), F32)],
        compiler_params=_cparams(("arbitrary",)),
        name="s5_conv_mixers",
    )(h, w, bre, bim, cre, cim, tables, dvec, gw, gb, w_conv, conv_w)


def _rwkv_kernel(h_ref, wrkv_ref, wlo_ref, mu1_ref, mu2_ref, w0_ref, w2_ref, a0_ref, a2_ref, g2_ref,
                 kk_ref, ka_ref, rk_ref, gng_ref, gnb_ref, eblk_ref,
                 y_ref,
                 cp_ref, cl_ref, hs_ref, r_s, k_s, v_s, a_s, b_s, ld_s, y_s):
    tm = h_ref.shape[1]
    g = y_ref.shape[2]
    npair = g // LANES
    ch = WKV_CHUNK

    @pl.when(pl.program_id(1) == 0)
    def _():
        cp_ref[...] = jnp.zeros_like(cp_ref)
        cl_ref[...] = jnp.zeros_like(cl_ref)
        hs_ref[...] = jnp.zeros_like(hs_ref)

    x = h_ref[0]
    p = _dot(x, wrkv_ref[...])
    plo = _dot(x, wlo_ref[...])
    pprev = _shift_rows(p, cp_ref[...], 1)
    lprev = _shift_rows(plo, cl_ref[...], 1)
    cp_ref[...] = p[tm - SUBLANES:, :]
    cl_ref[...] = plo[tm - SUBLANES:, :]
    p = p + (pprev - p) * mu1_ref[...]
    plo = plo + (lprev - plo) * mu2_ref[...]
    r, k, v = p[:, :g], p[:, g:2 * g], p[:, 2 * g:]
    w_lo, a_lo, g_lo = plo[:, :LANES], plo[:, LANES:2 * LANES], plo[:, 2 * LANES:]
    wraw = w0_ref[...] + _dot(jnp.tanh(w_lo), w2_ref[...])
    nz = -wraw
    softplus = jnp.maximum(nz, 0.0) + jnp.log(1.0 + jnp.exp(-jnp.abs(nz)))
    w = -softplus - 0.5
    ld_s[...] = -jnp.exp(w)
    a = _sigmoid(a0_ref[...] + _dot(a_lo, a2_ref[...]))
    gate = _dot(_sigmoid(g_lo), g2_ref[...])
    eblk = eblk_ref[...]
    kk = k * kk_ref[...]
    kk = kk / jnp.maximum(jnp.sqrt(_dot_split(kk * kk, eblk)), 1e-12)
    k = k * (1.0 + (a - 1.0) * ka_ref[...])
    r_s[...] = r
    k_s[...] = k
    v_s[...] = v
    a_s[...] = -kk
    b_s[...] = kk * a

    lane = lax.broadcasted_iota(I32, (1, LANES), 1)
    m0 = (lane < HEAD_DIM).astype(F32)
    m1 = 1.0 - m0
    ri = lax.broadcasted_iota(I32, (2 * ch, 2 * ch), 0)
    ci = lax.broadcasted_iota(I32, (2 * ch, 2 * ch), 1)
    same = (ri < ch) == (ci < ch)
    rloc = jnp.bitwise_and(ri, ch - 1)
    cloc = jnp.bitwise_and(ci, ch - 1)
    strict = jnp.where(jnp.logical_and(same, cloc < rloc), 1.0, 0.0)
    incl = jnp.where(jnp.logical_and(same, cloc <= rloc), 1.0, 0.0)
    eye = jnp.where(ri == ci, 1.0, 0.0)
    tri = jnp.where(lax.broadcasted_iota(I32, (ch, ch), 1) <= lax.broadcasted_iota(I32, (ch, ch), 0),
                    1.0, 0.0).astype(BF16)

    def bd(t):
        return jnp.concatenate([t * m0, t * m1], axis=0)

    nchunk = tm // ch
    per_chunk = []
    for c in range(nchunk):
        rows = slice(c * ch, (c + 1) * ch)
        ld = ld_s[rows, :]
        ld_hi = ld.astype(BF16)
        ld_lo = (ld - ld_hi.astype(F32)).astype(BF16)
        cum = (jnp.dot(tri, ld_hi, preferred_element_type=F32)
               + jnp.dot(tri, ld_lo, preferred_element_type=F32))
        gam = jnp.exp(cum)
        ginv = jnp.exp(-cum)
        per_chunk.append(dict(at=a_s[rows, :] * jnp.exp(cum - ld), rt=r_s[rows, :] * gam,
                              bt=b_s[rows, :] * ginv, kt=k_s[rows, :] * ginv, v=v_s[rows, :],
                              gl=gam[ch - 1:ch, :]))
    inst = [(c, q) for c in range(nchunk) for q in range(npair)]

    def part(name):
        return [per_chunk[c][name][:, q * LANES:(q + 1) * LANES] for c, q in inst]

    bt, kt, gl = part("bt"), part("kt"), part("gl")
    at_bd = [bd(t) for t in part("at")]
    rt_bd = [bd(t) for t in part("rt")]
    v_bd = [bd(t) for t in part("v")]
    bh_t = [bd(b * g_).T for b, g_ in zip(bt, gl)]
    kh_t = [bd(k_ * g_).T for k_, g_ in zip(kt, gl)]
    gmat = [_dot_nt(jnp.concatenate([a_, r_], axis=0), jnp.concatenate([b, b, k_, k_], axis=0))
            for a_, r_, b, k_ in zip(at_bd, rt_bd, bt, kt)]
    n_ab = [gm[:2 * ch, :2 * ch] * strict for gm in gmat]
    a_ak = [gm[:2 * ch, 2 * ch:] * strict for gm in gmat]
    m_rb = [gm[2 * ch:, :2 * ch] * incl for gm in gmat]
    m_rk = [gm[2 * ch:, 2 * ch:] * incl for gm in gmat]
    tinv = [eye + n for n in n_ab]
    npow = n_ab
    for step in range(1, 6):
        if step == 1:
            npow = [_dot(n, n) for n in npow]
        both = [_dot(n, jnp.concatenate([t, n], axis=1)) for n, t in zip(npow, tinv)]
        tinv = [t + b[:, :2 * ch] for t, b in zip(tinv, both)]
        npow = [b[:, 2 * ch:] for b in both]
    va = [_dot(jnp.concatenate([a_, k_, m_], axis=0), v_) for a_, k_, m_, v_ in zip(a_ak, kh_t, m_rk, v_bd)]
    wu = [_dot(t, jnp.concatenate([a_, x_[:2 * ch]], axis=1)) for t, a_, x_ in zip(tinv, at_bd, va)]
    pq = [_dot(jnp.concatenate([b, m_], axis=0), w_) for b, m_, w_ in zip(bh_t, m_rb, wu)]
    pmat = [eye * g_ + t[:2 * ch, :2 * ch] for g_, t in zip(gl, pq)]
    qmat = [t[:2 * ch, 2 * ch:] + x_[2 * ch:4 * ch] for t, x_ in zip(pq, va)]
    ry = [r_ + t[2 * ch:, :2 * ch] for r_, t in zip(rt_bd, pq)]
    y0 = [t[2 * ch:, 2 * ch:] + x_[4 * ch:] for t, x_ in zip(pq, va)]
    state = [hs_ref[q] for q in range(npair)]
    y_chunks = []
    for c in range(nchunk):
        ids = [c * npair + q for q in range(npair)]
        both = [_dot(jnp.concatenate([ry[i], pmat[i]], axis=0), st) for i, st in zip(ids, state)]
        yy = [t[:2 * ch] + y0[i] for i, t in zip(ids, both)]
        state = [t[2 * ch:] + qmat[i] for i, t in zip(ids, both)]
        y_chunks.append(jnp.concatenate([t[:ch] + t[ch:] for t in yy], axis=1))
    hs_ref[...] = jnp.stack(state, axis=0)

    y = jnp.concatenate(y_chunks, axis=0)
    inv_n = 1.0 / HEAD_DIM
    mean = _dot_split(y, eblk) * inv_n
    yc = y - mean
    var = _dot_split(yc * yc, eblk) * inv_n
    yn = yc * lax.rsqrt(var + RWKV_GN_EPS) * gng_ref[...] + gnb_ref[...]
    r = r_s[...]
    k = k_s[...]
    v = v_s[...]
    bonus = _dot_split(r * k * rk_ref[...], eblk) * v
    y_ref[0] = ((yn + bonus) * gate).astype(BF16)


def _rwkv_mixer(h, w_rkv, w_lora, mu, w0, w2, a0, a2, g2, k_k, k_a, r_k, gn_g, gn_b):
    bsz, s, d = h.shape
    g = w0.shape[0]
    tm = 256
    row = lambda t: t.astype(F32).reshape(1, -1)
    pad_rows = lambda t: jnp.zeros((LANES, g), F32).at[:t.shape[0]].set(t.astype(F32)).astype(BF16)
    mu1 = row(mu[:3 * g])
    mu2 = jnp.concatenate([
        jnp.zeros((LANES,), F32).at[:RWKV_DECAY_RANK].set(mu[3 * g:3 * g + RWKV_DECAY_RANK]),
        jnp.zeros((LANES,), F32).at[:RWKV_A_RANK].set(mu[3 * g + RWKV_DECAY_RANK:3 * g + RWKV_DECAY_RANK + RWKV_A_RANK]),
        mu[3 * g + RWKV_DECAY_RANK + RWKV_A_RANK:]]).reshape(1, -1)
    head = np.arange(g) // HEAD_DIM
    eblk = jnp.asarray(head[:, None] == head[None, :], BF16)
    args = (h, w_rkv, w_lora, mu1, mu2, row(w0), pad_rows(w2), row(a0), pad_rows(a2), g2.astype(BF16),
            row(k_k), row(k_a), row(r_k), row(gn_g), row(gn_b), eblk)
    full = lambda a: pl.BlockSpec(a.shape, lambda b, i: (0,) * a.ndim)
    return pl.pallas_call(
        _rwkv_kernel,
        grid=(bsz, s // tm),
        in_specs=[pl.BlockSpec((1, tm, d), lambda b, i: (b, i, 0))] + [full(a) for a in args[1:]],
        out_specs=pl.BlockSpec((1, tm, g), lambda b, i: (b, i, 0)),
        out_shape=jax.ShapeDtypeStruct((bsz, s, g), BF16),
        scratch_shapes=[pltpu.VMEM((SUBLANES, 3 * g), F32), pltpu.VMEM((SUBLANES, 3 * LANES), F32),
                        pltpu.VMEM((g // LANES, 2 * WKV_CHUNK, LANES), F32)]
                       + [pltpu.VMEM((tm, g), F32) for _ in range(7)],
        compiler_params=_cparams(("parallel", "arbitrary")),
        name="rwkv7_mixer",
    )(*args)


def _mixout_kernel(alpha, ya_ref, yb_ref, yc_ref, yd_ref, wo_ref, x_ref, gt_ref, lng_ref, lnb_ref,
                   sh_ref, sc_ref, wr_ref, br_ref, x1_ref, xs_ref, info_ref, seg_ref, h2s_ref, lgs_ref):
    g = ya_ref.shape[1]

    @pl.when(pl.program_id(0) == 0)
    def _():
        h2s_ref[...] = jnp.zeros_like(h2s_ref)
        lgs_ref[...] = jnp.zeros_like(lgs_ref)

    h2_prev = h2s_ref[...]
    lg_prev = lgs_ref[...]
    y = (jnp.dot(ya_ref[...], wo_ref[0:g, :], preferred_element_type=F32)
         + jnp.dot(yb_ref[...], wo_ref[g:2 * g, :], preferred_element_type=F32)
         + jnp.dot(yc_ref[...], wo_ref[2 * g:3 * g, :], preferred_element_type=F32)
         + jnp.dot(yd_ref[...], wo_ref[3 * g:, :], preferred_element_type=F32))
    _route_sort(h2_prev, lg_prev, xs_ref, info_ref, seg_ref)
    x1 = _layer_norm(alpha * x_ref[...] + (1.0 + gt_ref[0]) * y) * lng_ref[...] + lnb_ref[...]
    x1_ref[...] = x1
    h2 = _layer_norm(x1) * (1.0 + sc_ref[0]) + sh_ref[0]
    h2s_ref[...] = h2.astype(BF16)
    lgs_ref[...] = _dot(h2, wr_ref[...]) + br_ref[...]


def _mix_out(alpha, ys, w_out, x, gate, ln_g, ln_b, shift2, scale2, w_router, b_router):
    bsz, s, d = x.shape
    g = ys[0].shape[2]
    tm = MOE_TILE
    per_seq = s // tm
    nt = bsz * per_seq
    cur = lambda t: jnp.minimum(t, nt - 1)
    prev = lambda t: jnp.maximum(t - 1, 0)
    tok = lambda w: pl.BlockSpec((tm, w), lambda t: (cur(t), 0))
    per_b = pl.BlockSpec((1, 1, d), lambda t: (cur(t) // per_seq, 0, 0))
    full = lambda a: pl.BlockSpec(a.shape, lambda t: (0,) * a.ndim)
    row = lambda v: v.astype(F32).reshape(1, -1)
    args = (*[v.reshape(bsz * s, g) for v in ys], w_out, x.reshape(bsz * s, d), gate[:, None, :], row(ln_g),
            row(ln_b), shift2[:, None, :], scale2[:, None, :], w_router, b_router)
    x1, xs, info, seg = pl.pallas_call(
        functools.partial(_mixout_kernel, alpha),
        grid=(nt + 1,),
        in_specs=[tok(g)] * 4 + [full(w_out), tok(d), per_b, full(args[7]), full(args[8]), per_b, per_b,
                                 full(w_router), full(b_router)],
        out_specs=[tok(d),
                   pl.BlockSpec((MOE_TILE_ROWS, d // 2 + LANES), lambda t: (prev(t), 0)),
                   pl.BlockSpec((tm, LANES), lambda t: (prev(t), 0)),
                   pl.BlockSpec((1, SUBLANES, LANES), lambda t: (prev(t), 0, 0))],
        out_shape=[jax.ShapeDtypeStruct((bsz * s, d), F32),
                   jax.ShapeDtypeStruct((nt * MOE_TILE_ROWS, d // 2 + LANES), jnp.uint32),
                   jax.ShapeDtypeStruct((bsz * s, LANES), I32),
                   jax.ShapeDtypeStruct((nt, SUBLANES, LANES), I32)],
        scratch_shapes=[pltpu.VMEM((tm, d), BF16), pltpu.VMEM((tm, LANES), F32)],
        compiler_params=_cparams(("arbitrary",)),
        name="mix_out_route_sort",
    )(*args)
    return x1.reshape(bsz, s, d), xs, info, seg


def _route_sort(h2, lg, xs_ref, info_ref, seg_ref):
    tm = lg.shape[0]
    rt = xs_ref.shape[0]
    half = h2.shape[1] // 2
    lane = lax.broadcasted_iota(I32, (tm, LANES), 1)
    lane_f = lane.astype(F32)

    def top1(vals, mask):
        mv = jnp.where(mask, vals, -jnp.inf)
        m = jnp.max(mv, axis=-1, keepdims=True)
        idx = jnp.min(jnp.where(jnp.logical_and(mask, mv == m), lane_f, float(LANES)), axis=-1, keepdims=True)
        return m, idx.astype(I32)

    gmask = lane < N_EXPERT_GROUPS
    gm, gidx = top1(lg, gmask)
    g_val = 1.0 / jnp.sum(jnp.where(gmask, jnp.exp(lg - gm), 0.0), axis=-1, keepdims=True)
    elo = N_EXPERT_GROUPS + gidx * EXPERTS_PER_GROUP
    emask = jnp.logical_and(lane >= elo, lane < elo + EXPERTS_PER_GROUP)
    m1, i1 = top1(lg, emask)
    m2, i2 = top1(lg, jnp.logical_and(emask, lane != i1))
    e21 = jnp.exp(m2 - m1)
    w1 = g_val / (1.0 + e21)
    w2 = g_val * e21 / (1.0 + e21)
    e1 = i1 - N_EXPERT_GROUPS
    e2 = i2 - N_EXPERT_GROUPS
    oh1 = (lane == e1)
    oh2 = (lane == e2)
    ohs = jnp.where(jnp.logical_or(oh1, oh2), 1.0, 0.0)
    cnt = jnp.sum(ohs, axis=0, keepdims=True)
    units = jnp.floor((cnt + (MOE_UNIT - 1.0)) * (1.0 / MOE_UNIT))
    li = lax.broadcasted_iota(I32, (LANES, LANES), 0)
    lj = lax.broadcasted_iota(I32, (LANES, LANES), 1)
    upper = jnp.where(li < lj, 1.0, 0.0).astype(BF16)
    ustart = jnp.dot(jnp.broadcast_to(units, (SUBLANES, LANES)).astype(BF16), upper,
                     preferred_element_type=F32)[0:1, :]
    ri = lax.broadcasted_iota(I32, (tm, tm), 0)
    ci = lax.broadcasted_iota(I32, (tm, tm), 1)
    tri = jnp.where(ci < ri, 1.0, 0.0).astype(BF16)
    before = jnp.dot(tri, ohs.astype(BF16), preferred_element_type=F32)
    first = before + MOE_UNIT * ustart
    pos1 = jnp.sum(jnp.where(oh1, first, 0.0), axis=-1, keepdims=True)
    pos2 = jnp.sum(jnp.where(oh2, first, 0.0), axis=-1, keepdims=True)
    posm = jnp.where(lane == 0, pos1, jnp.where(lane == 1, pos2, -1.0))
    post = posm.T
    prow = lax.broadcasted_iota(I32, (rt, tm), 0).astype(F32)
    sel1 = prow == post[0:1, :]
    sel2 = prow == post[1:2, :]
    hb = h2.astype(BF16)
    xs = jnp.dot(jnp.where(jnp.logical_or(sel1, sel2), 1.0, 0.0).astype(BF16), hb, preferred_element_type=F32)
    bits = pltpu.bitcast(xs, jnp.uint32)
    xs_ref[:, :half] = jnp.bitwise_or(jnp.right_shift(bits[:, :half], jnp.uint32(16)), bits[:, half:])
    def terms(w):
        a = w.astype(BF16).astype(F32)
        b = (w - a).astype(BF16).astype(F32)
        return a, b, ((w - a) - b).astype(BF16).astype(F32)
    t1 = terms(w1)
    t2 = terms(w2)
    wm = jnp.zeros((tm, LANES), F32)
    for k, t in enumerate(t1 + t2):
        wm = jnp.where(lane == k, t, wm)
    wmb = wm.astype(BF16)
    s1 = jnp.dot(jnp.where(sel1, 1.0, 0.0).astype(BF16), wmb, preferred_element_type=F32)
    s2 = jnp.dot(jnp.where(sel2, 1.0, 0.0).astype(BF16), wmb, preferred_element_type=F32)
    wrow = (s1[:, 0:1] + s1[:, 1:2] + s1[:, 2:3]) + (s2[:, 3:4] + s2[:, 4:5] + s2[:, 5:6])
    mlane = lax.broadcasted_iota(I32, (rt, LANES), 1)
    wbits = pltpu.bitcast(jnp.broadcast_to(wrow, (rt, LANES)), jnp.uint32)
    xs_ref[:, half:] = jnp.where(mlane == 0, wbits, jnp.uint32(0))
    info_ref[...] = jnp.where(lane == 0, pos1, jnp.where(lane == 1, pos2, 0.0)).astype(I32)
    srow = lax.broadcasted_iota(I32, (SUBLANES, LANES), 0)
    total = jnp.sum(units, axis=-1, keepdims=True)
    seg = jnp.where(srow == 0, units, jnp.where(srow == 1, ustart, jnp.where(srow == 2, total, 0.0)))
    seg_ref[0] = seg.astype(I32)


def _worklist_kernel(units_ref, ustart_ref, uidx_ref, slot_ref, be_ref, nxt_ref, nb_ref):
    n_seg = units_ref.shape[0]
    n_tiles = n_seg // N_EXPERTS
    n_slots = uidx_ref.shape[0]
    n_blocks = be_ref.shape[0]
    tile_units = MOE_TILE_ROWS // MOE_UNIT

    def pad_slot(i, c):
        uidx_ref[i] = -1
        return c

    def unused_units(tau, c):
        last = tau * N_EXPERTS + (N_EXPERTS - 1)

        def mark_unused(u, c2):
            slot_ref[tau * tile_units + u] = -1
            return c2

        return lax.fori_loop(ustart_ref[last] + units_ref[last], tile_units, mark_unused, c)

    lax.fori_loop(0, n_tiles, unused_units, 0)

    def expert(e, blk0):
        def tile(tau, pos):
            n = units_ref[tau * N_EXPERTS + e]
            base = tau * tile_units + ustart_ref[tau * N_EXPERTS + e]

            def unit(j, p):
                uidx_ref[p] = base + j
                slot_ref[base + j] = p
                return p + 1

            return lax.fori_loop(0, n, unit, pos)

        end = lax.fori_loop(0, n_tiles, tile, blk0 * MOE_BLOCK_UNITS)
        nblk = lax.shift_right_logical(end - blk0 * MOE_BLOCK_UNITS + (MOE_BLOCK_UNITS - 1),
                                       int(math.log2(MOE_BLOCK_UNITS)))

        def mark(i, c):
            be_ref[blk0 + i] = e
            return c

        lax.fori_loop(0, nblk, mark, 0)
        lax.fori_loop(end, (blk0 + nblk) * MOE_BLOCK_UNITS, pad_slot, 0)
        return blk0 + nblk

    used = lax.fori_loop(0, N_EXPERTS, expert, 0)
    nb_ref[0] = used

    def tail(i, c):
        be_ref[i] = N_EXPERTS - 1
        nxt_ref[i] = -1
        return c

    lax.fori_loop(used, n_blocks, tail, 0)
    lax.fori_loop(used * MOE_BLOCK_UNITS, n_slots, pad_slot, 0)

    def following(i, carry):
        after_e, after_next = carry
        idx = used - 1 - i
        e = be_ref[idx]
        nxt = jnp.where(e == after_e, after_next, after_e)
        nxt_ref[idx] = nxt
        return e, nxt

    lax.fori_loop(0, used, following, (jnp.int32(-1), jnp.int32(-1)))


def _worklist(units, ustart, n_blocks):
    smem = pl.BlockSpec(memory_space=pltpu.SMEM)
    n_tiles = units.shape[0] // N_EXPERTS
    return pl.pallas_call(
        _worklist_kernel,
        in_specs=[smem, smem],
        out_specs=[smem, smem, smem, smem, smem],
        out_shape=[jax.ShapeDtypeStruct((n_blocks * MOE_BLOCK_UNITS,), I32),
                   jax.ShapeDtypeStruct((n_tiles * (MOE_TILE_ROWS // MOE_UNIT),), I32),
                   jax.ShapeDtypeStruct((n_blocks,), I32), jax.ShapeDtypeStruct((n_blocks,), I32),
                   jax.ShapeDtypeStruct((1,), I32)],
        name="moe_worklist",
    )(units, ustart)


def _expert_kernel(layer, be_ref, nxt_ref, uidx_ref, nb_ref, xs_hbm, w1_hbm, w3_hbm, w2_hbm, o_ref,
                   xbuf, gsem, w1f, w3f, w2f, wsem, wslot, w1b, w3b, w2b):
    b = pl.program_id(0)
    used = nb_ref[0]
    half = xs_hbm.shape[1] - LANES
    slot = lax.rem(b, 2)
    other = 1 - slot

    def weight_copies(e, s):
        return (pltpu.make_async_copy(w1_hbm.at[layer, e], w1f.at[s], wsem.at[s]),
                pltpu.make_async_copy(w3_hbm.at[layer, e], w3f.at[s], wsem.at[s]),
                pltpu.make_async_copy(w2_hbm.at[layer, e], w2f.at[s], wsem.at[s]))

    def gather_copy(s, j, unit):
        return pltpu.make_async_copy(xs_hbm.at[pl.ds(pl.multiple_of(unit * MOE_UNIT, MOE_UNIT), MOE_UNIT), :],
                                     xbuf.at[s, pl.ds(j * MOE_UNIT, MOE_UNIT), :], gsem.at[s])

    def gather_start(blk, s):
        for j in range(MOE_BLOCK_UNITS):
            gather_copy(s, j, jnp.maximum(uidx_ref[blk * MOE_BLOCK_UNITS + j], 0)).start(priority=j % 2)

    def gather_wait(s):
        for j in range(MOE_BLOCK_UNITS):
            gather_copy(s, j, 0).wait()

    @pl.when(b == 0)
    def _():
        gather_start(0, 0)
        wslot[0] = 1
        for cp in weight_copies(be_ref[0], 0):
            cp.start()

    @pl.when(b + 1 < used)
    def _():
        gather_start(b + 1, other)

    @pl.when(b >= used)
    def _():
        o_ref[...] = jnp.zeros_like(o_ref)

    @pl.when(b < used)
    def _():
        prev = be_ref[jnp.maximum(b - 1, 0)]

        @pl.when(jnp.logical_or(b == 0, be_ref[b] != prev))
        def _():
            s = 1 - wslot[0]
            wslot[0] = s
            for cp in weight_copies(0, s):
                cp.wait()
            w1b[...] = w1f[s].astype(BF16)
            w3b[...] = w3f[s].astype(BF16)
            w2b[...] = w2f[s].astype(BF16)

            @pl.when(nxt_ref[b] >= 0)
            def _():
                for cp in weight_copies(nxt_ref[b], 1 - s):
                    cp.start()

        gather_wait(slot)
        xw = xbuf[slot]
        x = _unpack_bf16_pairs(xw[:, :half])
        wrow = pltpu.bitcast(xw[:, half:], F32)[:, 0:1]
        a = jnp.dot(x, w1b[...], preferred_element_type=F32)
        gte = jnp.dot(x, w3b[...], preferred_element_type=F32)
        mid = (a * _sigmoid(a)) * gte
        o_ref[...] = _pack_bf16_pairs(jnp.dot(mid.astype(BF16), w2b[...], preferred_element_type=F32) * wrow)


def _expert_ffn(layer, xs, blk_expert, blk_next, unit_idx, n_used, w1, w3, w2):
    d, de = w1.shape[2], w1.shape[3]
    nb = blk_expert.shape[0]
    rows = MOE_BLOCK_UNITS * MOE_UNIT
    hbm = pl.BlockSpec(memory_space=pl.ANY)
    grid_spec = pltpu.PrefetchScalarGridSpec(
        num_scalar_prefetch=4,
        grid=(nb,),
        in_specs=[hbm, hbm, hbm, hbm],
        out_specs=pl.BlockSpec((rows, d // 2), lambda b, be, nx, ui, nu: (b, 0)),
        scratch_shapes=[pltpu.VMEM((2, rows, xs.shape[1]), jnp.uint32), pltpu.SemaphoreType.DMA((2,)),
                        pltpu.VMEM((2, d, de), F32), pltpu.VMEM((2, d, de), F32), pltpu.VMEM((2, de, d), F32),
                        pltpu.SemaphoreType.DMA((2,)), pltpu.SMEM((1,), I32),
                        pltpu.VMEM((d, de), BF16), pltpu.VMEM((d, de), BF16), pltpu.VMEM((de, d), BF16)],
    )
    return pl.pallas_call(
        functools.partial(_expert_kernel, layer),
        grid_spec=grid_spec,
        out_shape=jax.ShapeDtypeStruct((nb * rows, d // 2), jnp.uint32),
        compiler_params=_cparams(("arbitrary",)),
        name="moe_expert_ffn",
    )(blk_expert, blk_next, unit_idx, n_used, xs, w1, w3, w2)


def _combine_kernel(alpha, with_next, nu_ref, slot_ref, ys_hbm, info_ref, x_ref, gt_ref, lng_ref, lnb_ref, *rest):
    if with_next:
        sh_ref, sc_ref, o_ref, hn_ref, ybuf, sem = rest
    else:
        o_ref, ybuf, sem = rest
    i = pl.program_id(0)
    nt = pl.num_programs(0)
    tm = x_ref.shape[0]
    rt = ybuf.shape[1]
    slot = lax.rem(i, 2)
    other = 1 - slot

    def unit_copy(tile, s, j):
        src = pl.multiple_of(slot_ref[tile * (rt // MOE_UNIT) + j] * MOE_UNIT, MOE_UNIT)
        dst = pl.multiple_of(j * MOE_UNIT, MOE_UNIT)
        return pltpu.make_async_copy(ys_hbm.at[pl.ds(src, MOE_UNIT), :], ybuf.at[s, pl.ds(dst, MOE_UNIT), :], sem.at[s])

    def start(tile, s):
        def body(j, c):
            unit_copy(tile, s, j).start()
            return c
        lax.fori_loop(0, nu_ref[tile], body, 0)

    def wait(tile, s):
        def body(j, c):
            unit_copy(tile, s, j).wait()
            return c
        lax.fori_loop(0, nu_ref[tile], body, 0)

    @pl.when(i == 0)
    def _():
        ybuf[...] = jnp.zeros_like(ybuf)
        start(0, 0)

    @pl.when(i + 1 < nt)
    def _():
        start(i + 1, other)

    wait(i, slot)
    info = info_ref[...]
    col = lax.broadcasted_iota(I32, (tm, rt), 1)
    pick = jnp.where(jnp.logical_or(col == info[:, 0:1], col == info[:, 1:2]), 1.0, 0.0).astype(BF16)
    y = jnp.dot(pick, _unpack_bf16_pairs(ybuf[slot]), preferred_element_type=F32)
    x2 = _layer_norm(alpha * x_ref[...] + (1.0 + gt_ref[0]) * y) * lng_ref[...] + lnb_ref[...]
    o_ref[...] = x2
    if with_next:
        hn_ref[...] = (_layer_norm(x2) * (1.0 + sc_ref[0]) + sh_ref[0]).astype(BF16)


def _combine(alpha, ys, tile_units, unit_slot, info, x1, gate, ln_g, ln_b, seq, next_mod=None):
    t, d = x1.shape
    tm = MOE_TILE
    per_seq = seq // tm
    tok = pl.BlockSpec((tm, d), lambda i, nu, us: (i, 0))
    per_b = pl.BlockSpec((1, 1, d), lambda i, nu, us: (i // per_seq, 0, 0))
    row = pl.BlockSpec((1, d), lambda i, nu, us: (0, 0))
    with_next = next_mod is not None
    args = [tile_units, unit_slot, ys, info, x1, gate[:, None, :], ln_g.astype(F32).reshape(1, d),
            ln_b.astype(F32).reshape(1, d)]
    in_specs = [pl.BlockSpec(memory_space=pl.ANY), pl.BlockSpec((tm, LANES), lambda i, nu, us: (i, 0)),
                tok, per_b, row, row]
    out_specs, out_shape = tok, jax.ShapeDtypeStruct((t, d), F32)
    if with_next:
        args += [next_mod[0][:, None, :], next_mod[1][:, None, :]]
        in_specs += [per_b, per_b]
        out_specs, out_shape = [tok, tok], [out_shape, jax.ShapeDtypeStruct((t, d), BF16)]
    grid_spec = pltpu.PrefetchScalarGridSpec(
        num_scalar_prefetch=2,
        grid=(t // tm,),
        in_specs=in_specs,
        out_specs=out_specs,
        scratch_shapes=[pltpu.VMEM((2, MOE_TILE_ROWS, d // 2), jnp.uint32), pltpu.SemaphoreType.DMA((2,))],
    )
    return pl.pallas_call(
        functools.partial(_combine_kernel, alpha, with_next),
        grid_spec=grid_spec,
        out_shape=out_shape,
        compiler_params=_cparams(("arbitrary",)),
        name="moe_combine_ln",
    )(*args)


def _moe(layer, alpha, xs, info, seg, x1, gate, ln_g, ln_b, w1, w3, w2, next_mod):
    bsz, s, d = x1.shape
    t = bsz * s
    nt = t // MOE_TILE
    units = seg[:, 0, :N_EXPERTS].reshape(nt * N_EXPERTS)
    ustart = seg[:, 1, :N_EXPERTS].reshape(nt * N_EXPERTS)
    tile_units = seg[:, 2, 0]
    max_units = nt * (TOP_K * MOE_TILE // MOE_UNIT + N_EXPERTS * (MOE_UNIT - 1) // MOE_UNIT)
    n_blocks = max_units // MOE_BLOCK_UNITS + N_EXPERTS
    unit_idx, unit_slot, blk_expert, blk_next, n_used = _worklist(units, ustart, n_blocks)
    ys = _expert_ffn(layer, xs, blk_expert, blk_next, unit_idx, n_used, w1, w3, w2)
    out = _combine(alpha, ys, tile_units, unit_slot, info, x1.reshape(t, d), gate, ln_g, ln_b, s, next_mod)
    if next_mod is None:
        return out.reshape(bsz, s, d), None
    return out[0].reshape(bsz, s, d), out[1].reshape(bsz, s, d)


def kernel(x, c, w_ada, b_ada, ln_g, ln_b, w_in, w_out, conv_w, rwkv_mu, rwkv_w0, rwkv_w2, rwkv_a0, rwkv_a2, rwkv_g2, rwkv_kk, rwkv_ka, rwkv_rk, rwkv_gn_g, rwkv_gn_b, attn_sinks, rel_bias, s5_lambda_re, s5_lambda_im, s5_log_dt, s5_b_re, s5_b_im, s5_c_re, s5_c_im, s5_d, s5_glu_w, s5_glu_b, router_group_w, router_group_b, router_expert_w, router_expert_b, moe_w1, moe_w3, moe_w2):
    depth = w_ada.shape[0]
    d = x.shape[-1]
    g = d // 4
    alpha = (2 * depth) ** 0.25
    n_heads = g // HEAD_DIM
    att_kv = max(1, n_heads // 4) * HEAD_DIM
    rw_off = 3 * g
    lora = RWKV_DECAY_RANK + RWKV_A_RANK + RWKV_GATE_RANK
    att_off = rw_off + 3 * g + lora
    s5_off = att_off + g + 2 * att_kv

    mod = _modulation(c, w_ada, b_ada)
    for l in range(depth):
        sh1, sc1, gt1, sh2, sc2, gt2 = jnp.split(mod[l], 6, axis=-1)
        wl = w_in[l]
        w_conv = wl[:, :rw_off].astype(BF16)
        w_rkv = wl[:, rw_off:rw_off + 3 * g].astype(BF16)
        lo = rw_off + 3 * g
        zcol = lambda n: jnp.zeros((d, n), F32)
        w_lora = jnp.concatenate([
            wl[:, lo:lo + RWKV_DECAY_RANK], zcol(LANES - RWKV_DECAY_RANK),
            wl[:, lo + RWKV_DECAY_RANK:lo + RWKV_DECAY_RANK + RWKV_A_RANK], zcol(LANES - RWKV_A_RANK),
            wl[:, lo + RWKV_DECAY_RANK + RWKV_A_RANK:att_off]], axis=1).astype(BF16)
        w_q = wl[:, att_off:att_off + g].astype(BF16)
        w_kv = wl[:, att_off + g:s5_off].astype(BF16)
        w_s5 = wl[:, s5_off:].astype(BF16)

        if l == 0:
            h = _adaln(x, sh1, sc1)
        y_rwkv = _rwkv_mixer(h, w_rkv, w_lora, rwkv_mu[l], rwkv_w0[l], rwkv_w2[l], rwkv_a0[l], rwkv_a2[l],
                             rwkv_g2[l], rwkv_kk[l], rwkv_ka[l], rwkv_rk[l], rwkv_gn_g[l], rwkv_gn_b[l])
        y_att = _att_mixer(h, w_q, w_kv, attn_sinks[l], rel_bias)
        y_ssm, y_conv = _s5_conv_mixers(h, w_s5, s5_lambda_re[l], s5_lambda_im[l], s5_log_dt[l], s5_b_re[l],
                                        s5_b_im[l], s5_c_re[l], s5_c_im[l], s5_d[l], s5_glu_w[l], s5_glu_b[l],
                                        w_conv, conv_w[l].astype(F32))
        w_router = jnp.zeros((d, LANES), F32)
        w_router = w_router.at[:, :N_EXPERT_GROUPS].set(router_group_w[l])
        w_router = w_router.at[:, N_EXPERT_GROUPS:N_EXPERT_GROUPS + N_EXPERTS].set(router_expert_w[l]).astype(BF16)
        b_router = jnp.zeros((1, LANES), F32)
        b_router = b_router.at[0, :N_EXPERT_GROUPS].set(router_group_b[l])
        b_router = b_router.at[0, N_EXPERT_GROUPS:N_EXPERT_GROUPS + N_EXPERTS].set(router_expert_b[l])
        x1, xs, info, seg = _mix_out(alpha, (y_conv, y_rwkv, y_att, y_ssm), w_out[l].astype(BF16), x, gt1,
                                     ln_g[l, 0], ln_b[l, 0], sh2, sc2, w_router, b_router)
        next_mod = None
        if l + 1 < depth:
            nsh1, nsc1 = jnp.split(mod[l + 1], 6, axis=-1)[:2]
            next_mod = (nsh1, nsc1)
        x, h = _moe(l, alpha, xs, info, seg, x1, gt2, ln_g[l, 1], ln_b[l, 1], moe_w1, moe_w3, moe_w2, next_mod)
    return x
```

```python
import functools
import math

import numpy as np
import jax
import jax.numpy as jnp
from jax import lax
from jax.experimental import pallas as pl
from jax.experimental.pallas import tpu as pltpu

F32 = jnp.float32
BF16 = jnp.bfloat16
I32 = jnp.int32

HEAD_DIM = 64
CONV_WIDTH = 3
RWKV_DECAY_RANK = 96
RWKV_A_RANK = 96
RWKV_GATE_RANK = 128
RWKV_GN_EPS = 64e-5
ATT_BLOCK = 128
WINDOW = 128
N_BUCKETS = 32
NEG_INF = -1e30
S5_CH = 16
S5_STATE = 64
N_EXPERT_GROUPS = 4
EXPERTS_PER_GROUP = 8
N_EXPERTS = N_EXPERT_GROUPS * EXPERTS_PER_GROUP
TOP_K = 2
LN_EPS = 1e-5

LANES = 128
SUBLANES = 8
WKV_CHUNK = 64
MOE_TILE = 256
MOE_UNIT = SUBLANES
MOE_TILE_ROWS = 768
MOE_BLOCK_UNITS = 32
VMEM_LIMIT = 56 * 2 ** 20


def _cparams(sem, flags=None):
    return pltpu.CompilerParams(dimension_semantics=sem, vmem_limit_bytes=VMEM_LIMIT, flags=flags)


def _dot(a, b):
    return jnp.dot(a.astype(BF16), b.astype(BF16), preferred_element_type=F32)


def _dot_nt(a, b):
    return lax.dot_general(a.astype(BF16), b.astype(BF16), (((1,), (1,)), ((), ())),
                           preferred_element_type=F32)


def _dot_tn(a, b):
    return jnp.dot(a.T.astype(BF16), b.astype(BF16), preferred_element_type=F32)


def _dot_split(x, e, lhs=False):
    hi = x.astype(BF16)
    lo = (x - hi.astype(F32)).astype(BF16)
    if lhs:
        return jnp.dot(e, hi, preferred_element_type=F32) + jnp.dot(e, lo, preferred_element_type=F32)
    return jnp.dot(hi, e, preferred_element_type=F32) + jnp.dot(lo, e, preferred_element_type=F32)


def _pack_bf16_pairs(x):
    n = x.shape[1] // 2
    bits = pltpu.bitcast(x.astype(BF16).astype(F32), jnp.uint32)
    return jnp.bitwise_or(jnp.right_shift(bits[:, :n], jnp.uint32(16)), bits[:, n:])


def _unpack_bf16_pairs(word):
    lo = pltpu.bitcast(jnp.left_shift(word, jnp.uint32(16)), F32)
    hi = pltpu.bitcast(jnp.bitwise_and(word, jnp.uint32(0xFFFF0000)), F32)
    return jnp.concatenate([lo, hi], axis=1).astype(BF16)


def _sigmoid(x):
    return 1.0 / (1.0 + jnp.exp(-x))


def _layer_norm(x):
    mean = jnp.mean(x, axis=-1, keepdims=True)
    xc = x - mean
    var = jnp.mean(xc * xc, axis=-1, keepdims=True)
    return xc * lax.rsqrt(var + LN_EPS)


def _shift_rows(p, carry_row, n):
    row = lax.broadcasted_iota(I32, (p.shape[0], 1), 0)
    out = pltpu.roll(p, n, 0)
    for i in range(n):
        out = jnp.where(row == i, carry_row[SUBLANES - n + i:SUBLANES - n + i + 1, :], out)
    return out


def _mod_kernel(c_ref, w_ref, b_ref, o_ref):
    c = c_ref[...]
    a = c * _sigmoid(c)
    o_ref[0] = _dot(a, w_ref[0]) + b_ref[0]


def _modulation(c, w_ada, b_ada):
    depth, d, n = w_ada.shape
    bsz = c.shape[0]
    tn = 1536
    cp = jnp.zeros((SUBLANES, d), F32).at[:bsz].set(c)
    out = pl.pallas_call(
        _mod_kernel,
        grid=(depth, n // tn),
        in_specs=[pl.BlockSpec((SUBLANES, d), lambda l, j: (0, 0)),
                  pl.BlockSpec((1, d, tn), lambda l, j: (l, 0, j)),
                  pl.BlockSpec((1, 1, tn), lambda l, j: (l, 0, j))],
        out_specs=pl.BlockSpec((1, SUBLANES, tn), lambda l, j: (l, 0, j)),
        out_shape=jax.ShapeDtypeStruct((depth, SUBLANES, n), F32),
        compiler_params=_cparams(("parallel", "parallel")),
        name="adaln_modulation",
    )(cp, w_ada, b_ada.reshape(depth, 1, n))
    return out[:, :bsz]


def _adaln_kernel(x_ref, sh_ref, sc_ref, h_ref):
    h_ref[0] = (_layer_norm(x_ref[0]) * (1.0 + sc_ref[0]) + sh_ref[0]).astype(BF16)


def _adaln(x, shift, scale):
    bsz, s, d = x.shape
    tm = 512
    return pl.pallas_call(
        _adaln_kernel,
        grid=(bsz, s // tm),
        in_specs=[pl.BlockSpec((1, tm, d), lambda b, i: (b, i, 0)),
                  pl.BlockSpec((1, 1, d), lambda b, i: (b, 0, 0)),
                  pl.BlockSpec((1, 1, d), lambda b, i: (b, 0, 0))],
        out_specs=pl.BlockSpec((1, tm, d), lambda b, i: (b, i, 0)),
        out_shape=jax.ShapeDtypeStruct((bsz, s, d), BF16),
        compiler_params=_cparams(("parallel", "parallel")),
        name="adaln_input",
    )(x, shift[:, None, :], scale[:, None, :])


def _t5_bucket(rel):
    n = jnp.maximum(rel, 0)
    max_exact = N_BUCKETS // 2
    n_f = jnp.maximum(n, 1).astype(F32)
    large = max_exact + (jnp.log(n_f / max_exact) / math.log(WINDOW / max_exact)
                         * (N_BUCKETS - max_exact)).astype(I32)
    return jnp.where(n < max_exact, n, jnp.minimum(large, N_BUCKETS - 1))


def _swa_stages(x, first_tile, sink_ref, wq_ref, wkv_ref, bias_ref, kvc_ref, y_ref, b):
    tm = x.shape[0]
    n_heads = bias_ref.shape[0]
    kvw = wkv_ref.shape[1] // 2
    n_kv = kvw // HEAD_DIM
    rep = n_heads // n_kv
    blk = ATT_BLOCK
    q = _dot(x, wq_ref[...]) * (HEAD_DIM ** -0.5)
    kv = _dot(x, wkv_ref[...])
    kvext = jnp.concatenate([kvc_ref[b], kv], axis=0)
    kvc_ref[b] = kv[tm - blk:, :]
    col = lax.broadcasted_iota(I32, (blk, 2 * blk), 1)
    qb16 = q.astype(BF16)
    kv16 = kvext.astype(BF16)

    def scores(j):
        qb = qb16[j * blk:(j + 1) * blk]
        kw = kv16[j * blk:j * blk + 2 * blk, :kvw]
        kgs = [kw[:, gi * HEAD_DIM:(gi + 1) * HEAD_DIM] for gi in range(n_kv)]
        scs = [_dot_nt(qb[:, hh * HEAD_DIM:(hh + 1) * HEAD_DIM], kgs[hh // rep]) + bias_ref[hh]
               for hh in range(n_heads)]
        if j == 0:
            scs = [jnp.where(jnp.logical_and(first_tile, col < blk), NEG_INF, sc) for sc in scs]
        return scs

    def probs(scs):
        out = []
        for hh, sc in enumerate(scs):
            sink = sink_ref[hh]
            m = jnp.maximum(jnp.max(sc, axis=-1, keepdims=True), sink)
            e = jnp.exp(sc - m)
            den = jnp.sum(e, axis=-1, keepdims=True) + jnp.exp(sink - m)
            out.append((e / den).astype(BF16))
        return out

    def values(j, ps):
        vw = kv16[j * blk:j * blk + 2 * blk, kvw:]
        vgs = [vw[:, gi * HEAD_DIM:(gi + 1) * HEAD_DIM] for gi in range(n_kv)]
        outs = [jnp.dot(p, vgs[hh // rep], preferred_element_type=F32) for hh, p in enumerate(ps)]
        y_ref[b, j * blk:(j + 1) * blk, :] = jnp.concatenate(outs, axis=1).astype(BF16)

    return scores, lambda j, scs: values(j, probs(scs))


def _swa_bias_table(rel_bias):
    qi = jnp.arange(ATT_BLOCK)[:, None]
    kj = jnp.arange(2 * ATT_BLOCK)[None, :]
    rel = qi + ATT_BLOCK - kj
    valid = (rel >= 0) & (rel < WINDOW)
    onehot = (_t5_bucket(rel)[..., None] == jnp.arange(N_BUCKETS)).astype(F32)
    bias = jnp.einsum('qkb,bh->hqk', onehot, rel_bias.astype(F32), precision=lax.Precision.HIGHEST)
    return jnp.where(valid[None], bias, NEG_INF)


S5_GROUPS_PER_BLOCK = LANES // S5_CH


def _s5_tables(lam_re, lam_im, log_dt, b_re, b_im, c_re, c_im):
    n_groups, p = lam_re.shape
    lr, li = lam_re.astype(F32), lam_im.astype(F32)
    delta = jnp.exp(log_dt.astype(F32))[:, None]
    mag = jnp.exp(lr * delta)
    ab_re, ab_im = mag * jnp.cos(li * delta), mag * jnp.sin(li * delta)
    den = lr * lr + li * li
    z_re = ((ab_re - 1.0) * lr + ab_im * li) / den
    z_im = (ab_im * lr - (ab_re - 1.0) * li) / den
    br, bi = b_re.astype(F32), b_im.astype(F32)
    bb_re = z_re[..., None] * br - z_im[..., None] * bi
    bb_im = z_re[..., None] * bi + z_im[..., None] * br
    nblk = n_groups // S5_GROUPS_PER_BLOCK
    eye = jnp.eye(S5_GROUPS_PER_BLOCK, dtype=F32)

    def in_blocks(bb):
        bb = bb.reshape(nblk, S5_GROUPS_PER_BLOCK, p, S5_CH)
        return jnp.einsum('qgpc,gh->qgchp', bb, eye).reshape(nblk, LANES, S5_GROUPS_PER_BLOCK * p)

    def out_blocks(cc):
        cc = cc.astype(F32).reshape(nblk, S5_GROUPS_PER_BLOCK, S5_CH, p)
        return jnp.einsum('qgcp,gh->qgphc', cc, eye).reshape(nblk, S5_GROUPS_PER_BLOCK * p, LANES)

    def power(m):
        mg = jnp.exp(m * lr * delta)
        return (mg * jnp.cos(m * li * delta)).reshape(1, -1), (mg * jnp.sin(m * li * delta)).reshape(1, -1)

    row = jnp.arange(SUBLANES, dtype=F32)[:, None]
    tabs = []
    for sft in (1, 2, 4):
        pr, pi = power(float(sft))
        keep = row >= sft
        tabs += [jnp.where(keep, pr, 0.0), jnp.where(keep, pi, 0.0)]
    n_state = n_groups * p
    lrd = (lr * delta).reshape(1, n_state)
    lid = (li * delta).reshape(1, n_state)
    mg = jnp.exp((row + 1.0) * lrd)
    tabs += [mg * jnp.cos((row + 1.0) * lid), mg * jnp.sin((row + 1.0) * lid)]
    tables = jnp.stack(tabs, axis=0)
    return (in_blocks(bb_re).astype(BF16), in_blocks(bb_im).astype(BF16),
            out_blocks(c_re).astype(BF16), out_blocks(c_im).astype(BF16), tables)


def _s5_kernel(sink_ref, h_ref, w_ref, bre_ref, bim_ref, cre_ref, cim_ref, tab_ref, d_ref, gw_ref, gb_ref,
               wc_ref, cw_ref, wq_ref, wkv_ref, bias_ref, y_ref, yc_ref, ya_ref,
               xr_ref, xi_ref, cr_ref, ci_ref, cc_ref, kvc_ref):
    nseq = h_ref.shape[0]
    tm = h_ref.shape[1]
    nblk = bre_ref.shape[0]
    sw = bre_ref.shape[2]

    @pl.when(pl.program_id(0) == 0)
    def _():
        cr_ref[...] = jnp.zeros_like(cr_ref)
        ci_ref[...] = jnp.zeros_like(ci_ref)
        cc_ref[...] = jnp.zeros_like(cc_ref)
        kvc_ref[...] = jnp.zeros_like(kvc_ref)

    def project(b):
        u = _dot(h_ref[b], w_ref[...])
        ub = u.astype(BF16)
        for q in range(nblk):
            uq = ub[:, q * LANES:(q + 1) * LANES]
            xr_ref[b, :, q * sw:(q + 1) * sw] = jnp.dot(uq, bre_ref[q], preferred_element_type=F32)
            xi_ref[b, :, q * sw:(q + 1) * sw] = jnp.dot(uq, bim_ref[q], preferred_element_type=F32)
        return u

    def scan(b):
        cr = cr_ref[b, 0:1, :]
        ci = ci_ref[b, 0:1, :]
        for i in range(tm // SUBLANES):
            rows = slice(i * SUBLANES, (i + 1) * SUBLANES)
            xr = xr_ref[b, rows, :]
            xi = xi_ref[b, rows, :]
            for k, sft in enumerate((1, 2, 4)):
                mr = tab_ref[2 * k]
                mi = tab_ref[2 * k + 1]
                sr = pltpu.roll(xr, sft, 0)
                si = pltpu.roll(xi, sft, 0)
                xr, xi = xr + mr * sr - mi * si, xi + mr * si + mi * sr
            pr = tab_ref[6]
            pi = tab_ref[7]
            xr, xi = xr + pr * cr - pi * ci, xi + pr * ci + pi * cr
            xr_ref[b, rows, :] = xr
            xi_ref[b, rows, :] = xi
            cr = xr[SUBLANES - 1:SUBLANES, :]
            ci = xi[SUBLANES - 1:SUBLANES, :]
        cr_ref[b, 0:1, :] = cr
        ci_ref[b, 0:1, :] = ci

    def readout(b, u):
        ys = []
        for q in range(nblk):
            xr = xr_ref[b, :, q * sw:(q + 1) * sw].astype(BF16)
            xi = xi_ref[b, :, q * sw:(q + 1) * sw].astype(BF16)
            ys.append(jnp.dot(xr, cre_ref[q], preferred_element_type=F32)
                      - jnp.dot(xi, cim_ref[q], preferred_element_type=F32))
        y = jnp.concatenate(ys, axis=1) + d_ref[...] * u
        y = 0.5 * y * (1.0 + jnp.tanh(math.sqrt(2.0 / math.pi) * (y + 0.044715 * (y * y * y))))
        y_ref[b] = (y * _sigmoid(_dot(y, gw_ref[...]) + gb_ref[...])).astype(BF16)

    def conv(b):
        g = yc_ref.shape[2]
        p = _dot(h_ref[b], wc_ref[...])
        b_gate, c_gate, hh = p[:, :g], p[:, g:2 * g], p[:, 2 * g:]
        z = c_gate * hh
        carry = cc_ref[b]
        cw = cw_ref[...]
        out = cw[0:1] * _shift_rows(z, carry, 2) + cw[1:2] * _shift_rows(z, carry, 1) + cw[2:3] * z
        yc_ref[b] = (b_gate * out).astype(BF16)
        cc_ref[b] = z[tm - SUBLANES:, :]

    us = [project(b) for b in range(nseq)]
    first_tile = pl.program_id(0) == 0
    att = [_swa_stages(h_ref[b], first_tile, sink_ref, wq_ref, wkv_ref, bias_ref, kvc_ref, ya_ref, b)
           for b in range(nseq)]
    fillers = [functools.partial(conv, b) for b in range(nseq)]
    for j in range(tm // ATT_BLOCK):
        scs = [att[b][0](j) for b in range(nseq)]
        if fillers:
            fillers.pop(0)()
        for b in range(nseq):
            att[b][1](j, scs[b])
    for f in fillers:
        f()
    for b in range(nseq):
        scan(b)
        readout(b, us[b])


def _s5_conv_swa_mixers(h, w, lam_re, lam_im, log_dt, b_re, b_im, c_re, c_im, d_skip, glu_w, glu_b,
                        w_conv, conv_w, w_q, w_kv, sinks, rel_bias):
    bsz, s, d = h.shape
    bias = _swa_bias_table(rel_bias)
    g = w.shape[1]
    tm = 256
    bre, bim, cre, cim, tables = _s5_tables(lam_re, lam_im, log_dt, b_re, b_im, c_re, c_im)
    n_state = tables.shape[2]
    full = lambda a: pl.BlockSpec(a.shape, lambda i: (0,) * a.ndim)
    dvec = d_skip.astype(F32).reshape(1, g)
    gw = glu_w.astype(BF16)
    gb = glu_b.astype(F32).reshape(1, g)
    return pl.pallas_call(
        _s5_kernel,
        grid=(s // tm,),
        in_specs=[pl.BlockSpec(memory_space=pltpu.SMEM),
                  pl.BlockSpec((bsz, tm, d), lambda i: (0, i, 0)),
                  full(w), full(bre), full(bim), full(cre), full(cim), full(tables),
                  full(dvec), full(gw), full(gb), full(w_conv), full(conv_w),
                  full(w_q), full(w_kv), full(bias)],
        out_specs=[pl.BlockSpec((bsz, tm, g), lambda i: (0, i, 0)),
                   pl.BlockSpec((bsz, tm, g), lambda i: (0, i, 0)),
                   pl.BlockSpec((bsz, tm, w_q.shape[1]), lambda i: (0, i, 0))],
        out_shape=[jax.ShapeDtypeStruct((bsz, s, g), BF16), jax.ShapeDtypeStruct((bsz, s, g), BF16),
                   jax.ShapeDtypeStruct((bsz, s, w_q.shape[1]), BF16)],
        scratch_shapes=[pltpu.VMEM((bsz, tm, n_state), F32), pltpu.VMEM((bsz, tm, n_state), F32),
                        pltpu.VMEM((bsz, SUBLANES, n_state), F32), pltpu.VMEM((bsz, SUBLANES, n_state), F32),
                        pltpu.VMEM((bsz, SUBLANES, g), F32), pltpu.VMEM((bsz, ATT_BLOCK, w_kv.shape[1]), F32)],
        compiler_params=_cparams(("arbitrary",)),
        name="s5_conv_swa_mixers",
    )(sinks.astype(F32), h, w, bre, bim, cre, cim, tables, dvec, gw, gb, w_conv, conv_w, w_q, w_kv, bias)


def _rwkv_kernel(h_ref, wrkv_ref, wlo_ref, mu1_ref, mu2_ref, w0_ref, w2_ref, a0_ref, a2_ref, g2_ref,
                 kk_ref, ka_ref, rk_ref, gng_ref, gnb_ref, eblk_ref,
                 y_ref,
                 cp_ref, cl_ref, hs_ref, r_s, k_s, v_s, a_s, b_s, ld_s, y_s):
    tm = h_ref.shape[1]
    g = y_ref.shape[2]
    npair = g // LANES
    ch = WKV_CHUNK

    @pl.when(pl.program_id(1) == 0)
    def _():
        cp_ref[...] = jnp.zeros_like(cp_ref)
        cl_ref[...] = jnp.zeros_like(cl_ref)
        hs_ref[...] = jnp.zeros_like(hs_ref)

    x = h_ref[0]
    p = _dot(x, wrkv_ref[...])
    plo = _dot(x, wlo_ref[...])
    pprev = _shift_rows(p, cp_ref[...], 1)
    lprev = _shift_rows(plo, cl_ref[...], 1)
    cp_ref[...] = p[tm - SUBLANES:, :]
    cl_ref[...] = plo[tm - SUBLANES:, :]
    p = p + (pprev - p) * mu1_ref[...]
    plo = plo + (lprev - plo) * mu2_ref[...]
    r, k, v = p[:, :g], p[:, g:2 * g], p[:, 2 * g:]
    w_lo, a_lo, g_lo = plo[:, :LANES], plo[:, LANES:2 * LANES], plo[:, 2 * LANES:]
    wraw = w0_ref[...] + _dot(jnp.tanh(w_lo), w2_ref[...])
    nz = -wraw
    softplus = jnp.maximum(nz, 0.0) + jnp.log(1.0 + jnp.exp(-jnp.abs(nz)))
    w = -softplus - 0.5
    ld_s[...] = -jnp.exp(w)
    a = _sigmoid(a0_ref[...] + _dot(a_lo, a2_ref[...]))
    gate = _dot(_sigmoid(g_lo), g2_ref[...])
    eblk = eblk_ref[...]
    kk = k * kk_ref[...]
    kk = kk / jnp.maximum(jnp.sqrt(_dot_split(kk * kk, eblk)), 1e-12)
    k = k * (1.0 + (a - 1.0) * ka_ref[...])
    r_s[...] = r
    k_s[...] = k
    v_s[...] = v
    a_s[...] = -kk
    b_s[...] = kk * a

    lane = lax.broadcasted_iota(I32, (1, LANES), 1)
    m0 = (lane < HEAD_DIM).astype(F32)
    m1 = 1.0 - m0
    ri = lax.broadcasted_iota(I32, (2 * ch, 2 * ch), 0)
    ci = lax.broadcasted_iota(I32, (2 * ch, 2 * ch), 1)
    same = (ri < ch) == (ci < ch)
    rloc = jnp.bitwise_and(ri, ch - 1)
    cloc = jnp.bitwise_and(ci, ch - 1)
    strict = jnp.where(jnp.logical_and(same, cloc < rloc), 1.0, 0.0)
    incl = jnp.where(jnp.logical_and(same, cloc <= rloc), 1.0, 0.0)
    eye = jnp.where(ri == ci, 1.0, 0.0)
    tri = jnp.where(lax.broadcasted_iota(I32, (ch, ch), 1) <= lax.broadcasted_iota(I32, (ch, ch), 0),
                    1.0, 0.0).astype(BF16)

    def bd(t):
        return jnp.concatenate([t * m0, t * m1], axis=0)

    nchunk = tm // ch
    per_chunk = []
    for c in range(nchunk):
        rows = slice(c * ch, (c + 1) * ch)
        ld = ld_s[rows, :]
        ld_hi = ld.astype(BF16)
        ld_lo = (ld - ld_hi.astype(F32)).astype(BF16)
        cum = (jnp.dot(tri, ld_hi, preferred_element_type=F32)
               + jnp.dot(tri, ld_lo, preferred_element_type=F32))
        gam = jnp.exp(cum)
        ginv = jnp.exp(-cum)
        per_chunk.append(dict(at=a_s[rows, :] * jnp.exp(cum - ld), rt=r_s[rows, :] * gam,
                              bt=b_s[rows, :] * ginv, kt=k_s[rows, :] * ginv, v=v_s[rows, :],
                              gl=gam[ch - 1:ch, :]))
    inst = [(c, q) for c in range(nchunk) for q in range(npair)]

    def part(name):
        return [per_chunk[c][name][:, q * LANES:(q + 1) * LANES] for c, q in inst]

    bt, kt, gl = part("bt"), part("kt"), part("gl")
    at_bd = [bd(t) for t in part("at")]
    rt_bd = [bd(t) for t in part("rt")]
    v_bd = [bd(t) for t in part("v")]
    bh_t = [bd(b * g_).T for b, g_ in zip(bt, gl)]
    kh_t = [bd(k_ * g_).T for k_, g_ in zip(kt, gl)]
    gmat = [_dot_nt(jnp.concatenate([a_, r_], axis=0), jnp.concatenate([b, b, k_, k_], axis=0))
            for a_, r_, b, k_ in zip(at_bd, rt_bd, bt, kt)]
    n_ab = [gm[:2 * ch, :2 * ch] * strict for gm in gmat]
    a_ak = [gm[:2 * ch, 2 * ch:] * strict for gm in gmat]
    m_rb = [gm[2 * ch:, :2 * ch] * incl for gm in gmat]
    m_rk = [gm[2 * ch:, 2 * ch:] * incl for gm in gmat]
    tinv = [eye + n for n in n_ab]
    npow = n_ab
    for step in range(1, 6):
        if step == 1:
            npow = [_dot(n, n) for n in npow]
        both = [_dot(n, jnp.concatenate([t, n], axis=1)) for n, t in zip(npow, tinv)]
        tinv = [t + b[:, :2 * ch] for t, b in zip(tinv, both)]
        npow = [b[:, 2 * ch:] for b in both]
    va = [_dot(jnp.concatenate([a_, k_, m_], axis=0), v_) for a_, k_, m_, v_ in zip(a_ak, kh_t, m_rk, v_bd)]
    wu = [_dot(t, jnp.concatenate([a_, x_[:2 * ch]], axis=1)) for t, a_, x_ in zip(tinv, at_bd, va)]
    pq = [_dot(jnp.concatenate([b, m_], axis=0), w_) for b, m_, w_ in zip(bh_t, m_rb, wu)]
    pmat = [eye * g_ + t[:2 * ch, :2 * ch] for g_, t in zip(gl, pq)]
    qmat = [t[:2 * ch, 2 * ch:] + x_[2 * ch:4 * ch] for t, x_ in zip(pq, va)]
    ry = [r_ + t[2 * ch:, :2 * ch] for r_, t in zip(rt_bd, pq)]
    y0 = [t[2 * ch:, 2 * ch:] + x_[4 * ch:] for t, x_ in zip(pq, va)]
    state = [hs_ref[q] for q in range(npair)]
    y_chunks = []
    for c in range(nchunk):
        ids = [c * npair + q for q in range(npair)]
        both = [_dot(jnp.concatenate([ry[i], pmat[i]], axis=0), st) for i, st in zip(ids, state)]
        yy = [t[:2 * ch] + y0[i] for i, t in zip(ids, both)]
        state = [t[2 * ch:] + qmat[i] for i, t in zip(ids, both)]
        y_chunks.append(jnp.concatenate([t[:ch] + t[ch:] for t in yy], axis=1))
    hs_ref[...] = jnp.stack(state, axis=0)

    y = jnp.concatenate(y_chunks, axis=0)
    inv_n = 1.0 / HEAD_DIM
    mean = _dot_split(y, eblk) * inv_n
    yc = y - mean
    var = _dot_split(yc * yc, eblk) * inv_n
    yn = yc * lax.rsqrt(var + RWKV_GN_EPS) * gng_ref[...] + gnb_ref[...]
    r = r_s[...]
    k = k_s[...]
    v = v_s[...]
    bonus = _dot_split(r * k * rk_ref[...], eblk) * v
    y_ref[0] = ((yn + bonus) * gate).astype(BF16)


def _rwkv_mixer(h, w_rkv, w_lora, mu, w0, w2, a0, a2, g2, k_k, k_a, r_k, gn_g, gn_b):
    bsz, s, d = h.shape
    g = w0.shape[0]
    tm = 256
    row = lambda t: t.astype(F32).reshape(1, -1)
    pad_rows = lambda t: jnp.zeros((LANES, g), F32).at[:t.shape[0]].set(t.astype(F32)).astype(BF16)
    mu1 = row(mu[:3 * g])
    mu2 = jnp.concatenate([
        jnp.zeros((LANES,), F32).at[:RWKV_DECAY_RANK].set(mu[3 * g:3 * g + RWKV_DECAY_RANK]),
        jnp.zeros((LANES,), F32).at[:RWKV_A_RANK].set(mu[3 * g + RWKV_DECAY_RANK:3 * g + RWKV_DECAY_RANK + RWKV_A_RANK]),
        mu[3 * g + RWKV_DECAY_RANK + RWKV_A_RANK:]]).reshape(1, -1)
    head = np.arange(g) // HEAD_DIM
    eblk = jnp.asarray(head[:, None] == head[None, :], BF16)
    args = (h, w_rkv, w_lora, mu1, mu2, row(w0), pad_rows(w2), row(a0), pad_rows(a2), g2.astype(BF16),
            row(k_k), row(k_a), row(r_k), row(gn_g), row(gn_b), eblk)
    full = lambda a: pl.BlockSpec(a.shape, lambda b, i: (0,) * a.ndim)
    return pl.pallas_call(
        _rwkv_kernel,
        grid=(bsz, s // tm),
        in_specs=[pl.BlockSpec((1, tm, d), lambda b, i: (b, i, 0))] + [full(a) for a in args[1:]],
        out_specs=pl.BlockSpec((1, tm, g), lambda b, i: (b, i, 0)),
        out_shape=jax.ShapeDtypeStruct((bsz, s, g), BF16),
        scratch_shapes=[pltpu.VMEM((SUBLANES, 3 * g), F32), pltpu.VMEM((SUBLANES, 3 * LANES), F32),
                        pltpu.VMEM((g // LANES, 2 * WKV_CHUNK, LANES), F32)]
                       + [pltpu.VMEM((tm, g), F32) for _ in range(7)],
        compiler_params=_cparams(("parallel", "arbitrary")),
        name="rwkv7_mixer",
    )(*args)


def _mixout_kernel(alpha, ya_ref, yb_ref, yc_ref, yd_ref, wo_ref, x_ref, gt_ref, lng_ref, lnb_ref,
                   sh_ref, sc_ref, wr_ref, br_ref, x1_ref, xs_ref, info_ref, seg_ref, h2s_ref, lgs_ref):
    g = ya_ref.shape[1]

    @pl.when(pl.program_id(0) == 0)
    def _():
        h2s_ref[...] = jnp.zeros_like(h2s_ref)
        lgs_ref[...] = jnp.zeros_like(lgs_ref)

    h2_prev = h2s_ref[...]
    lg_prev = lgs_ref[...]
    y = (jnp.dot(ya_ref[...], wo_ref[0:g, :], preferred_element_type=F32)
         + jnp.dot(yb_ref[...], wo_ref[g:2 * g, :], preferred_element_type=F32)
         + jnp.dot(yc_ref[...], wo_ref[2 * g:3 * g, :], preferred_element_type=F32)
         + jnp.dot(yd_ref[...], wo_ref[3 * g:, :], preferred_element_type=F32))
    _route_sort(h2_prev, lg_prev, xs_ref, info_ref, seg_ref)
    x1 = _layer_norm(alpha * x_ref[...] + (1.0 + gt_ref[0]) * y) * lng_ref[...] + lnb_ref[...]
    x1_ref[...] = x1
    h2 = _layer_norm(x1) * (1.0 + sc_ref[0]) + sh_ref[0]
    h2s_ref[...] = h2.astype(BF16)
    lgs_ref[...] = _dot(h2, wr_ref[...]) + br_ref[...]


def _mix_out(alpha, ys, w_out, x, gate, ln_g, ln_b, shift2, scale2, w_router, b_router):
    bsz, s, d = x.shape
    g = ys[0].shape[2]
    tm = MOE_TILE
    per_seq = s // tm
    nt = bsz * per_seq
    cur = lambda t: jnp.minimum(t, nt - 1)
    prev = lambda t: jnp.maximum(t - 1, 0)
    tok = lambda w: pl.BlockSpec((tm, w), lambda t: (cur(t), 0))
    per_b = pl.BlockSpec((1, 1, d), lambda t: (cur(t) // per_seq, 0, 0))
    full = lambda a: pl.BlockSpec(a.shape, lambda t: (0,) * a.ndim)
    row = lambda v: v.astype(F32).reshape(1, -1)
    args = (*[v.reshape(bsz * s, g) for v in ys], w_out, x.reshape(bsz * s, d), gate[:, None, :], row(ln_g),
            row(ln_b), shift2[:, None, :], scale2[:, None, :], w_router, b_router)
    x1, xs, info, seg = pl.pallas_call(
        functools.partial(_mixout_kernel, alpha),
        grid=(nt + 1,),
        in_specs=[tok(g)] * 4 + [full(w_out), tok(d), per_b, full(args[7]), full(args[8]), per_b, per_b,
                                 full(w_router), full(b_router)],
        out_specs=[tok(d),
                   pl.BlockSpec((MOE_TILE_ROWS, d // 2 + LANES), lambda t: (prev(t), 0)),
                   pl.BlockSpec((tm, LANES), lambda t: (prev(t), 0)),
                   pl.BlockSpec((1, SUBLANES, LANES), lambda t: (prev(t), 0, 0))],
        out_shape=[jax.ShapeDtypeStruct((bsz * s, d), F32),
                   jax.ShapeDtypeStruct((nt * MOE_TILE_ROWS, d // 2 + LANES), jnp.uint32),
                   jax.ShapeDtypeStruct((bsz * s, LANES), I32),
                   jax.ShapeDtypeStruct((nt, SUBLANES, LANES), I32)],
        scratch_shapes=[pltpu.VMEM((tm, d), BF16), pltpu.VMEM((tm, LANES), F32)],
        compiler_params=_cparams(("arbitrary",)),
        name="mix_out_route_sort",
    )(*args)
    return x1.reshape(bsz, s, d), xs, info, seg


def _route_sort(h2, lg, xs_ref, info_ref, seg_ref):
    tm = lg.shape[0]
    rt = xs_ref.shape[0]
    half = h2.shape[1] // 2
    lane = lax.broadcasted_iota(I32, (tm, LANES), 1)
    lane_f = lane.astype(F32)

    def top1(vals, mask):
        mv = jnp.where(mask, vals, -jnp.inf)
        m = jnp.max(mv, axis=-1, keepdims=True)
        idx = jnp.min(jnp.where(jnp.logical_and(mask, mv == m), lane_f, float(LANES)), axis=-1, keepdims=True)
        return m, idx.astype(I32)

    gmask = lane < N_EXPERT_GROUPS
    gm, gidx = top1(lg, gmask)
    g_val = 1.0 / jnp.sum(jnp.where(gmask, jnp.exp(lg - gm), 0.0), axis=-1, keepdims=True)
    elo = N_EXPERT_GROUPS + gidx * EXPERTS_PER_GROUP
    emask = jnp.logical_and(lane >= elo, lane < elo + EXPERTS_PER_GROUP)
    m1, i1 = top1(lg, emask)
    m2, i2 = top1(lg, jnp.logical_and(emask, lane != i1))
    e21 = jnp.exp(m2 - m1)
    w1 = g_val / (1.0 + e21)
    w2 = g_val * e21 / (1.0 + e21)
    e1 = i1 - N_EXPERT_GROUPS
    e2 = i2 - N_EXPERT_GROUPS
    oh1 = (lane == e1)
    oh2 = (lane == e2)
    ohs = jnp.where(jnp.logical_or(oh1, oh2), 1.0, 0.0)
    cnt = jnp.sum(ohs, axis=0, keepdims=True)
    units = jnp.floor((cnt + (MOE_UNIT - 1.0)) * (1.0 / MOE_UNIT))
    li = lax.broadcasted_iota(I32, (LANES, LANES), 0)
    lj = lax.broadcasted_iota(I32, (LANES, LANES), 1)
    upper = jnp.where(li < lj, 1.0, 0.0).astype(BF16)
    ustart = jnp.dot(jnp.broadcast_to(units, (SUBLANES, LANES)).astype(BF16), upper,
                     preferred_element_type=F32)[0:1, :]
    ri = lax.broadcasted_iota(I32, (tm, tm), 0)
    ci = lax.broadcasted_iota(I32, (tm, tm), 1)
    tri = jnp.where(ci < ri, 1.0, 0.0).astype(BF16)
    before = jnp.dot(tri, ohs.astype(BF16), preferred_element_type=F32)
    first = before + MOE_UNIT * ustart
    pos1 = jnp.sum(jnp.where(oh1, first, 0.0), axis=-1, keepdims=True)
    pos2 = jnp.sum(jnp.where(oh2, first, 0.0), axis=-1, keepdims=True)
    posm = jnp.where(lane == 0, pos1, jnp.where(lane == 1, pos2, -1.0))
    post = posm.T
    prow = lax.broadcasted_iota(I32, (rt, tm), 0).astype(F32)
    sel1 = prow == post[0:1, :]
    sel2 = prow == post[1:2, :]
    hb = h2.astype(BF16)
    xs = jnp.dot(jnp.where(jnp.logical_or(sel1, sel2), 1.0, 0.0).astype(BF16), hb, preferred_element_type=F32)
    bits = pltpu.bitcast(xs, jnp.uint32)
    xs_ref[:, :half] = jnp.bitwise_or(jnp.right_shift(bits[:, :half], jnp.uint32(16)), bits[:, half:])
    def terms(w):
        a = w.astype(BF16).astype(F32)
        b = (w - a).astype(BF16).astype(F32)
        return a, b, ((w - a) - b).astype(BF16).astype(F32)
    t1 = terms(w1)
    t2 = terms(w2)
    wm = jnp.zeros((tm, LANES), F32)
    for k, t in enumerate(t1 + t2):
        wm = jnp.where(lane == k, t, wm)
    wmb = wm.astype(BF16)
    s1 = jnp.dot(jnp.where(sel1, 1.0, 0.0).astype(BF16), wmb, preferred_element_type=F32)
    s2 = jnp.dot(jnp.where(sel2, 1.0, 0.0).astype(BF16), wmb, preferred_element_type=F32)
    wrow = (s1[:, 0:1] + s1[:, 1:2] + s1[:, 2:3]) + (s2[:, 3:4] + s2[:, 4:5] + s2[:, 5:6])
    mlane = lax.broadcasted_iota(I32, (rt, LANES), 1)
    wbits = pltpu.bitcast(jnp.broadcast_to(wrow, (rt, LANES)), jnp.uint32)
    xs_ref[:, half:] = jnp.where(mlane == 0, wbits, jnp.uint32(0))
    info_ref[...] = jnp.where(lane == 0, pos1, jnp.where(lane == 1, pos2, 0.0)).astype(I32)
    srow = lax.broadcasted_iota(I32, (SUBLANES, LANES), 0)
    total = jnp.sum(units, axis=-1, keepdims=True)
    seg = jnp.where(srow == 0, units, jnp.where(srow == 1, ustart, jnp.where(srow == 2, total, 0.0)))
    seg_ref[0] = seg.astype(I32)


def _worklist_vec_kernel(units_ref, ustart_ref, uidx_ref, slot_ref, be_ref, nxt_ref, nb_ref):
    nt = units_ref.shape[0]
    nbk = uidx_ref.shape[0]
    tile_units = MOE_TILE_ROWS // MOE_UNIT
    bu = MOE_BLOCK_UNITS
    u = units_ref[...].astype(F32)
    us = ustart_ref[...].astype(F32)
    ri = lax.broadcasted_iota(I32, (nt, nt), 0)
    ci = lax.broadcasted_iota(I32, (nt, nt), 1)
    tri = jnp.where(ci < ri, 1.0, 0.0).astype(BF16)
    cex = jnp.dot(tri, u.astype(BF16), preferred_element_type=F32)
    cin = cex + u
    tot = jnp.sum(u, axis=0, keepdims=True)
    nblk = jnp.floor((tot + (bu - 1.0)) * (1.0 / bu))
    li = lax.broadcasted_iota(I32, (LANES, LANES), 0)
    lj = lax.broadcasted_iota(I32, (LANES, LANES), 1)
    upper = jnp.where(li < lj, 1.0, 0.0).astype(BF16)
    b0 = jnp.dot(jnp.broadcast_to(nblk, (SUBLANES, LANES)).astype(BF16), upper,
                 preferred_element_type=F32)[0:1, :]
    b0in = b0 + nblk
    used = jnp.sum(nblk, axis=-1, keepdims=True)
    cand = jnp.where(jnp.logical_and(lj > li, jnp.broadcast_to(nblk, (LANES, LANES)) > 0.0),
                     lj.astype(F32), 999.0)
    nxt_col = jnp.min(cand, axis=-1, keepdims=True)
    nxt_col = jnp.where(nxt_col > 900.0, -1.0, nxt_col)

    ul = lax.broadcasted_iota(I32, (nt, LANES), 1).astype(F32)
    slot = jnp.full((nt, LANES), -1.0, F32)
    for e in range(N_EXPERTS):
        st = us[:, e:e + 1]
        sel = jnp.logical_and(ul >= st, ul < st + u[:, e:e + 1])
        slot = jnp.where(sel, bu * b0[:, e:e + 1] + cex[:, e:e + 1] + (ul - st), slot)
    slot_ref[...] = slot.astype(I32)

    def by_tile(t):
        return jnp.concatenate([t, jnp.zeros((LANES - nt, LANES), F32)], axis=0).T

    cin_t, cex_t, us_t = by_tile(cin), by_tile(cex), by_tile(us)
    brow = lax.broadcasted_iota(I32, (nbk, 1), 0).astype(F32)
    eb = jnp.zeros((nbk, 1), F32)
    b0b = jnp.zeros((nbk, 1), F32)
    totb = jnp.zeros((nbk, 1), F32)
    nxtb = jnp.full((nbk, 1), -1.0, F32)
    cin_row = jnp.zeros((nbk, LANES), F32)
    cex_row = jnp.zeros((nbk, LANES), F32)
    us_row = jnp.zeros((nbk, LANES), F32)
    for e in range(N_EXPERTS):
        ine = jnp.logical_and(brow >= b0[:, e:e + 1], brow < b0in[:, e:e + 1])
        eb = jnp.where(ine, float(e), eb)
        b0b = jnp.where(ine, b0[:, e:e + 1], b0b)
        totb = jnp.where(ine, tot[:, e:e + 1], totb)
        nxtb = jnp.where(ine, nxt_col[e:e + 1, :], nxtb)
        cin_row = jnp.where(ine, cin_t[e:e + 1, :], cin_row)
        cex_row = jnp.where(ine, cex_t[e:e + 1, :], cex_row)
        us_row = jnp.where(ine, us_t[e:e + 1, :], us_row)
    active = brow < used
    jl = lax.broadcasted_iota(I32, (1, LANES), 1).astype(F32)
    q = (brow - b0b) * bu + jl
    tau = jnp.zeros((nbk, LANES), F32)
    for t in range(nt):
        tau = tau + jnp.where(cin_row[:, t:t + 1] <= q, 1.0, 0.0)
    base = jnp.zeros((nbk, LANES), F32)
    for t in range(nt):
        base = jnp.where(tau == float(t), us_row[:, t:t + 1] - cex_row[:, t:t + 1] + float(t * tile_units), base)
    ok = jnp.logical_and(jnp.logical_and(active, q < totb), jl < bu)
    uidx_ref[...] = jnp.where(ok, base + q, -1.0).astype(I32)
    be_ref[...] = jnp.broadcast_to(jnp.where(active, eb, N_EXPERTS - 1.0), (nbk, LANES)).astype(I32)
    nxt_ref[...] = jnp.broadcast_to(jnp.where(active, nxtb, -1.0), (nbk, LANES)).astype(I32)
    nb_ref[...] = jnp.broadcast_to(used, (SUBLANES, LANES)).astype(I32)


def _worklist_vec(units2, ustart2, n_blocks):
    nt = units2.shape[0]
    tile_units = MOE_TILE_ROWS // MOE_UNIT
    uidx, slot, be, nxt, nb = pl.pallas_call(
        _worklist_vec_kernel,
        out_shape=[jax.ShapeDtypeStruct((n_blocks, LANES), I32), jax.ShapeDtypeStruct((nt, LANES), I32),
                   jax.ShapeDtypeStruct((n_blocks, LANES), I32), jax.ShapeDtypeStruct((n_blocks, LANES), I32),
                   jax.ShapeDtypeStruct((SUBLANES, LANES), I32)],
        name="moe_worklist",
    )(units2, ustart2)
    return (uidx[:, :MOE_BLOCK_UNITS].reshape(n_blocks * MOE_BLOCK_UNITS),
            slot[:, :tile_units].reshape(nt * tile_units), be[:, 0], nxt[:, 0], nb[0, :1])


def _expert_kernel(layer, be_ref, nxt_ref, uidx_ref, nb_ref, xs_hbm, w1_hbm, w3_hbm, w2_hbm, o_ref,
                   xbuf, gsem, w1f, w3f, w2f, wsem, wslot, w1b, w3b, w2b):
    b = pl.program_id(0)
    used = nb_ref[0]
    half = xs_hbm.shape[1] - LANES
    slot = lax.rem(b, 2)
    other = 1 - slot

    def weight_copies(e, s):
        return (pltpu.make_async_copy(w1_hbm.at[layer, e], w1f.at[s], wsem.at[s]),
                pltpu.make_async_copy(w3_hbm.at[layer, e], w3f.at[s], wsem.at[s]),
                pltpu.make_async_copy(w2_hbm.at[layer, e], w2f.at[s], wsem.at[s]))

    def gather_copy(s, j, unit):
        return pltpu.make_async_copy(xs_hbm.at[pl.ds(pl.multiple_of(unit * MOE_UNIT, MOE_UNIT), MOE_UNIT), :],
                                     xbuf.at[s, pl.ds(j * MOE_UNIT, MOE_UNIT), :], gsem.at[s])

    def gather_start(blk, s):
        for j in range(MOE_BLOCK_UNITS):
            gather_copy(s, j, jnp.maximum(uidx_ref[blk * MOE_BLOCK_UNITS + j], 0)).start(priority=j % 2)

    def gather_wait(s):
        for j in range(MOE_BLOCK_UNITS):
            gather_copy(s, j, 0).wait()

    @pl.when(b == 0)
    def _():
        gather_start(0, 0)
        wslot[0] = 1
        for cp in weight_copies(be_ref[0], 0):
            cp.start()

    @pl.when(b + 1 < used)
    def _():
        gather_start(b + 1, other)

    @pl.when(b >= used)
    def _():
        o_ref[...] = jnp.zeros_like(o_ref)

    @pl.when(b < used)
    def _():
        prev = be_ref[jnp.maximum(b - 1, 0)]

        @pl.when(jnp.logical_or(b == 0, be_ref[b] != prev))
        def _():
            s = 1 - wslot[0]
            wslot[0] = s
            for cp in weight_copies(0, s):
                cp.wait()
            w1b[...] = w1f[s].astype(BF16)
            w3b[...] = w3f[s].astype(BF16)
            w2b[...] = w2f[s].astype(BF16)

            @pl.when(nxt_ref[b] >= 0)
            def _():
                for cp in weight_copies(nxt_ref[b], 1 - s):
                    cp.start()

        gather_wait(slot)
        xw = xbuf[slot]
        x = _unpack_bf16_pairs(xw[:, :half])
        wrow = pltpu.bitcast(xw[:, half:], F32)[:, 0:1]
        a = jnp.dot(x, w1b[...], preferred_element_type=F32)
        gte = jnp.dot(x, w3b[...], preferred_element_type=F32)
        mid = (a * _sigmoid(a)) * gte
        o_ref[...] = _pack_bf16_pairs(jnp.dot(mid.astype(BF16), w2b[...], preferred_element_type=F32) * wrow)


def _expert_ffn(layer, xs, blk_expert, blk_next, unit_idx, n_used, w1, w3, w2):
    d, de = w1.shape[2], w1.shape[3]
    nb = blk_expert.shape[0]
    rows = MOE_BLOCK_UNITS * MOE_UNIT
    hbm = pl.BlockSpec(memory_space=pl.ANY)
    grid_spec = pltpu.PrefetchScalarGridSpec(
        num_scalar_prefetch=4,
        grid=(nb,),
        in_specs=[hbm, hbm, hbm, hbm],
        out_specs=pl.BlockSpec((rows, d // 2), lambda b, be, nx, ui, nu: (b, 0)),
        scratch_shapes=[pltpu.VMEM((2, rows, xs.shape[1]), jnp.uint32), pltpu.SemaphoreType.DMA((2,)),
                        pltpu.VMEM((2, d, de), F32), pltpu.VMEM((2, d, de), F32), pltpu.VMEM((2, de, d), F32),
                        pltpu.SemaphoreType.DMA((2,)), pltpu.SMEM((1,), I32),
                        pltpu.VMEM((d, de), BF16), pltpu.VMEM((d, de), BF16), pltpu.VMEM((de, d), BF16)],
    )
    return pl.pallas_call(
        functools.partial(_expert_kernel, layer),
        grid_spec=grid_spec,
        out_shape=jax.ShapeDtypeStruct((nb * rows, d // 2), jnp.uint32),
        compiler_params=_cparams(("arbitrary",)),
        name="moe_expert_ffn",
    )(blk_expert, blk_next, unit_idx, n_used, xs, w1, w3, w2)


def _combine_kernel(alpha, with_next, nu_ref, slot_ref, ys_hbm, info_ref, x_ref, gt_ref, lng_ref, lnb_ref, *rest):
    if with_next:
        sh_ref, sc_ref, o_ref, hn_ref, ybuf, sem = rest
    else:
        o_ref, ybuf, sem = rest
    i = pl.program_id(0)
    nt = pl.num_programs(0)
    tm = x_ref.shape[0]
    rt = ybuf.shape[1]
    slot = lax.rem(i, 2)
    other = 1 - slot

    def unit_copy(tile, s, j):
        src = pl.multiple_of(slot_ref[tile * (rt // MOE_UNIT) + j] * MOE_UNIT, MOE_UNIT)
        dst = pl.multiple_of(j * MOE_UNIT, MOE_UNIT)
        return pltpu.make_async_copy(ys_hbm.at[pl.ds(src, MOE_UNIT), :], ybuf.at[s, pl.ds(dst, MOE_UNIT), :], sem.at[s])

    def start(tile, s):
        def body(j, c):
            unit_copy(tile, s, j).start()
            return c
        lax.fori_loop(0, nu_ref[tile], body, 0)

    def wait(tile, s):
        def body(j, c):
            unit_copy(tile, s, j).wait()
            return c
        lax.fori_loop(0, nu_ref[tile], body, 0)

    @pl.when(i == 0)
    def _():
        ybuf[...] = jnp.zeros_like(ybuf)
        start(0, 0)

    @pl.when(i + 1 < nt)
    def _():
        start(i + 1, other)

    wait(i, slot)
    info = info_ref[...]
    col = lax.broadcasted_iota(I32, (tm, rt), 1)
    pick = jnp.where(jnp.logical_or(col == info[:, 0:1], col == info[:, 1:2]), 1.0, 0.0).astype(BF16)
    y = jnp.dot(pick, _unpack_bf16_pairs(ybuf[slot]), preferred_element_type=F32)
    x2 = _layer_norm(alpha * x_ref[...] + (1.0 + gt_ref[0]) * y) * lng_ref[...] + lnb_ref[...]
    o_ref[...] = x2
    if with_next:
        hn_ref[...] = (_layer_norm(x2) * (1.0 + sc_ref[0]) + sh_ref[0]).astype(BF16)


def _combine(alpha, ys, tile_units, unit_slot, info, x1, gate, ln_g, ln_b, seq, next_mod=None):
    t, d = x1.shape
    tm = MOE_TILE
    per_seq = seq // tm
    tok = pl.BlockSpec((tm, d), lambda i, nu, us: (i, 0))
    per_b = pl.BlockSpec((1, 1, d), lambda i, nu, us: (i // per_seq, 0, 0))
    row = pl.BlockSpec((1, d), lambda i, nu, us: (0, 0))
    with_next = next_mod is not None
    args = [tile_units, unit_slot, ys, info, x1, gate[:, None, :], ln_g.astype(F32).reshape(1, d),
            ln_b.astype(F32).reshape(1, d)]
    in_specs = [pl.BlockSpec(memory_space=pl.ANY), pl.BlockSpec((tm, LANES), lambda i, nu, us: (i, 0)),
                tok, per_b, row, row]
    out_specs, out_shape = tok, jax.ShapeDtypeStruct((t, d), F32)
    if with_next:
        args += [next_mod[0][:, None, :], next_mod[1][:, None, :]]
        in_specs += [per_b, per_b]
        out_specs, out_shape = [tok, tok], [out_shape, jax.ShapeDtypeStruct((t, d), BF16)]
    grid_spec = pltpu.PrefetchScalarGridSpec(
        num_scalar_prefetch=2,
        grid=(t // tm,),
        in_specs=in_specs,
        out_specs=out_specs,
        scratch_shapes=[pltpu.VMEM((2, MOE_TILE_ROWS, d // 2), jnp.uint32), pltpu.SemaphoreType.DMA((2,))],
    )
    return pl.pallas_call(
        functools.partial(_combine_kernel, alpha, with_next),
        grid_spec=grid_spec,
        out_shape=out_shape,
        compiler_params=_cparams(("arbitrary",)),
        name="moe_combine_ln",
    )(*args)


def _moe(layer, alpha, xs, info, seg, x1, gate, ln_g, ln_b, w1, w3, w2, next_mod):
    bsz, s, d = x1.shape
    t = bsz * s
    nt = t // MOE_TILE
    tile_units = seg[:, 2, 0]
    max_units = nt * (TOP_K * MOE_TILE // MOE_UNIT + N_EXPERTS * (MOE_UNIT - 1) // MOE_UNIT)
    n_blocks = max_units // MOE_BLOCK_UNITS + N_EXPERTS
    unit_idx, unit_slot, blk_expert, blk_next, n_used = _worklist_vec(seg[:, 0, :], seg[:, 1, :], n_blocks)
    ys = _expert_ffn(layer, xs, blk_expert, blk_next, unit_idx, n_used, w1, w3, w2)
    out = _combine(alpha, ys, tile_units, unit_slot, info, x1.reshape(t, d), gate, ln_g, ln_b, s, next_mod)
    if next_mod is None:
        return out.reshape(bsz, s, d), None
    return out[0].reshape(bsz, s, d), out[1].reshape(bsz, s, d)


def kernel(x, c, w_ada, b_ada, ln_g, ln_b, w_in, w_out, conv_w, rwkv_mu, rwkv_w0, rwkv_w2, rwkv_a0, rwkv_a2, rwkv_g2, rwkv_kk, rwkv_ka, rwkv_rk, rwkv_gn_g, rwkv_gn_b, attn_sinks, rel_bias, s5_lambda_re, s5_lambda_im, s5_log_dt, s5_b_re, s5_b_im, s5_c_re, s5_c_im, s5_d, s5_glu_w, s5_glu_b, router_group_w, router_group_b, router_expert_w, router_expert_b, moe_w1, moe_w3, moe_w2):
    depth = w_ada.shape[0]
    d = x.shape[-1]
    g = d // 4
    alpha = (2 * depth) ** 0.25
    n_heads = g // HEAD_DIM
    att_kv = max(1, n_heads // 4) * HEAD_DIM
    rw_off = 3 * g
    lora = RWKV_DECAY_RANK + RWKV_A_RANK + RWKV_GATE_RANK
    att_off = rw_off + 3 * g + lora
    s5_off = att_off + g + 2 * att_kv

    mod = _modulation(c, w_ada, b_ada)
    for l in range(depth):
        sh1, sc1, gt1, sh2, sc2, gt2 = jnp.split(mod[l], 6, axis=-1)
        wl = w_in[l]
        w_conv = wl[:, :rw_off].astype(BF16)
        w_rkv = wl[:, rw_off:rw_off + 3 * g].astype(BF16)
        lo = rw_off + 3 * g
        zcol = lambda n: jnp.zeros((d, n), F32)
        w_lora = jnp.concatenate([
            wl[:, lo:lo + RWKV_DECAY_RANK], zcol(LANES - RWKV_DECAY_RANK),
            wl[:, lo + RWKV_DECAY_RANK:lo + RWKV_DECAY_RANK + RWKV_A_RANK], zcol(LANES - RWKV_A_RANK),
            wl[:, lo + RWKV_DECAY_RANK + RWKV_A_RANK:att_off]], axis=1).astype(BF16)
        w_q = wl[:, att_off:att_off + g].astype(BF16)
        w_kv = wl[:, att_off + g:s5_off].astype(BF16)
        w_s5 = wl[:, s5_off:].astype(BF16)

        if l == 0:
            h = _adaln(x, sh1, sc1)
        y_rwkv = _rwkv_mixer(h, w_rkv, w_lora, rwkv_mu[l], rwkv_w0[l], rwkv_w2[l], rwkv_a0[l], rwkv_a2[l],
                             rwkv_g2[l], rwkv_kk[l], rwkv_ka[l], rwkv_rk[l], rwkv_gn_g[l], rwkv_gn_b[l])
        y_ssm, y_conv, y_att = _s5_conv_swa_mixers(
            h, w_s5, s5_lambda_re[l], s5_lambda_im[l], s5_log_dt[l], s5_b_re[l], s5_b_im[l], s5_c_re[l],
            s5_c_im[l], s5_d[l], s5_glu_w[l], s5_glu_b[l], w_conv, conv_w[l].astype(F32), w_q, w_kv,
            attn_sinks[l], rel_bias)
        w_router = jnp.zeros((d, LANES), F32)
        w_router = w_router.at[:, :N_EXPERT_GROUPS].set(router_group_w[l])
        w_router = w_router.at[:, N_EXPERT_GROUPS:N_EXPERT_GROUPS + N_EXPERTS].set(router_expert_w[l]).astype(BF16)
        b_router = jnp.zeros((1, LANES), F32)
        b_router = b_router.at[0, :N_EXPERT_GROUPS].set(router_group_b[l])
        b_router = b_router.at[0, N_EXPERT_GROUPS:N_EXPERT_GROUPS + N_EXPERTS].set(router_expert_b[l])
        x1, xs, info, seg = _mix_out(alpha, (y_conv, y_rwkv, y_att, y_ssm), w_out[l].astype(BF16), x, gt1,
                                     ln_g[l, 0], ln_b[l, 0], sh2, sc2, w_router, b_router)
        next_mod = None
        if l + 1 < depth:
            nsh1, nsc1 = jnp.split(mod[l + 1], 6, axis=-1)[:2]
            next_mod = (nsh1, nsc1)
        x, h = _moe(l, alpha, xs, info, seg, x1, gt2, ln_g[l, 1], ln_b[l, 1], moe_w1, moe_w3, moe_w2, next_mod)
    return x
```

```python
import functools
import math

import numpy as np
import jax
import jax.numpy as jnp
from jax import lax
from jax.experimental import pallas as pl
from jax.experimental.pallas import tpu as pltpu

F32 = jnp.float32
BF16 = jnp.bfloat16
I32 = jnp.int32

HEAD_DIM = 64
CONV_WIDTH = 3
RWKV_DECAY_RANK = 96
RWKV_A_RANK = 96
RWKV_GATE_RANK = 128
RWKV_GN_EPS = 64e-5
ATT_BLOCK = 128
WINDOW = 128
N_BUCKETS = 32
NEG_INF = -1e30
S5_CH = 16
S5_STATE = 64
N_EXPERT_GROUPS = 4
EXPERTS_PER_GROUP = 8
N_EXPERTS = N_EXPERT_GROUPS * EXPERTS_PER_GROUP
TOP_K = 2
LN_EPS = 1e-5

LANES = 128
SUBLANES = 8
WKV_CHUNK = 64
MIXER_TILE = 256
ADALN_TILE = 512
MODULATION_COLS = 1536
MOE_TILE = 256
MOE_UNIT = SUBLANES
MOE_TILE_ROWS = 768
MOE_BLOCK_UNITS = 32
VMEM_LIMIT = 56 * 2 ** 20


def _cparams(sem, flags=None):
    return pltpu.CompilerParams(dimension_semantics=sem, vmem_limit_bytes=VMEM_LIMIT, flags=flags)


def _dot(a, b):
    return jnp.dot(a.astype(BF16), b.astype(BF16), preferred_element_type=F32)


def _dot_nt(a, b):
    return lax.dot_general(a.astype(BF16), b.astype(BF16), (((1,), (1,)), ((), ())),
                           preferred_element_type=F32)


def _dot_tn(a, b):
    return jnp.dot(a.T.astype(BF16), b.astype(BF16), preferred_element_type=F32)


def _dot_split(x, e):
    hi = x.astype(BF16)
    lo = (x - hi.astype(F32)).astype(BF16)
    return jnp.dot(hi, e, preferred_element_type=F32) + jnp.dot(lo, e, preferred_element_type=F32)


def _pack_bf16_pairs(x):
    n = x.shape[1] // 2
    bits = pltpu.bitcast(x.astype(BF16).astype(F32), jnp.uint32)
    return jnp.bitwise_or(jnp.right_shift(bits[:, :n], jnp.uint32(16)), bits[:, n:])


def _unpack_bf16_pairs(word):
    lo = pltpu.bitcast(jnp.left_shift(word, jnp.uint32(16)), F32)
    hi = pltpu.bitcast(jnp.bitwise_and(word, jnp.uint32(0xFFFF0000)), F32)
    return jnp.concatenate([lo, hi], axis=1).astype(BF16)


def _sigmoid(x):
    return 1.0 / (1.0 + jnp.exp(-x))


def _layer_norm(x):
    mean = jnp.mean(x, axis=-1, keepdims=True)
    xc = x - mean
    var = jnp.mean(xc * xc, axis=-1, keepdims=True)
    return xc * lax.rsqrt(var + LN_EPS)


def _shift_rows(p, carry_row, n):
    row = lax.broadcasted_iota(I32, (p.shape[0], 1), 0)
    out = pltpu.roll(p, n, 0)
    for i in range(n):
        out = jnp.where(row == i, carry_row[SUBLANES - n + i:SUBLANES - n + i + 1, :], out)
    return out


def _mod_kernel(c_ref, w_ref, b_ref, o_ref):
    c = c_ref[...]
    a = c * _sigmoid(c)
    o_ref[0] = _dot(a, w_ref[0]) + b_ref[0]


def _modulation(c, w_ada, b_ada):
    depth, d, n = w_ada.shape
    bsz = c.shape[0]
    tn = MODULATION_COLS
    cp = jnp.zeros((SUBLANES, d), F32).at[:bsz].set(c)
    out = pl.pallas_call(
        _mod_kernel,
        grid=(depth, n // tn),
        in_specs=[pl.BlockSpec((SUBLANES, d), lambda l, j: (0, 0)),
                  pl.BlockSpec((1, d, tn), lambda l, j: (l, 0, j)),
                  pl.BlockSpec((1, 1, tn), lambda l, j: (l, 0, j))],
        out_specs=pl.BlockSpec((1, SUBLANES, tn), lambda l, j: (l, 0, j)),
        out_shape=jax.ShapeDtypeStruct((depth, SUBLANES, n), F32),
        compiler_params=_cparams(("parallel", "parallel")),
        name="adaln_modulation",
    )(cp, w_ada, b_ada.reshape(depth, 1, n))
    return out[:, :bsz]


def _adaln_kernel(x_ref, sh_ref, sc_ref, h_ref):
    h_ref[0] = (_layer_norm(x_ref[0]) * (1.0 + sc_ref[0]) + sh_ref[0]).astype(BF16)


def _adaln(x, shift, scale):
    bsz, s, d = x.shape
    tm = ADALN_TILE
    return pl.pallas_call(
        _adaln_kernel,
        grid=(bsz, s // tm),
        in_specs=[pl.BlockSpec((1, tm, d), lambda b, i: (b, i, 0)),
                  pl.BlockSpec((1, 1, d), lambda b, i: (b, 0, 0)),
                  pl.BlockSpec((1, 1, d), lambda b, i: (b, 0, 0))],
        out_specs=pl.BlockSpec((1, tm, d), lambda b, i: (b, i, 0)),
        out_shape=jax.ShapeDtypeStruct((bsz, s, d), BF16),
        compiler_params=_cparams(("parallel", "parallel")),
        name="adaln_input",
    )(x, shift[:, None, :], scale[:, None, :])


def _t5_bucket(rel):
    n = jnp.maximum(rel, 0)
    max_exact = N_BUCKETS // 2
    n_f = jnp.maximum(n, 1).astype(F32)
    large = max_exact + (jnp.log(n_f / max_exact) / math.log(WINDOW / max_exact)
                         * (N_BUCKETS - max_exact)).astype(I32)
    return jnp.where(n < max_exact, n, jnp.minimum(large, N_BUCKETS - 1))


def _swa_stages(x, first_tile, sink_ref, wq_ref, wkv_ref, bias_ref, kvc_ref, y_ref, b):
    tm = x.shape[0]
    n_heads = bias_ref.shape[0]
    kvw = wkv_ref.shape[1] // 2
    n_kv = kvw // HEAD_DIM
    rep = n_heads // n_kv
    blk = ATT_BLOCK
    q = _dot(x, wq_ref[...]) * (HEAD_DIM ** -0.5)
    kv = _dot(x, wkv_ref[...])
    kvext = jnp.concatenate([kvc_ref[b], kv], axis=0)
    kvc_ref[b] = kv[tm - blk:, :]
    col = lax.broadcasted_iota(I32, (blk, 2 * blk), 1)
    qb16 = q.astype(BF16)
    kv16 = kvext.astype(BF16)

    def scores(j):
        qb = qb16[j * blk:(j + 1) * blk]
        kw = kv16[j * blk:j * blk + 2 * blk, :kvw]
        kgs = [kw[:, gi * HEAD_DIM:(gi + 1) * HEAD_DIM] for gi in range(n_kv)]
        scs = [_dot_nt(qb[:, hh * HEAD_DIM:(hh + 1) * HEAD_DIM], kgs[hh // rep]) + bias_ref[hh]
               for hh in range(n_heads)]
        if j == 0:
            scs = [jnp.where(jnp.logical_and(first_tile, col < blk), NEG_INF, sc) for sc in scs]
        return scs

    def probs(scs):
        out = []
        for hh, sc in enumerate(scs):
            sink = sink_ref[hh]
            m = jnp.maximum(jnp.max(sc, axis=-1, keepdims=True), sink)
            e = jnp.exp(sc - m)
            den = jnp.sum(e, axis=-1, keepdims=True) + jnp.exp(sink - m)
            out.append((e / den).astype(BF16))
        return out

    def values(j, ps):
        vw = kv16[j * blk:j * blk + 2 * blk, kvw:]
        vgs = [vw[:, gi * HEAD_DIM:(gi + 1) * HEAD_DIM] for gi in range(n_kv)]
        outs = [jnp.dot(p, vgs[hh // rep], preferred_element_type=F32) for hh, p in enumerate(ps)]
        y_ref[b, j * blk:(j + 1) * blk, :] = jnp.concatenate(outs, axis=1).astype(BF16)

    return scores, lambda j, scs: values(j, probs(scs))


def _swa_bias_table(rel_bias):
    qi = jnp.arange(ATT_BLOCK)[:, None]
    kj = jnp.arange(2 * ATT_BLOCK)[None, :]
    rel = qi + ATT_BLOCK - kj
    valid = (rel >= 0) & (rel < WINDOW)
    onehot = (_t5_bucket(rel)[..., None] == jnp.arange(N_BUCKETS)).astype(F32)
    bias = jnp.einsum('qkb,bh->hqk', onehot, rel_bias.astype(F32), precision=lax.Precision.HIGHEST)
    return jnp.where(valid[None], bias, NEG_INF)


S5_GROUPS_PER_BLOCK = LANES // S5_CH


def _s5_tables(lam_re, lam_im, log_dt, b_re, b_im, c_re, c_im):
    n_groups, p = lam_re.shape
    lr, li = lam_re.astype(F32), lam_im.astype(F32)
    delta = jnp.exp(log_dt.astype(F32))[:, None]
    mag = jnp.exp(lr * delta)
    ab_re, ab_im = mag * jnp.cos(li * delta), mag * jnp.sin(li * delta)
    den = lr * lr + li * li
    z_re = ((ab_re - 1.0) * lr + ab_im * li) / den
    z_im = (ab_im * lr - (ab_re - 1.0) * li) / den
    br, bi = b_re.astype(F32), b_im.astype(F32)
    bb_re = z_re[..., None] * br - z_im[..., None] * bi
    bb_im = z_re[..., None] * bi + z_im[..., None] * br
    nblk = n_groups // S5_GROUPS_PER_BLOCK
    eye = jnp.eye(S5_GROUPS_PER_BLOCK, dtype=F32)

    def in_blocks(bb):
        bb = bb.reshape(nblk, S5_GROUPS_PER_BLOCK, p, S5_CH)
        return jnp.einsum('qgpc,gh->qgchp', bb, eye).reshape(nblk, LANES, S5_GROUPS_PER_BLOCK * p)

    def out_blocks(cc):
        cc = cc.astype(F32).reshape(nblk, S5_GROUPS_PER_BLOCK, S5_CH, p)
        return jnp.einsum('qgcp,gh->qgphc', cc, eye).reshape(nblk, S5_GROUPS_PER_BLOCK * p, LANES)

    def power(m):
        mg = jnp.exp(m * lr * delta)
        return (mg * jnp.cos(m * li * delta)).reshape(1, -1), (mg * jnp.sin(m * li * delta)).reshape(1, -1)

    row = jnp.arange(SUBLANES, dtype=F32)[:, None]
    tabs = []
    for sft in (1, 2, 4):
        pr, pi = power(float(sft))
        keep = row >= sft
        tabs += [jnp.where(keep, pr, 0.0), jnp.where(keep, pi, 0.0)]
    n_state = n_groups * p
    lrd = (lr * delta).reshape(1, n_state)
    lid = (li * delta).reshape(1, n_state)
    mg = jnp.exp((row + 1.0) * lrd)
    tabs += [mg * jnp.cos((row + 1.0) * lid), mg * jnp.sin((row + 1.0) * lid)]
    tables = jnp.stack(tabs, axis=0)
    return (in_blocks(bb_re).astype(BF16), in_blocks(bb_im).astype(BF16),
            out_blocks(c_re).astype(BF16), out_blocks(c_im).astype(BF16), tables)


def _s5_kernel(sink_ref, h_ref, w_ref, bre_ref, bim_ref, cre_ref, cim_ref, tab_ref, d_ref, gw_ref, gb_ref,
               wc_ref, cw_ref, wq_ref, wkv_ref, bias_ref, y_ref, yc_ref, ya_ref,
               xr_ref, xi_ref, cr_ref, ci_ref, cc_ref, kvc_ref):
    nseq = h_ref.shape[0]
    tm = h_ref.shape[1]
    nblk = bre_ref.shape[0]
    sw = bre_ref.shape[2]

    @pl.when(pl.program_id(0) == 0)
    def _():
        cr_ref[...] = jnp.zeros_like(cr_ref)
        ci_ref[...] = jnp.zeros_like(ci_ref)
        cc_ref[...] = jnp.zeros_like(cc_ref)
        kvc_ref[...] = jnp.zeros_like(kvc_ref)

    def project(b):
        u = _dot(h_ref[b], w_ref[...])
        ub = u.astype(BF16)
        for q in range(nblk):
            uq = ub[:, q * LANES:(q + 1) * LANES]
            xr_ref[b, :, q * sw:(q + 1) * sw] = jnp.dot(uq, bre_ref[q], preferred_element_type=F32)
            xi_ref[b, :, q * sw:(q + 1) * sw] = jnp.dot(uq, bim_ref[q], preferred_element_type=F32)
        return u

    def scan(b):
        cr = cr_ref[b, 0:1, :]
        ci = ci_ref[b, 0:1, :]
        for i in range(tm // SUBLANES):
            rows = slice(i * SUBLANES, (i + 1) * SUBLANES)
            xr = xr_ref[b, rows, :]
            xi = xi_ref[b, rows, :]
            for k, sft in enumerate((1, 2, 4)):
                mr = tab_ref[2 * k]
                mi = tab_ref[2 * k + 1]
                sr = pltpu.roll(xr, sft, 0)
                si = pltpu.roll(xi, sft, 0)
                xr, xi = xr + mr * sr - mi * si, xi + mr * si + mi * sr
            pr = tab_ref[6]
            pi = tab_ref[7]
            xr, xi = xr + pr * cr - pi * ci, xi + pr * ci + pi * cr
            xr_ref[b, rows, :] = xr
            xi_ref[b, rows, :] = xi
            cr = xr[SUBLANES - 1:SUBLANES, :]
            ci = xi[SUBLANES - 1:SUBLANES, :]
        cr_ref[b, 0:1, :] = cr
        ci_ref[b, 0:1, :] = ci

    def readout(b, u):
        ys = []
        for q in range(nblk):
            xr = xr_ref[b, :, q * sw:(q + 1) * sw].astype(BF16)
            xi = xi_ref[b, :, q * sw:(q + 1) * sw].astype(BF16)
            ys.append(jnp.dot(xr, cre_ref[q], preferred_element_type=F32)
                      - jnp.dot(xi, cim_ref[q], preferred_element_type=F32))
        y = jnp.concatenate(ys, axis=1) + d_ref[...] * u
        y = 0.5 * y * (1.0 + jnp.tanh(math.sqrt(2.0 / math.pi) * (y + 0.044715 * (y * y * y))))
        y_ref[b] = (y * _sigmoid(_dot(y, gw_ref[...]) + gb_ref[...])).astype(BF16)

    def conv(b):
        g = yc_ref.shape[2]
        p = _dot(h_ref[b], wc_ref[...])
        b_gate, c_gate, hh = p[:, :g], p[:, g:2 * g], p[:, 2 * g:]
        z = c_gate * hh
        carry = cc_ref[b]
        cw = cw_ref[...]
        out = cw[0:1] * _shift_rows(z, carry, 2) + cw[1:2] * _shift_rows(z, carry, 1) + cw[2:3] * z
        yc_ref[b] = (b_gate * out).astype(BF16)
        cc_ref[b] = z[tm - SUBLANES:, :]

    us = [project(b) for b in range(nseq)]
    first_tile = pl.program_id(0) == 0
    att = [_swa_stages(h_ref[b], first_tile, sink_ref, wq_ref, wkv_ref, bias_ref, kvc_ref, ya_ref, b)
           for b in range(nseq)]
    fillers = [functools.partial(conv, b) for b in range(nseq)]
    for j in range(tm // ATT_BLOCK):
        scs = [att[b][0](j) for b in range(nseq)]
        if fillers:
            fillers.pop(0)()
        for b in range(nseq):
            att[b][1](j, scs[b])
    for f in fillers:
        f()
    for b in range(nseq):
        scan(b)
        readout(b, us[b])


def _s5_conv_swa_mixers(h, w, lam_re, lam_im, log_dt, b_re, b_im, c_re, c_im, d_skip, glu_w, glu_b,
                        w_conv, conv_w, w_q, w_kv, sinks, rel_bias):
    bsz, s, d = h.shape
    bias = _swa_bias_table(rel_bias)
    g = w.shape[1]
    tm = MIXER_TILE
    bre, bim, cre, cim, tables = _s5_tables(lam_re, lam_im, log_dt, b_re, b_im, c_re, c_im)
    n_state = tables.shape[2]
    full = lambda a: pl.BlockSpec(a.shape, lambda i: (0,) * a.ndim)
    dvec = d_skip.astype(F32).reshape(1, g)
    gw = glu_w.astype(BF16)
    gb = glu_b.astype(F32).reshape(1, g)
    return pl.pallas_call(
        _s5_kernel,
        grid=(s // tm,),
        in_specs=[pl.BlockSpec(memory_space=pltpu.SMEM),
                  pl.BlockSpec((bsz, tm, d), lambda i: (0, i, 0)),
                  full(w), full(bre), full(bim), full(cre), full(cim), full(tables),
                  full(dvec), full(gw), full(gb), full(w_conv), full(conv_w),
                  full(w_q), full(w_kv), full(bias)],
        out_specs=[pl.BlockSpec((bsz, tm, g), lambda i: (0, i, 0)),
                   pl.BlockSpec((bsz, tm, g), lambda i: (0, i, 0)),
                   pl.BlockSpec((bsz, tm, w_q.shape[1]), lambda i: (0, i, 0))],
        out_shape=[jax.ShapeDtypeStruct((bsz, s, g), BF16), jax.ShapeDtypeStruct((bsz, s, g), BF16),
                   jax.ShapeDtypeStruct((bsz, s, w_q.shape[1]), BF16)],
        scratch_shapes=[pltpu.VMEM((bsz, tm, n_state), F32), pltpu.VMEM((bsz, tm, n_state), F32),
                        pltpu.VMEM((bsz, SUBLANES, n_state), F32), pltpu.VMEM((bsz, SUBLANES, n_state), F32),
                        pltpu.VMEM((bsz, SUBLANES, g), F32), pltpu.VMEM((bsz, ATT_BLOCK, w_kv.shape[1]), F32)],
        compiler_params=_cparams(("arbitrary",)),
        name="s5_conv_swa_mixers",
    )(sinks.astype(F32), h, w, bre, bim, cre, cim, tables, dvec, gw, gb, w_conv, conv_w, w_q, w_kv, bias)


def _rwkv_kernel(h_ref, wrkv_ref, wlo_ref, mu1_ref, mu2_ref, w0_ref, w2_ref, a0_ref, a2_ref, g2_ref,
                 kk_ref, ka_ref, rk_ref, gng_ref, gnb_ref, eblk_ref,
                 y_ref,
                 cp_ref, cl_ref, hs_ref, r_s, k_s, v_s, a_s, b_s, ld_s):
    tm = h_ref.shape[1]
    g = y_ref.shape[2]
    npair = g // LANES
    ch = WKV_CHUNK

    @pl.when(pl.program_id(1) == 0)
    def _():
        cp_ref[...] = jnp.zeros_like(cp_ref)
        cl_ref[...] = jnp.zeros_like(cl_ref)
        hs_ref[...] = jnp.zeros_like(hs_ref)

    x = h_ref[0]
    p = _dot(x, wrkv_ref[...])
    plo = _dot(x, wlo_ref[...])
    pprev = _shift_rows(p, cp_ref[...], 1)
    lprev = _shift_rows(plo, cl_ref[...], 1)
    cp_ref[...] = p[tm - SUBLANES:, :]
    cl_ref[...] = plo[tm - SUBLANES:, :]
    p = p + (pprev - p) * mu1_ref[...]
    plo = plo + (lprev - plo) * mu2_ref[...]
    r, k, v = p[:, :g], p[:, g:2 * g], p[:, 2 * g:]
    w_lo, a_lo, g_lo = plo[:, :LANES], plo[:, LANES:2 * LANES], plo[:, 2 * LANES:]
    wraw = w0_ref[...] + _dot(jnp.tanh(w_lo), w2_ref[...])
    nz = -wraw
    softplus = jnp.maximum(nz, 0.0) + jnp.log(1.0 + jnp.exp(-jnp.abs(nz)))
    w = -softplus - 0.5
    ld_s[...] = -jnp.exp(w)
    a = _sigmoid(a0_ref[...] + _dot(a_lo, a2_ref[...]))
    gate = _dot(_sigmoid(g_lo), g2_ref[...])
    eblk = eblk_ref[...]
    kk = k * kk_ref[...]
    kk = kk / jnp.maximum(jnp.sqrt(_dot_split(kk * kk, eblk)), 1e-12)
    k = k * (1.0 + (a - 1.0) * ka_ref[...])
    r_s[...] = r
    k_s[...] = k
    v_s[...] = v
    a_s[...] = -kk
    b_s[...] = kk * a

    lane = lax.broadcasted_iota(I32, (1, LANES), 1)
    m0 = (lane < HEAD_DIM).astype(F32)
    m1 = 1.0 - m0
    ri = lax.broadcasted_iota(I32, (2 * ch, 2 * ch), 0)
    ci = lax.broadcasted_iota(I32, (2 * ch, 2 * ch), 1)
    same = (ri < ch) == (ci < ch)
    rloc = jnp.bitwise_and(ri, ch - 1)
    cloc = jnp.bitwise_and(ci, ch - 1)
    strict = jnp.where(jnp.logical_and(same, cloc < rloc), 1.0, 0.0)
    incl = jnp.where(jnp.logical_and(same, cloc <= rloc), 1.0, 0.0)
    eye = jnp.where(ri == ci, 1.0, 0.0)
    tri = jnp.where(lax.broadcasted_iota(I32, (ch, ch), 1) <= lax.broadcasted_iota(I32, (ch, ch), 0),
                    1.0, 0.0).astype(BF16)

    def bd(t):
        return jnp.concatenate([t * m0, t * m1], axis=0)

    nchunk = tm // ch
    per_chunk = []
    for c in range(nchunk):
        rows = slice(c * ch, (c + 1) * ch)
        ld = ld_s[rows, :]
        ld_hi = ld.astype(BF16)
        ld_lo = (ld - ld_hi.astype(F32)).astype(BF16)
        cum = (jnp.dot(tri, ld_hi, preferred_element_type=F32)
               + jnp.dot(tri, ld_lo, preferred_element_type=F32))
        gam = jnp.exp(cum)
        ginv = jnp.exp(-cum)
        per_chunk.append(dict(at=a_s[rows, :] * jnp.exp(cum - ld), rt=r_s[rows, :] * gam,
                              bt=b_s[rows, :] * ginv, kt=k_s[rows, :] * ginv, v=v_s[rows, :],
                              gl=gam[ch - 1:ch, :]))
    inst = [(c, q) for c in range(nchunk) for q in range(npair)]

    def part(name):
        return [per_chunk[c][name][:, q * LANES:(q + 1) * LANES] for c, q in inst]

    bt, kt, gl = part("bt"), part("kt"), part("gl")
    at_bd = [bd(t) for t in part("at")]
    rt_bd = [bd(t) for t in part("rt")]
    v_bd = [bd(t) for t in part("v")]
    bh_t = [bd(b * g_).T for b, g_ in zip(bt, gl)]
    kh_t = [bd(k_ * g_).T for k_, g_ in zip(kt, gl)]
    gmat = [_dot_nt(jnp.concatenate([a_, r_], axis=0), jnp.concatenate([b, b, k_, k_], axis=0))
            for a_, r_, b, k_ in zip(at_bd, rt_bd, bt, kt)]
    n_ab = [gm[:2 * ch, :2 * ch] * strict for gm in gmat]
    a_ak = [gm[:2 * ch, 2 * ch:] * strict for gm in gmat]
    m_rb = [gm[2 * ch:, :2 * ch] * incl for gm in gmat]
    m_rk = [gm[2 * ch:, 2 * ch:] * incl for gm in gmat]
    tinv = [eye + n for n in n_ab]
    npow = n_ab
    for step in range(1, 6):
        if step == 1:
            npow = [_dot(n, n) for n in npow]
        both = [_dot(n, jnp.concatenate([t, n], axis=1)) for n, t in zip(npow, tinv)]
        tinv = [t + b[:, :2 * ch] for t, b in zip(tinv, both)]
        npow = [b[:, 2 * ch:] for b in both]
    va = [_dot(jnp.concatenate([a_, k_, m_], axis=0), v_) for a_, k_, m_, v_ in zip(a_ak, kh_t, m_rk, v_bd)]
    wu = [_dot(t, jnp.concatenate([a_, x_[:2 * ch]], axis=1)) for t, a_, x_ in zip(tinv, at_bd, va)]
    pq = [_dot(jnp.concatenate([b, m_], axis=0), w_) for b, m_, w_ in zip(bh_t, m_rb, wu)]
    pmat = [eye * g_ + t[:2 * ch, :2 * ch] for g_, t in zip(gl, pq)]
    qmat = [t[:2 * ch, 2 * ch:] + x_[2 * ch:4 * ch] for t, x_ in zip(pq, va)]
    ry = [r_ + t[2 * ch:, :2 * ch] for r_, t in zip(rt_bd, pq)]
    y0 = [t[2 * ch:, 2 * ch:] + x_[4 * ch:] for t, x_ in zip(pq, va)]
    state = [hs_ref[q] for q in range(npair)]
    y_chunks = []
    for c in range(nchunk):
        ids = [c * npair + q for q in range(npair)]
        both = [_dot(jnp.concatenate([ry[i], pmat[i]], axis=0), st) for i, st in zip(ids, state)]
        yy = [t[:2 * ch] + y0[i] for i, t in zip(ids, both)]
        state = [t[2 * ch:] + qmat[i] for i, t in zip(ids, both)]
        y_chunks.append(jnp.concatenate([t[:ch] + t[ch:] for t in yy], axis=1))
    hs_ref[...] = jnp.stack(state, axis=0)

    y = jnp.concatenate(y_chunks, axis=0)
    inv_n = 1.0 / HEAD_DIM
    mean = _dot_split(y, eblk) * inv_n
    yc = y - mean
    var = _dot_split(yc * yc, eblk) * inv_n
    yn = yc * lax.rsqrt(var + RWKV_GN_EPS) * gng_ref[...] + gnb_ref[...]
    r = r_s[...]
    k = k_s[...]
    v = v_s[...]
    bonus = _dot_split(r * k * rk_ref[...], eblk) * v
    y_ref[0] = ((yn + bonus) * gate).astype(BF16)


def _rwkv_mixer(h, w_rkv, w_lora, mu, w0, w2, a0, a2, g2, k_k, k_a, r_k, gn_g, gn_b):
    bsz, s, d = h.shape
    g = w0.shape[0]
    tm = MIXER_TILE
    row = lambda t: t.astype(F32).reshape(1, -1)
    pad_rows = lambda t: jnp.zeros((LANES, g), F32).at[:t.shape[0]].set(t.astype(F32)).astype(BF16)
    mu1 = row(mu[:3 * g])
    mu2 = jnp.concatenate([
        jnp.zeros((LANES,), F32).at[:RWKV_DECAY_RANK].set(mu[3 * g:3 * g + RWKV_DECAY_RANK]),
        jnp.zeros((LANES,), F32).at[:RWKV_A_RANK].set(mu[3 * g + RWKV_DECAY_RANK:3 * g + RWKV_DECAY_RANK + RWKV_A_RANK]),
        mu[3 * g + RWKV_DECAY_RANK + RWKV_A_RANK:]]).reshape(1, -1)
    head = np.arange(g) // HEAD_DIM
    eblk = jnp.asarray(head[:, None] == head[None, :], BF16)
    args = (h, w_rkv, w_lora, mu1, mu2, row(w0), pad_rows(w2), row(a0), pad_rows(a2), g2.astype(BF16),
            row(k_k), row(k_a), row(r_k), row(gn_g), row(gn_b), eblk)
    full = lambda a: pl.BlockSpec(a.shape, lambda b, i: (0,) * a.ndim)
    return pl.pallas_call(
        _rwkv_kernel,
        grid=(bsz, s // tm),
        in_specs=[pl.BlockSpec((1, tm, d), lambda b, i: (b, i, 0))] + [full(a) for a in args[1:]],
        out_specs=pl.BlockSpec((1, tm, g), lambda b, i: (b, i, 0)),
        out_shape=jax.ShapeDtypeStruct((bsz, s, g), BF16),
        scratch_shapes=[pltpu.VMEM((SUBLANES, 3 * g), F32), pltpu.VMEM((SUBLANES, 3 * LANES), F32),
                        pltpu.VMEM((g // LANES, 2 * WKV_CHUNK, LANES), F32)]
                       + [pltpu.VMEM((tm, g), F32) for _ in range(6)],
        compiler_params=_cparams(("parallel", "arbitrary")),
        name="rwkv7_mixer",
    )(*args)


def _mixout_kernel(alpha, ya_ref, yb_ref, yc_ref, yd_ref, wo_ref, x_ref, gt_ref, lng_ref, lnb_ref,
                   sh_ref, sc_ref, wr_ref, br_ref, x1_ref, xs_ref, info_ref, seg_ref, h2s_ref, lgs_ref):
    g = ya_ref.shape[1]

    @pl.when(pl.program_id(0) == 0)
    def _():
        h2s_ref[...] = jnp.zeros_like(h2s_ref)
        lgs_ref[...] = jnp.zeros_like(lgs_ref)

    h2_prev = h2s_ref[...]
    lg_prev = lgs_ref[...]
    y = (jnp.dot(ya_ref[...], wo_ref[0:g, :], preferred_element_type=F32)
         + jnp.dot(yb_ref[...], wo_ref[g:2 * g, :], preferred_element_type=F32)
         + jnp.dot(yc_ref[...], wo_ref[2 * g:3 * g, :], preferred_element_type=F32)
         + jnp.dot(yd_ref[...], wo_ref[3 * g:, :], preferred_element_type=F32))
    _route_sort(h2_prev, lg_prev, xs_ref, info_ref, seg_ref)
    x1 = _layer_norm(alpha * x_ref[...] + (1.0 + gt_ref[0]) * y) * lng_ref[...] + lnb_ref[...]
    x1_ref[...] = x1
    h2 = _layer_norm(x1) * (1.0 + sc_ref[0]) + sh_ref[0]
    h2s_ref[...] = h2.astype(BF16)
    lgs_ref[...] = _dot(h2, wr_ref[...]) + br_ref[...]


def _mix_out(alpha, ys, w_out, x, gate, ln_g, ln_b, shift2, scale2, w_router, b_router):
    bsz, s, d = x.shape
    g = ys[0].shape[2]
    tm = MOE_TILE
    per_seq = s // tm
    nt = bsz * per_seq
    cur = lambda t: jnp.minimum(t, nt - 1)
    prev = lambda t: jnp.maximum(t - 1, 0)
    tok = lambda w: pl.BlockSpec((tm, w), lambda t: (cur(t), 0))
    per_b = pl.BlockSpec((1, 1, d), lambda t: (cur(t) // per_seq, 0, 0))
    full = lambda a: pl.BlockSpec(a.shape, lambda t: (0,) * a.ndim)
    row = lambda v: v.astype(F32).reshape(1, -1)
    args = (*[v.reshape(bsz * s, g) for v in ys], w_out, x.reshape(bsz * s, d), gate[:, None, :], row(ln_g),
            row(ln_b), shift2[:, None, :], scale2[:, None, :], w_router, b_router)
    x1, xs, info, seg = pl.pallas_call(
        functools.partial(_mixout_kernel, alpha),
        grid=(nt + 1,),
        in_specs=[tok(g)] * 4 + [full(w_out), tok(d), per_b, full(args[7]), full(args[8]), per_b, per_b,
                                 full(w_router), full(b_router)],
        out_specs=[tok(d),
                   pl.BlockSpec((MOE_TILE_ROWS, d // 2 + LANES), lambda t: (prev(t), 0)),
                   pl.BlockSpec((tm, LANES), lambda t: (prev(t), 0)),
                   pl.BlockSpec((1, SUBLANES, LANES), lambda t: (prev(t), 0, 0))],
        out_shape=[jax.ShapeDtypeStruct((bsz * s, d), F32),
                   jax.ShapeDtypeStruct((nt * MOE_TILE_ROWS, d // 2 + LANES), jnp.uint32),
                   jax.ShapeDtypeStruct((bsz * s, LANES), I32),
                   jax.ShapeDtypeStruct((nt, SUBLANES, LANES), I32)],
        scratch_shapes=[pltpu.VMEM((tm, d), BF16), pltpu.VMEM((tm, LANES), F32)],
        compiler_params=_cparams(("arbitrary",)),
        name="mix_out_route_sort",
    )(*args)
    return x1.reshape(bsz, s, d), xs, info, seg


def _route_sort(h2, lg, xs_ref, info_ref, seg_ref):
    tm = lg.shape[0]
    rt = xs_ref.shape[0]
    half = h2.shape[1] // 2
    lane = lax.broadcasted_iota(I32, (tm, LANES), 1)
    lane_f = lane.astype(F32)

    def top1(vals, mask):
        mv = jnp.where(mask, vals, -jnp.inf)
        m = jnp.max(mv, axis=-1, keepdims=True)
        idx = jnp.min(jnp.where(jnp.logical_and(mask, mv == m), lane_f, float(LANES)), axis=-1, keepdims=True)
        return m, idx.astype(I32)

    gmask = lane < N_EXPERT_GROUPS
    gm, gidx = top1(lg, gmask)
    g_val = 1.0 / jnp.sum(jnp.where(gmask, jnp.exp(lg - gm), 0.0), axis=-1, keepdims=True)
    elo = N_EXPERT_GROUPS + gidx * EXPERTS_PER_GROUP
    emask = jnp.logical_and(lane >= elo, lane < elo + EXPERTS_PER_GROUP)
    m1, i1 = top1(lg, emask)
    m2, i2 = top1(lg, jnp.logical_and(emask, lane != i1))
    e21 = jnp.exp(m2 - m1)
    w1 = g_val / (1.0 + e21)
    w2 = g_val * e21 / (1.0 + e21)
    e1 = i1 - N_EXPERT_GROUPS
    e2 = i2 - N_EXPERT_GROUPS
    oh1 = (lane == e1)
    oh2 = (lane == e2)
    ohs = jnp.where(jnp.logical_or(oh1, oh2), 1.0, 0.0)
    cnt = jnp.sum(ohs, axis=0, keepdims=True)
    units = jnp.floor((cnt + (MOE_UNIT - 1.0)) * (1.0 / MOE_UNIT))
    li = lax.broadcasted_iota(I32, (LANES, LANES), 0)
    lj = lax.broadcasted_iota(I32, (LANES, LANES), 1)
    upper = jnp.where(li < lj, 1.0, 0.0).astype(BF16)
    ustart = jnp.dot(jnp.broadcast_to(units, (SUBLANES, LANES)).astype(BF16), upper,
                     preferred_element_type=F32)[0:1, :]
    ri = lax.broadcasted_iota(I32, (tm, tm), 0)
    ci = lax.broadcasted_iota(I32, (tm, tm), 1)
    tri = jnp.where(ci < ri, 1.0, 0.0).astype(BF16)
    before = jnp.dot(tri, ohs.astype(BF16), preferred_element_type=F32)
    first = before + MOE_UNIT * ustart
    pos1 = jnp.sum(jnp.where(oh1, first, 0.0), axis=-1, keepdims=True)
    pos2 = jnp.sum(jnp.where(oh2, first, 0.0), axis=-1, keepdims=True)
    posm = jnp.where(lane == 0, pos1, jnp.where(lane == 1, pos2, -1.0))
    post = posm.T
    prow = lax.broadcasted_iota(I32, (rt, tm), 0).astype(F32)
    sel1 = prow == post[0:1, :]
    sel2 = prow == post[1:2, :]
    hb = h2.astype(BF16)
    xs = jnp.dot(jnp.where(jnp.logical_or(sel1, sel2), 1.0, 0.0).astype(BF16), hb, preferred_element_type=F32)
    bits = pltpu.bitcast(xs, jnp.uint32)
    xs_ref[:, :half] = jnp.bitwise_or(jnp.right_shift(bits[:, :half], jnp.uint32(16)), bits[:, half:])
    def terms(w):
        a = w.astype(BF16).astype(F32)
        b = (w - a).astype(BF16).astype(F32)
        return a, b, ((w - a) - b).astype(BF16).astype(F32)
    t1 = terms(w1)
    t2 = terms(w2)
    wm = jnp.zeros((tm, LANES), F32)
    for k, t in enumerate(t1 + t2):
        wm = jnp.where(lane == k, t, wm)
    wmb = wm.astype(BF16)
    s1 = jnp.dot(jnp.where(sel1, 1.0, 0.0).astype(BF16), wmb, preferred_element_type=F32)
    s2 = jnp.dot(jnp.where(sel2, 1.0, 0.0).astype(BF16), wmb, preferred_element_type=F32)
    wrow = (s1[:, 0:1] + s1[:, 1:2] + s1[:, 2:3]) + (s2[:, 3:4] + s2[:, 4:5] + s2[:, 5:6])
    mlane = lax.broadcasted_iota(I32, (rt, LANES), 1)
    wbits = pltpu.bitcast(jnp.broadcast_to(wrow, (rt, LANES)), jnp.uint32)
    xs_ref[:, half:] = jnp.where(mlane == 0, wbits, jnp.uint32(0))
    info_ref[...] = jnp.where(lane == 0, pos1, jnp.where(lane == 1, pos2, 0.0)).astype(I32)
    srow = lax.broadcasted_iota(I32, (SUBLANES, LANES), 0)
    total = jnp.sum(units, axis=-1, keepdims=True)
    seg = jnp.where(srow == 0, units, jnp.where(srow == 1, ustart, jnp.where(srow == 2, total, 0.0)))
    seg_ref[0] = seg.astype(I32)


def _worklist_vec_kernel(units_ref, ustart_ref, uidx_ref, slot_ref, be_ref, nxt_ref, nb_ref):
    nt = units_ref.shape[0]
    nbk = uidx_ref.shape[0]
    tile_units = MOE_TILE_ROWS // MOE_UNIT
    bu = MOE_BLOCK_UNITS
    u = units_ref[...].astype(F32)
    us = ustart_ref[...].astype(F32)
    ri = lax.broadcasted_iota(I32, (nt, nt), 0)
    ci = lax.broadcasted_iota(I32, (nt, nt), 1)
    tri = jnp.where(ci < ri, 1.0, 0.0).astype(BF16)
    cex = jnp.dot(tri, u.astype(BF16), preferred_element_type=F32)
    cin = cex + u
    tot = jnp.sum(u, axis=0, keepdims=True)
    nblk = jnp.floor((tot + (bu - 1.0)) * (1.0 / bu))
    li = lax.broadcasted_iota(I32, (LANES, LANES), 0)
    lj = lax.broadcasted_iota(I32, (LANES, LANES), 1)
    upper = jnp.where(li < lj, 1.0, 0.0).astype(BF16)
    b0 = jnp.dot(jnp.broadcast_to(nblk, (SUBLANES, LANES)).astype(BF16), upper,
                 preferred_element_type=F32)[0:1, :]
    b0in = b0 + nblk
    used = jnp.sum(nblk, axis=-1, keepdims=True)
    cand = jnp.where(jnp.logical_and(lj > li, jnp.broadcast_to(nblk, (LANES, LANES)) > 0.0),
                     lj.astype(F32), 999.0)
    nxt_col = jnp.min(cand, axis=-1, keepdims=True)
    nxt_col = jnp.where(nxt_col > 900.0, -1.0, nxt_col)

    ul = lax.broadcasted_iota(I32, (nt, LANES), 1).astype(F32)
    slot = jnp.full((nt, LANES), -1.0, F32)
    for e in range(N_EXPERTS):
        st = us[:, e:e + 1]
        sel = jnp.logical_and(ul >= st, ul < st + u[:, e:e + 1])
        slot = jnp.where(sel, bu * b0[:, e:e + 1] + cex[:, e:e + 1] + (ul - st), slot)
    slot_ref[...] = slot.astype(I32)

    def by_tile(t):
        return jnp.concatenate([t, jnp.zeros((LANES - nt, LANES), F32)], axis=0).T

    cin_t, cex_t, us_t = by_tile(cin), by_tile(cex), by_tile(us)
    brow = lax.broadcasted_iota(I32, (nbk, 1), 0).astype(F32)
    eb = jnp.zeros((nbk, 1), F32)
    b0b = jnp.zeros((nbk, 1), F32)
    totb = jnp.zeros((nbk, 1), F32)
    nxtb = jnp.full((nbk, 1), -1.0, F32)
    cin_row = jnp.zeros((nbk, LANES), F32)
    cex_row = jnp.zeros((nbk, LANES), F32)
    us_row = jnp.zeros((nbk, LANES), F32)
    for e in range(N_EXPERTS):
        ine = jnp.logical_and(brow >= b0[:, e:e + 1], brow < b0in[:, e:e + 1])
        eb = jnp.where(ine, float(e), eb)
        b0b = jnp.where(ine, b0[:, e:e + 1], b0b)
        totb = jnp.where(ine, tot[:, e:e + 1], totb)
        nxtb = jnp.where(ine, nxt_col[e:e + 1, :], nxtb)
        cin_row = jnp.where(ine, cin_t[e:e + 1, :], cin_row)
        cex_row = jnp.where(ine, cex_t[e:e + 1, :], cex_row)
        us_row = jnp.where(ine, us_t[e:e + 1, :], us_row)
    active = brow < used
    jl = lax.broadcasted_iota(I32, (1, LANES), 1).astype(F32)
    q = (brow - b0b) * bu + jl
    tau = jnp.zeros((nbk, LANES), F32)
    for t in range(nt):
        tau = tau + jnp.where(cin_row[:, t:t + 1] <= q, 1.0, 0.0)
    base = jnp.zeros((nbk, LANES), F32)
    for t in range(nt):
        base = jnp.where(tau == float(t), us_row[:, t:t + 1] - cex_row[:, t:t + 1] + float(t * tile_units), base)
    ok = jnp.logical_and(jnp.logical_and(active, q < totb), jl < bu)
    uidx_ref[...] = jnp.where(ok, base + q, -1.0).astype(I32)
    be_ref[...] = jnp.broadcast_to(jnp.where(active, eb, N_EXPERTS - 1.0), (nbk, LANES)).astype(I32)
    nxt_ref[...] = jnp.broadcast_to(jnp.where(active, nxtb, -1.0), (nbk, LANES)).astype(I32)
    nb_ref[...] = jnp.broadcast_to(used, (SUBLANES, LANES)).astype(I32)


def _worklist_vec(units2, ustart2, n_blocks):
    nt = units2.shape[0]
    tile_units = MOE_TILE_ROWS // MOE_UNIT
    uidx, slot, be, nxt, nb = pl.pallas_call(
        _worklist_vec_kernel,
        out_shape=[jax.ShapeDtypeStruct((n_blocks, LANES), I32), jax.ShapeDtypeStruct((nt, LANES), I32),
                   jax.ShapeDtypeStruct((n_blocks, LANES), I32), jax.ShapeDtypeStruct((n_blocks, LANES), I32),
                   jax.ShapeDtypeStruct((SUBLANES, LANES), I32)],
        name="moe_worklist",
    )(units2, ustart2)
    return (uidx[:, :MOE_BLOCK_UNITS].reshape(n_blocks * MOE_BLOCK_UNITS),
            slot[:, :tile_units].reshape(nt * tile_units), be[:, 0], nxt[:, 0], nb[0, :1])


def _expert_kernel(layer, be_ref, nxt_ref, uidx_ref, nb_ref, xs_hbm, w1_hbm, w3_hbm, w2_hbm, o_ref,
                   xbuf, gsem, w1f, w3f, w2f, wsem, wslot, w1b, w3b, w2b):
    b = pl.program_id(0)
    used = nb_ref[0]
    half = xs_hbm.shape[1] - LANES
    slot = lax.rem(b, 2)
    other = 1 - slot

    def weight_copies(e, s):
        return (pltpu.make_async_copy(w1_hbm.at[layer, e], w1f.at[s], wsem.at[s]),
                pltpu.make_async_copy(w3_hbm.at[layer, e], w3f.at[s], wsem.at[s]),
                pltpu.make_async_copy(w2_hbm.at[layer, e], w2f.at[s], wsem.at[s]))

    def gather_copy(s, j, unit):
        return pltpu.make_async_copy(xs_hbm.at[pl.ds(pl.multiple_of(unit * MOE_UNIT, MOE_UNIT), MOE_UNIT), :],
                                     xbuf.at[s, pl.ds(j * MOE_UNIT, MOE_UNIT), :], gsem.at[s])

    def gather_start(blk, s):
        for j in range(MOE_BLOCK_UNITS):
            gather_copy(s, j, jnp.maximum(uidx_ref[blk * MOE_BLOCK_UNITS + j], 0)).start(priority=j % 2)

    def gather_wait(s):
        for j in range(MOE_BLOCK_UNITS):
            gather_copy(s, j, 0).wait()

    @pl.when(b == 0)
    def _():
        gather_start(0, 0)
        wslot[0] = 1
        for cp in weight_copies(be_ref[0], 0):
            cp.start()

    @pl.when(b + 1 < used)
    def _():
        gather_start(b + 1, other)

    @pl.when(b >= used)
    def _():
        o_ref[...] = jnp.zeros_like(o_ref)

    @pl.when(b < used)
    def _():
        prev = be_ref[jnp.maximum(b - 1, 0)]

        @pl.when(jnp.logical_or(b == 0, be_ref[b] != prev))
        def _():
            s = 1 - wslot[0]
            wslot[0] = s
            for cp in weight_copies(0, s):
                cp.wait()
            w1b[...] = w1f[s].astype(BF16)
            w3b[...] = w3f[s].astype(BF16)
            w2b[...] = w2f[s].astype(BF16)

            @pl.when(nxt_ref[b] >= 0)
            def _():
                for cp in weight_copies(nxt_ref[b], 1 - s):
                    cp.start()

        gather_wait(slot)
        xw = xbuf[slot]
        x = _unpack_bf16_pairs(xw[:, :half])
        wrow = pltpu.bitcast(xw[:, half:], F32)[:, 0:1]
        a = jnp.dot(x, w1b[...], preferred_element_type=F32)
        gte = jnp.dot(x, w3b[...], preferred_element_type=F32)
        mid = (a * _sigmoid(a)) * gte
        o_ref[...] = _pack_bf16_pairs(jnp.dot(mid.astype(BF16), w2b[...], preferred_element_type=F32) * wrow)


def _expert_ffn(layer, xs, blk_expert, blk_next, unit_idx, n_used, w1, w3, w2):
    d, de = w1.shape[2], w1.shape[3]
    nb = blk_expert.shape[0]
    rows = MOE_BLOCK_UNITS * MOE_UNIT
    hbm = pl.BlockSpec(memory_space=pl.ANY)
    grid_spec = pltpu.PrefetchScalarGridSpec(
        num_scalar_prefetch=4,
        grid=(nb,),
        in_specs=[hbm, hbm, hbm, hbm],
        out_specs=pl.BlockSpec((rows, d // 2), lambda b, be, nx, ui, nu: (b, 0)),
        scratch_shapes=[pltpu.VMEM((2, rows, xs.shape[1]), jnp.uint32), pltpu.SemaphoreType.DMA((2,)),
                        pltpu.VMEM((2, d, de), F32), pltpu.VMEM((2, d, de), F32), pltpu.VMEM((2, de, d), F32),
                        pltpu.SemaphoreType.DMA((2,)), pltpu.SMEM((1,), I32),
                        pltpu.VMEM((d, de), BF16), pltpu.VMEM((d, de), BF16), pltpu.VMEM((de, d), BF16)],
    )
    return pl.pallas_call(
        functools.partial(_expert_kernel, layer),
        grid_spec=grid_spec,
        out_shape=jax.ShapeDtypeStruct((nb * rows, d // 2), jnp.uint32),
        compiler_params=_cparams(("arbitrary",)),
        name="moe_expert_ffn",
    )(blk_expert, blk_next, unit_idx, n_used, xs, w1, w3, w2)


def _combine_kernel(alpha, with_next, nu_ref, slot_ref, ys_hbm, info_ref, x_ref, gt_ref, lng_ref, lnb_ref, *rest):
    if with_next:
        sh_ref, sc_ref, o_ref, hn_ref, ybuf, sem = rest
    else:
        o_ref, ybuf, sem = rest
    i = pl.program_id(0)
    nt = pl.num_programs(0)
    tm = x_ref.shape[0]
    rt = ybuf.shape[1]
    slot = lax.rem(i, 2)
    other = 1 - slot

    def unit_copy(tile, s, j):
        src = pl.multiple_of(slot_ref[tile * (rt // MOE_UNIT) + j] * MOE_UNIT, MOE_UNIT)
        dst = pl.multiple_of(j * MOE_UNIT, MOE_UNIT)
        return pltpu.make_async_copy(ys_hbm.at[pl.ds(src, MOE_UNIT), :], ybuf.at[s, pl.ds(dst, MOE_UNIT), :], sem.at[s])

    def start(tile, s):
        def body(j, c):
            unit_copy(tile, s, j).start()
            return c
        lax.fori_loop(0, nu_ref[tile], body, 0)

    def wait(tile, s):
        def body(j, c):
            unit_copy(tile, s, j).wait()
            return c
        lax.fori_loop(0, nu_ref[tile], body, 0)

    @pl.when(i == 0)
    def _():
        ybuf[...] = jnp.zeros_like(ybuf)
        start(0, 0)

    @pl.when(i + 1 < nt)
    def _():
        start(i + 1, other)

    wait(i, slot)
    info = info_ref[...]
    col = lax.broadcasted_iota(I32, (tm, rt), 1)
    pick = jnp.where(jnp.logical_or(col == info[:, 0:1], col == info[:, 1:2]), 1.0, 0.0).astype(BF16)
    y = jnp.dot(pick, _unpack_bf16_pairs(ybuf[slot]), preferred_element_type=F32)
    x2 = _layer_norm(alpha * x_ref[...] + (1.0 + gt_ref[0]) * y) * lng_ref[...] + lnb_ref[...]
    o_ref[...] = x2
    if with_next:
        hn_ref[...] = (_layer_norm(x2) * (1.0 + sc_ref[0]) + sh_ref[0]).astype(BF16)


def _combine(alpha, ys, tile_units, unit_slot, info, x1, gate, ln_g, ln_b, seq, next_mod=None):
    t, d = x1.shape
    tm = MOE_TILE
    per_seq = seq // tm
    tok = pl.BlockSpec((tm, d), lambda i, nu, us: (i, 0))
    per_b = pl.BlockSpec((1, 1, d), lambda i, nu, us: (i // per_seq, 0, 0))
    row = pl.BlockSpec((1, d), lambda i, nu, us: (0, 0))
    with_next = next_mod is not None
    args = [tile_units, unit_slot, ys, info, x1, gate[:, None, :], ln_g.astype(F32).reshape(1, d),
            ln_b.astype(F32).reshape(1, d)]
    in_specs = [pl.BlockSpec(memory_space=pl.ANY), pl.BlockSpec((tm, LANES), lambda i, nu, us: (i, 0)),
                tok, per_b, row, row]
    out_specs, out_shape = tok, jax.ShapeDtypeStruct((t, d), F32)
    if with_next:
        args += [next_mod[0][:, None, :], next_mod[1][:, None, :]]
        in_specs += [per_b, per_b]
        out_specs, out_shape = [tok, tok], [out_shape, jax.ShapeDtypeStruct((t, d), BF16)]
    grid_spec = pltpu.PrefetchScalarGridSpec(
        num_scalar_prefetch=2,
        grid=(t // tm,),
        in_specs=in_specs,
        out_specs=out_specs,
        scratch_shapes=[pltpu.VMEM((2, MOE_TILE_ROWS, d // 2), jnp.uint32), pltpu.SemaphoreType.DMA((2,))],
    )
    return pl.pallas_call(
        functools.partial(_combine_kernel, alpha, with_next),
        grid_spec=grid_spec,
        out_shape=out_shape,
        compiler_params=_cparams(("arbitrary",)),
        name="moe_combine_ln",
    )(*args)


def _moe(layer, alpha, xs, info, seg, x1, gate, ln_g, ln_b, w1, w3, w2, next_mod):
    bsz, s, d = x1.shape
    t = bsz * s
    nt = t // MOE_TILE
    tile_units = seg[:, 2, 0]
    max_units = nt * (TOP_K * MOE_TILE // MOE_UNIT + N_EXPERTS * (MOE_UNIT - 1) // MOE_UNIT)
    n_blocks = max_units // MOE_BLOCK_UNITS + N_EXPERTS
    unit_idx, unit_slot, blk_expert, blk_next, n_used = _worklist_vec(seg[:, 0, :], seg[:, 1, :], n_blocks)
    ys = _expert_ffn(layer, xs, blk_expert, blk_next, unit_idx, n_used, w1, w3, w2)
    out = _combine(alpha, ys, tile_units, unit_slot, info, x1.reshape(t, d), gate, ln_g, ln_b, s, next_mod)
    if next_mod is None:
        return out.reshape(bsz, s, d), None
    return out[0].reshape(bsz, s, d), out[1].reshape(bsz, s, d)


def kernel(x, c, w_ada, b_ada, ln_g, ln_b, w_in, w_out, conv_w, rwkv_mu, rwkv_w0, rwkv_w2, rwkv_a0, rwkv_a2, rwkv_g2, rwkv_kk, rwkv_ka, rwkv_rk, rwkv_gn_g, rwkv_gn_b, attn_sinks, rel_bias, s5_lambda_re, s5_lambda_im, s5_log_dt, s5_b_re, s5_b_im, s5_c_re, s5_c_im, s5_d, s5_glu_w, s5_glu_b, router_group_w, router_group_b, router_expert_w, router_expert_b, moe_w1, moe_w3, moe_w2):
    depth = w_ada.shape[0]
    d = x.shape[-1]
    g = d // 4
    alpha = (2 * depth) ** 0.25
    n_heads = g // HEAD_DIM
    att_kv = max(1, n_heads // 4) * HEAD_DIM
    rw_off = 3 * g
    lora = RWKV_DECAY_RANK + RWKV_A_RANK + RWKV_GATE_RANK
    att_off = rw_off + 3 * g + lora
    s5_off = att_off + g + 2 * att_kv

    mod = _modulation(c, w_ada, b_ada)
    for l in range(depth):
        sh1, sc1, gt1, sh2, sc2, gt2 = jnp.split(mod[l], 6, axis=-1)
        wl = w_in[l]
        w_conv = wl[:, :rw_off].astype(BF16)
        w_rkv = wl[:, rw_off:rw_off + 3 * g].astype(BF16)
        lo = rw_off + 3 * g
        zcol = lambda n: jnp.zeros((d, n), F32)
        w_lora = jnp.concatenate([
            wl[:, lo:lo + RWKV_DECAY_RANK], zcol(LANES - RWKV_DECAY_RANK),
            wl[:, lo + RWKV_DECAY_RANK:lo + RWKV_DECAY_RANK + RWKV_A_RANK], zcol(LANES - RWKV_A_RANK),
            wl[:, lo + RWKV_DECAY_RANK + RWKV_A_RANK:att_off]], axis=1).astype(BF16)
        w_q = wl[:, att_off:att_off + g].astype(BF16)
        w_kv = wl[:, att_off + g:s5_off].astype(BF16)
        w_s5 = wl[:, s5_off:].astype(BF16)

        if l == 0:
            h = _adaln(x, sh1, sc1)
        y_rwkv = _rwkv_mixer(h, w_rkv, w_lora, rwkv_mu[l], rwkv_w0[l], rwkv_w2[l], rwkv_a0[l], rwkv_a2[l],
                             rwkv_g2[l], rwkv_kk[l], rwkv_ka[l], rwkv_rk[l], rwkv_gn_g[l], rwkv_gn_b[l])
        y_ssm, y_conv, y_att = _s5_conv_swa_mixers(
            h, w_s5, s5_lambda_re[l], s5_lambda_im[l], s5_log_dt[l], s5_b_re[l], s5_b_im[l], s5_c_re[l],
            s5_c_im[l], s5_d[l], s5_glu_w[l], s5_glu_b[l], w_conv, conv_w[l].astype(F32), w_q, w_kv,
            attn_sinks[l], rel_bias)
        w_router = jnp.zeros((d, LANES), F32)
        w_router = w_router.at[:, :N_EXPERT_GROUPS].set(router_group_w[l])
        w_router = w_router.at[:, N_EXPERT_GROUPS:N_EXPERT_GROUPS + N_EXPERTS].set(router_expert_w[l]).astype(BF16)
        b_router = jnp.zeros((1, LANES), F32)
        b_router = b_router.at[0, :N_EXPERT_GROUPS].set(router_group_b[l])
        b_router = b_router.at[0, N_EXPERT_GROUPS:N_EXPERT_GROUPS + N_EXPERTS].set(router_expert_b[l])
        x1, xs, info, seg = _mix_out(alpha, (y_conv, y_rwkv, y_att, y_ssm), w_out[l].astype(BF16), x, gt1,
                                     ln_g[l, 0], ln_b[l, 0], sh2, sc2, w_router, b_router)
        next_mod = None
        if l + 1 < depth:
            nsh1, nsc1 = jnp.split(mod[l + 1], 6, axis=-1)[:2]
            next_mod = (nsh1, nsc1)
        x, h = _moe(l, alpha, xs, info, seg, x1, gt2, ln_g[l, 1], ln_b[l, 1], moe_w1, moe_w3, moe_w2, next_mod)
    return x
```

```python
import functools
import math

import numpy as np
import jax
import jax.numpy as jnp
from jax import lax
from jax.experimental import pallas as pl
from jax.experimental.pallas import tpu as pltpu

F32 = jnp.float32
BF16 = jnp.bfloat16
I32 = jnp.int32

HEAD_DIM = 64
CONV_WIDTH = 3
RWKV_DECAY_RANK = 96
RWKV_A_RANK = 96
RWKV_GATE_RANK = 128
RWKV_GN_EPS = 64e-5
ATT_BLOCK = 128
WINDOW = 128
N_BUCKETS = 32
NEG_INF = -1e30
S5_CH = 16
S5_STATE = 64
N_EXPERT_GROUPS = 4
EXPERTS_PER_GROUP = 8
N_EXPERTS = N_EXPERT_GROUPS * EXPERTS_PER_GROUP
TOP_K = 2
LN_EPS = 1e-5

LANES = 128
SUBLANES = 8
WKV_CHUNK = 64
MIXER_TILE = 256
ADALN_TILE = 512
MODULATION_COLS = 1536
MOE_TILE = 256
MOE_UNIT = SUBLANES
MOE_TILE_ROWS = 768
MOE_BLOCK_UNITS = 32
VMEM_LIMIT = 56 * 2 ** 20


def _cparams(sem, flags=None):
    return pltpu.CompilerParams(dimension_semantics=sem, vmem_limit_bytes=VMEM_LIMIT, flags=flags)


def _dot(a, b):
    return jnp.dot(a.astype(BF16), b.astype(BF16), preferred_element_type=F32)


def _dot_nt(a, b):
    return lax.dot_general(a.astype(BF16), b.astype(BF16), (((1,), (1,)), ((), ())),
                           preferred_element_type=F32)


def _dot_tn(a, b):
    return jnp.dot(a.T.astype(BF16), b.astype(BF16), preferred_element_type=F32)


def _dot_split(x, e):
    hi = x.astype(BF16)
    lo = (x - hi.astype(F32)).astype(BF16)
    return jnp.dot(hi, e, preferred_element_type=F32) + jnp.dot(lo, e, preferred_element_type=F32)


def _pack_bf16_pairs(x):
    n = x.shape[1] // 2
    bits = pltpu.bitcast(x.astype(BF16).astype(F32), jnp.uint32)
    return jnp.bitwise_or(jnp.right_shift(bits[:, :n], jnp.uint32(16)), bits[:, n:])


def _unpack_bf16_pairs(word):
    lo = pltpu.bitcast(jnp.left_shift(word, jnp.uint32(16)), F32)
    hi = pltpu.bitcast(jnp.bitwise_and(word, jnp.uint32(0xFFFF0000)), F32)
    return jnp.concatenate([lo, hi], axis=1).astype(BF16)


def _sigmoid(x):
    return 1.0 / (1.0 + jnp.exp(-x))


def _layer_norm(x):
    mean = jnp.mean(x, axis=-1, keepdims=True)
    xc = x - mean
    var = jnp.mean(xc * xc, axis=-1, keepdims=True)
    return xc * lax.rsqrt(var + LN_EPS)


def _shift_rows(p, carry_row, n):
    row = lax.broadcasted_iota(I32, (p.shape[0], 1), 0)
    out = pltpu.roll(p, n, 0)
    for i in range(n):
        out = jnp.where(row == i, carry_row[SUBLANES - n + i:SUBLANES - n + i + 1, :], out)
    return out


def _mod_kernel(c_ref, w_ref, b_ref, o_ref):
    c = c_ref[...]
    a = c * _sigmoid(c)
    o_ref[0] = _dot(a, w_ref[0]) + b_ref[0]


def _modulation(c, w_ada, b_ada):
    depth, d, n = w_ada.shape
    bsz = c.shape[0]
    tn = MODULATION_COLS
    cp = jnp.zeros((SUBLANES, d), F32).at[:bsz].set(c)
    out = pl.pallas_call(
        _mod_kernel,
        grid=(depth, n // tn),
        in_specs=[pl.BlockSpec((SUBLANES, d), lambda l, j: (0, 0)),
                  pl.BlockSpec((1, d, tn), lambda l, j: (l, 0, j)),
                  pl.BlockSpec((1, 1, tn), lambda l, j: (l, 0, j))],
        out_specs=pl.BlockSpec((1, SUBLANES, tn), lambda l, j: (l, 0, j)),
        out_shape=jax.ShapeDtypeStruct((depth, SUBLANES, n), F32),
        compiler_params=_cparams(("parallel", "parallel")),
        name="adaln_modulation",
    )(cp, w_ada, b_ada.reshape(depth, 1, n))
    return out[:, :bsz]


def _adaln_kernel(x_ref, sh_ref, sc_ref, h_ref):
    h_ref[0] = (_layer_norm(x_ref[0]) * (1.0 + sc_ref[0]) + sh_ref[0]).astype(BF16)


def _adaln(x, shift, scale):
    bsz, s, d = x.shape
    tm = ADALN_TILE
    return pl.pallas_call(
        _adaln_kernel,
        grid=(bsz, s // tm),
        in_specs=[pl.BlockSpec((1, tm, d), lambda b, i: (b, i, 0)),
                  pl.BlockSpec((1, 1, d), lambda b, i: (b, 0, 0)),
                  pl.BlockSpec((1, 1, d), lambda b, i: (b, 0, 0))],
        out_specs=pl.BlockSpec((1, tm, d), lambda b, i: (b, i, 0)),
        out_shape=jax.ShapeDtypeStruct((bsz, s, d), BF16),
        compiler_params=_cparams(("parallel", "parallel")),
        name="adaln_input",
    )(x, shift[:, None, :], scale[:, None, :])


def _t5_bucket(rel):
    n = jnp.maximum(rel, 0)
    max_exact = N_BUCKETS // 2
    n_f = jnp.maximum(n, 1).astype(F32)
    large = max_exact + (jnp.log(n_f / max_exact) / math.log(WINDOW / max_exact)
                         * (N_BUCKETS - max_exact)).astype(I32)
    return jnp.where(n < max_exact, n, jnp.minimum(large, N_BUCKETS - 1))


def _swa_stages(x, first_tile, sink_ref, wq_ref, wkv_ref, bias_ref, kvc_ref, y_ref, b):
    tm = x.shape[0]
    n_heads = bias_ref.shape[0]
    kvw = wkv_ref.shape[1] // 2
    n_kv = kvw // HEAD_DIM
    rep = n_heads // n_kv
    blk = ATT_BLOCK
    q = _dot(x, wq_ref[...]) * (HEAD_DIM ** -0.5)
    kv = _dot(x, wkv_ref[...])
    kvext = jnp.concatenate([kvc_ref[b], kv], axis=0)
    kvc_ref[b] = kv[tm - blk:, :]
    col = lax.broadcasted_iota(I32, (blk, 2 * blk), 1)
    qb16 = q.astype(BF16)
    kv16 = kvext.astype(BF16)

    def scores(j):
        qb = qb16[j * blk:(j + 1) * blk]
        kw = kv16[j * blk:j * blk + 2 * blk, :kvw]
        kgs = [kw[:, gi * HEAD_DIM:(gi + 1) * HEAD_DIM] for gi in range(n_kv)]
        scs = [_dot_nt(qb[:, hh * HEAD_DIM:(hh + 1) * HEAD_DIM], kgs[hh // rep]) + bias_ref[hh]
               for hh in range(n_heads)]
        if j == 0:
            scs = [jnp.where(jnp.logical_and(first_tile, col < blk), NEG_INF, sc) for sc in scs]
        return scs

    def probs(scs):
        out = []
        for hh, sc in enumerate(scs):
            sink = sink_ref[hh]
            m = jnp.maximum(jnp.max(sc, axis=-1, keepdims=True), sink)
            e = jnp.exp(sc - m)
            den = jnp.sum(e, axis=-1, keepdims=True) + jnp.exp(sink - m)
            out.append((e / den).astype(BF16))
        return out

    def values(j, ps):
        vw = kv16[j * blk:j * blk + 2 * blk, kvw:]
        vgs = [vw[:, gi * HEAD_DIM:(gi + 1) * HEAD_DIM] for gi in range(n_kv)]
        outs = [jnp.dot(p, vgs[hh // rep], preferred_element_type=F32) for hh, p in enumerate(ps)]
        y_ref[b, j * blk:(j + 1) * blk, :] = jnp.concatenate(outs, axis=1).astype(BF16)

    return scores, lambda j, scs: values(j, probs(scs))


def _swa_bias_table(rel_bias):
    qi = jnp.arange(ATT_BLOCK)[:, None]
    kj = jnp.arange(2 * ATT_BLOCK)[None, :]
    rel = qi + ATT_BLOCK - kj
    valid = (rel >= 0) & (rel < WINDOW)
    onehot = (_t5_bucket(rel)[..., None] == jnp.arange(N_BUCKETS)).astype(F32)
    bias = jnp.einsum('qkb,bh->hqk', onehot, rel_bias.astype(F32), precision=lax.Precision.HIGHEST)
    return jnp.where(valid[None], bias, NEG_INF)


S5_GROUPS_PER_BLOCK = LANES // S5_CH


def _s5_tables(lam_re, lam_im, log_dt, b_re, b_im, c_re, c_im):
    n_groups, p = lam_re.shape
    lr, li = lam_re.astype(F32), lam_im.astype(F32)
    delta = jnp.exp(log_dt.astype(F32))[:, None]
    mag = jnp.exp(lr * delta)
    ab_re, ab_im = mag * jnp.cos(li * delta), mag * jnp.sin(li * delta)
    den = lr * lr + li * li
    z_re = ((ab_re - 1.0) * lr + ab_im * li) / den
    z_im = (ab_im * lr - (ab_re - 1.0) * li) / den
    br, bi = b_re.astype(F32), b_im.astype(F32)
    bb_re = z_re[..., None] * br - z_im[..., None] * bi
    bb_im = z_re[..., None] * bi + z_im[..., None] * br
    nblk = n_groups // S5_GROUPS_PER_BLOCK
    eye = jnp.eye(S5_GROUPS_PER_BLOCK, dtype=F32)

    def in_blocks(bb):
        bb = bb.reshape(nblk, S5_GROUPS_PER_BLOCK, p, S5_CH)
        return jnp.einsum('qgpc,gh->qgchp', bb, eye).reshape(nblk, LANES, S5_GROUPS_PER_BLOCK * p)

    def out_blocks(cc):
        cc = cc.astype(F32).reshape(nblk, S5_GROUPS_PER_BLOCK, S5_CH, p)
        return jnp.einsum('qgcp,gh->qgphc', cc, eye).reshape(nblk, S5_GROUPS_PER_BLOCK * p, LANES)

    def power(m):
        mg = jnp.exp(m * lr * delta)
        return (mg * jnp.cos(m * li * delta)).reshape(1, -1), (mg * jnp.sin(m * li * delta)).reshape(1, -1)

    row = jnp.arange(SUBLANES, dtype=F32)[:, None]
    tabs = []
    for sft in (1, 2, 4):
        pr, pi = power(float(sft))
        keep = row >= sft
        tabs += [jnp.where(keep, pr, 0.0), jnp.where(keep, pi, 0.0)]
    n_state = n_groups * p
    lrd = (lr * delta).reshape(1, n_state)
    lid = (li * delta).reshape(1, n_state)
    mg = jnp.exp((row + 1.0) * lrd)
    tabs += [mg * jnp.cos((row + 1.0) * lid), mg * jnp.sin((row + 1.0) * lid)]
    tables = jnp.stack(tabs, axis=0)
    return (in_blocks(bb_re).astype(BF16), in_blocks(bb_im).astype(BF16),
            out_blocks(c_re).astype(BF16), out_blocks(c_im).astype(BF16), tables)


def _s5_kernel(sink_ref, h_ref, w_ref, bre_ref, bim_ref, cre_ref, cim_ref, tab_ref, d_ref, gw_ref, gb_ref,
               wc_ref, cw_ref, wq_ref, wkv_ref, bias_ref, y_ref, yc_ref, ya_ref,
               xr_ref, xi_ref, cr_ref, ci_ref, cc_ref, kvc_ref):
    nseq = h_ref.shape[0]
    tm = h_ref.shape[1]
    nblk = bre_ref.shape[0]
    sw = bre_ref.shape[2]

    @pl.when(pl.program_id(0) == 0)
    def _():
        cr_ref[...] = jnp.zeros_like(cr_ref)
        ci_ref[...] = jnp.zeros_like(ci_ref)
        cc_ref[...] = jnp.zeros_like(cc_ref)
        kvc_ref[...] = jnp.zeros_like(kvc_ref)

    def project(b):
        u = _dot(h_ref[b], w_ref[...])
        ub = u.astype(BF16)
        for q in range(nblk):
            uq = ub[:, q * LANES:(q + 1) * LANES]
            xr_ref[b, :, q * sw:(q + 1) * sw] = jnp.dot(uq, bre_ref[q], preferred_element_type=F32)
            xi_ref[b, :, q * sw:(q + 1) * sw] = jnp.dot(uq, bim_ref[q], preferred_element_type=F32)
        return u

    def scan(b):
        cr = cr_ref[b, 0:1, :]
        ci = ci_ref[b, 0:1, :]
        for i in range(tm // SUBLANES):
            rows = slice(i * SUBLANES, (i + 1) * SUBLANES)
            xr = xr_ref[b, rows, :]
            xi = xi_ref[b, rows, :]
            for k, sft in enumerate((1, 2, 4)):
                mr = tab_ref[2 * k]
                mi = tab_ref[2 * k + 1]
                sr = pltpu.roll(xr, sft, 0)
                si = pltpu.roll(xi, sft, 0)
                xr, xi = xr + mr * sr - mi * si, xi + mr * si + mi * sr
            pr = tab_ref[6]
            pi = tab_ref[7]
            xr, xi = xr + pr * cr - pi * ci, xi + pr * ci + pi * cr
            xr_ref[b, rows, :] = xr
            xi_ref[b, rows, :] = xi
            cr = xr[SUBLANES - 1:SUBLANES, :]
            ci = xi[SUBLANES - 1:SUBLANES, :]
        cr_ref[b, 0:1, :] = cr
        ci_ref[b, 0:1, :] = ci

    def readout(b, u):
        ys = []
        for q in range(nblk):
            xr = xr_ref[b, :, q * sw:(q + 1) * sw].astype(BF16)
            xi = xi_ref[b, :, q * sw:(q + 1) * sw].astype(BF16)
            ys.append(jnp.dot(xr, cre_ref[q], preferred_element_type=F32)
                      - jnp.dot(xi, cim_ref[q], preferred_element_type=F32))
        y = jnp.concatenate(ys, axis=1) + d_ref[...] * u
        y = 0.5 * y * (1.0 + jnp.tanh(math.sqrt(2.0 / math.pi) * (y + 0.044715 * (y * y * y))))
        y_ref[b] = (y * _sigmoid(_dot(y, gw_ref[...]) + gb_ref[...])).astype(BF16)

    def conv(b):
        g = yc_ref.shape[2]
        p = _dot(h_ref[b], wc_ref[...])
        b_gate, c_gate, hh = p[:, :g], p[:, g:2 * g], p[:, 2 * g:]
        z = c_gate * hh
        carry = cc_ref[b]
        cw = cw_ref[...]
        out = cw[0:1] * _shift_rows(z, carry, 2) + cw[1:2] * _shift_rows(z, carry, 1) + cw[2:3] * z
        yc_ref[b] = (b_gate * out).astype(BF16)
        cc_ref[b] = z[tm - SUBLANES:, :]

    us = [project(b) for b in range(nseq)]
    first_tile = pl.program_id(0) == 0
    att = [_swa_stages(h_ref[b], first_tile, sink_ref, wq_ref, wkv_ref, bias_ref, kvc_ref, ya_ref, b)
           for b in range(nseq)]
    fillers = [functools.partial(conv, b) for b in range(nseq)]
    for j in range(tm // ATT_BLOCK):
        scs = [att[b][0](j) for b in range(nseq)]
        if fillers:
            fillers.pop(0)()
        for b in range(nseq):
            att[b][1](j, scs[b])
    for f in fillers:
        f()
    for b in range(nseq):
        scan(b)
        readout(b, us[b])


def _s5_conv_swa_mixers(h, w, lam_re, lam_im, log_dt, b_re, b_im, c_re, c_im, d_skip, glu_w, glu_b,
                        w_conv, conv_w, w_q, w_kv, sinks, rel_bias):
    bsz, s, d = h.shape
    bias = _swa_bias_table(rel_bias)
    g = w.shape[1]
    tm = MIXER_TILE
    bre, bim, cre, cim, tables = _s5_tables(lam_re, lam_im, log_dt, b_re, b_im, c_re, c_im)
    n_state = tables.shape[2]
    full = lambda a: pl.BlockSpec(a.shape, lambda i: (0,) * a.ndim)
    dvec = d_skip.astype(F32).reshape(1, g)
    gw = glu_w.astype(BF16)
    gb = glu_b.astype(F32).reshape(1, g)
    return pl.pallas_call(
        _s5_kernel,
        grid=(s // tm,),
        in_specs=[pl.BlockSpec(memory_space=pltpu.SMEM),
                  pl.BlockSpec((bsz, tm, d), lambda i: (0, i, 0)),
                  full(w), full(bre), full(bim), full(cre), full(cim), full(tables),
                  full(dvec), full(gw), full(gb), full(w_conv), full(conv_w),
                  full(w_q), full(w_kv), full(bias)],
        out_specs=[pl.BlockSpec((bsz, tm, g), lambda i: (0, i, 0)),
                   pl.BlockSpec((bsz, tm, g), lambda i: (0, i, 0)),
                   pl.BlockSpec((bsz, tm, w_q.shape[1]), lambda i: (0, i, 0))],
        out_shape=[jax.ShapeDtypeStruct((bsz, s, g), BF16), jax.ShapeDtypeStruct((bsz, s, g), BF16),
                   jax.ShapeDtypeStruct((bsz, s, w_q.shape[1]), BF16)],
        scratch_shapes=[pltpu.VMEM((bsz, tm, n_state), F32), pltpu.VMEM((bsz, tm, n_state), F32),
                        pltpu.VMEM((bsz, SUBLANES, n_state), F32), pltpu.VMEM((bsz, SUBLANES, n_state), F32),
                        pltpu.VMEM((bsz, SUBLANES, g), F32), pltpu.VMEM((bsz, ATT_BLOCK, w_kv.shape[1]), F32)],
        compiler_params=_cparams(("arbitrary",)),
        name="s5_conv_swa_mixers",
    )(sinks.astype(F32), h, w, bre, bim, cre, cim, tables, dvec, gw, gb, w_conv, conv_w, w_q, w_kv, bias)


def _rwkv_kernel(h_ref, wrkv_ref, wlo_ref, mu1_ref, mu2_ref, w0_ref, w2_ref, a0_ref, a2_ref, g2_ref,
                 kk_ref, ka_ref, rk_ref, gng_ref, gnb_ref, eblk_ref,
                 y_ref,
                 cp_ref, cl_ref, hs_ref, r_s, k_s, v_s, a_s, b_s, ld_s):
    tm = h_ref.shape[1]
    g = y_ref.shape[2]
    npair = g // LANES
    ch = WKV_CHUNK

    @pl.when(pl.program_id(1) == 0)
    def _():
        cp_ref[...] = jnp.zeros_like(cp_ref)
        cl_ref[...] = jnp.zeros_like(cl_ref)
        hs_ref[...] = jnp.zeros_like(hs_ref)

    x = h_ref[0]
    p = _dot(x, wrkv_ref[...])
    plo = _dot(x, wlo_ref[...])
    pprev = _shift_rows(p, cp_ref[...], 1)
    lprev = _shift_rows(plo, cl_ref[...], 1)
    cp_ref[...] = p[tm - SUBLANES:, :]
    cl_ref[...] = plo[tm - SUBLANES:, :]
    p = p + (pprev - p) * mu1_ref[...]
    plo = plo + (lprev - plo) * mu2_ref[...]
    r, k, v = p[:, :g], p[:, g:2 * g], p[:, 2 * g:]
    w_lo, a_lo, g_lo = plo[:, :LANES], plo[:, LANES:2 * LANES], plo[:, 2 * LANES:]
    wraw = w0_ref[...] + _dot(jnp.tanh(w_lo), w2_ref[...])
    nz = -wraw
    softplus = jnp.maximum(nz, 0.0) + jnp.log(1.0 + jnp.exp(-jnp.abs(nz)))
    w = -softplus - 0.5
    ld_s[...] = -jnp.exp(w)
    a = _sigmoid(a0_ref[...] + _dot(a_lo, a2_ref[...]))
    gate = _dot(_sigmoid(g_lo), g2_ref[...])
    eblk = eblk_ref[...]
    kk = k * kk_ref[...]
    kk = kk / jnp.maximum(jnp.sqrt(_dot_split(kk * kk, eblk)), 1e-12)
    k = k * (1.0 + (a - 1.0) * ka_ref[...])
    r_s[...] = r
    k_s[...] = k
    v_s[...] = v
    a_s[...] = -kk
    b_s[...] = kk * a

    lane = lax.broadcasted_iota(I32, (1, LANES), 1)
    m0 = (lane < HEAD_DIM).astype(F32)
    m1 = 1.0 - m0
    ri = lax.broadcasted_iota(I32, (2 * ch, 2 * ch), 0)
    ci = lax.broadcasted_iota(I32, (2 * ch, 2 * ch), 1)
    same = (ri < ch) == (ci < ch)
    rloc = jnp.bitwise_and(ri, ch - 1)
    cloc = jnp.bitwise_and(ci, ch - 1)
    strict = jnp.where(jnp.logical_and(same, cloc < rloc), 1.0, 0.0)
    incl = jnp.where(jnp.logical_and(same, cloc <= rloc), 1.0, 0.0)
    eye = jnp.where(ri == ci, 1.0, 0.0)
    tri = jnp.where(lax.broadcasted_iota(I32, (ch, ch), 1) <= lax.broadcasted_iota(I32, (ch, ch), 0),
                    1.0, 0.0).astype(BF16)

    def bd(t):
        return jnp.concatenate([t * m0, t * m1], axis=0)

    nchunk = tm // ch
    per_chunk = []
    for c in range(nchunk):
        rows = slice(c * ch, (c + 1) * ch)
        ld = ld_s[rows, :]
        ld_hi = ld.astype(BF16)
        ld_lo = (ld - ld_hi.astype(F32)).astype(BF16)
        cum = (jnp.dot(tri, ld_hi, preferred_element_type=F32)
               + jnp.dot(tri, ld_lo, preferred_element_type=F32))
        gam = jnp.exp(cum)
        ginv = jnp.exp(-cum)
        per_chunk.append(dict(at=a_s[rows, :] * jnp.exp(cum - ld), rt=r_s[rows, :] * gam,
                              bt=b_s[rows, :] * ginv, kt=k_s[rows, :] * ginv, v=v_s[rows, :],
                              gl=gam[ch - 1:ch, :]))
    inst = [(c, q) for c in range(nchunk) for q in range(npair)]

    def part(name):
        return [per_chunk[c][name][:, q * LANES:(q + 1) * LANES] for c, q in inst]

    bt, kt, gl = part("bt"), part("kt"), part("gl")
    at_bd = [bd(t) for t in part("at")]
    rt_bd = [bd(t) for t in part("rt")]
    v_bd = [bd(t) for t in part("v")]
    bh_t = [bd(b * g_).T for b, g_ in zip(bt, gl)]
    kh_t = [bd(k_ * g_).T for k_, g_ in zip(kt, gl)]
    gmat = [_dot_nt(jnp.concatenate([a_, r_], axis=0), jnp.concatenate([b, b, k_, k_], axis=0))
            for a_, r_, b, k_ in zip(at_bd, rt_bd, bt, kt)]
    n_ab = [gm[:2 * ch, :2 * ch] * strict for gm in gmat]
    a_ak = [gm[:2 * ch, 2 * ch:] * strict for gm in gmat]
    m_rb = [gm[2 * ch:, :2 * ch] * incl for gm in gmat]
    m_rk = [gm[2 * ch:, 2 * ch:] * incl for gm in gmat]
    tinv = [eye + n for n in n_ab]
    npow = n_ab
    for step in range(1, 6):
        if step == 1:
            npow = [_dot(n, n) for n in npow]
        both = [_dot(n, jnp.concatenate([t, n], axis=1)) for n, t in zip(npow, tinv)]
        tinv = [t + b[:, :2 * ch] for t, b in zip(tinv, both)]
        npow = [b[:, 2 * ch:] for b in both]
    va = [_dot(jnp.concatenate([a_, k_, m_], axis=0), v_) for a_, k_, m_, v_ in zip(a_ak, kh_t, m_rk, v_bd)]
    wu = [_dot(t, jnp.concatenate([a_, x_[:2 * ch]], axis=1)) for t, a_, x_ in zip(tinv, at_bd, va)]
    pq = [_dot(jnp.concatenate([b, m_], axis=0), w_) for b, m_, w_ in zip(bh_t, m_rb, wu)]
    pmat = [eye * g_ + t[:2 * ch, :2 * ch] for g_, t in zip(gl, pq)]
    qmat = [t[:2 * ch, 2 * ch:] + x_[2 * ch:4 * ch] for t, x_ in zip(pq, va)]
    ry = [r_ + t[2 * ch:, :2 * ch] for r_, t in zip(rt_bd, pq)]
    y0 = [t[2 * ch:, 2 * ch:] + x_[4 * ch:] for t, x_ in zip(pq, va)]
    state = [hs_ref[q] for q in range(npair)]
    y_chunks = []
    for c in range(nchunk):
        ids = [c * npair + q for q in range(npair)]
        both = [_dot(jnp.concatenate([ry[i], pmat[i]], axis=0), st) for i, st in zip(ids, state)]
        yy = [t[:2 * ch] + y0[i] for i, t in zip(ids, both)]
        state = [t[2 * ch:] + qmat[i] for i, t in zip(ids, both)]
        y_chunks.append(jnp.concatenate([t[:ch] + t[ch:] for t in yy], axis=1))
    hs_ref[...] = jnp.stack(state, axis=0)

    y = jnp.concatenate(y_chunks, axis=0)
    inv_n = 1.0 / HEAD_DIM
    mean = _dot_split(y, eblk) * inv_n
    yc = y - mean
    var = _dot(yc * yc, eblk) * inv_n
    yn = yc * lax.rsqrt(var + RWKV_GN_EPS) * gng_ref[...] + gnb_ref[...]
    r = r_s[...]
    k = k_s[...]
    v = v_s[...]
    bonus = _dot(r * k * rk_ref[...], eblk) * v
    y_ref[0] = ((yn + bonus) * gate).astype(BF16)


def _rwkv_mixer(h, w_rkv, w_lora, mu, w0, w2, a0, a2, g2, k_k, k_a, r_k, gn_g, gn_b):
    bsz, s, d = h.shape
    g = w0.shape[0]
    tm = MIXER_TILE
    row = lambda t: t.astype(F32).reshape(1, -1)
    pad_rows = lambda t: jnp.zeros((LANES, g), F32).at[:t.shape[0]].set(t.astype(F32)).astype(BF16)
    mu1 = row(mu[:3 * g])
    mu2 = jnp.concatenate([
        jnp.zeros((LANES,), F32).at[:RWKV_DECAY_RANK].set(mu[3 * g:3 * g + RWKV_DECAY_RANK]),
        jnp.zeros((LANES,), F32).at[:RWKV_A_RANK].set(mu[3 * g + RWKV_DECAY_RANK:3 * g + RWKV_DECAY_RANK + RWKV_A_RANK]),
        mu[3 * g + RWKV_DECAY_RANK + RWKV_A_RANK:]]).reshape(1, -1)
    head = np.arange(g) // HEAD_DIM
    eblk = jnp.asarray(head[:, None] == head[None, :], BF16)
    args = (h, w_rkv, w_lora, mu1, mu2, row(w0), pad_rows(w2), row(a0), pad_rows(a2), g2.astype(BF16),
            row(k_k), row(k_a), row(r_k), row(gn_g), row(gn_b), eblk)
    full = lambda a: pl.BlockSpec(a.shape, lambda b, i: (0,) * a.ndim)
    return pl.pallas_call(
        _rwkv_kernel,
        grid=(bsz, s // tm),
        in_specs=[pl.BlockSpec((1, tm, d), lambda b, i: (b, i, 0))] + [full(a) for a in args[1:]],
        out_specs=pl.BlockSpec((1, tm, g), lambda b, i: (b, i, 0)),
        out_shape=jax.ShapeDtypeStruct((bsz, s, g), BF16),
        scratch_shapes=[pltpu.VMEM((SUBLANES, 3 * g), F32), pltpu.VMEM((SUBLANES, 3 * LANES), F32),
                        pltpu.VMEM((g // LANES, 2 * WKV_CHUNK, LANES), F32)]
                       + [pltpu.VMEM((tm, g), F32) for _ in range(6)],
        compiler_params=_cparams(("parallel", "arbitrary")),
        name="rwkv7_mixer",
    )(*args)


def _mixout_kernel(alpha, ya_ref, yb_ref, yc_ref, yd_ref, wo_ref, x_ref, gt_ref, lng_ref, lnb_ref,
                   sh_ref, sc_ref, wr_ref, br_ref, x1_ref, xs_ref, info_ref, seg_ref, h2s_ref, lgs_ref):
    g = ya_ref.shape[1]

    @pl.when(pl.program_id(0) == 0)
    def _():
        h2s_ref[...] = jnp.zeros_like(h2s_ref)
        lgs_ref[...] = jnp.zeros_like(lgs_ref)

    h2_prev = h2s_ref[...]
    lg_prev = lgs_ref[...]
    y = (jnp.dot(ya_ref[...], wo_ref[0:g, :], preferred_element_type=F32)
         + jnp.dot(yb_ref[...], wo_ref[g:2 * g, :], preferred_element_type=F32)
         + jnp.dot(yc_ref[...], wo_ref[2 * g:3 * g, :], preferred_element_type=F32)
         + jnp.dot(yd_ref[...], wo_ref[3 * g:, :], preferred_element_type=F32))
    _route_sort(h2_prev, lg_prev, xs_ref, info_ref, seg_ref)
    x1 = _layer_norm(alpha * x_ref[...] + (1.0 + gt_ref[0]) * y) * lng_ref[...] + lnb_ref[...]
    x1_ref[...] = x1
    h2 = _layer_norm(x1) * (1.0 + sc_ref[0]) + sh_ref[0]
    h2s_ref[...] = h2.astype(BF16)
    lgs_ref[...] = _dot(h2, wr_ref[...]) + br_ref[...]


def _mix_out(alpha, ys, w_out, x, gate, ln_g, ln_b, shift2, scale2, w_router, b_router):
    bsz, s, d = x.shape
    g = ys[0].shape[2]
    tm = MOE_TILE
    per_seq = s // tm
    nt = bsz * per_seq
    cur = lambda t: jnp.minimum(t, nt - 1)
    prev = lambda t: jnp.maximum(t - 1, 0)
    tok = lambda w: pl.BlockSpec((tm, w), lambda t: (cur(t), 0))
    per_b = pl.BlockSpec((1, 1, d), lambda t: (cur(t) // per_seq, 0, 0))
    full = lambda a: pl.BlockSpec(a.shape, lambda t: (0,) * a.ndim)
    row = lambda v: v.astype(F32).reshape(1, -1)
    args = (*[v.reshape(bsz * s, g) for v in ys], w_out, x.reshape(bsz * s, d), gate[:, None, :], row(ln_g),
            row(ln_b), shift2[:, None, :], scale2[:, None, :], w_router, b_router)
    x1, xs, info, seg = pl.pallas_call(
        functools.partial(_mixout_kernel, alpha),
        grid=(nt + 1,),
        in_specs=[tok(g)] * 4 + [full(w_out), tok(d), per_b, full(args[7]), full(args[8]), per_b, per_b,
                                 full(w_router), full(b_router)],
        out_specs=[tok(d),
                   pl.BlockSpec((MOE_TILE_ROWS, d // 2 + LANES), lambda t: (prev(t), 0)),
                   pl.BlockSpec((tm, LANES), lambda t: (prev(t), 0)),
                   pl.BlockSpec((1, SUBLANES, LANES), lambda t: (prev(t), 0, 0))],
        out_shape=[jax.ShapeDtypeStruct((bsz * s, d), F32),
                   jax.ShapeDtypeStruct((nt * MOE_TILE_ROWS, d // 2 + LANES), jnp.uint32),
                   jax.ShapeDtypeStruct((bsz * s, LANES), I32),
                   jax.ShapeDtypeStruct((nt, SUBLANES, LANES), I32)],
        scratch_shapes=[pltpu.VMEM((tm, d), BF16), pltpu.VMEM((tm, LANES), F32)],
        compiler_params=_cparams(("arbitrary",)),
        name="mix_out_route_sort",
    )(*args)
    return x1.reshape(bsz, s, d), xs, info, seg


def _route_sort(h2, lg, xs_ref, info_ref, seg_ref):
    tm = lg.shape[0]
    rt = xs_ref.shape[0]
    half = h2.shape[1] // 2
    lane = lax.broadcasted_iota(I32, (tm, LANES), 1)
    lane_f = lane.astype(F32)

    def top1(vals, mask):
        mv = jnp.where(mask, vals, -jnp.inf)
        m = jnp.max(mv, axis=-1, keepdims=True)
        idx = jnp.min(jnp.where(jnp.logical_and(mask, mv == m), lane_f, float(LANES)), axis=-1, keepdims=True)
        return m, idx.astype(I32)

    gmask = lane < N_EXPERT_GROUPS
    gm, gidx = top1(lg, gmask)
    g_val = 1.0 / jnp.sum(jnp.where(gmask, jnp.exp(lg - gm), 0.0), axis=-1, keepdims=True)
    elo = N_EXPERT_GROUPS + gidx * EXPERTS_PER_GROUP
    emask = jnp.logical_and(lane >= elo, lane < elo + EXPERTS_PER_GROUP)
    m1, i1 = top1(lg, emask)
    m2, i2 = top1(lg, jnp.logical_and(emask, lane != i1))
    e21 = jnp.exp(m2 - m1)
    w1 = g_val / (1.0 + e21)
    w2 = g_val * e21 / (1.0 + e21)
    e1 = i1 - N_EXPERT_GROUPS
    e2 = i2 - N_EXPERT_GROUPS
    oh1 = (lane == e1)
    oh2 = (lane == e2)
    ohs = jnp.where(jnp.logical_or(oh1, oh2), 1.0, 0.0)
    cnt = jnp.sum(ohs, axis=0, keepdims=True)
    units = jnp.floor((cnt + (MOE_UNIT - 1.0)) * (1.0 / MOE_UNIT))
    li = lax.broadcasted_iota(I32, (LANES, LANES), 0)
    lj = lax.broadcasted_iota(I32, (LANES, LANES), 1)
    upper = jnp.where(li < lj, 1.0, 0.0).astype(BF16)
    ustart = jnp.dot(jnp.broadcast_to(units, (SUBLANES, LANES)).astype(BF16), upper,
                     preferred_element_type=F32)[0:1, :]
    ri = lax.broadcasted_iota(I32, (tm, tm), 0)
    ci = lax.broadcasted_iota(I32, (tm, tm), 1)
    tri = jnp.where(ci < ri, 1.0, 0.0).astype(BF16)
    before = jnp.dot(tri, ohs.astype(BF16), preferred_element_type=F32)
    first = before + MOE_UNIT * ustart
    pos1 = jnp.sum(jnp.where(oh1, first, 0.0), axis=-1, keepdims=True)
    pos2 = jnp.sum(jnp.where(oh2, first, 0.0), axis=-1, keepdims=True)
    posm = jnp.where(lane == 0, pos1, jnp.where(lane == 1, pos2, -1.0))
    post = posm.T
    prow = lax.broadcasted_iota(I32, (rt, tm), 0).astype(F32)
    sel1 = prow == post[0:1, :]
    sel2 = prow == post[1:2, :]
    hb = h2.astype(BF16)
    xs = jnp.dot(jnp.where(jnp.logical_or(sel1, sel2), 1.0, 0.0).astype(BF16), hb, preferred_element_type=F32)
    bits = pltpu.bitcast(xs, jnp.uint32)
    xs_ref[:, :half] = jnp.bitwise_or(jnp.right_shift(bits[:, :half], jnp.uint32(16)), bits[:, half:])
    def terms(w):
        a = w.astype(BF16).astype(F32)
        b = (w - a).astype(BF16).astype(F32)
        return a, b, ((w - a) - b).astype(BF16).astype(F32)
    t1 = terms(w1)
    t2 = terms(w2)
    wm = jnp.zeros((tm, LANES), F32)
    for k, t in enumerate(t1 + t2):
        wm = jnp.where(lane == k, t, wm)
    wmb = wm.astype(BF16)
    s1 = jnp.dot(jnp.where(sel1, 1.0, 0.0).astype(BF16), wmb, preferred_element_type=F32)
    s2 = jnp.dot(jnp.where(sel2, 1.0, 0.0).astype(BF16), wmb, preferred_element_type=F32)
    wrow = (s1[:, 0:1] + s1[:, 1:2] + s1[:, 2:3]) + (s2[:, 3:4] + s2[:, 4:5] + s2[:, 5:6])
    mlane = lax.broadcasted_iota(I32, (rt, LANES), 1)
    wbits = pltpu.bitcast(jnp.broadcast_to(wrow, (rt, LANES)), jnp.uint32)
    xs_ref[:, half:] = jnp.where(mlane == 0, wbits, jnp.uint32(0))
    info_ref[...] = jnp.where(lane == 0, pos1, jnp.where(lane == 1, pos2, 0.0)).astype(I32)
    srow = lax.broadcasted_iota(I32, (SUBLANES, LANES), 0)
    total = jnp.sum(units, axis=-1, keepdims=True)
    seg = jnp.where(srow == 0, units, jnp.where(srow == 1, ustart, jnp.where(srow == 2, total, 0.0)))
    seg_ref[0] = seg.astype(I32)


def _worklist_vec_kernel(units_ref, ustart_ref, uidx_ref, slot_ref, be_ref, nxt_ref, nb_ref):
    nt = units_ref.shape[0]
    nbk = uidx_ref.shape[0]
    tile_units = MOE_TILE_ROWS // MOE_UNIT
    bu = MOE_BLOCK_UNITS
    u = units_ref[...].astype(F32)
    us = ustart_ref[...].astype(F32)
    ri = lax.broadcasted_iota(I32, (nt, nt), 0)
    ci = lax.broadcasted_iota(I32, (nt, nt), 1)
    tri = jnp.where(ci < ri, 1.0, 0.0).astype(BF16)
    cex = jnp.dot(tri, u.astype(BF16), preferred_element_type=F32)
    cin = cex + u
    tot = jnp.sum(u, axis=0, keepdims=True)
    nblk = jnp.floor((tot + (bu - 1.0)) * (1.0 / bu))
    li = lax.broadcasted_iota(I32, (LANES, LANES), 0)
    lj = lax.broadcasted_iota(I32, (LANES, LANES), 1)
    upper = jnp.where(li < lj, 1.0, 0.0).astype(BF16)
    b0 = jnp.dot(jnp.broadcast_to(nblk, (SUBLANES, LANES)).astype(BF16), upper,
                 preferred_element_type=F32)[0:1, :]
    b0in = b0 + nblk
    used = jnp.sum(nblk, axis=-1, keepdims=True)
    cand = jnp.where(jnp.logical_and(lj > li, jnp.broadcast_to(nblk, (LANES, LANES)) > 0.0),
                     lj.astype(F32), 999.0)
    nxt_col = jnp.min(cand, axis=-1, keepdims=True)
    nxt_col = jnp.where(nxt_col > 900.0, -1.0, nxt_col)

    ul = lax.broadcasted_iota(I32, (nt, LANES), 1).astype(F32)
    slot = jnp.full((nt, LANES), -1.0, F32)
    for e in range(N_EXPERTS):
        st = us[:, e:e + 1]
        sel = jnp.logical_and(ul >= st, ul < st + u[:, e:e + 1])
        slot = jnp.where(sel, bu * b0[:, e:e + 1] + cex[:, e:e + 1] + (ul - st), slot)
    slot_ref[...] = slot.astype(I32)

    def by_tile(t):
        return jnp.concatenate([t, jnp.zeros((LANES - nt, LANES), F32)], axis=0).T

    cin_t, cex_t, us_t = by_tile(cin), by_tile(cex), by_tile(us)
    brow = lax.broadcasted_iota(I32, (nbk, 1), 0).astype(F32)
    eb = jnp.zeros((nbk, 1), F32)
    b0b = jnp.zeros((nbk, 1), F32)
    totb = jnp.zeros((nbk, 1), F32)
    nxtb = jnp.full((nbk, 1), -1.0, F32)
    cin_row = jnp.zeros((nbk, LANES), F32)
    cex_row = jnp.zeros((nbk, LANES), F32)
    us_row = jnp.zeros((nbk, LANES), F32)
    for e in range(N_EXPERTS):
        ine = jnp.logical_and(brow >= b0[:, e:e + 1], brow < b0in[:, e:e + 1])
        eb = jnp.where(ine, float(e), eb)
        b0b = jnp.where(ine, b0[:, e:e + 1], b0b)
        totb = jnp.where(ine, tot[:, e:e + 1], totb)
        nxtb = jnp.where(ine, nxt_col[e:e + 1, :], nxtb)
        cin_row = jnp.where(ine, cin_t[e:e + 1, :], cin_row)
        cex_row = jnp.where(ine, cex_t[e:e + 1, :], cex_row)
        us_row = jnp.where(ine, us_t[e:e + 1, :], us_row)
    active = brow < used
    jl = lax.broadcasted_iota(I32, (1, LANES), 1).astype(F32)
    q = (brow - b0b) * bu + jl
    tau = jnp.zeros((nbk, LANES), F32)
    for t in range(nt):
        tau = tau + jnp.where(cin_row[:, t:t + 1] <= q, 1.0, 0.0)
    base = jnp.zeros((nbk, LANES), F32)
    for t in range(nt):
        base = jnp.where(tau == float(t), us_row[:, t:t + 1] - cex_row[:, t:t + 1] + float(t * tile_units), base)
    ok = jnp.logical_and(jnp.logical_and(active, q < totb), jl < bu)
    uidx_ref[...] = jnp.where(ok, base + q, -1.0).astype(I32)
    be_ref[...] = jnp.broadcast_to(jnp.where(active, eb, N_EXPERTS - 1.0), (nbk, LANES)).astype(I32)
    nxt_ref[...] = jnp.broadcast_to(jnp.where(active, nxtb, -1.0), (nbk, LANES)).astype(I32)
    nb_ref[...] = jnp.broadcast_to(used, (SUBLANES, LANES)).astype(I32)


def _worklist_vec(units2, ustart2, n_blocks):
    nt = units2.shape[0]
    tile_units = MOE_TILE_ROWS // MOE_UNIT
    uidx, slot, be, nxt, nb = pl.pallas_call(
        _worklist_vec_kernel,
        out_shape=[jax.ShapeDtypeStruct((n_blocks, LANES), I32), jax.ShapeDtypeStruct((nt, LANES), I32),
                   jax.ShapeDtypeStruct((n_blocks, LANES), I32), jax.ShapeDtypeStruct((n_blocks, LANES), I32),
                   jax.ShapeDtypeStruct((SUBLANES, LANES), I32)],
        name="moe_worklist",
    )(units2, ustart2)
    return (uidx[:, :MOE_BLOCK_UNITS].reshape(n_blocks * MOE_BLOCK_UNITS),
            slot[:, :tile_units].reshape(nt * tile_units), be[:, 0], nxt[:, 0], nb[0, :1])


def _expert_kernel(layer, be_ref, nxt_ref, uidx_ref, nb_ref, xs_hbm, w1_hbm, w3_hbm, w2_hbm, o_ref,
                   xbuf, gsem, w1f, w3f, w2f, wsem, wslot, w1b, w3b, w2b):
    b = pl.program_id(0)
    used = nb_ref[0]
    half = xs_hbm.shape[1] - LANES
    slot = lax.rem(b, 2)
    other = 1 - slot

    def weight_copies(e, s):
        return (pltpu.make_async_copy(w1_hbm.at[layer, e], w1f.at[s], wsem.at[s]),
                pltpu.make_async_copy(w3_hbm.at[layer, e], w3f.at[s], wsem.at[s]),
                pltpu.make_async_copy(w2_hbm.at[layer, e], w2f.at[s], wsem.at[s]))

    def gather_copy(s, j, unit):
        return pltpu.make_async_copy(xs_hbm.at[pl.ds(pl.multiple_of(unit * MOE_UNIT, MOE_UNIT), MOE_UNIT), :],
                                     xbuf.at[s, pl.ds(j * MOE_UNIT, MOE_UNIT), :], gsem.at[s])

    def gather_start(blk, s):
        for j in range(MOE_BLOCK_UNITS):
            gather_copy(s, j, jnp.maximum(uidx_ref[blk * MOE_BLOCK_UNITS + j], 0)).start(priority=0)

    def gather_wait(s):
        for j in range(MOE_BLOCK_UNITS):
            gather_copy(s, j, 0).wait()

    @pl.when(b == 0)
    def _():
        gather_start(0, 0)
        wslot[0] = 1
        for cp in weight_copies(be_ref[0], 0):
            cp.start(priority=1)

    @pl.when(b + 1 < used)
    def _():
        gather_start(b + 1, other)

    @pl.when(b >= used)
    def _():
        o_ref[...] = jnp.zeros_like(o_ref)

    @pl.when(b < used)
    def _():
        prev = be_ref[jnp.maximum(b - 1, 0)]

        @pl.when(jnp.logical_or(b == 0, be_ref[b] != prev))
        def _():
            s = 1 - wslot[0]
            wslot[0] = s
            for cp in weight_copies(0, s):
                cp.wait()
            w1b[...] = w1f[s].astype(BF16)
            w3b[...] = w3f[s].astype(BF16)
            w2b[...] = w2f[s].astype(BF16)

            @pl.when(nxt_ref[b] >= 0)
            def _():
                for cp in weight_copies(nxt_ref[b], 1 - s):
                    cp.start(priority=1)

        gather_wait(slot)
        xw = xbuf[slot]
        x = _unpack_bf16_pairs(xw[:, :half])
        wrow = pltpu.bitcast(xw[:, half:], F32)[:, 0:1]
        a = jnp.dot(x, w1b[...], preferred_element_type=F32)
        gte = jnp.dot(x, w3b[...], preferred_element_type=F32)
        mid = (a * _sigmoid(a)) * gte
        o_ref[...] = _pack_bf16_pairs(jnp.dot(mid.astype(BF16), w2b[...], preferred_element_type=F32) * wrow)


def _expert_ffn(layer, xs, blk_expert, blk_next, unit_idx, n_used, w1, w3, w2):
    d, de = w1.shape[2], w1.shape[3]
    nb = blk_expert.shape[0]
    rows = MOE_BLOCK_UNITS * MOE_UNIT
    hbm = pl.BlockSpec(memory_space=pl.ANY)
    grid_spec = pltpu.PrefetchScalarGridSpec(
        num_scalar_prefetch=4,
        grid=(nb,),
        in_specs=[hbm, hbm, hbm, hbm],
        out_specs=pl.BlockSpec((rows, d // 2), lambda b, be, nx, ui, nu: (b, 0)),
        scratch_shapes=[pltpu.VMEM((2, rows, xs.shape[1]), jnp.uint32), pltpu.SemaphoreType.DMA((2,)),
                        pltpu.VMEM((2, d, de), F32), pltpu.VMEM((2, d, de), F32), pltpu.VMEM((2, de, d), F32),
                        pltpu.SemaphoreType.DMA((2,)), pltpu.SMEM((1,), I32),
                        pltpu.VMEM((d, de), BF16), pltpu.VMEM((d, de), BF16), pltpu.VMEM((de, d), BF16)],
    )
    return pl.pallas_call(
        functools.partial(_expert_kernel, layer),
        grid_spec=grid_spec,
        out_shape=jax.ShapeDtypeStruct((nb * rows, d // 2), jnp.uint32),
        compiler_params=_cparams(("arbitrary",)),
        name="moe_expert_ffn",
    )(blk_expert, blk_next, unit_idx, n_used, xs, w1, w3, w2)


def _combine_kernel(alpha, with_next, nu_ref, slot_ref, ys_hbm, info_ref, x_ref, gt_ref, lng_ref, lnb_ref, *rest):
    if with_next:
        sh_ref, sc_ref, o_ref, hn_ref, ybuf, sem = rest
    else:
        o_ref, ybuf, sem = rest
    i = pl.program_id(0)
    nt = pl.num_programs(0)
    tm = x_ref.shape[0]
    rt = ybuf.shape[1]
    slot = lax.rem(i, 2)
    other = 1 - slot

    def unit_copy(tile, s, j):
        src = pl.multiple_of(slot_ref[tile * (rt // MOE_UNIT) + j] * MOE_UNIT, MOE_UNIT)
        dst = pl.multiple_of(j * MOE_UNIT, MOE_UNIT)
        return pltpu.make_async_copy(ys_hbm.at[pl.ds(src, MOE_UNIT), :], ybuf.at[s, pl.ds(dst, MOE_UNIT), :], sem.at[s])

    def start(tile, s):
        def body(j, c):
            unit_copy(tile, s, j).start()
            return c
        lax.fori_loop(0, nu_ref[tile], body, 0)

    def wait(tile, s):
        def body(j, c):
            unit_copy(tile, s, j).wait()
            return c
        lax.fori_loop(0, nu_ref[tile], body, 0)

    @pl.when(i == 0)
    def _():
        ybuf[...] = jnp.zeros_like(ybuf)
        start(0, 0)

    @pl.when(i + 1 < nt)
    def _():
        start(i + 1, other)

    wait(i, slot)
    info = info_ref[...]
    col = lax.broadcasted_iota(I32, (tm, rt), 1)
    pick = jnp.where(jnp.logical_or(col == info[:, 0:1], col == info[:, 1:2]), 1.0, 0.0).astype(BF16)
    y = jnp.dot(pick, _unpack_bf16_pairs(ybuf[slot]), preferred_element_type=F32)
    x2 = _layer_norm(alpha * x_ref[...] + (1.0 + gt_ref[0]) * y) * lng_ref[...] + lnb_ref[...]
    o_ref[...] = x2
    if with_next:
        hn_ref[...] = (_layer_norm(x2) * (1.0 + sc_ref[0]) + sh_ref[0]).astype(BF16)


def _combine(alpha, ys, tile_units, unit_slot, info, x1, gate, ln_g, ln_b, seq, next_mod=None):
    t, d = x1.shape
    tm = MOE_TILE
    per_seq = seq // tm
    tok = pl.BlockSpec((tm, d), lambda i, nu, us: (i, 0))
    per_b = pl.BlockSpec((1, 1, d), lambda i, nu, us: (i // per_seq, 0, 0))
    row = pl.BlockSpec((1, d), lambda i, nu, us: (0, 0))
    with_next = next_mod is not None
    args = [tile_units, unit_slot, ys, info, x1, gate[:, None, :], ln_g.astype(F32).reshape(1, d),
            ln_b.astype(F32).reshape(1, d)]
    in_specs = [pl.BlockSpec(memory_space=pl.ANY), pl.BlockSpec((tm, LANES), lambda i, nu, us: (i, 0)),
                tok, per_b, row, row]
    out_specs, out_shape = tok, jax.ShapeDtypeStruct((t, d), F32)
    if with_next:
        args += [next_mod[0][:, None, :], next_mod[1][:, None, :]]
        in_specs += [per_b, per_b]
        out_specs, out_shape = [tok, tok], [out_shape, jax.ShapeDtypeStruct((t, d), BF16)]
    grid_spec = pltpu.PrefetchScalarGridSpec(
        num_scalar_prefetch=2,
        grid=(t // tm,),
        in_specs=in_specs,
        out_specs=out_specs,
        scratch_shapes=[pltpu.VMEM((2, MOE_TILE_ROWS, d // 2), jnp.uint32), pltpu.SemaphoreType.DMA((2,))],
    )
    return pl.pallas_call(
        functools.partial(_combine_kernel, alpha, with_next),
        grid_spec=grid_spec,
        out_shape=out_shape,
        compiler_params=_cparams(("arbitrary",)),
        name="moe_combine_ln",
    )(*args)


def _moe(layer, alpha, xs, info, seg, x1, gate, ln_g, ln_b, w1, w3, w2, next_mod):
    bsz, s, d = x1.shape
    t = bsz * s
    nt = t // MOE_TILE
    tile_units = seg[:, 2, 0]
    max_units = nt * (TOP_K * MOE_TILE // MOE_UNIT + N_EXPERTS * (MOE_UNIT - 1) // MOE_UNIT)
    n_blocks = max_units // MOE_BLOCK_UNITS + N_EXPERTS
    unit_idx, unit_slot, blk_expert, blk_next, n_used = _worklist_vec(seg[:, 0, :], seg[:, 1, :], n_blocks)
    ys = _expert_ffn(layer, xs, blk_expert, blk_next, unit_idx, n_used, w1, w3, w2)
    out = _combine(alpha, ys, tile_units, unit_slot, info, x1.reshape(t, d), gate, ln_g, ln_b, s, next_mod)
    if next_mod is None:
        return out.reshape(bsz, s, d), None
    return out[0].reshape(bsz, s, d), out[1].reshape(bsz, s, d)


def kernel(x, c, w_ada, b_ada, ln_g, ln_b, w_in, w_out, conv_w, rwkv_mu, rwkv_w0, rwkv_w2, rwkv_a0, rwkv_a2, rwkv_g2, rwkv_kk, rwkv_ka, rwkv_rk, rwkv_gn_g, rwkv_gn_b, attn_sinks, rel_bias, s5_lambda_re, s5_lambda_im, s5_log_dt, s5_b_re, s5_b_im, s5_c_re, s5_c_im, s5_d, s5_glu_w, s5_glu_b, router_group_w, router_group_b, router_expert_w, router_expert_b, moe_w1, moe_w3, moe_w2):
    depth = w_ada.shape[0]
    d = x.shape[-1]
    g = d // 4
    alpha = (2 * depth) ** 0.25
    n_heads = g // HEAD_DIM
    att_kv = max(1, n_heads // 4) * HEAD_DIM
    rw_off = 3 * g
    lora = RWKV_DECAY_RANK + RWKV_A_RANK + RWKV_GATE_RANK
    att_off = rw_off + 3 * g + lora
    s5_off = att_off + g + 2 * att_kv

    mod = _modulation(c, w_ada, b_ada)
    for l in range(depth):
        sh1, sc1, gt1, sh2, sc2, gt2 = jnp.split(mod[l], 6, axis=-1)
        wl = w_in[l]
        w_conv = wl[:, :rw_off].astype(BF16)
        w_rkv = wl[:, rw_off:rw_off + 3 * g].astype(BF16)
        lo = rw_off + 3 * g
        zcol = lambda n: jnp.zeros((d, n), F32)
        w_lora = jnp.concatenate([
            wl[:, lo:lo + RWKV_DECAY_RANK], zcol(LANES - RWKV_DECAY_RANK),
            wl[:, lo + RWKV_DECAY_RANK:lo + RWKV_DECAY_RANK + RWKV_A_RANK], zcol(LANES - RWKV_A_RANK),
            wl[:, lo + RWKV_DECAY_RANK + RWKV_A_RANK:att_off]], axis=1).astype(BF16)
        w_q = wl[:, att_off:att_off + g].astype(BF16)
        w_kv = wl[:, att_off + g:s5_off].astype(BF16)
        w_s5 = wl[:, s5_off:].astype(BF16)

        if l == 0:
            h = _adaln(x, sh1, sc1)
        y_rwkv = _rwkv_mixer(h, w_rkv, w_lora, rwkv_mu[l], rwkv_w0[l], rwkv_w2[l], rwkv_a0[l], rwkv_a2[l],
                             rwkv_g2[l], rwkv_kk[l], rwkv_ka[l], rwkv_rk[l], rwkv_gn_g[l], rwkv_gn_b[l])
        y_ssm, y_conv, y_att = _s5_conv_swa_mixers(
            h, w_s5, s5_lambda_re[l], s5_lambda_im[l], s5_log_dt[l], s5_b_re[l], s5_b_im[l], s5_c_re[l],
            s5_c_im[l], s5_d[l], s5_glu_w[l], s5_glu_b[l], w_conv, conv_w[l].astype(F32), w_q, w_kv,
            attn_sinks[l], rel_bias)
        w_router = jnp.zeros((d, LANES), F32)
        w_router = w_router.at[:, :N_EXPERT_GROUPS].set(router_group_w[l])
        w_router = w_router.at[:, N_EXPERT_GROUPS:N_EXPERT_GROUPS + N_EXPERTS].set(router_expert_w[l]).astype(BF16)
        b_router = jnp.zeros((1, LANES), F32)
        b_router = b_router.at[0, :N_EXPERT_GROUPS].set(router_group_b[l])
        b_router = b_router.at[0, N_EXPERT_GROUPS:N_EXPERT_GROUPS + N_EXPERTS].set(router_expert_b[l])
        x1, xs, info, seg = _mix_out(alpha, (y_conv, y_rwkv, y_att, y_ssm), w_out[l].astype(BF16), x, gt1,
                                     ln_g[l, 0], ln_b[l, 0], sh2, sc2, w_router, b_router)
        next_mod = None
        if l + 1 < depth:
            nsh1, nsc1 = jnp.split(mod[l + 1], 6, axis=-1)[:2]
            next_mod = (nsh1, nsc1)
        x, h = _moe(l, alpha, xs, info, seg, x1, gt2, ln_g[l, 1], ln_b[l, 1], moe_w1, moe_w3, moe_w2, next_mod)
    return x
```

```python
import functools
import math

import numpy as np
import jax
import jax.numpy as jnp
from jax import lax
from jax.experimental import pallas as pl
from jax.experimental.pallas import tpu as pltpu

F32 = jnp.float32
BF16 = jnp.bfloat16
I32 = jnp.int32

HEAD_DIM = 64
CONV_WIDTH = 3
RWKV_DECAY_RANK = 96
RWKV_A_RANK = 96
RWKV_GATE_RANK = 128
RWKV_GN_EPS = 64e-5
ATT_BLOCK = 128
WINDOW = 128
N_BUCKETS = 32
NEG_INF = -1e30
S5_CH = 16
S5_STATE = 64
N_EXPERT_GROUPS = 4
EXPERTS_PER_GROUP = 8
N_EXPERTS = N_EXPERT_GROUPS * EXPERTS_PER_GROUP
TOP_K = 2
LN_EPS = 1e-5

LANES = 128
SUBLANES = 8
WKV_CHUNK = 64
MIXER_TILE = 256
ADALN_TILE = 512
MODULATION_COLS = 1536
MOE_TILE = 256
MOE_UNIT = SUBLANES
MOE_TILE_ROWS = 768
MOE_BLOCK_UNITS = 32
VMEM_LIMIT = 56 * 2 ** 20


def _cparams(sem, flags=None):
    return pltpu.CompilerParams(dimension_semantics=sem, vmem_limit_bytes=VMEM_LIMIT, flags=flags)


def _dot(a, b):
    return jnp.dot(a.astype(BF16), b.astype(BF16), preferred_element_type=F32)


def _dot_nt(a, b):
    return lax.dot_general(a.astype(BF16), b.astype(BF16), (((1,), (1,)), ((), ())),
                           preferred_element_type=F32)


def _dot_tn(a, b):
    return jnp.dot(a.T.astype(BF16), b.astype(BF16), preferred_element_type=F32)


def _dot_split(x, e):
    hi = x.astype(BF16)
    lo = (x - hi.astype(F32)).astype(BF16)
    return jnp.dot(hi, e, preferred_element_type=F32) + jnp.dot(lo, e, preferred_element_type=F32)


def _pack_bf16_pairs(x):
    n = x.shape[1] // 2
    bits = pltpu.bitcast(x.astype(BF16).astype(F32), jnp.uint32)
    return jnp.bitwise_or(jnp.right_shift(bits[:, :n], jnp.uint32(16)), bits[:, n:])


def _unpack_bf16_pairs(word):
    lo = pltpu.bitcast(jnp.left_shift(word, jnp.uint32(16)), F32)
    hi = pltpu.bitcast(jnp.bitwise_and(word, jnp.uint32(0xFFFF0000)), F32)
    return jnp.concatenate([lo, hi], axis=1).astype(BF16)


def _sigmoid(x):
    return 1.0 / (1.0 + jnp.exp(-x))


def _layer_norm(x):
    mean = jnp.mean(x, axis=-1, keepdims=True)
    xc = x - mean
    var = jnp.mean(xc * xc, axis=-1, keepdims=True)
    return xc * lax.rsqrt(var + LN_EPS)


def _shift_rows(p, carry_row, n):
    row = lax.broadcasted_iota(I32, (p.shape[0], 1), 0)
    out = pltpu.roll(p, n, 0)
    for i in range(n):
        out = jnp.where(row == i, carry_row[SUBLANES - n + i:SUBLANES - n + i + 1, :], out)
    return out


def _mod_kernel(c_ref, w_ref, b_ref, o_ref):
    c = c_ref[...]
    a = c * _sigmoid(c)
    o_ref[0] = _dot(a, w_ref[0]) + b_ref[0]


def _modulation(c, w_ada, b_ada):
    depth, d, n = w_ada.shape
    bsz = c.shape[0]
    tn = MODULATION_COLS
    cp = jnp.zeros((SUBLANES, d), F32).at[:bsz].set(c)
    out = pl.pallas_call(
        _mod_kernel,
        grid=(depth, n // tn),
        in_specs=[pl.BlockSpec((SUBLANES, d), lambda l, j: (0, 0)),
                  pl.BlockSpec((1, d, tn), lambda l, j: (l, 0, j)),
                  pl.BlockSpec((1, 1, tn), lambda l, j: (l, 0, j))],
        out_specs=pl.BlockSpec((1, SUBLANES, tn), lambda l, j: (l, 0, j)),
        out_shape=jax.ShapeDtypeStruct((depth, SUBLANES, n), F32),
        compiler_params=_cparams(("parallel", "parallel")),
        name="adaln_modulation",
    )(cp, w_ada, b_ada.reshape(depth, 1, n))
    return out[:, :bsz]


def _adaln_kernel(x_ref, sh_ref, sc_ref, h_ref):
    h_ref[0] = (_layer_norm(x_ref[0]) * (1.0 + sc_ref[0]) + sh_ref[0]).astype(BF16)


def _adaln(x, shift, scale):
    bsz, s, d = x.shape
    tm = ADALN_TILE
    return pl.pallas_call(
        _adaln_kernel,
        grid=(bsz, s // tm),
        in_specs=[pl.BlockSpec((1, tm, d), lambda b, i: (b, i, 0)),
                  pl.BlockSpec((1, 1, d), lambda b, i: (b, 0, 0)),
                  pl.BlockSpec((1, 1, d), lambda b, i: (b, 0, 0))],
        out_specs=pl.BlockSpec((1, tm, d), lambda b, i: (b, i, 0)),
        out_shape=jax.ShapeDtypeStruct((bsz, s, d), BF16),
        compiler_params=_cparams(("parallel", "parallel")),
        name="adaln_input",
    )(x, shift[:, None, :], scale[:, None, :])


def _t5_bucket(rel):
    n = jnp.maximum(rel, 0)
    max_exact = N_BUCKETS // 2
    n_f = jnp.maximum(n, 1).astype(F32)
    large = max_exact + (jnp.log(n_f / max_exact) / math.log(WINDOW / max_exact)
                         * (N_BUCKETS - max_exact)).astype(I32)
    return jnp.where(n < max_exact, n, jnp.minimum(large, N_BUCKETS - 1))


def _swa_stages(x, first_tile, sink_ref, wq_ref, wkv_ref, bias_ref, kvc_ref, y_ref, b):
    tm = x.shape[0]
    n_heads = bias_ref.shape[0]
    kvw = wkv_ref.shape[1] // 2
    n_kv = kvw // HEAD_DIM
    rep = n_heads // n_kv
    blk = ATT_BLOCK
    q = _dot(x, wq_ref[...]) * (HEAD_DIM ** -0.5)
    kv = _dot(x, wkv_ref[...])
    kvext = jnp.concatenate([kvc_ref[b], kv], axis=0)
    kvc_ref[b] = kv[tm - blk:, :]
    col = lax.broadcasted_iota(I32, (blk, 2 * blk), 1)
    qb16 = q.astype(BF16)
    kv16 = kvext.astype(BF16)

    def scores(j):
        qb = qb16[j * blk:(j + 1) * blk]
        kw = kv16[j * blk:j * blk + 2 * blk, :kvw]
        kgs = [kw[:, gi * HEAD_DIM:(gi + 1) * HEAD_DIM] for gi in range(n_kv)]
        scs = [_dot_nt(qb[:, hh * HEAD_DIM:(hh + 1) * HEAD_DIM], kgs[hh // rep]) + bias_ref[hh]
               for hh in range(n_heads)]
        if j == 0:
            scs = [jnp.where(jnp.logical_and(first_tile, col < blk), NEG_INF, sc) for sc in scs]
        return scs

    def probs(scs):
        out = []
        for hh, sc in enumerate(scs):
            sink = sink_ref[hh]
            m = jnp.maximum(jnp.max(sc, axis=-1, keepdims=True), sink)
            e = jnp.exp(sc - m)
            den = jnp.sum(e, axis=-1, keepdims=True) + jnp.exp(sink - m)
            out.append((e / den).astype(BF16))
        return out

    def values(j, ps):
        vw = kv16[j * blk:j * blk + 2 * blk, kvw:]
        vgs = [vw[:, gi * HEAD_DIM:(gi + 1) * HEAD_DIM] for gi in range(n_kv)]
        outs = [jnp.dot(p, vgs[hh // rep], preferred_element_type=F32) for hh, p in enumerate(ps)]
        y_ref[b, j * blk:(j + 1) * blk, :] = jnp.concatenate(outs, axis=1).astype(BF16)

    return scores, lambda j, scs: values(j, probs(scs))


def _swa_bias_table(rel_bias):
    qi = jnp.arange(ATT_BLOCK)[:, None]
    kj = jnp.arange(2 * ATT_BLOCK)[None, :]
    rel = qi + ATT_BLOCK - kj
    valid = (rel >= 0) & (rel < WINDOW)
    onehot = (_t5_bucket(rel)[..., None] == jnp.arange(N_BUCKETS)).astype(F32)
    bias = jnp.einsum('qkb,bh->hqk', onehot, rel_bias.astype(F32), precision=lax.Precision.HIGHEST)
    return jnp.where(valid[None], bias, NEG_INF)


S5_GROUPS_PER_BLOCK = LANES // S5_CH


def _s5_tables(lam_re, lam_im, log_dt, b_re, b_im, c_re, c_im):
    n_groups, p = lam_re.shape
    lr, li = lam_re.astype(F32), lam_im.astype(F32)
    delta = jnp.exp(log_dt.astype(F32))[:, None]
    mag = jnp.exp(lr * delta)
    ab_re, ab_im = mag * jnp.cos(li * delta), mag * jnp.sin(li * delta)
    den = lr * lr + li * li
    z_re = ((ab_re - 1.0) * lr + ab_im * li) / den
    z_im = (ab_im * lr - (ab_re - 1.0) * li) / den
    br, bi = b_re.astype(F32), b_im.astype(F32)
    bb_re = z_re[..., None] * br - z_im[..., None] * bi
    bb_im = z_re[..., None] * bi + z_im[..., None] * br
    nblk = n_groups // S5_GROUPS_PER_BLOCK
    eye = jnp.eye(S5_GROUPS_PER_BLOCK, dtype=F32)

    def in_blocks(bb):
        bb = bb.reshape(nblk, S5_GROUPS_PER_BLOCK, p, S5_CH)
        return jnp.einsum('qgpc,gh->qgchp', bb, eye).reshape(nblk, LANES, S5_GROUPS_PER_BLOCK * p)

    def out_blocks(cc):
        cc = cc.astype(F32).reshape(nblk, S5_GROUPS_PER_BLOCK, S5_CH, p)
        return jnp.einsum('qgcp,gh->qgphc', cc, eye).reshape(nblk, S5_GROUPS_PER_BLOCK * p, LANES)

    def power(m):
        mg = jnp.exp(m * lr * delta)
        return (mg * jnp.cos(m * li * delta)).reshape(1, -1), (mg * jnp.sin(m * li * delta)).reshape(1, -1)

    row = jnp.arange(SUBLANES, dtype=F32)[:, None]
    tabs = []
    for sft in (1, 2, 4):
        pr, pi = power(float(sft))
        keep = row >= sft
        tabs += [jnp.where(keep, pr, 0.0), jnp.where(keep, pi, 0.0)]
    n_state = n_groups * p
    lrd = (lr * delta).reshape(1, n_state)
    lid = (li * delta).reshape(1, n_state)
    mg = jnp.exp((row + 1.0) * lrd)
    tabs += [mg * jnp.cos((row + 1.0) * lid), mg * jnp.sin((row + 1.0) * lid)]
    tables = jnp.stack(tabs, axis=0)
    return (in_blocks(bb_re).astype(BF16), in_blocks(bb_im).astype(BF16),
            out_blocks(c_re).astype(BF16), out_blocks(c_im).astype(BF16), tables)


def _s5_kernel(sink_ref, h_ref, w_ref, bre_ref, bim_ref, cre_ref, cim_ref, tab_ref, d_ref, gw_ref, gb_ref,
               wc_ref, cw_ref, wq_ref, wkv_ref, bias_ref, y_ref, yc_ref, ya_ref,
               xr_ref, xi_ref, cr_ref, ci_ref, cc_ref, kvc_ref):
    nseq = h_ref.shape[0]
    tm = h_ref.shape[1]
    nblk = bre_ref.shape[0]
    sw = bre_ref.shape[2]

    @pl.when(pl.program_id(0) == 0)
    def _():
        cr_ref[...] = jnp.zeros_like(cr_ref)
        ci_ref[...] = jnp.zeros_like(ci_ref)
        cc_ref[...] = jnp.zeros_like(cc_ref)
        kvc_ref[...] = jnp.zeros_like(kvc_ref)

    def project(b):
        u = _dot(h_ref[b], w_ref[...])
        ub = u.astype(BF16)
        for q in range(nblk):
            uq = ub[:, q * LANES:(q + 1) * LANES]
            xr_ref[b, :, q * sw:(q + 1) * sw] = jnp.dot(uq, bre_ref[q], preferred_element_type=F32)
            xi_ref[b, :, q * sw:(q + 1) * sw] = jnp.dot(uq, bim_ref[q], preferred_element_type=F32)
        return u

    def scan(b):
        cr = cr_ref[b, 0:1, :]
        ci = ci_ref[b, 0:1, :]
        for i in range(tm // SUBLANES):
            rows = slice(i * SUBLANES, (i + 1) * SUBLANES)
            xr = xr_ref[b, rows, :]
            xi = xi_ref[b, rows, :]
            for k, sft in enumerate((1, 2, 4)):
                mr = tab_ref[2 * k]
                mi = tab_ref[2 * k + 1]
                sr = pltpu.roll(xr, sft, 0)
                si = pltpu.roll(xi, sft, 0)
                xr, xi = xr + mr * sr - mi * si, xi + mr * si + mi * sr
            pr = tab_ref[6]
            pi = tab_ref[7]
            xr, xi = xr + pr * cr - pi * ci, xi + pr * ci + pi * cr
            xr_ref[b, rows, :] = xr
            xi_ref[b, rows, :] = xi
            cr = xr[SUBLANES - 1:SUBLANES, :]
            ci = xi[SUBLANES - 1:SUBLANES, :]
        cr_ref[b, 0:1, :] = cr
        ci_ref[b, 0:1, :] = ci

    def readout(b, u):
        ys = []
        for q in range(nblk):
            xr = xr_ref[b, :, q * sw:(q + 1) * sw].astype(BF16)
            xi = xi_ref[b, :, q * sw:(q + 1) * sw].astype(BF16)
            ys.append(jnp.dot(xr, cre_ref[q], preferred_element_type=F32)
                      - jnp.dot(xi, cim_ref[q], preferred_element_type=F32))
        y = jnp.concatenate(ys, axis=1) + d_ref[...] * u
        y = 0.5 * y * (1.0 + jnp.tanh(math.sqrt(2.0 / math.pi) * (y + 0.044715 * (y * y * y))))
        y_ref[b] = (y * _sigmoid(_dot(y, gw_ref[...]) + gb_ref[...])).astype(BF16)

    def conv(b):
        g = yc_ref.shape[2]
        p = _dot(h_ref[b], wc_ref[...])
        b_gate, c_gate, hh = p[:, :g], p[:, g:2 * g], p[:, 2 * g:]
        z = c_gate * hh
        carry = cc_ref[b]
        cw = cw_ref[...]
        out = cw[0:1] * _shift_rows(z, carry, 2) + cw[1:2] * _shift_rows(z, carry, 1) + cw[2:3] * z
        yc_ref[b] = (b_gate * out).astype(BF16)
        cc_ref[b] = z[tm - SUBLANES:, :]

    us = [project(b) for b in range(nseq)]
    first_tile = pl.program_id(0) == 0
    att = [_swa_stages(h_ref[b], first_tile, sink_ref, wq_ref, wkv_ref, bias_ref, kvc_ref, ya_ref, b)
           for b in range(nseq)]
    fillers = [functools.partial(conv, b) for b in range(nseq)]
    for j in range(tm // ATT_BLOCK):
        scs = [att[b][0](j) for b in range(nseq)]
        if fillers:
            fillers.pop(0)()
        for b in range(nseq):
            att[b][1](j, scs[b])
    for f in fillers:
        f()
    for b in range(nseq):
        scan(b)
        readout(b, us[b])


def _s5_conv_swa_mixers(h, w, lam_re, lam_im, log_dt, b_re, b_im, c_re, c_im, d_skip, glu_w, glu_b,
                        w_conv, conv_w, w_q, w_kv, sinks, rel_bias):
    bsz, s, d = h.shape
    bias = _swa_bias_table(rel_bias)
    g = w.shape[1]
    tm = MIXER_TILE
    bre, bim, cre, cim, tables = _s5_tables(lam_re, lam_im, log_dt, b_re, b_im, c_re, c_im)
    n_state = tables.shape[2]
    full = lambda a: pl.BlockSpec(a.shape, lambda i: (0,) * a.ndim)
    dvec = d_skip.astype(F32).reshape(1, g)
    gw = glu_w.astype(BF16)
    gb = glu_b.astype(F32).reshape(1, g)
    return pl.pallas_call(
        _s5_kernel,
        grid=(s // tm,),
        in_specs=[pl.BlockSpec(memory_space=pltpu.SMEM),
                  pl.BlockSpec((bsz, tm, d), lambda i: (0, i, 0)),
                  full(w), full(bre), full(bim), full(cre), full(cim), full(tables),
                  full(dvec), full(gw), full(gb), full(w_conv), full(conv_w),
                  full(w_q), full(w_kv), full(bias)],
        out_specs=[pl.BlockSpec((bsz, tm, g), lambda i: (0, i, 0)),
                   pl.BlockSpec((bsz, tm, g), lambda i: (0, i, 0)),
                   pl.BlockSpec((bsz, tm, w_q.shape[1]), lambda i: (0, i, 0))],
        out_shape=[jax.ShapeDtypeStruct((bsz, s, g), BF16), jax.ShapeDtypeStruct((bsz, s, g), BF16),
                   jax.ShapeDtypeStruct((bsz, s, w_q.shape[1]), BF16)],
        scratch_shapes=[pltpu.VMEM((bsz, tm, n_state), F32), pltpu.VMEM((bsz, tm, n_state), F32),
                        pltpu.VMEM((bsz, SUBLANES, n_state), F32), pltpu.VMEM((bsz, SUBLANES, n_state), F32),
                        pltpu.VMEM((bsz, SUBLANES, g), F32), pltpu.VMEM((bsz, ATT_BLOCK, w_kv.shape[1]), F32)],
        compiler_params=_cparams(("arbitrary",)),
        name="s5_conv_swa_mixers",
    )(sinks.astype(F32), h, w, bre, bim, cre, cim, tables, dvec, gw, gb, w_conv, conv_w, w_q, w_kv, bias)


def _rwkv_kernel(h_ref, wrkv_ref, wlo_ref, mu1_ref, mu2_ref, w0_ref, w2_ref, a0_ref, a2_ref, g2_ref,
                 kk_ref, ka_ref, rk_ref, gng_ref, gnb_ref, eblk_ref,
                 y_ref,
                 cp_ref, cl_ref, hs_ref, r_s, k_s, v_s, a_s, b_s, ld_s):
    tm = h_ref.shape[1]
    g = y_ref.shape[2]
    npair = g // LANES
    ch = WKV_CHUNK

    @pl.when(pl.program_id(1) == 0)
    def _():
        cp_ref[...] = jnp.zeros_like(cp_ref)
        cl_ref[...] = jnp.zeros_like(cl_ref)
        hs_ref[...] = jnp.zeros_like(hs_ref)

    x = h_ref[0]
    p = _dot(x, wrkv_ref[...])
    plo = _dot(x, wlo_ref[...])
    pprev = _shift_rows(p, cp_ref[...], 1)
    lprev = _shift_rows(plo, cl_ref[...], 1)
    cp_ref[...] = p[tm - SUBLANES:, :]
    cl_ref[...] = plo[tm - SUBLANES:, :]
    p = p + (pprev - p) * mu1_ref[...]
    plo = plo + (lprev - plo) * mu2_ref[...]
    r, k, v = p[:, :g], p[:, g:2 * g], p[:, 2 * g:]
    w_lo, a_lo, g_lo = plo[:, :LANES], plo[:, LANES:2 * LANES], plo[:, 2 * LANES:]
    wraw = w0_ref[...] + _dot(jnp.tanh(w_lo), w2_ref[...])
    nz = -wraw
    softplus = jnp.maximum(nz, 0.0) + jnp.log(1.0 + jnp.exp(-jnp.abs(nz)))
    w = -softplus - 0.5
    ld_s[...] = -jnp.exp(w)
    a = _sigmoid(a0_ref[...] + _dot(a_lo, a2_ref[...]))
    gate = _dot(_sigmoid(g_lo), g2_ref[...])
    eblk = eblk_ref[...]
    kk = k * kk_ref[...]
    kk = kk / jnp.maximum(jnp.sqrt(_dot_split(kk * kk, eblk)), 1e-12)
    k = k * (1.0 + (a - 1.0) * ka_ref[...])
    r_s[...] = r
    k_s[...] = k
    v_s[...] = v
    a_s[...] = -kk
    b_s[...] = kk * a

    lane = lax.broadcasted_iota(I32, (1, LANES), 1)
    m0 = (lane < HEAD_DIM).astype(F32)
    m1 = 1.0 - m0
    ri = lax.broadcasted_iota(I32, (2 * ch, 2 * ch), 0)
    ci = lax.broadcasted_iota(I32, (2 * ch, 2 * ch), 1)
    same = (ri < ch) == (ci < ch)
    rloc = jnp.bitwise_and(ri, ch - 1)
    cloc = jnp.bitwise_and(ci, ch - 1)
    strict = jnp.where(jnp.logical_and(same, cloc < rloc), 1.0, 0.0)
    incl = jnp.where(jnp.logical_and(same, cloc <= rloc), 1.0, 0.0)
    eye = jnp.where(ri == ci, 1.0, 0.0)
    tri = jnp.where(lax.broadcasted_iota(I32, (ch, ch), 1) <= lax.broadcasted_iota(I32, (ch, ch), 0),
                    1.0, 0.0).astype(BF16)

    def bd(t):
        return jnp.concatenate([t * m0, t * m1], axis=0)

    nchunk = tm // ch
    per_chunk = []
    for c in range(nchunk):
        rows = slice(c * ch, (c + 1) * ch)
        ld = ld_s[rows, :]
        ld_hi = ld.astype(BF16)
        ld_lo = (ld - ld_hi.astype(F32)).astype(BF16)
        cum = (jnp.dot(tri, ld_hi, preferred_element_type=F32)
               + jnp.dot(tri, ld_lo, preferred_element_type=F32))
        gam = jnp.exp(cum)
        ginv = jnp.exp(-cum)
        per_chunk.append(dict(at=a_s[rows, :] * jnp.exp(cum - ld), rt=r_s[rows, :] * gam,
                              bt=b_s[rows, :] * ginv, kt=k_s[rows, :] * ginv, v=v_s[rows, :],
                              gl=gam[ch - 1:ch, :]))
    inst = [(c, q) for c in range(nchunk) for q in range(npair)]

    def part(name):
        return [per_chunk[c][name][:, q * LANES:(q + 1) * LANES] for c, q in inst]

    bt, kt, gl = part("bt"), part("kt"), part("gl")
    at_bd = [bd(t) for t in part("at")]
    rt_bd = [bd(t) for t in part("rt")]
    v_bd = [bd(t) for t in part("v")]
    bh_t = [bd(b * g_).T for b, g_ in zip(bt, gl)]
    kh_t = [bd(k_ * g_).T for k_, g_ in zip(kt, gl)]
    gmat = [_dot_nt(jnp.concatenate([a_, r_], axis=0), jnp.concatenate([b, b, k_, k_], axis=0))
            for a_, r_, b, k_ in zip(at_bd, rt_bd, bt, kt)]
    n_ab = [gm[:2 * ch, :2 * ch] * strict for gm in gmat]
    a_ak = [gm[:2 * ch, 2 * ch:] * strict for gm in gmat]
    m_rb = [gm[2 * ch:, :2 * ch] * incl for gm in gmat]
    m_rk = [gm[2 * ch:, 2 * ch:] * incl for gm in gmat]
    tinv = [eye + n for n in n_ab]
    npow = n_ab
    for step in range(1, 6):
        if step == 1:
            npow = [_dot(n, n) for n in npow]
        both = [_dot(n, jnp.concatenate([t, n], axis=1)) for n, t in zip(npow, tinv)]
        tinv = [t + b[:, :2 * ch] for t, b in zip(tinv, both)]
        npow = [b[:, 2 * ch:] for b in both]
    va = [_dot(jnp.concatenate([a_, k_, m_], axis=0), v_) for a_, k_, m_, v_ in zip(a_ak, kh_t, m_rk, v_bd)]
    wu = [_dot(t, jnp.concatenate([a_, x_[:2 * ch]], axis=1)) for t, a_, x_ in zip(tinv, at_bd, va)]
    pq = [_dot(jnp.concatenate([b, m_], axis=0), w_) for b, m_, w_ in zip(bh_t, m_rb, wu)]
    pmat = [eye * g_ + t[:2 * ch, :2 * ch] for g_, t in zip(gl, pq)]
    qmat = [t[:2 * ch, 2 * ch:] + x_[2 * ch:4 * ch] for t, x_ in zip(pq, va)]
    ry = [r_ + t[2 * ch:, :2 * ch] for r_, t in zip(rt_bd, pq)]
    y0 = [t[2 * ch:, 2 * ch:] + x_[4 * ch:] for t, x_ in zip(pq, va)]
    state = [hs_ref[q] for q in range(npair)]
    y_chunks = []
    for c in range(nchunk):
        ids = [c * npair + q for q in range(npair)]
        both = [_dot(jnp.concatenate([ry[i], pmat[i]], axis=0), st) for i, st in zip(ids, state)]
        yy = [t[:2 * ch] + y0[i] for i, t in zip(ids, both)]
        state = [t[2 * ch:] + qmat[i] for i, t in zip(ids, both)]
        y_chunks.append(jnp.concatenate([t[:ch] + t[ch:] for t in yy], axis=1))
    hs_ref[...] = jnp.stack(state, axis=0)

    y = jnp.concatenate(y_chunks, axis=0)
    inv_n = 1.0 / HEAD_DIM
    mean = _dot_split(y, eblk) * inv_n
    yc = y - mean
    var = _dot(yc * yc, eblk) * inv_n
    yn = yc * lax.rsqrt(var + RWKV_GN_EPS) * gng_ref[...] + gnb_ref[...]
    r = r_s[...]
    k = k_s[...]
    v = v_s[...]
    bonus = _dot(r * k * rk_ref[...], eblk) * v
    y_ref[0] = ((yn + bonus) * gate).astype(BF16)


def _rwkv_mixer(h, w_rkv, w_lora, mu, w0, w2, a0, a2, g2, k_k, k_a, r_k, gn_g, gn_b):
    bsz, s, d = h.shape
    g = w0.shape[0]
    tm = MIXER_TILE
    row = lambda t: t.astype(F32).reshape(1, -1)
    pad_rows = lambda t: jnp.zeros((LANES, g), F32).at[:t.shape[0]].set(t.astype(F32)).astype(BF16)
    mu1 = row(mu[:3 * g])
    mu2 = jnp.concatenate([
        jnp.zeros((LANES,), F32).at[:RWKV_DECAY_RANK].set(mu[3 * g:3 * g + RWKV_DECAY_RANK]),
        jnp.zeros((LANES,), F32).at[:RWKV_A_RANK].set(mu[3 * g + RWKV_DECAY_RANK:3 * g + RWKV_DECAY_RANK + RWKV_A_RANK]),
        mu[3 * g + RWKV_DECAY_RANK + RWKV_A_RANK:]]).reshape(1, -1)
    head = np.arange(g) // HEAD_DIM
    eblk = jnp.asarray(head[:, None] == head[None, :], BF16)
    args = (h, w_rkv, w_lora, mu1, mu2, row(w0), pad_rows(w2), row(a0), pad_rows(a2), g2.astype(BF16),
            row(k_k), row(k_a), row(r_k), row(gn_g), row(gn_b), eblk)
    full = lambda a: pl.BlockSpec(a.shape, lambda b, i: (0,) * a.ndim)
    return pl.pallas_call(
        _rwkv_kernel,
        grid=(bsz, s // tm),
        in_specs=[pl.BlockSpec((1, tm, d), lambda b, i: (b, i, 0))] + [full(a) for a in args[1:]],
        out_specs=pl.BlockSpec((1, tm, g), lambda b, i: (b, i, 0)),
        out_shape=jax.ShapeDtypeStruct((bsz, s, g), BF16),
        scratch_shapes=[pltpu.VMEM((SUBLANES, 3 * g), F32), pltpu.VMEM((SUBLANES, 3 * LANES), F32),
                        pltpu.VMEM((g // LANES, 2 * WKV_CHUNK, LANES), F32)]
                       + [pltpu.VMEM((tm, g), F32) for _ in range(6)],
        compiler_params=_cparams(("parallel", "arbitrary")),
        name="rwkv7_mixer",
    )(*args)


def _mixout_kernel(alpha, ya_ref, yb_ref, yc_ref, yd_ref, wo_ref, x_ref, gt_ref, lng_ref, lnb_ref,
                   sh_ref, sc_ref, wr_ref, br_ref, x1_ref, xs_ref, info_ref, seg_ref, h2s_ref, lgs_ref):
    g = ya_ref.shape[1]

    @pl.when(pl.program_id(0) == 0)
    def _():
        h2s_ref[...] = jnp.zeros_like(h2s_ref)
        lgs_ref[...] = jnp.zeros_like(lgs_ref)

    h2_prev = h2s_ref[...]
    lg_prev = lgs_ref[...]
    y = (jnp.dot(ya_ref[...], wo_ref[0:g, :], preferred_element_type=F32)
         + jnp.dot(yb_ref[...], wo_ref[g:2 * g, :], preferred_element_type=F32)
         + jnp.dot(yc_ref[...], wo_ref[2 * g:3 * g, :], preferred_element_type=F32)
         + jnp.dot(yd_ref[...], wo_ref[3 * g:, :], preferred_element_type=F32))
    _route_sort(h2_prev, lg_prev, xs_ref, info_ref, seg_ref)
    x1 = _layer_norm(alpha * x_ref[...] + (1.0 + gt_ref[0]) * y) * lng_ref[...] + lnb_ref[...]
    x1_ref[...] = x1
    h2 = _layer_norm(x1) * (1.0 + sc_ref[0]) + sh_ref[0]
    h2s_ref[...] = h2.astype(BF16)
    lgs_ref[...] = _dot(h2, wr_ref[...]) + br_ref[...]


def _mix_out(alpha, ys, w_out, x, gate, ln_g, ln_b, shift2, scale2, w_router, b_router):
    bsz, s, d = x.shape
    g = ys[0].shape[2]
    tm = MOE_TILE
    per_seq = s // tm
    nt = bsz * per_seq
    cur = lambda t: jnp.minimum(t, nt - 1)
    prev = lambda t: jnp.maximum(t - 1, 0)
    tok = lambda w: pl.BlockSpec((tm, w), lambda t: (cur(t), 0))
    per_b = pl.BlockSpec((1, 1, d), lambda t: (cur(t) // per_seq, 0, 0))
    full = lambda a: pl.BlockSpec(a.shape, lambda t: (0,) * a.ndim)
    row = lambda v: v.astype(F32).reshape(1, -1)
    args = (*[v.reshape(bsz * s, g) for v in ys], w_out, x.reshape(bsz * s, d), gate[:, None, :], row(ln_g),
            row(ln_b), shift2[:, None, :], scale2[:, None, :], w_router, b_router)
    x1, xs, info, seg = pl.pallas_call(
        functools.partial(_mixout_kernel, alpha),
        grid=(nt + 1,),
        in_specs=[tok(g)] * 4 + [full(w_out), tok(d), per_b, full(args[7]), full(args[8]), per_b, per_b,
                                 full(w_router), full(b_router)],
        out_specs=[tok(d),
                   pl.BlockSpec((MOE_TILE_ROWS, d // 2 + LANES), lambda t: (prev(t), 0)),
                   pl.BlockSpec((tm, LANES), lambda t: (prev(t), 0)),
                   pl.BlockSpec((1, SUBLANES, LANES), lambda t: (prev(t), 0, 0))],
        out_shape=[jax.ShapeDtypeStruct((bsz * s, d), F32),
                   jax.ShapeDtypeStruct((nt * MOE_TILE_ROWS, d // 2 + LANES), jnp.uint32),
                   jax.ShapeDtypeStruct((bsz * s, LANES), I32),
                   jax.ShapeDtypeStruct((nt, SUBLANES, LANES), I32)],
        scratch_shapes=[pltpu.VMEM((tm, d), BF16), pltpu.VMEM((tm, LANES), F32)],
        compiler_params=_cparams(("arbitrary",)),
        name="mix_out_route_sort",
    )(*args)
    return x1.reshape(bsz, s, d), xs, info, seg


def _route_sort(h2, lg, xs_ref, info_ref, seg_ref):
    tm = lg.shape[0]
    rt = xs_ref.shape[0]
    half = h2.shape[1] // 2
    lane = lax.broadcasted_iota(I32, (tm, LANES), 1)
    lane_f = lane.astype(F32)

    def top1(vals, mask):
        mv = jnp.where(mask, vals, -jnp.inf)
        m = jnp.max(mv, axis=-1, keepdims=True)
        idx = jnp.min(jnp.where(jnp.logical_and(mask, mv == m), lane_f, float(LANES)), axis=-1, keepdims=True)
        return m, idx.astype(I32)

    gmask = lane < N_EXPERT_GROUPS
    gm, gidx = top1(lg, gmask)
    g_val = 1.0 / jnp.sum(jnp.where(gmask, jnp.exp(lg - gm), 0.0), axis=-1, keepdims=True)
    elo = N_EXPERT_GROUPS + gidx * EXPERTS_PER_GROUP
    emask = jnp.logical_and(lane >= elo, lane < elo + EXPERTS_PER_GROUP)
    m1, i1 = top1(lg, emask)
    m2, i2 = top1(lg, jnp.logical_and(emask, lane != i1))
    e21 = jnp.exp(m2 - m1)
    w1 = g_val / (1.0 + e21)
    w2 = g_val * e21 / (1.0 + e21)
    e1 = i1 - N_EXPERT_GROUPS
    e2 = i2 - N_EXPERT_GROUPS
    oh1 = (lane == e1)
    oh2 = (lane == e2)
    ohs = jnp.where(jnp.logical_or(oh1, oh2), 1.0, 0.0)
    cnt = jnp.sum(ohs, axis=0, keepdims=True)
    units = jnp.floor((cnt + (MOE_UNIT - 1.0)) * (1.0 / MOE_UNIT))
    li = lax.broadcasted_iota(I32, (LANES, LANES), 0)
    lj = lax.broadcasted_iota(I32, (LANES, LANES), 1)
    upper = jnp.where(li < lj, 1.0, 0.0).astype(BF16)
    ustart = jnp.dot(jnp.broadcast_to(units, (SUBLANES, LANES)).astype(BF16), upper,
                     preferred_element_type=F32)[0:1, :]
    ri = lax.broadcasted_iota(I32, (tm, tm), 0)
    ci = lax.broadcasted_iota(I32, (tm, tm), 1)
    tri = jnp.where(ci < ri, 1.0, 0.0).astype(BF16)
    before = jnp.dot(tri, ohs.astype(BF16), preferred_element_type=F32)
    first = before + MOE_UNIT * ustart
    pos1 = jnp.sum(jnp.where(oh1, first, 0.0), axis=-1, keepdims=True)
    pos2 = jnp.sum(jnp.where(oh2, first, 0.0), axis=-1, keepdims=True)
    posm = jnp.where(lane == 0, pos1, jnp.where(lane == 1, pos2, -1.0))
    post = posm.T
    prow = lax.broadcasted_iota(I32, (rt, tm), 0).astype(F32)
    sel1 = prow == post[0:1, :]
    sel2 = prow == post[1:2, :]
    hb = h2.astype(BF16)
    xs = jnp.dot(jnp.where(jnp.logical_or(sel1, sel2), 1.0, 0.0).astype(BF16), hb, preferred_element_type=F32)
    bits = pltpu.bitcast(xs, jnp.uint32)
    xs_ref[:, :half] = jnp.bitwise_or(jnp.right_shift(bits[:, :half], jnp.uint32(16)), bits[:, half:])
    def terms(w):
        a = w.astype(BF16).astype(F32)
        b = (w - a).astype(BF16).astype(F32)
        return a, b, ((w - a) - b).astype(BF16).astype(F32)
    t1 = terms(w1)
    t2 = terms(w2)
    wm = jnp.zeros((tm, LANES), F32)
    for k, t in enumerate(t1 + t2):
        wm = jnp.where(lane == k, t, wm)
    wmb = wm.astype(BF16)
    s1 = jnp.dot(jnp.where(sel1, 1.0, 0.0).astype(BF16), wmb, preferred_element_type=F32)
    s2 = jnp.dot(jnp.where(sel2, 1.0, 0.0).astype(BF16), wmb, preferred_element_type=F32)
    wrow = (s1[:, 0:1] + s1[:, 1:2] + s1[:, 2:3]) + (s2[:, 3:4] + s2[:, 4:5] + s2[:, 5:6])
    mlane = lax.broadcasted_iota(I32, (rt, LANES), 1)
    wbits = pltpu.bitcast(jnp.broadcast_to(wrow, (rt, LANES)), jnp.uint32)
    xs_ref[:, half:] = jnp.where(mlane == 0, wbits, jnp.uint32(0))
    info_ref[...] = jnp.where(lane == 0, pos1, jnp.where(lane == 1, pos2, 0.0)).astype(I32)
    srow = lax.broadcasted_iota(I32, (SUBLANES, LANES), 0)
    total = jnp.sum(units, axis=-1, keepdims=True)
    seg = jnp.where(srow == 0, units, jnp.where(srow == 1, ustart, jnp.where(srow == 2, total, 0.0)))
    seg_ref[0] = seg.astype(I32)


def _worklist_vec_kernel(units_ref, ustart_ref, uidx_ref, slot_ref, be_ref, nxt_ref, nb_ref):
    nt = units_ref.shape[0]
    nbk = uidx_ref.shape[0]
    tile_units = MOE_TILE_ROWS // MOE_UNIT
    bu = MOE_BLOCK_UNITS
    u = units_ref[...].astype(F32)
    us = ustart_ref[...].astype(F32)
    ri = lax.broadcasted_iota(I32, (nt, nt), 0)
    ci = lax.broadcasted_iota(I32, (nt, nt), 1)
    tri = jnp.where(ci < ri, 1.0, 0.0).astype(BF16)
    cex = jnp.dot(tri, u.astype(BF16), preferred_element_type=F32)
    cin = cex + u
    tot = jnp.sum(u, axis=0, keepdims=True)
    nblk = jnp.floor((tot + (bu - 1.0)) * (1.0 / bu))
    li = lax.broadcasted_iota(I32, (LANES, LANES), 0)
    lj = lax.broadcasted_iota(I32, (LANES, LANES), 1)
    upper = jnp.where(li < lj, 1.0, 0.0).astype(BF16)
    b0 = jnp.dot(jnp.broadcast_to(nblk, (SUBLANES, LANES)).astype(BF16), upper,
                 preferred_element_type=F32)[0:1, :]
    b0in = b0 + nblk
    used = jnp.sum(nblk, axis=-1, keepdims=True)
    cand = jnp.where(jnp.logical_and(lj > li, jnp.broadcast_to(nblk, (LANES, LANES)) > 0.0),
                     lj.astype(F32), 999.0)
    nxt_col = jnp.min(cand, axis=-1, keepdims=True)
    nxt_col = jnp.where(nxt_col > 900.0, -1.0, nxt_col)

    ul = lax.broadcasted_iota(I32, (nt, LANES), 1).astype(F32)
    slot = jnp.full((nt, LANES), -1.0, F32)
    for e in range(N_EXPERTS):
        st = us[:, e:e + 1]
        sel = jnp.logical_and(ul >= st, ul < st + u[:, e:e + 1])
        slot = jnp.where(sel, bu * b0[:, e:e + 1] + cex[:, e:e + 1] + (ul - st), slot)
    slot_ref[...] = slot.astype(I32)

    def by_tile(t):
        return jnp.concatenate([t, jnp.zeros((LANES - nt, LANES), F32)], axis=0).T

    cin_t, cex_t, us_t = by_tile(cin), by_tile(cex), by_tile(us)
    brow = lax.broadcasted_iota(I32, (nbk, 1), 0).astype(F32)
    eb = jnp.zeros((nbk, 1), F32)
    b0b = jnp.zeros((nbk, 1), F32)
    totb = jnp.zeros((nbk, 1), F32)
    nxtb = jnp.full((nbk, 1), -1.0, F32)
    cin_row = jnp.zeros((nbk, LANES), F32)
    cex_row = jnp.zeros((nbk, LANES), F32)
    us_row = jnp.zeros((nbk, LANES), F32)
    for e in range(N_EXPERTS):
        ine = jnp.logical_and(brow >= b0[:, e:e + 1], brow < b0in[:, e:e + 1])
        eb = jnp.where(ine, float(e), eb)
        b0b = jnp.where(ine, b0[:, e:e + 1], b0b)
        totb = jnp.where(ine, tot[:, e:e + 1], totb)
        nxtb = jnp.where(ine, nxt_col[e:e + 1, :], nxtb)
        cin_row = jnp.where(ine, cin_t[e:e + 1, :], cin_row)
        cex_row = jnp.where(ine, cex_t[e:e + 1, :], cex_row)
        us_row = jnp.where(ine, us_t[e:e + 1, :], us_row)
    active = brow < used
    jl = lax.broadcasted_iota(I32, (1, LANES), 1).astype(F32)
    q = (brow - b0b) * bu + jl
    tau = jnp.zeros((nbk, LANES), F32)
    for t in range(nt):
        tau = tau + jnp.where(cin_row[:, t:t + 1] <= q, 1.0, 0.0)
    base = jnp.zeros((nbk, LANES), F32)
    for t in range(nt):
        base = jnp.where(tau == float(t), us_row[:, t:t + 1] - cex_row[:, t:t + 1] + float(t * tile_units), base)
    ok = jnp.logical_and(jnp.logical_and(active, q < totb), jl < bu)
    uidx_ref[...] = jnp.where(ok, base + q, -1.0).astype(I32)
    be_ref[...] = jnp.broadcast_to(jnp.where(active, eb, N_EXPERTS - 1.0), (nbk, LANES)).astype(I32)
    nxt_ref[...] = jnp.broadcast_to(jnp.where(active, nxtb, -1.0), (nbk, LANES)).astype(I32)
    nb_ref[...] = jnp.broadcast_to(used, (SUBLANES, LANES)).astype(I32)


def _worklist_vec(units2, ustart2, n_blocks):
    nt = units2.shape[0]
    tile_units = MOE_TILE_ROWS // MOE_UNIT
    uidx, slot, be, nxt, nb = pl.pallas_call(
        _worklist_vec_kernel,
        out_shape=[jax.ShapeDtypeStruct((n_blocks, LANES), I32), jax.ShapeDtypeStruct((nt, LANES), I32),
                   jax.ShapeDtypeStruct((n_blocks, LANES), I32), jax.ShapeDtypeStruct((n_blocks, LANES), I32),
                   jax.ShapeDtypeStruct((SUBLANES, LANES), I32)],
        name="moe_worklist",
    )(units2, ustart2)
    return (uidx[:, :MOE_BLOCK_UNITS].reshape(n_blocks * MOE_BLOCK_UNITS),
            slot[:, :tile_units].reshape(nt * tile_units), be[:, 0], nxt[:, 0], nb[0, :1])


def _expert_kernel(layer, be_ref, nxt_ref, uidx_ref, nb_ref, xs_hbm, w1_hbm, w3_hbm, w2_hbm, o_ref,
                   xbuf, gsem, w1f, w3f, w2f, wsem, wslot, w1b, w3b, w2b):
    b = pl.program_id(0)
    used = nb_ref[0]
    half = xs_hbm.shape[1] - LANES
    slot = lax.rem(b, 2)
    other = 1 - slot

    def weight_copies(e, s):
        return (pltpu.make_async_copy(w1_hbm.at[layer, e], w1f.at[s], wsem.at[s]),
                pltpu.make_async_copy(w3_hbm.at[layer, e], w3f.at[s], wsem.at[s]),
                pltpu.make_async_copy(w2_hbm.at[layer, e], w2f.at[s], wsem.at[s]))

    def gather_copy(s, j, unit):
        return pltpu.make_async_copy(xs_hbm.at[pl.ds(pl.multiple_of(unit * MOE_UNIT, MOE_UNIT), MOE_UNIT), :],
                                     xbuf.at[s, pl.ds(j * MOE_UNIT, MOE_UNIT), :], gsem.at[s])

    def gather_start(blk, s):
        for j in range(MOE_BLOCK_UNITS):
            gather_copy(s, j, jnp.maximum(uidx_ref[blk * MOE_BLOCK_UNITS + j], 0)).start(priority=0)

    def gather_wait(s):
        for j in range(MOE_BLOCK_UNITS):
            gather_copy(s, j, 0).wait()

    @pl.when(b == 0)
    def _():
        gather_start(0, 0)
        wslot[0] = 1
        for cp in weight_copies(be_ref[0], 0):
            cp.start(priority=1)

    @pl.when(b + 1 < used)
    def _():
        gather_start(b + 1, other)

    @pl.when(b >= used)
    def _():
        o_ref[...] = jnp.zeros_like(o_ref)

    @pl.when(b < used)
    def _():
        prev = be_ref[jnp.maximum(b - 1, 0)]

        @pl.when(jnp.logical_or(b == 0, be_ref[b] != prev))
        def _():
            s = 1 - wslot[0]
            wslot[0] = s
            for cp in weight_copies(0, s):
                cp.wait()
            w1b[...] = w1f[s].astype(BF16)
            w3b[...] = w3f[s].astype(BF16)
            w2b[...] = w2f[s].astype(BF16)

            @pl.when(nxt_ref[b] >= 0)
            def _():
                for cp in weight_copies(nxt_ref[b], 1 - s):
                    cp.start(priority=1)

        gather_wait(slot)
        xw = xbuf[slot]
        x = _unpack_bf16_pairs(xw[:, :half])
        wrow = pltpu.bitcast(xw[:, half:], F32)[:, 0:1]
        a = jnp.dot(x, w1b[...], preferred_element_type=F32)
        gte = jnp.dot(x, w3b[...], preferred_element_type=F32)
        mid = (a * _sigmoid(a)) * gte
        o_ref[...] = _pack_bf16_pairs(jnp.dot(mid.astype(BF16), w2b[...], preferred_element_type=F32) * wrow)


def _expert_ffn(layer, xs, blk_expert, blk_next, unit_idx, n_used, w1, w3, w2):
    d, de = w1.shape[2], w1.shape[3]
    nb = blk_expert.shape[0]
    rows = MOE_BLOCK_UNITS * MOE_UNIT
    hbm = pl.BlockSpec(memory_space=pl.ANY)
    grid_spec = pltpu.PrefetchScalarGridSpec(
        num_scalar_prefetch=4,
        grid=(nb,),
        in_specs=[hbm, hbm, hbm, hbm],
        out_specs=pl.BlockSpec((rows, d // 2), lambda b, be, nx, ui, nu: (b, 0)),
        scratch_shapes=[pltpu.VMEM((2, rows, xs.shape[1]), jnp.uint32), pltpu.SemaphoreType.DMA((2,)),
                        pltpu.VMEM((2, d, de), F32), pltpu.VMEM((2, d, de), F32), pltpu.VMEM((2, de, d), F32),
                        pltpu.SemaphoreType.DMA((2,)), pltpu.SMEM((1,), I32),
                        pltpu.VMEM((d, de), BF16), pltpu.VMEM((d, de), BF16), pltpu.VMEM((de, d), BF16)],
    )
    return pl.pallas_call(
        functools.partial(_expert_kernel, layer),
        grid_spec=grid_spec,
        out_shape=jax.ShapeDtypeStruct((nb * rows, d // 2), jnp.uint32),
        compiler_params=_cparams(("arbitrary",)),
        name="moe_expert_ffn",
    )(blk_expert, blk_next, unit_idx, n_used, xs, w1, w3, w2)


def _combine_kernel(alpha, with_next, nu_ref, slot_ref, ys_hbm, info_ref, x_ref, gt_ref, lng_ref, lnb_ref, *rest):
    if with_next:
        sh_ref, sc_ref, o_ref, hn_ref, ybuf, sem = rest
    else:
        o_ref, ybuf, sem = rest
    i = pl.program_id(0)
    nt = pl.num_programs(0)
    tm = x_ref.shape[0]
    rt = ybuf.shape[1]
    slot = lax.rem(i, 2)
    other = 1 - slot

    def unit_copy(tile, s, j):
        src = pl.multiple_of(slot_ref[tile * (rt // MOE_UNIT) + j] * MOE_UNIT, MOE_UNIT)
        dst = pl.multiple_of(j * MOE_UNIT, MOE_UNIT)
        return pltpu.make_async_copy(ys_hbm.at[pl.ds(src, MOE_UNIT), :], ybuf.at[s, pl.ds(dst, MOE_UNIT), :], sem.at[s])

    def start(tile, s):
        def body(j, c):
            unit_copy(tile, s, j).start(priority=1)
            return c
        lax.fori_loop(0, nu_ref[tile], body, 0)

    def wait(tile, s):
        def body(j, c):
            unit_copy(tile, s, j).wait()
            return c
        lax.fori_loop(0, nu_ref[tile], body, 0)

    @pl.when(i == 0)
    def _():
        ybuf[...] = jnp.zeros_like(ybuf)
        start(0, 0)

    @pl.when(i + 1 < nt)
    def _():
        start(i + 1, other)

    wait(i, slot)
    info = info_ref[...]
    col = lax.broadcasted_iota(I32, (tm, rt), 1)
    pick = jnp.where(jnp.logical_or(col == info[:, 0:1], col == info[:, 1:2]), 1.0, 0.0).astype(BF16)
    y = jnp.dot(pick, _unpack_bf16_pairs(ybuf[slot]), preferred_element_type=F32)
    x2 = _layer_norm(alpha * x_ref[...] + (1.0 + gt_ref[0]) * y) * lng_ref[...] + lnb_ref[...]
    o_ref[...] = x2
    if with_next:
        hn_ref[...] = (_layer_norm(x2) * (1.0 + sc_ref[0]) + sh_ref[0]).astype(BF16)


def _combine(alpha, ys, tile_units, unit_slot, info, x1, gate, ln_g, ln_b, seq, next_mod=None):
    t, d = x1.shape
    tm = MOE_TILE
    per_seq = seq // tm
    tok = pl.BlockSpec((tm, d), lambda i, nu, us: (i, 0))
    per_b = pl.BlockSpec((1, 1, d), lambda i, nu, us: (i // per_seq, 0, 0))
    row = pl.BlockSpec((1, d), lambda i, nu, us: (0, 0))
    with_next = next_mod is not None
    args = [tile_units, unit_slot, ys, info, x1, gate[:, None, :], ln_g.astype(F32).reshape(1, d),
            ln_b.astype(F32).reshape(1, d)]
    in_specs = [pl.BlockSpec(memory_space=pl.ANY), pl.BlockSpec((tm, LANES), lambda i, nu, us: (i, 0)),
                tok, per_b, row, row]
    out_specs, out_shape = tok, jax.ShapeDtypeStruct((t, d), F32)
    if with_next:
        args += [next_mod[0][:, None, :], next_mod[1][:, None, :]]
        in_specs += [per_b, per_b]
        out_specs, out_shape = [tok, tok], [out_shape, jax.ShapeDtypeStruct((t, d), BF16)]
    grid_spec = pltpu.PrefetchScalarGridSpec(
        num_scalar_prefetch=2,
        grid=(t // tm,),
        in_specs=in_specs,
        out_specs=out_specs,
        scratch_shapes=[pltpu.VMEM((2, MOE_TILE_ROWS, d // 2), jnp.uint32), pltpu.SemaphoreType.DMA((2,))],
    )
    return pl.pallas_call(
        functools.partial(_combine_kernel, alpha, with_next),
        grid_spec=grid_spec,
        out_shape=out_shape,
        compiler_params=_cparams(("arbitrary",)),
        name="moe_combine_ln",
    )(*args)


def _moe(layer, alpha, xs, info, seg, x1, gate, ln_g, ln_b, w1, w3, w2, next_mod):
    bsz, s, d = x1.shape
    t = bsz * s
    nt = t // MOE_TILE
    tile_units = seg[:, 2, 0]
    max_units = nt * (TOP_K * MOE_TILE // MOE_UNIT + N_EXPERTS * (MOE_UNIT - 1) // MOE_UNIT)
    n_blocks = max_units // MOE_BLOCK_UNITS + N_EXPERTS
    unit_idx, unit_slot, blk_expert, blk_next, n_used = _worklist_vec(seg[:, 0, :], seg[:, 1, :], n_blocks)
    ys = _expert_ffn(layer, xs, blk_expert, blk_next, unit_idx, n_used, w1, w3, w2)
    out = _combine(alpha, ys, tile_units, unit_slot, info, x1.reshape(t, d), gate, ln_g, ln_b, s, next_mod)
    if next_mod is None:
        return out.reshape(bsz, s, d), None
    return out[0].reshape(bsz, s, d), out[1].reshape(bsz, s, d)


def kernel(x, c, w_ada, b_ada, ln_g, ln_b, w_in, w_out, conv_w, rwkv_mu, rwkv_w0, rwkv_w2, rwkv_a0, rwkv_a2, rwkv_g2, rwkv_kk, rwkv_ka, rwkv_rk, rwkv_gn_g, rwkv_gn_b, attn_sinks, rel_bias, s5_lambda_re, s5_lambda_im, s5_log_dt, s5_b_re, s5_b_im, s5_c_re, s5_c_im, s5_d, s5_glu_w, s5_glu_b, router_group_w, router_group_b, router_expert_w, router_expert_b, moe_w1, moe_w3, moe_w2):
    depth = w_ada.shape[0]
    d = x.shape[-1]
    g = d // 4
    alpha = (2 * depth) ** 0.25
    n_heads = g // HEAD_DIM
    att_kv = max(1, n_heads // 4) * HEAD_DIM
    rw_off = 3 * g
    lora = RWKV_DECAY_RANK + RWKV_A_RANK + RWKV_GATE_RANK
    att_off = rw_off + 3 * g + lora
    s5_off = att_off + g + 2 * att_kv

    mod = _modulation(c, w_ada, b_ada)
    for l in range(depth):
        sh1, sc1, gt1, sh2, sc2, gt2 = jnp.split(mod[l], 6, axis=-1)
        wl = w_in[l]
        w_conv = wl[:, :rw_off].astype(BF16)
        w_rkv = wl[:, rw_off:rw_off + 3 * g].astype(BF16)
        lo = rw_off + 3 * g
        zcol = lambda n: jnp.zeros((d, n), F32)
        w_lora = jnp.concatenate([
            wl[:, lo:lo + RWKV_DECAY_RANK], zcol(LANES - RWKV_DECAY_RANK),
            wl[:, lo + RWKV_DECAY_RANK:lo + RWKV_DECAY_RANK + RWKV_A_RANK], zcol(LANES - RWKV_A_RANK),
            wl[:, lo + RWKV_DECAY_RANK + RWKV_A_RANK:att_off]], axis=1).astype(BF16)
        w_q = wl[:, att_off:att_off + g].astype(BF16)
        w_kv = wl[:, att_off + g:s5_off].astype(BF16)
        w_s5 = wl[:, s5_off:].astype(BF16)

        if l == 0:
            h = _adaln(x, sh1, sc1)
        y_rwkv = _rwkv_mixer(h, w_rkv, w_lora, rwkv_mu[l], rwkv_w0[l], rwkv_w2[l], rwkv_a0[l], rwkv_a2[l],
                             rwkv_g2[l], rwkv_kk[l], rwkv_ka[l], rwkv_rk[l], rwkv_gn_g[l], rwkv_gn_b[l])
        y_ssm, y_conv, y_att = _s5_conv_swa_mixers(
            h, w_s5, s5_lambda_re[l], s5_lambda_im[l], s5_log_dt[l], s5_b_re[l], s5_b_im[l], s5_c_re[l],
            s5_c_im[l], s5_d[l], s5_glu_w[l], s5_glu_b[l], w_conv, conv_w[l].astype(F32), w_q, w_kv,
            attn_sinks[l], rel_bias)
        w_router = jnp.zeros((d, LANES), F32)
        w_router = w_router.at[:, :N_EXPERT_GROUPS].set(router_group_w[l])
        w_router = w_router.at[:, N_EXPERT_GROUPS:N_EXPERT_GROUPS + N_EXPERTS].set(router_expert_w[l]).astype(BF16)
        b_router = jnp.zeros((1, LANES), F32)
        b_router = b_router.at[0, :N_EXPERT_GROUPS].set(router_group_b[l])
        b_router = b_router.at[0, N_EXPERT_GROUPS:N_EXPERT_GROUPS + N_EXPERTS].set(router_expert_b[l])
        x1, xs, info, seg = _mix_out(alpha, (y_conv, y_rwkv, y_att, y_ssm), w_out[l].astype(BF16), x, gt1,
                                     ln_g[l, 0], ln_b[l, 0], sh2, sc2, w_router, b_router)
        next_mod = None
        if l + 1 < depth:
            nsh1, nsc1 = jnp.split(mod[l + 1], 6, axis=-1)[:2]
            next_mod = (nsh1, nsc1)
        x, h = _moe(l, alpha, xs, info, seg, x1, gt2, ln_g[l, 1], ln_b[l, 1], moe_w1, moe_w3, moe_w2, next_mod)
    return x
```
